```python
import jax, jax.numpy as jnp
from jax import lax
import numpy as np

D_MODEL = 1024
BATCH = 8
SEQ = 4096
DEPTH = 4

ROPE_THETA = 10000.0
EPS = 1e-6
BLOCK = 128

FOX_HEADS = 8
FOX_DIM = D_MODEL // 16
GLA_HEADS = 4
GLA_DK = D_MODEL // 16
GLA_DV = D_MODEL // 8
GLA_RANK = 16
GLA_TAU = 16.0
GLA_CHUNK = 64
RET_HEADS = 4
RET_DK = D_MODEL // 16
RET_DV = D_MODEL // 8
RET_CHUNK = 128
DIL_HEADS = 8
DIL_DIM = D_MODEL // 16
DIL_PATTERNS = ((128, 1), (512, 4), (2048, 16))

FOX_W = FOX_HEADS * FOX_DIM
GLA_KW = GLA_HEADS * GLA_DK
GLA_VW = GLA_HEADS * GLA_DV
EVEN_MIX = FOX_W + GLA_VW
EVEN_SIZES = (FOX_W, FOX_W, FOX_W, FOX_HEADS, GLA_KW, GLA_KW, GLA_VW, GLA_RANK, EVEN_MIX)
EVEN_IN = sum(EVEN_SIZES)

RET_KW = RET_HEADS * RET_DK
RET_VW = RET_HEADS * RET_DV
DIL_W = DIL_HEADS * DIL_DIM
ODD_MIX = RET_VW + DIL_W
ODD_SIZES = (RET_KW, RET_KW, RET_VW, DIL_W, DIL_W, DIL_W, ODD_MIX)
ODD_IN = sum(ODD_SIZES)

kernel_name = "hybrid_fox_gla_retnet_dilated"

F32 = jnp.float32


def _offsets(sizes):
    return np.cumsum(np.array(sizes))[:-1].tolist()


def rmsnorm(x, g):
    xf = x.astype(F32)
    y = xf * lax.rsqrt(jnp.mean(xf * xf, axis=-1, keepdims=True) + EPS)
    return (y * g.astype(F32)).astype(x.dtype)


def split_heads(t, h):
    B, S, _ = t.shape
    return t.reshape(B, S, h, -1).transpose(0, 2, 1, 3)


def merge_heads(t):
    B, H, S, d = t.shape
    return t.transpose(0, 2, 1, 3).reshape(B, S, H * d)


def rope(x):
    S, dh = x.shape[-2], x.shape[-1]
    inv = jnp.power(ROPE_THETA, -jnp.arange(0, dh, 2, dtype=F32) / dh)
    ang = jnp.arange(S, dtype=F32)[:, None] * inv[None, :]
    cos, sin = jnp.cos(ang), jnp.sin(ang)
    xf = x.astype(F32)
    x1, x2 = xf[..., : dh // 2], xf[..., dh // 2:]
    return jnp.concatenate([x1 * cos - x2 * sin, x2 * cos + x1 * sin], axis=-1).astype(x.dtype)


def head_norm(o, center):
    of = o.astype(F32)
    if center:
        of = of - jnp.mean(of, axis=-1, keepdims=True)
    return of * lax.rsqrt(jnp.mean(of * of, axis=-1, keepdims=True) + EPS)


def forgetting_attention(q, k, v, logf):
    B, H, S, dh = q.shape
    c = jnp.cumsum(logf, axis=-1)
    nb = S // BLOCK
    qb = q.reshape(B, H, nb, BLOCK, dh).transpose(2, 0, 1, 3, 4)
    cb = c.reshape(B, H, nb, BLOCK).transpose(2, 0, 1, 3)
    kpos = jnp.arange(S)
    scale = dh ** -0.5

    def one_block(args):
        i, qi, ci = args
        s = jnp.einsum('bhqd,bhkd->bhqk', qi, k, preferred_element_type=F32) * scale
        s = s + ci[..., :, None] - c[..., None, :]
        qpos = i * BLOCK + jnp.arange(BLOCK)
        s = jnp.where(kpos[None, :] <= qpos[:, None], s, -jnp.inf)
        p = jax.nn.softmax(s, axis=-1)
        return jnp.einsum('bhqk,bhkd->bhqd', p.astype(v.dtype), v)

    o = lax.map(one_block, (jnp.arange(nb), qb, cb))
    return o.transpose(1, 2, 0, 3, 4).reshape(B, H, S, dh)


def gla_chunked(q, k, v, g):
    B, H, S, dk = q.shape
    dv = v.shape[-1]
    C = GLA_CHUNK
    n = S // C
    r = lambda t: t.astype(F32).reshape(B, H, n, C, t.shape[-1])
    qf, kf, vf, gf = r(q) * dk ** -0.5, r(k), r(v), r(g)
    b = jnp.cumsum(gf, axis=3)
    b_last = b[:, :, :, -1:, :]
    q_t = qf * jnp.exp(b)
    k_t = kf * jnp.exp(-b)
    mask = jnp.tril(jnp.ones((C, C), dtype=bool))
    att = jnp.where(mask, jnp.einsum('bhnqd,bhnkd->bhnqk', q_t, k_t), 0.0)
    o_intra = jnp.einsum('bhnqk,bhnkv->bhnqv', att, vf)
    kv = jnp.einsum('bhnkd,bhnkv->bhndv', kf * jnp.exp(b_last - b), vf)
    decay = jnp.exp(b_last[:, :, :, 0, :])

    def step(state, inp):
        kv_n, dec_n = inp
        return state * dec_n[..., None] + kv_n, state

    _, prev = lax.scan(step, jnp.zeros((B, H, dk, dv), F32),
                       (kv.transpose(2, 0, 1, 3, 4), decay.transpose(2, 0, 1, 3)))
    prev = prev.transpose(1, 2, 0, 3, 4)
    o_inter = jnp.einsum('bhnqd,bhndv->bhnqv', q_t, prev)
    return (o_intra + o_inter).reshape(B, H, S, dv).astype(v.dtype)


def retention_chunked(q, k, v):
    B, H, S, dk = q.shape
    dv = v.shape[-1]
    C = RET_CHUNK
    n = S // C
    log_g = jnp.log(1.0 - jnp.power(2.0, -5.0 - jnp.arange(H, dtype=F32)))
    idx = jnp.arange(C, dtype=F32)
    diff = idx[:, None] - idx[None, :]
    dmat = jnp.where(diff >= 0, jnp.exp(jnp.maximum(diff, 0.0)[None] * log_g[:, None, None]), 0.0)
    xi = jnp.exp((idx + 1.0)[None, :] * log_g[:, None])
    zeta = jnp.exp((C - 1.0 - idx)[None, :] * log_g[:, None])
    gC = jnp.exp(C * log_g)
    r = lambda t: t.astype(F32).reshape(B, H, n, C, t.shape[-1])
    qf, kf, vf = r(q) * dk ** -0.5, r(k), r(v)
    att = jnp.einsum('bhnqd,bhnkd->bhnqk', qf, kf) * dmat[None, :, None]
    o_intra = jnp.einsum('bhnqk,bhnkv->bhnqv', att, vf)
    kv = jnp.einsum('bhnkd,bhnkv->bhndv', kf * zeta[None, :, None, :, None], vf)

    def step(state, kv_n):
        return state * gC[None, :, None, None] + kv_n, state

    _, prev = lax.scan(step, jnp.zeros((B, H, dk, dv), F32), kv.transpose(2, 0, 1, 3, 4))
    prev = prev.transpose(1, 2, 0, 3, 4)
    o_inter = jnp.einsum('bhnqd,bhndv->bhnqv', qf * xi[None, :, None, :, None], prev)
    return (o_intra + o_inter).reshape(B, H, S, dv).astype(v.dtype)


def _dilated_branch(q, k, v, window, dil):
    B, H, S, dh = q.shape
    span = window // dil
    unit = dil * BLOCK
    Sp = -(-S // unit) * unit
    m = Sp // dil
    nb = m // BLOCK

    def to_strided(t):
        t = jnp.pad(t, ((0, 0), (0, 0), (0, Sp - S), (0, 0)))
        return t.reshape(B, H, m, dil, dh).transpose(0, 1, 3, 2, 4).reshape(B, H, dil, nb, BLOCK, dh)

    def with_prev(t):
        prev = jnp.pad(t, ((0, 0), (0, 0), (0, 0), (1, 0), (0, 0), (0, 0)))[:, :, :, :-1]
        return jnp.concatenate([prev, t], axis=4)

    qs = to_strided(q)
    kb = with_prev(to_strided(k))
    vb = with_prev(to_strided(v))
    s = jnp.einsum('bhrnqd,bhrnkd->bhrnqk', qs, kb, preferred_element_type=F32) * dh ** -0.5
    qi = jnp.arange(BLOCK)
    ki = jnp.arange(2 * BLOCK) - BLOCK
    rel = qi[:, None] - ki[None, :]
    valid = (rel >= 0) & (rel <= span) & ((jnp.arange(nb)[:, None, None] * BLOCK + ki[None, None, :]) >= 0)
    s = jnp.where(valid, s, -jnp.inf)
    mx = jnp.max(s, axis=-1, keepdims=True)
    p = jnp.exp(s - mx)
    den = jnp.sum(p, axis=-1, keepdims=True)
    o = jnp.einsum('bhrnqk,bhrnkd->bhrnqd', p, vb.astype(F32)) / den
    lse = (mx + jnp.log(den))[..., 0]

    def from_strided(t):
        tail = t.shape[5:]
        t = t.reshape((B, H, dil, m) + tail)
        t = jnp.moveaxis(t, 2, 3).reshape((B, H, Sp) + tail)
        return t[:, :, :S]

    return from_strided(o), from_strided(lse)


def dilated_attention(q, k, v):
    outs, lses = [], []
    for window, dil in DIL_PATTERNS:
        o, lse = _dilated_branch(q, k, v, window, dil)
        outs.append(o)
        lses.append(lse)
    w = jax.nn.softmax(jnp.stack(lses, axis=0), axis=0)
    o = jnp.sum(w[..., None] * jnp.stack(outs, axis=0), axis=0)
    return o.astype(v.dtype)


def even_layer(h, norm_g, w_in, b_f, w_lr, b_lr, gla_g, w_out):
    u = rmsnorm(h, norm_g)
    z = u @ w_in
    fq, fk, fv, ff, gq, gk, gv, glr, gate = jnp.split(z, _offsets(EVEN_SIZES), axis=-1)
    logf = jax.nn.log_sigmoid((ff + b_f).astype(F32)).transpose(0, 2, 1)
    o_a = forgetting_attention(split_heads(fq, FOX_HEADS), split_heads(fk, FOX_HEADS),
                               split_heads(fv, FOX_HEADS), logf)
    glog = jax.nn.log_sigmoid((glr @ w_lr + b_lr).astype(F32)) / GLA_TAU
    o_b = gla_chunked(split_heads(gq, GLA_HEADS), split_heads(gk, GLA_HEADS),
                      split_heads(gv, GLA_HEADS), split_heads(glog, GLA_HEADS))
    o_b = merge_heads(head_norm(o_b, center=False)) * gla_g.astype(F32)
    mix = jnp.concatenate([merge_heads(o_a).astype(F32), o_b], axis=-1).astype(h.dtype)
    return h + (mix * jax.nn.silu(gate)) @ w_out


def odd_layer(h, norm_g, w_in, gn_w, gn_b, w_out):
    u = rmsnorm(h, norm_g)
    z = u @ w_in
    rq, rk, rv, dq, dk, dv, gate = jnp.split(z, _offsets(ODD_SIZES), axis=-1)
    o_c = retention_chunked(rope(split_heads(rq, RET_HEADS)), rope(split_heads(rk, RET_HEADS)),
                            split_heads(rv, RET_HEADS))
    o_c = merge_heads(head_norm(o_c, center=True)) * gn_w.astype(F32) + gn_b.astype(F32)
    o_d = dilated_attention(rope(split_heads(dq, DIL_HEADS)), rope(split_heads(dk, DIL_HEADS)),
                            split_heads(dv, DIL_HEADS))
    mix = jnp.concatenate([o_c, merge_heads(o_d).astype(F32)], axis=-1).astype(h.dtype)
    return h + (mix * jax.nn.silu(gate)) @ w_out


def setup_inputs(seed: int = 0) -> dict:
    key = jax.random.key(seed)
    ks = jax.random.split(key, 16)
    NE = (DEPTH + 1) // 2
    NO = DEPTH // 2
    nrm = lambda k, shape: jax.random.normal(k, shape, F32)
    return {
        "x": nrm(ks[0], (BATCH, SEQ, D_MODEL)),
        "norm_even": 1.0 + 0.02 * nrm(ks[1], (NE, D_MODEL)),
        "w_in_even": nrm(ks[2], (NE, D_MODEL, EVEN_IN)) * D_MODEL ** -0.5,
        "b_f_even": 3.0 + 0.5 * nrm(ks[3], (NE, FOX_HEADS)),
        "w_lr_even": nrm(ks[4], (NE, GLA_RANK, GLA_KW)) * GLA_RANK ** -0.5,
        "b_lr_even": 0.02 * nrm(ks[5], (NE, GLA_KW)),
        "gla_norm_even": 1.0 + 0.02 * nrm(ks[6], (NE, GLA_VW)),
        "w_out_even": nrm(ks[7], (NE, EVEN_MIX, D_MODEL)) * EVEN_MIX ** -0.5,
        "norm_odd": 1.0 + 0.02 * nrm(ks[8], (NO, D_MODEL)),
        "w_in_odd": nrm(ks[9], (NO, D_MODEL, ODD_IN)) * D_MODEL ** -0.5,
        "ret_gn_w_odd": 1.0 + 0.02 * nrm(ks[10], (NO, RET_VW)),
        "ret_gn_b_odd": 0.02 * nrm(ks[11], (NO, RET_VW)),
        "w_out_odd": nrm(ks[12], (NO, ODD_MIX, D_MODEL)) * ODD_MIX ** -0.5,
        "final_norm": 1.0 + 0.02 * nrm(ks[13], (D_MODEL,)),
    }


def reference(x, norm_even, w_in_even, b_f_even, w_lr_even, b_lr_even, gla_norm_even, w_out_even,
              norm_odd, w_in_odd, ret_gn_w_odd, ret_gn_b_odd, w_out_odd, final_norm):
    h = x
    for i in range(DEPTH):
        j = i // 2
        if i % 2 == 0:
            h = even_layer(h, norm_even[j], w_in_even[j], b_f_even[j], w_lr_even[j],
                           b_lr_even[j], gla_norm_even[j], w_out_even[j])
        else:
            h = odd_layer(h, norm_odd[j], w_in_odd[j], ret_gn_w_odd[j], ret_gn_b_odd[j],
                          w_out_odd[j])
    return rmsnorm(h, final_norm)
```

```python
import functools
import math

import numpy as np
import jax
import jax.numpy as jnp
from jax import lax
from jax.experimental import pallas as pl
from jax.experimental.pallas import tpu as pltpu

F32 = jnp.float32
BF16 = jnp.bfloat16

EPS = 1e-6
ROPE_THETA = 10000.0
HEAD_DIM = 64
VAL_DIM = 128
ATT_HEADS = 8
REC_HEADS = 4
GLA_RANK = 16
GLA_TAU = 16.0
GLA_CHUNK = 64
RET_CHUNK = 128
DIL_BLOCK = 128
DIL_PATTERNS = ((128, 1), (512, 4), (2048, 16))

Z_WIDTH = 3584
COL_TILE = 512
LANES = 128
V7X_VMEM_LIMIT = 56 * 1024 * 1024

NT_DIMS = (((1,), (1,)), ((), ()))
TN_DIMS = (((0,), (0,)), ((), ()))


def _params(*sem):
    return pltpu.CompilerParams(dimension_semantics=sem, vmem_limit_bytes=V7X_VMEM_LIMIT)


def _split2(x):
    hi = x.astype(BF16)
    lo = (x - hi.astype(F32)).astype(BF16)
    return hi, lo


def _log_sigmoid(x):
    return jnp.minimum(x, 0.0) - jnp.log(1.0 + jnp.exp(-jnp.abs(x)))


def _rms_normed(x, g):
    ms = jnp.mean(x * x, axis=-1, keepdims=True)
    return x * lax.rsqrt(ms + EPS) * g


def _in_even_kernel(x_ref, g_ref, w_ref, ws_ref, bf_ref, wlr_ref, blr_ref,
                    z_ref, c_ref, glog_ref, u_ref, carry_ref):
    tm = x_ref.shape[1]

    @pl.when(pl.program_id(1) == 0)
    def _():
        carry_ref[...] = jnp.zeros_like(carry_ref)

    u_ref[...] = _rms_normed(x_ref[0], g_ref[...]).astype(BF16)
    u = u_ref[...]
    for j in range(Z_WIDTH // COL_TILE):
        cols = slice(j * COL_TILE, (j + 1) * COL_TILE)
        z_ref[0, :, cols] = jnp.dot(u, w_ref[:, cols], preferred_element_type=F32).astype(BF16)

    zs = jnp.dot(u, ws_ref[...], preferred_element_type=F32)
    logf = _log_sigmoid(zs + bf_ref[...])
    row = lax.broadcasted_iota(jnp.int32, (tm, tm), 0)
    col = lax.broadcasted_iota(jnp.int32, (tm, tm), 1)
    tril = jnp.where(row >= col, 1.0, 0.0).astype(BF16)
    hi, lo = _split2(logf)
    lo2 = (logf - hi.astype(F32) - lo.astype(F32)).astype(BF16)
    csum = (jnp.dot(tril, hi, preferred_element_type=F32)
            + jnp.dot(tril, lo, preferred_element_type=F32)
            + jnp.dot(tril, lo2, preferred_element_type=F32)) + carry_ref[...]
    carry_ref[...] = csum[tm - 1:tm, :]
    c_ref[0] = csum[:, :ATT_HEADS]

    glr = jnp.dot(zs.astype(BF16), wlr_ref[...], preferred_element_type=F32)
    glog_ref[0] = _log_sigmoid(glr + blr_ref[...]) * (1.0 / GLA_TAU)


def _rope_tile(x, cos, sin):
    lane = lax.broadcasted_iota(jnp.int32, (x.shape[0], LANES), 1)
    first_half = (lane % HEAD_DIM) < (HEAD_DIM // 2)
    outs = []
    for c in range(x.shape[1] // LANES):
        xc = x[:, c * LANES:(c + 1) * LANES]
        partner = jnp.where(first_half,
                            pltpu.roll(xc, LANES - HEAD_DIM // 2, 1),
                            pltpu.roll(xc, HEAD_DIM // 2, 1))
        outs.append(xc * cos + partner * sin)
    return jnp.concatenate(outs, axis=-1)


def _in_odd_kernel(x_ref, g_ref, w_ref, cos_ref, sin_ref, z_ref, u_ref):
    u_ref[...] = _rms_normed(x_ref[0], g_ref[...]).astype(BF16)
    u = u_ref[...]
    cos = cos_ref[...]
    sin = sin_ref[...]
    for j in range(Z_WIDTH // COL_TILE):
        cols = slice(j * COL_TILE, (j + 1) * COL_TILE)
        r = jnp.dot(u, w_ref[:, cols], preferred_element_type=F32)
        if j in (3, 4, 6):
            r = _rope_tile(r, cos, sin)
        z_ref[0, :, cols] = r.astype(BF16)


def _in_proj_even(h, g, w_main, w_small, b_f, w_lr, b_lr, tm):
    B, S, D = h.shape
    grid = (B, S // tm)
    const = lambda b, s: (0, 0)
    return pl.pallas_call(
        _in_even_kernel,
        grid=grid,
        in_specs=[
            pl.BlockSpec((1, tm, D), lambda b, s: (b, s, 0)),
            pl.BlockSpec((1, D), const),
            pl.BlockSpec((D, Z_WIDTH), const),
            pl.BlockSpec((D, LANES), const),
            pl.BlockSpec((1, LANES), const),
            pl.BlockSpec((LANES, w_lr.shape[1]), const),
            pl.BlockSpec((1, w_lr.shape[1]), const),
        ],
        out_specs=[
            pl.BlockSpec((1, tm, Z_WIDTH), lambda b, s: (b, s, 0)),
            pl.BlockSpec((1, tm, ATT_HEADS), lambda b, s: (b, s, 0)),
            pl.BlockSpec((1, tm, w_lr.shape[1]), lambda b, s: (b, s, 0)),
        ],
        out_shape=[
            jax.ShapeDtypeStruct((B, S, Z_WIDTH), BF16),
            jax.ShapeDtypeStruct((B, S, ATT_HEADS), F32),
            jax.ShapeDtypeStruct((B, S, w_lr.shape[1]), F32),
        ],
        scratch_shapes=[pltpu.VMEM((tm, D), BF16), pltpu.VMEM((1, LANES), F32)],
        compiler_params=_params("arbitrary", "arbitrary"),
        name="in_proj_even",
    )(h, g, w_main, w_small, b_f, w_lr, b_lr)


def _in_proj_odd(h, g, w_main, cos, sin, tm):
    B, S, D = h.shape
    grid = (B, S // tm)
    const = lambda b, s: (0, 0)
    return pl.pallas_call(
        _in_odd_kernel,
        grid=grid,
        in_specs=[
            pl.BlockSpec((1, tm, D), lambda b, s: (b, s, 0)),
            pl.BlockSpec((1, D), const),
            pl.BlockSpec((D, Z_WIDTH), const),
            pl.BlockSpec((tm, LANES), lambda b, s: (s, 0)),
            pl.BlockSpec((tm, LANES), lambda b, s: (s, 0)),
        ],
        out_specs=pl.BlockSpec((1, tm, Z_WIDTH), lambda b, s: (b, s, 0)),
        out_shape=jax.ShapeDtypeStruct((B, S, Z_WIDTH), BF16),
        scratch_shapes=[pltpu.VMEM((tm, D), BF16)],
        compiler_params=_params("arbitrary", "arbitrary"),
        name="in_proj_odd",
    )(h, g, w_main, cos, sin)


def _fox_kernel(q_ref, k_ref, v_ref, c_ref, o_ref, *, blk):
    i = pl.program_id(1)
    row = lax.broadcasted_iota(jnp.int32, (blk, blk), 0)
    col = lax.broadcasted_iota(jnp.int32, (blk, blk), 1)
    causal = col <= row

    for h in range(ATT_HEADS):
        hs = slice(h * HEAD_DIM, (h + 1) * HEAD_DIM)
        q = q_ref[0, :, hs]

        def scores(j):
            ks = pl.multiple_of(j * blk, blk)
            k = k_ref[0, pl.ds(ks, blk), hs]
            s = lax.dot_general(q, k, NT_DIMS, preferred_element_type=F32)
            return s - c_ref[0, h:h + 1, pl.ds(ks, blk)], ks

        def update(s, ks, carry):
            m, l, acc = carry
            m_new = jnp.maximum(m, jnp.max(s, axis=-1, keepdims=True))
            alpha = jnp.exp(m - m_new)
            p = jnp.exp(s - m_new)
            l = alpha * l + jnp.sum(p, axis=-1, keepdims=True)
            v = v_ref[0, pl.ds(ks, blk), hs]
            acc = alpha * acc + jnp.dot(p.astype(BF16), v, preferred_element_type=F32)
            return m_new, l, acc

        def body(j, carry):
            s, ks = scores(j)
            return update(s, ks, carry)

        init = (jnp.full((blk, 1), -jnp.inf, F32), jnp.zeros((blk, 1), F32),
                jnp.zeros((blk, HEAD_DIM), F32))
        carry = lax.fori_loop(0, i, body, init)
        s, ks = scores(i)
        _, l, acc = update(jnp.where(causal, s, -jnp.inf), ks, carry)
        o_ref[0, :, hs] = (acc / l).astype(BF16)


def _fox_attention(z, c_rows, blk):
    B, S, _ = z.shape
    return pl.pallas_call(
        functools.partial(_fox_kernel, blk=blk),
        grid=(B, S // blk),
        in_specs=[
            pl.BlockSpec((1, blk, COL_TILE), lambda b, i: (b, i, 2)),
            pl.BlockSpec((1, S, COL_TILE), lambda b, i: (b, 0, 3)),
            pl.BlockSpec((1, S, COL_TILE), lambda b, i: (b, 0, 4)),
            pl.BlockSpec((1, ATT_HEADS, S), lambda b, i: (b, 0, 0)),
        ],
        out_specs=pl.BlockSpec((1, blk, COL_TILE), lambda b, i: (b, i, 0)),
        out_shape=jax.ShapeDtypeStruct((B, S, COL_TILE), BF16),
        compiler_params=_params("arbitrary", "arbitrary"),
        name="fox_attention",
    )(z, z, z, c_rows)


def _gla_kernel(q_ref, k_ref, v_ref, g_ref, gn_ref, o_ref,
                qt_ref, kt_ref, kd_ref, dec_ref, state_ref):
    tg = q_ref.shape[1]
    C = GLA_CHUNK
    n_chunks = tg // C

    @pl.when(pl.program_id(1) == 0)
    def _():
        state_ref[...] = jnp.zeros_like(state_ref)

    row = lax.broadcasted_iota(jnp.int32, (tg, tg), 0)
    col = lax.broadcasted_iota(jnp.int32, (tg, tg), 1)
    chunk_start = row & (-C)
    tril = jnp.where(col <= row, jnp.where(col >= chunk_start, 1.0, 0.0), 0.0).astype(BF16)
    ghi, glo = _split2(g_ref[0])
    b = (jnp.dot(tril, ghi, preferred_element_type=F32)
         + jnp.dot(tril, glo, preferred_element_type=F32))
    q = q_ref[0].astype(F32)
    k = k_ref[0].astype(F32)
    qt_ref[...] = (q * jnp.exp(b)).astype(BF16)
    kt_ref[...] = (k * jnp.exp(-b)).astype(BF16)
    for c in range(n_chunks):
        rows = slice(c * C, (c + 1) * C)
        b_last = b[(c + 1) * C - 1:(c + 1) * C, :]
        kd_ref[rows, :] = (k[rows, :] * jnp.exp(b_last - b[rows, :])).astype(BF16)
        dec_ref[c:c + 1, :] = jnp.exp(b_last)

    r64 = lax.broadcasted_iota(jnp.int32, (C, C), 0)
    c64 = lax.broadcasted_iota(jnp.int32, (C, C), 1)
    lower = c64 <= r64

    def chunk(c, _):
        r0 = pl.multiple_of(c * C, C)
        dec = dec_ref[pl.ds(c, 1), :]
        for h in range(REC_HEADS):
            ks = slice(h * HEAD_DIM, (h + 1) * HEAD_DIM)
            vs = slice(h * VAL_DIM, (h + 1) * VAL_DIM)
            qt = qt_ref[pl.ds(r0, C), ks]
            kt = kt_ref[pl.ds(r0, C), ks]
            kd = kd_ref[pl.ds(r0, C), ks]
            v = v_ref[0, pl.ds(r0, C), vs]
            att = lax.dot_general(qt, kt, NT_DIMS, preferred_element_type=F32)
            att = jnp.where(lower, att, 0.0).astype(BF16)
            st = state_ref[h]
            o = (jnp.dot(att, v, preferred_element_type=F32)
                 + lax.dot_general(qt, st.astype(BF16), NT_DIMS, preferred_element_type=F32))
            kv_t = lax.dot_general(v, kd, TN_DIMS, preferred_element_type=F32)
            state_ref[h] = st * dec[:, ks] + kv_t
            ms = jnp.mean(o * o, axis=-1, keepdims=True)
            o_ref[0, pl.ds(r0, C), vs] = (o * lax.rsqrt(ms + EPS) * gn_ref[:, vs]).astype(BF16)
        return 0

    lax.fori_loop(0, n_chunks, chunk, 0)


def _gla(z, glog, gla_g, tg):
    B, S, _ = z.shape
    kw = REC_HEADS * HEAD_DIM
    vw = REC_HEADS * VAL_DIM
    return pl.pallas_call(
        _gla_kernel,
        grid=(B, S // tg),
        in_specs=[
            pl.BlockSpec((1, tg, kw), lambda b, s: (b, s, 12)),
            pl.BlockSpec((1, tg, kw), lambda b, s: (b, s, 13)),
            pl.BlockSpec((1, tg, vw), lambda b, s: (b, s, 5)),
            pl.BlockSpec((1, tg, kw), lambda b, s: (b, s, 0)),
            pl.BlockSpec((1, vw), lambda b, s: (0, 0)),
        ],
        out_specs=pl.BlockSpec((1, tg, vw), lambda b, s: (b, s, 0)),
        out_shape=jax.ShapeDtypeStruct((B, S, vw), BF16),
        scratch_shapes=[
            pltpu.VMEM((tg, kw), BF16), pltpu.VMEM((tg, kw), BF16), pltpu.VMEM((tg, kw), BF16),
            pltpu.VMEM((tg // GLA_CHUNK, kw), F32),
            pltpu.VMEM((REC_HEADS, VAL_DIM, HEAD_DIM), F32),
        ],
        compiler_params=_params("arbitrary", "arbitrary"),
        name="gla",
    )(z, z, z, glog, gla_g)


def _ret_kernel(q_ref, k_ref, v_ref, gw_ref, gb_ref, o_ref, state_ref):
    tg = q_ref.shape[1]
    C = RET_CHUNK
    n_chunks = tg // C

    @pl.when(pl.program_id(1) == 0)
    def _():
        state_ref[...] = jnp.zeros_like(state_ref)

    ri = lax.broadcasted_iota(jnp.int32, (C, C), 0)
    ci = lax.broadcasted_iota(jnp.int32, (C, C), 1)
    diff = (ri - ci).astype(F32)
    idx = lax.broadcasted_iota(jnp.int32, (C, 1), 0).astype(F32)

    def chunk(c, _):
        r0 = pl.multiple_of(c * C, C)
        for h in range(REC_HEADS):
            log_g = math.log(1.0 - 2.0 ** (-5.0 - h))
            dmat = jnp.where(diff >= 0, jnp.exp(jnp.maximum(diff, 0.0) * log_g), 0.0)
            xi = jnp.exp((idx + 1.0) * log_g)
            zeta = jnp.exp((C - 1.0 - idx) * log_g)
            ks = slice(h * HEAD_DIM, (h + 1) * HEAD_DIM)
            vs = slice(h * VAL_DIM, (h + 1) * VAL_DIM)
            q = q_ref[0, pl.ds(r0, C), ks]
            k = k_ref[0, pl.ds(r0, C), ks]
            v = v_ref[0, pl.ds(r0, C), vs]
            att = lax.dot_general(q, k, NT_DIMS, preferred_element_type=F32) * dmat
            st = state_ref[h]
            o = (jnp.dot(att.astype(BF16), v, preferred_element_type=F32)
                 + xi * lax.dot_general(q, st.astype(BF16), NT_DIMS, preferred_element_type=F32))
            vz = (v.astype(F32) * zeta).astype(BF16)
            kv_t = lax.dot_general(vz, k, TN_DIMS, preferred_element_type=F32)
            state_ref[h] = st * math.exp(C * log_g) + kv_t
            o = o - jnp.mean(o, axis=-1, keepdims=True)
            ms = jnp.mean(o * o, axis=-1, keepdims=True)
            o = o * lax.rsqrt(ms + EPS) * gw_ref[:, vs] + gb_ref[:, vs]
            o_ref[0, pl.ds(r0, C), vs] = o.astype(BF16)
        return 0

    lax.fori_loop(0, n_chunks, chunk, 0)


def _retention(z, gn_w, gn_b, tg):
    B, S, _ = z.shape
    kw = REC_HEADS * HEAD_DIM
    vw = REC_HEADS * VAL_DIM
    return pl.pallas_call(
        _ret_kernel,
        grid=(B, S // tg),
        in_specs=[
            pl.BlockSpec((1, tg, kw), lambda b, s: (b, s, 12)),
            pl.BlockSpec((1, tg, kw), lambda b, s: (b, s, 13)),
            pl.BlockSpec((1, tg, vw), lambda b, s: (b, s, 2)),
            pl.BlockSpec((1, vw), lambda b, s: (0, 0)),
            pl.BlockSpec((1, vw), lambda b, s: (0, 0)),
        ],
        out_specs=pl.BlockSpec((1, tg, vw), lambda b, s: (b, s, 0)),
        out_shape=jax.ShapeDtypeStruct((B, S, vw), BF16),
        scratch_shapes=[pltpu.VMEM((REC_HEADS, VAL_DIM, HEAD_DIM), F32)],
        compiler_params=_params("arbitrary", "arbitrary"),
        name="retention",
    )(z, z, z, gn_w, gn_b)


def _dil_kernel(q_ref, k_ref, v_ref, o_ref, lse_ref, *, blk, span):
    n = pl.program_id(2)
    start = pl.multiple_of(jnp.maximum(n - 1, 0) * blk, blk)
    qpos = n * blk + lax.broadcasted_iota(jnp.int32, (blk, 2 * blk), 0)
    kpos = start + lax.broadcasted_iota(jnp.int32, (blk, 2 * blk), 1)
    rel = qpos - kpos
    valid = (rel >= 0) & (rel <= span)
    head_lane = lax.broadcasted_iota(jnp.int32, (blk, ATT_HEADS), 1)
    lse_tile = jnp.zeros((blk, ATT_HEADS), F32)
    for h in range(ATT_HEADS):
        hs = slice(h * HEAD_DIM, (h + 1) * HEAD_DIM)
        q = q_ref[0, :, hs]
        k = k_ref[0, pl.ds(start, 2 * blk), hs]
        v = v_ref[0, pl.ds(start, 2 * blk), hs]
        s = lax.dot_general(q, k, NT_DIMS, preferred_element_type=F32)
        s = jnp.where(valid, s, -jnp.inf)
        m = jnp.max(s, axis=-1, keepdims=True)
        p = jnp.exp(s - m)
        den = jnp.sum(p, axis=-1, keepdims=True)
        o = jnp.dot(p.astype(BF16), v, preferred_element_type=F32) / den
        o_ref[0, :, hs] = o.astype(BF16)
        lse_tile = jnp.where(head_lane == h, m + jnp.log(den), lse_tile)
    lse_ref[0, 0] = lse_tile


def _dilated_branch(z, window, dil):
    B, S, _ = z.shape
    blk = DIL_BLOCK
    m = S // dil
    tiles = Z_WIDTH // COL_TILE
    zv = z.reshape(B, m, dil * Z_WIDTH)
    o, lse = pl.pallas_call(
        functools.partial(_dil_kernel, blk=blk, span=window // dil),
        grid=(B, dil, m // blk),
        in_specs=[
            pl.BlockSpec((1, blk, COL_TILE), lambda b, r, n: (b, n, r * tiles + 3)),
            pl.BlockSpec((1, m, COL_TILE), lambda b, r, n: (b, 0, r * tiles + 4)),
            pl.BlockSpec((1, m, COL_TILE), lambda b, r, n: (b, 0, r * tiles + 5)),
        ],
        out_specs=[
            pl.BlockSpec((1, blk, COL_TILE), lambda b, r, n: (b, n, r)),
            pl.BlockSpec((1, 1, blk, ATT_HEADS), lambda b, r, n: (b, r, n, 0)),
        ],
        out_shape=[
            jax.ShapeDtypeStruct((B, m, dil * COL_TILE), BF16),
            jax.ShapeDtypeStruct((B, dil, m, ATT_HEADS), F32),
        ],
        compiler_params=_params("arbitrary", "arbitrary", "arbitrary"),
        name=f"dilated_{dil}",
    )(zv, zv, zv)
    o = o.reshape(B, S, COL_TILE)
    lse = lse.transpose(0, 2, 1, 3).reshape(B, S, ATT_HEADS)
    return o, lse


def _silu(x):
    return x / (1.0 + jnp.exp(-x))


def _out_tail(mix_a, mix_b, gate_ref, h_ref, w_ref, fg_ref, out_ref, final):
    half = mix_a.shape[1]
    gate = _silu(gate_ref[0].astype(F32))
    a = (mix_a * gate[:, :half]).astype(BF16)
    b = (mix_b * gate[:, half:]).astype(BF16)
    y = (jnp.dot(a, w_ref[:half, :], preferred_element_type=F32)
         + jnp.dot(b, w_ref[half:, :], preferred_element_type=F32))
    hn = h_ref[0] + y
    if final:
        hn = _rms_normed(hn, fg_ref[...])
    out_ref[0] = hn


def _out_even_kernel(oa_ref, ob_ref, gate_ref, h_ref, w_ref, fg_ref, out_ref, *, final):
    _out_tail(oa_ref[0].astype(F32), ob_ref[0].astype(F32), gate_ref, h_ref, w_ref, fg_ref,
              out_ref, final)


def _out_odd_kernel(oc_ref, d1_ref, d2_ref, d3_ref, l1_ref, l2_ref, l3_ref, e_ref,
                    gate_ref, h_ref, w_ref, fg_ref, out_ref, *, final):
    l1, l2, l3 = l1_ref[0], l2_ref[0], l3_ref[0]
    mx = jnp.maximum(jnp.maximum(l1, l2), l3)
    e1, e2, e3 = jnp.exp(l1 - mx), jnp.exp(l2 - mx), jnp.exp(l3 - mx)
    inv = 1.0 / (e1 + e2 + e3)
    e = e_ref[...]
    expand = lambda w: jnp.dot(w, e, preferred_element_type=F32, precision=lax.Precision.HIGHEST)
    o_d = (expand(e1 * inv) * d1_ref[0].astype(F32)
           + expand(e2 * inv) * d2_ref[0].astype(F32)
           + expand(e3 * inv) * d3_ref[0].astype(F32))
    _out_tail(oc_ref[0].astype(F32), o_d, gate_ref, h_ref, w_ref, fg_ref, out_ref, final)


def _out_proj(mix_parts, lses, z, h, w_out, final_g, final, tm):
    B, S, D = h.shape
    half = COL_TILE
    tok = lambda b, s: (b, s, 0)
    const = lambda b, s: (0, 0)
    part_spec = pl.BlockSpec((1, tm, half), tok)
    tail_specs = [
        pl.BlockSpec((1, tm, 2 * half), tok),
        pl.BlockSpec((1, tm, D), tok),
        pl.BlockSpec((2 * half, D), const),
        pl.BlockSpec((1, D), const),
    ]
    if lses:
        expand = jnp.repeat(jnp.eye(ATT_HEADS, dtype=F32), HEAD_DIM, axis=1)
        kern = functools.partial(_out_odd_kernel, final=final)
        in_specs = ([part_spec] * 4 + [pl.BlockSpec((1, tm, ATT_HEADS), tok)] * 3
                    + [pl.BlockSpec((ATT_HEADS, half), const)] + tail_specs)
        args = (*mix_parts, *lses, expand, z, h, w_out, final_g)
    else:
        kern = functools.partial(_out_even_kernel, final=final)
        in_specs = [part_spec] * 2 + tail_specs
        args = (*mix_parts, z, h, w_out, final_g)
    return pl.pallas_call(
        kern,
        grid=(B, S // tm),
        in_specs=in_specs,
        out_specs=pl.BlockSpec((1, tm, D), tok),
        out_shape=jax.ShapeDtypeStruct((B, S, D), F32),
        compiler_params=_params("arbitrary", "arbitrary"),
        name="out_proj_odd" if lses else "out_proj_even",
    )(*args)


def _prep_even(w_in, b_f, w_lr, b_lr):
    fw = ATT_HEADS * HEAD_DIM
    kw = REC_HEADS * HEAD_DIM
    vw = REC_HEADS * VAL_DIM
    sizes = (fw, fw, fw, ATT_HEADS, kw, kw, vw, GLA_RANK, fw + vw)
    offs = np.cumsum((0,) + sizes)
    fq, fk, fv, ff, gq, gk, gv, glr, gate = (w_in[:, offs[i]:offs[i + 1]] for i in range(9))
    scale = HEAD_DIM ** -0.5
    w_main = jnp.concatenate([gate, fq * scale, fk, fv, gv, gq * scale, gk], axis=1).astype(BF16)
    pad = LANES - ATT_HEADS - GLA_RANK
    w_small = jnp.pad(jnp.concatenate([ff, glr], axis=1), ((0, 0), (0, pad))).astype(BF16)
    b_f_pad = jnp.pad(b_f, (0, LANES - ATT_HEADS)).reshape(1, LANES)
    w_lr_pad = jnp.pad(w_lr, ((ATT_HEADS, pad), (0, 0))).astype(BF16)
    return w_main, w_small, b_f_pad, w_lr_pad, b_lr.reshape(1, kw)


def _prep_odd(w_in):
    kw = REC_HEADS * HEAD_DIM
    vw = REC_HEADS * VAL_DIM
    dw = ATT_HEADS * HEAD_DIM
    sizes = (kw, kw, vw, dw, dw, dw, vw + dw)
    offs = np.cumsum((0,) + sizes)
    rq, rk, rv, dq, dk, dv, gate = (w_in[:, offs[i]:offs[i + 1]] for i in range(7))
    scale = HEAD_DIM ** -0.5
    return jnp.concatenate([gate, rv, dq * scale, dk, dv, rq * scale, rk], axis=1).astype(BF16)


def _rope_tables(S):
    half = HEAD_DIM // 2
    inv = jnp.power(ROPE_THETA, -jnp.arange(0, HEAD_DIM, 2, dtype=F32) / HEAD_DIM)
    ang = jnp.arange(S, dtype=F32)[:, None] * inv[None, :]
    cos, sin = jnp.cos(ang), jnp.sin(ang)
    reps = LANES // HEAD_DIM
    cos_t = jnp.tile(jnp.concatenate([cos, cos], axis=1), (1, reps))
    sin_t = jnp.tile(jnp.concatenate([-sin, sin], axis=1), (1, reps))
    return cos_t, sin_t


def _tile(S, pref):
    return min(S, pref)


def _even_layer(h, norm_g, w_in, b_f, w_lr, b_lr, gla_g, w_out, final_g, final):
    B, S, D = h.shape
    w_main, w_small, b_f_pad, w_lr_pad, b_lr_row = _prep_even(w_in, b_f, w_lr, b_lr)
    z, c, glog = _in_proj_even(h, norm_g.reshape(1, D), w_main, w_small, b_f_pad, w_lr_pad,
                               b_lr_row, _tile(S, 512))
    o_a = _fox_attention(z, c.transpose(0, 2, 1), _tile(S, 256))
    o_b = _gla(z, glog, gla_g.reshape(1, -1), _tile(S, 512))
    return _out_proj((o_a, o_b), (), z, h, w_out.astype(BF16), final_g.reshape(1, D), final,
                     _tile(S, 512))


def _odd_layer(h, norm_g, w_in, gn_w, gn_b, w_out, final_g, final, cos, sin):
    B, S, D = h.shape
    z = _in_proj_odd(h, norm_g.reshape(1, D), _prep_odd(w_in), cos, sin, _tile(S, 512))
    o_c = _retention(z, gn_w.reshape(1, -1), gn_b.reshape(1, -1), _tile(S, 512))
    outs, lses = zip(*[_dilated_branch(z, window, dil) for window, dil in DIL_PATTERNS])
    return _out_proj((o_c, *outs), lses, z, h, w_out.astype(BF16), final_g.reshape(1, D), final,
                     _tile(S, 512))


def kernel(x, norm_even, w_in_even, b_f_even, w_lr_even, b_lr_even, gla_norm_even, w_out_even,
           norm_odd, w_in_odd, ret_gn_w_odd, ret_gn_b_odd, w_out_odd, final_norm):
    depth = norm_even.shape[0] + norm_odd.shape[0]
    cos, sin = _rope_tables(x.shape[1])
    h = x
    for i in range(depth):
        j = i // 2
        final = i == depth - 1
        if i % 2 == 0:
            h = _even_layer(h, norm_even[j], w_in_even[j], b_f_even[j], w_lr_even[j],
                            b_lr_even[j], gla_norm_even[j], w_out_even[j], final_norm, final)
        else:
            h = _odd_layer(h, norm_odd[j], w_in_odd[j], ret_gn_w_odd[j], ret_gn_b_odd[j],
                           w_out_odd[j], final_norm, final, cos, sin)
    return h
```

```python
import functools
import math

import numpy as np
import jax
import jax.numpy as jnp
from jax import lax
from jax.experimental import pallas as pl
from jax.experimental.pallas import tpu as pltpu

F32 = jnp.float32
BF16 = jnp.bfloat16

EPS = 1e-6
ROPE_THETA = 10000.0
HEAD_DIM = 64
VAL_DIM = 128
ATT_HEADS = 8
REC_HEADS = 4
GLA_RANK = 16
GLA_TAU = 16.0
GLA_CHUNK = 64
RET_CHUNK = 128
DIL_BLOCK = 128
DIL_PATTERNS = ((128, 1), (512, 4), (2048, 16))

Z_WIDTH = 3584
ZE_WIDTH = 3072
FOX_PAD = 128
COL_TILE = 512
LANES = 128
V7X_VMEM_LIMIT = 56 * 1024 * 1024

NT_DIMS = (((1,), (1,)), ((), ()))
TN_DIMS = (((0,), (0,)), ((), ()))


def _params(*sem):
    return pltpu.CompilerParams(dimension_semantics=sem, vmem_limit_bytes=V7X_VMEM_LIMIT)


def _split2(x):
    hi = x.astype(BF16)
    lo = (x - hi.astype(F32)).astype(BF16)
    return hi, lo


def _log_sigmoid(x):
    return jnp.minimum(x, 0.0) - jnp.log(1.0 + jnp.exp(-jnp.abs(x)))


def _rms_normed(x, g):
    ms = jnp.mean(x * x, axis=-1, keepdims=True)
    return x * lax.rsqrt(ms + EPS) * g


def _split3(x):
    hi, lo = _split2(x)
    lo2 = (x - hi.astype(F32) - lo.astype(F32)).astype(BF16)
    return hi, lo, lo2


def _in_even_kernel(x_ref, g_ref, w_ref, wq_ref, wv_ref, ws_ref, bf_ref, wlr_ref, blr_ref, place_ref,
                    z_ref, qt_ref, vt_ref, glog_ref, u_ref, carry_ref):
    tm = x_ref.shape[1]

    @pl.when(pl.program_id(1) == 0)
    def _():
        carry_ref[...] = jnp.zeros_like(carry_ref)

    u_ref[...] = _rms_normed(x_ref[0], g_ref[...]).astype(BF16)
    u = u_ref[...]

    zs = jnp.dot(u, ws_ref[...], preferred_element_type=F32)
    logf = _log_sigmoid(zs + bf_ref[...])
    row = lax.broadcasted_iota(jnp.int32, (tm, tm), 0)
    col = lax.broadcasted_iota(jnp.int32, (tm, tm), 1)
    tril = jnp.where(row >= col, 1.0, 0.0).astype(BF16)
    csum = sum(jnp.dot(tril, t, preferred_element_type=F32) for t in _split3(logf)) + carry_ref[...]
    carry_ref[...] = csum[tm - 1:tm, :]
    c_cols = sum(jnp.dot(t, place_ref[i], preferred_element_type=F32)
                 for i, t in enumerate(_split3(csum)))

    glr = jnp.dot(zs.astype(BF16), wlr_ref[...], preferred_element_type=F32)
    glog_ref[0] = _log_sigmoid(glr + blr_ref[...]) * (1.0 / GLA_TAU)

    for j in range(ZE_WIDTH // COL_TILE):
        cols = slice(j * COL_TILE, (j + 1) * COL_TILE)
        r = jnp.dot(u, w_ref[:, cols], preferred_element_type=F32)
        if j in (2, 3):
            r = r + c_cols[:, (j - 2) * COL_TILE:(j - 1) * COL_TILE]
        z_ref[0, :, cols] = r.astype(BF16)

    ridx = lax.broadcasted_iota(jnp.int32, (COL_TILE, tm), 0) & (FOX_PAD - 1)
    for j in range(ATT_HEADS * FOX_PAD // COL_TILE):
        rows = slice(j * COL_TILE, (j + 1) * COL_TILE)
        r = lax.dot_general(wq_ref[rows, :], u, NT_DIMS, preferred_element_type=F32)
        r = jnp.where(ridx >= HEAD_DIM, jnp.where(ridx < HEAD_DIM + 3, -1.0, r), r)
        qt_ref[0, rows, :] = r.astype(BF16)
        r = lax.dot_general(wv_ref[rows, :], u, NT_DIMS, preferred_element_type=F32)
        vt_ref[0, rows, :] = jnp.where(ridx == HEAD_DIM, 1.0, r).astype(BF16)


def _rope_tile(x, cos, sin):
    lane = lax.broadcasted_iota(jnp.int32, (x.shape[0], LANES), 1)
    first_half = (lane % HEAD_DIM) < (HEAD_DIM // 2)
    outs = []
    for c in range(x.shape[1] // LANES):
        xc = x[:, c * LANES:(c + 1) * LANES]
        partner = jnp.where(first_half,
                            pltpu.roll(xc, LANES - HEAD_DIM // 2, 1),
                            pltpu.roll(xc, HEAD_DIM // 2, 1))
        outs.append(xc * cos + partner * sin)
    return jnp.concatenate(outs, axis=-1)


def _in_odd_kernel(x_ref, g_ref, w_ref, cos_ref, sin_ref, z_ref, u_ref):
    u_ref[...] = _rms_normed(x_ref[0], g_ref[...]).astype(BF16)
    u = u_ref[...]
    cos = cos_ref[...]
    sin = sin_ref[...]
    for j in range(Z_WIDTH // COL_TILE):
        cols = slice(j * COL_TILE, (j + 1) * COL_TILE)
        r = jnp.dot(u, w_ref[:, cols], preferred_element_type=F32)
        if j in (3, 4, 6):
            r = _rope_tile(r, cos, sin)
        z_ref[0, :, cols] = r.astype(BF16)


def _in_proj_even(h, g, w_main, wq_t, wv_t, w_small, b_f, w_lr, b_lr, place, tm):
    B, S, D = h.shape
    grid = (B, S // tm)
    const = lambda b, s: (0, 0)
    fox_rows = ATT_HEADS * FOX_PAD
    kw = w_lr.shape[1]
    return pl.pallas_call(
        _in_even_kernel,
        grid=grid,
        in_specs=[
            pl.BlockSpec((1, tm, D), lambda b, s: (b, s, 0)),
            pl.BlockSpec((1, D), const),
            pl.BlockSpec((D, ZE_WIDTH), const),
            pl.BlockSpec((fox_rows, D), const),
            pl.BlockSpec((fox_rows, D), const),
            pl.BlockSpec((D, LANES), const),
            pl.BlockSpec((1, LANES), const),
            pl.BlockSpec((LANES, kw), const),
            pl.BlockSpec((1, kw), const),
            pl.BlockSpec((3, LANES, fox_rows), lambda b, s: (0, 0, 0)),
        ],
        out_specs=[
            pl.BlockSpec((1, tm, ZE_WIDTH), lambda b, s: (b, s, 0)),
            pl.BlockSpec((1, fox_rows, tm), lambda b, s: (b, 0, s)),
            pl.BlockSpec((1, fox_rows, tm), lambda b, s: (b, 0, s)),
            pl.BlockSpec((1, tm, kw), lambda b, s: (b, s, 0)),
        ],
        out_shape=[
            jax.ShapeDtypeStruct((B, S, ZE_WIDTH), BF16),
            jax.ShapeDtypeStruct((B, fox_rows, S), BF16),
            jax.ShapeDtypeStruct((B, fox_rows, S), BF16),
            jax.ShapeDtypeStruct((B, S, kw), F32),
        ],
        scratch_shapes=[pltpu.VMEM((tm, D), BF16), pltpu.VMEM((1, LANES), F32)],
        compiler_params=_params("arbitrary", "arbitrary"),
        name="in_proj_even",
    )(h, g, w_main, wq_t, wv_t, w_small, b_f, w_lr, b_lr, place)


def _in_proj_odd(h, g, w_main, cos, sin, tm):
    B, S, D = h.shape
    grid = (B, S // tm)
    const = lambda b, s: (0, 0)
    return pl.pallas_call(
        _in_odd_kernel,
        grid=grid,
        in_specs=[
            pl.BlockSpec((1, tm, D), lambda b, s: (b, s, 0)),
            pl.BlockSpec((1, D), const),
            pl.BlockSpec((D, Z_WIDTH), const),
            pl.BlockSpec((tm, LANES), lambda b, s: (s, 0)),
            pl.BlockSpec((tm, LANES), lambda b, s: (s, 0)),
        ],
        out_specs=pl.BlockSpec((1, tm, Z_WIDTH), lambda b, s: (b, s, 0)),
        out_shape=jax.ShapeDtypeStruct((B, S, Z_WIDTH), BF16),
        scratch_shapes=[pltpu.VMEM((tm, D), BF16)],
        compiler_params=_params("arbitrary", "arbitrary"),
        name="in_proj_odd",
    )(h, g, w_main, cos, sin)


def _fox_kernel(qt_ref, kp_ref, vt_ref, o_ref, s_ref, acc_ref, mp_ref, mc_ref, ot_ref, *, blk):
    i = pl.program_id(1)
    key = lax.broadcasted_iota(jnp.int32, (blk, blk), 0)
    qry = lax.broadcasted_iota(jnp.int32, (blk, blk), 1)
    heads = range(ATT_HEADS)
    rows = lambda h: slice(h * FOX_PAD, (h + 1) * FOX_PAD)

    def score_stage(h, j, mask):
        ks = pl.multiple_of(j * blk, blk)
        s = jnp.dot(kp_ref[0, pl.ds(ks, blk), rows(h)], qt_ref[0, rows(h), :],
                    preferred_element_type=F32)
        if mask is not None:
            s = jnp.where(mask, s, -jnp.inf)
        s_ref[h] = s
        m_old = mc_ref[h]
        mp_ref[h] = m_old
        mc_ref[h] = jnp.maximum(m_old, jnp.max(s, axis=0, keepdims=True))

    def value_stage(h, j):
        ks = pl.multiple_of(j * blk, blk)
        m_new = mc_ref[h]
        alpha = jnp.exp(mp_ref[h] - m_new)
        p = jnp.exp(s_ref[h] - m_new).astype(BF16)
        pv = jnp.dot(vt_ref[0, rows(h), pl.ds(ks, blk)], p, preferred_element_type=F32)
        acc_ref[h] = alpha * acc_ref[h] + pv

    first_mask = key <= qry + jnp.where(i > 0, blk, 0)
    for h in heads:
        acc_ref[h] = jnp.zeros(acc_ref.shape[1:], F32)
        mc_ref[h] = jnp.full((1, blk), -jnp.inf, F32)
        score_stage(h, 0, first_mask)

    def trip(j, _):
        for h in heads:
            value_stage(h, j - 1)
            score_stage(h, j, None)
        return 0

    lax.fori_loop(1, i, trip, 0)

    @pl.when(i > 0)
    def _():
        for h in heads:
            value_stage(h, i - 1)
            score_stage(h, i, key <= qry)

    for h in heads:
        value_stage(h, i)
        acc = acc_ref[h]
        ot_ref[h * HEAD_DIM:(h + 1) * HEAD_DIM, :] = (
            acc[:HEAD_DIM, :] / acc[HEAD_DIM:HEAD_DIM + 1, :])
    o_ref[0] = ot_ref[...].T.astype(BF16)


def _fox_attention(z, q_t, v_t, blk):
    B, S, _ = z.shape
    fox_rows = ATT_HEADS * FOX_PAD
    return pl.pallas_call(
        functools.partial(_fox_kernel, blk=blk),
        grid=(B, S // blk),
        in_specs=[
            pl.BlockSpec((1, fox_rows, blk), lambda b, i: (b, 0, i)),
            pl.BlockSpec((1, S, fox_rows), lambda b, i: (b, 0, 1)),
            pl.BlockSpec((1, fox_rows, S), lambda b, i: (b, 0, 0)),
        ],
        out_specs=pl.BlockSpec((1, blk, COL_TILE), lambda b, i: (b, i, 0)),
        out_shape=jax.ShapeDtypeStruct((B, S, COL_TILE), BF16),
        scratch_shapes=[pltpu.VMEM((ATT_HEADS, blk, blk), F32),
                        pltpu.VMEM((ATT_HEADS, FOX_PAD, blk), F32),
                        pltpu.VMEM((ATT_HEADS, 1, blk), F32),
                        pltpu.VMEM((ATT_HEADS, 1, blk), F32),
                        pltpu.VMEM((ATT_HEADS * HEAD_DIM, blk), F32)],
        compiler_params=_params("arbitrary", "arbitrary"),
        name="fox_attention",
    )(q_t, z, v_t)


def _gla_kernel(q_ref, k_ref, v_ref, g_ref, gn_ref, o_ref,
                qt_ref, kt_ref, kd_ref, dec_ref, state_ref):
    tg = q_ref.shape[1]
    C = GLA_CHUNK
    n_chunks = tg // C

    @pl.when(pl.program_id(1) == 0)
    def _():
        state_ref[...] = jnp.zeros_like(state_ref)

    row = lax.broadcasted_iota(jnp.int32, (tg, tg), 0)
    col = lax.broadcasted_iota(jnp.int32, (tg, tg), 1)
    chunk_start = row & (-C)
    tril = jnp.where(col <= row, jnp.where(col >= chunk_start, 1.0, 0.0), 0.0).astype(BF16)
    ghi, glo = _split2(g_ref[0])
    b = (jnp.dot(tril, ghi, preferred_element_type=F32)
         + jnp.dot(tril, glo, preferred_element_type=F32))
    q = q_ref[0].astype(F32)
    k = k_ref[0].astype(F32)
    qt_ref[...] = (q * jnp.exp(b)).astype(BF16)
    kt_ref[...] = (k * jnp.exp(-b)).astype(BF16)
    for c in range(n_chunks):
        rows = slice(c * C, (c + 1) * C)
        b_last = b[(c + 1) * C - 1:(c + 1) * C, :]
        kd_ref[rows, :] = (k[rows, :] * jnp.exp(b_last - b[rows, :])).astype(BF16)
        dec_ref[c:c + 1, :] = jnp.exp(b_last)

    r64 = lax.broadcasted_iota(jnp.int32, (C, C), 0)
    c64 = lax.broadcasted_iota(jnp.int32, (C, C), 1)
    lower = c64 <= r64

    def chunk(c, _):
        r0 = pl.multiple_of(c * C, C)
        dec = dec_ref[pl.ds(c, 1), :]
        for h in range(REC_HEADS):
            ks = slice(h * HEAD_DIM, (h + 1) * HEAD_DIM)
            vs = slice(h * VAL_DIM, (h + 1) * VAL_DIM)
            qt = qt_ref[pl.ds(r0, C), ks]
            kt = kt_ref[pl.ds(r0, C), ks]
            kd = kd_ref[pl.ds(r0, C), ks]
            v = v_ref[0, pl.ds(r0, C), vs]
            att = lax.dot_general(qt, kt, NT_DIMS, preferred_element_type=F32)
            att = jnp.where(lower, att, 0.0).astype(BF16)
            st = state_ref[h]
            o = (jnp.dot(att, v, preferred_element_type=F32)
                 + lax.dot_general(qt, st.astype(BF16), NT_DIMS, preferred_element_type=F32))
            kv_t = lax.dot_general(v, kd, TN_DIMS, preferred_element_type=F32)
            state_ref[h] = st * dec[:, ks] + kv_t
            ms = jnp.mean(o * o, axis=-1, keepdims=True)
            o_ref[0, pl.ds(r0, C), vs] = (o * lax.rsqrt(ms + EPS) * gn_ref[:, vs]).astype(BF16)
        return 0

    lax.fori_loop(0, n_chunks, chunk, 0)


def _gla(z, glog, gla_g, tg):
    B, S, _ = z.shape
    kw = REC_HEADS * HEAD_DIM
    vw = REC_HEADS * VAL_DIM
    return pl.pallas_call(
        _gla_kernel,
        grid=(B, S // tg),
        in_specs=[
            pl.BlockSpec((1, tg, kw), lambda b, s: (b, s, 10)),
            pl.BlockSpec((1, tg, kw), lambda b, s: (b, s, 11)),
            pl.BlockSpec((1, tg, vw), lambda b, s: (b, s, 4)),
            pl.BlockSpec((1, tg, kw), lambda b, s: (b, s, 0)),
            pl.BlockSpec((1, vw), lambda b, s: (0, 0)),
        ],
        out_specs=pl.BlockSpec((1, tg, vw), lambda b, s: (b, s, 0)),
        out_shape=jax.ShapeDtypeStruct((B, S, vw), BF16),
        scratch_shapes=[
            pltpu.VMEM((tg, kw), BF16), pltpu.VMEM((tg, kw), BF16), pltpu.VMEM((tg, kw), BF16),
            pltpu.VMEM((tg // GLA_CHUNK, kw), F32),
            pltpu.VMEM((REC_HEADS, VAL_DIM, HEAD_DIM), F32),
        ],
        compiler_params=_params("arbitrary", "arbitrary"),
        name="gla",
    )(z, z, z, glog, gla_g)


def _ret_kernel(q_ref, k_ref, v_ref, gw_ref, gb_ref, o_ref, state_ref):
    tg = q_ref.shape[1]
    C = RET_CHUNK
    n_chunks = tg // C

    @pl.when(pl.program_id(1) == 0)
    def _():
        state_ref[...] = jnp.zeros_like(state_ref)

    ri = lax.broadcasted_iota(jnp.int32, (C, C), 0)
    ci = lax.broadcasted_iota(jnp.int32, (C, C), 1)
    diff = (ri - ci).astype(F32)
    idx = lax.broadcasted_iota(jnp.int32, (C, 1), 0).astype(F32)

    def chunk(c, _):
        r0 = pl.multiple_of(c * C, C)
        for h in range(REC_HEADS):
            log_g = math.log(1.0 - 2.0 ** (-5.0 - h))
            dmat = jnp.where(diff >= 0, jnp.exp(jnp.maximum(diff, 0.0) * log_g), 0.0)
            xi = jnp.exp((idx + 1.0) * log_g)
            zeta = jnp.exp((C - 1.0 - idx) * log_g)
            ks = slice(h * HEAD_DIM, (h + 1) * HEAD_DIM)
            vs = slice(h * VAL_DIM, (h + 1) * VAL_DIM)
            q = q_ref[0, pl.ds(r0, C), ks]
            k = k_ref[0, pl.ds(r0, C), ks]
            v = v_ref[0, pl.ds(r0, C), vs]
            att = lax.dot_general(q, k, NT_DIMS, preferred_element_type=F32) * dmat
            st = state_ref[h]
            o = (jnp.dot(att.astype(BF16), v, preferred_element_type=F32)
                 + xi * lax.dot_general(q, st.astype(BF16), NT_DIMS, preferred_element_type=F32))
            vz = (v.astype(F32) * zeta).astype(BF16)
            kv_t = lax.dot_general(vz, k, TN_DIMS, preferred_element_type=F32)
            state_ref[h] = st * math.exp(C * log_g) + kv_t
            o = o - jnp.mean(o, axis=-1, keepdims=True)
            ms = jnp.mean(o * o, axis=-1, keepdims=True)
            o = o * lax.rsqrt(ms + EPS) * gw_ref[:, vs] + gb_ref[:, vs]
            o_ref[0, pl.ds(r0, C), vs] = o.astype(BF16)
        return 0

    lax.fori_loop(0, n_chunks, chunk, 0)


def _retention(z, gn_w, gn_b, tg):
    B, S, _ = z.shape
    kw = REC_HEADS * HEAD_DIM
    vw = REC_HEADS * VAL_DIM
    return pl.pallas_call(
        _ret_kernel,
        grid=(B, S // tg),
        in_specs=[
            pl.BlockSpec((1, tg, kw), lambda b, s: (b, s, 12)),
            pl.BlockSpec((1, tg, kw), lambda b, s: (b, s, 13)),
            pl.BlockSpec((1, tg, vw), lambda b, s: (b, s, 2)),
            pl.BlockSpec((1, vw), lambda b, s: (0, 0)),
            pl.BlockSpec((1, vw), lambda b, s: (0, 0)),
        ],
        out_specs=pl.BlockSpec((1, tg, vw), lambda b, s: (b, s, 0)),
        out_shape=jax.ShapeDtypeStruct((B, S, vw), BF16),
        scratch_shapes=[pltpu.VMEM((REC_HEADS, VAL_DIM, HEAD_DIM), F32)],
        compiler_params=_params("arbitrary", "arbitrary"),
        name="retention",
    )(z, z, z, gn_w, gn_b)


def _dil_kernel(q_ref, k_ref, v_ref, o_ref, lse_ref, *, blk, span):
    n = pl.program_id(2)
    start = pl.multiple_of(jnp.maximum(n - 1, 0) * blk, blk)
    qpos = n * blk + lax.broadcasted_iota(jnp.int32, (blk, 2 * blk), 0)
    kpos = start + lax.broadcasted_iota(jnp.int32, (blk, 2 * blk), 1)
    rel = qpos - kpos
    valid = (rel >= 0) & (rel <= span)
    head_lane = lax.broadcasted_iota(jnp.int32, (blk, ATT_HEADS), 1)
    lse_tile = jnp.zeros((blk, ATT_HEADS), F32)
    for h in range(ATT_HEADS):
        hs = slice(h * HEAD_DIM, (h + 1) * HEAD_DIM)
        q = q_ref[0, :, hs]
        k = k_ref[0, pl.ds(start, 2 * blk), hs]
        v = v_ref[0, pl.ds(start, 2 * blk), hs]
        s = lax.dot_general(q, k, NT_DIMS, preferred_element_type=F32)
        s = jnp.where(valid, s, -jnp.inf)
        m = jnp.max(s, axis=-1, keepdims=True)
        p = jnp.exp(s - m)
        den = jnp.sum(p, axis=-1, keepdims=True)
        o = jnp.dot(p.astype(BF16), v, preferred_element_type=F32) / den
        o_ref[0, :, hs] = o.astype(BF16)
        lse_tile = jnp.where(head_lane == h, m + jnp.log(den), lse_tile)
    lse_ref[0, 0] = lse_tile


def _dilated_branch(z, window, dil):
    B, S, _ = z.shape
    blk = DIL_BLOCK
    m = S // dil
    tiles = Z_WIDTH // COL_TILE
    zv = z.reshape(B, m, dil * Z_WIDTH)
    o, lse = pl.pallas_call(
        functools.partial(_dil_kernel, blk=blk, span=window // dil),
        grid=(B, dil, m // blk),
        in_specs=[
            pl.BlockSpec((1, blk, COL_TILE), lambda b, r, n: (b, n, r * tiles + 3)),
            pl.BlockSpec((1, m, COL_TILE), lambda b, r, n: (b, 0, r * tiles + 4)),
            pl.BlockSpec((1, m, COL_TILE), lambda b, r, n: (b, 0, r * tiles + 5)),
        ],
        out_specs=[
            pl.BlockSpec((1, blk, COL_TILE), lambda b, r, n: (b, n, r)),
            pl.BlockSpec((1, 1, blk, ATT_HEADS), lambda b, r, n: (b, r, n, 0)),
        ],
        out_shape=[
            jax.ShapeDtypeStruct((B, m, dil * COL_TILE), BF16),
            jax.ShapeDtypeStruct((B, dil, m, ATT_HEADS), F32),
        ],
        compiler_params=_params("arbitrary", "arbitrary", "arbitrary"),
        name=f"dilated_{dil}",
    )(zv, zv, zv)
    o = o.reshape(B, S, COL_TILE)
    lse = lse.transpose(0, 2, 1, 3).reshape(B, S, ATT_HEADS)
    return o, lse


def _silu(x):
    return x / (1.0 + jnp.exp(-x))


def _out_tail(mix_a, mix_b, gate_ref, h_ref, w_ref, fg_ref, out_ref, final):
    half = mix_a.shape[1]
    gate = _silu(gate_ref[0].astype(F32))
    a = (mix_a * gate[:, :half]).astype(BF16)
    b = (mix_b * gate[:, half:]).astype(BF16)
    y = (jnp.dot(a, w_ref[:half, :], preferred_element_type=F32)
         + jnp.dot(b, w_ref[half:, :], preferred_element_type=F32))
    hn = h_ref[0] + y
    if final:
        hn = _rms_normed(hn, fg_ref[...])
    out_ref[0] = hn


def _out_even_kernel(oa_ref, ob_ref, gate_ref, h_ref, w_ref, fg_ref, out_ref, *, final):
    _out_tail(oa_ref[0].astype(F32), ob_ref[0].astype(F32), gate_ref, h_ref, w_ref, fg_ref,
              out_ref, final)


def _out_odd_kernel(oc_ref, d1_ref, d2_ref, d3_ref, l1_ref, l2_ref, l3_ref, e_ref,
                    gate_ref, h_ref, w_ref, fg_ref, out_ref, *, final):
    l1, l2, l3 = l1_ref[0], l2_ref[0], l3_ref[0]
    mx = jnp.maximum(jnp.maximum(l1, l2), l3)
    e1, e2, e3 = jnp.exp(l1 - mx), jnp.exp(l2 - mx), jnp.exp(l3 - mx)
    inv = 1.0 / (e1 + e2 + e3)
    e = e_ref[...]
    expand = lambda w: jnp.dot(w, e, preferred_element_type=F32, precision=lax.Precision.HIGHEST)
    o_d = (expand(e1 * inv) * d1_ref[0].astype(F32)
           + expand(e2 * inv) * d2_ref[0].astype(F32)
           + expand(e3 * inv) * d3_ref[0].astype(F32))
    _out_tail(oc_ref[0].astype(F32), o_d, gate_ref, h_ref, w_ref, fg_ref, out_ref, final)


def _out_proj(mix_parts, lses, z, h, w_out, final_g, final, tm):
    B, S, D = h.shape
    half = COL_TILE
    tok = lambda b, s: (b, s, 0)
    const = lambda b, s: (0, 0)
    part_spec = pl.BlockSpec((1, tm, half), tok)
    tail_specs = [
        pl.BlockSpec((1, tm, 2 * half), tok),
        pl.BlockSpec((1, tm, D), tok),
        pl.BlockSpec((2 * half, D), const),
        pl.BlockSpec((1, D), const),
    ]
    if lses:
        expand = jnp.repeat(jnp.eye(ATT_HEADS, dtype=F32), HEAD_DIM, axis=1)
        kern = functools.partial(_out_odd_kernel, final=final)
        in_specs = ([part_spec] * 4 + [pl.BlockSpec((1, tm, ATT_HEADS), tok)] * 3
                    + [pl.BlockSpec((ATT_HEADS, half), const)] + tail_specs)
        args = (*mix_parts, *lses, expand, z, h, w_out, final_g)
    else:
        kern = functools.partial(_out_even_kernel, final=final)
        in_specs = [part_spec] * 2 + tail_specs
        args = (*mix_parts, z, h, w_out, final_g)
    return pl.pallas_call(
        kern,
        grid=(B, S // tm),
        in_specs=in_specs,
        out_specs=pl.BlockSpec((1, tm, D), tok),
        out_shape=jax.ShapeDtypeStruct((B, S, D), F32),
        compiler_params=_params("arbitrary", "arbitrary"),
        name="out_proj_odd" if lses else "out_proj_even",
    )(*args)


def _prep_even(w_in, b_f, w_lr, b_lr):
    fw = ATT_HEADS * HEAD_DIM
    kw = REC_HEADS * HEAD_DIM
    vw = REC_HEADS * VAL_DIM
    sizes = (fw, fw, fw, ATT_HEADS, kw, kw, vw, GLA_RANK, fw + vw)
    offs = np.cumsum((0,) + sizes)
    fq, fk, fv, ff, gq, gk, gv, glr, gate = (w_in[:, offs[i]:offs[i + 1]] for i in range(9))
    scale = HEAD_DIM ** -0.5
    D = w_in.shape[0]

    def pad_heads(w):
        w = w.reshape(D, ATT_HEADS, HEAD_DIM)
        return jnp.pad(w, ((0, 0), (0, 0), (0, FOX_PAD - HEAD_DIM))).reshape(D, ATT_HEADS * FOX_PAD)

    w_main = jnp.concatenate([gate, pad_heads(fk), gv, gq * scale, gk], axis=1).astype(BF16)
    wq_t = pad_heads(fq * scale).T.astype(BF16)
    wv_t = pad_heads(fv).T.astype(BF16)
    pad = LANES - ATT_HEADS - GLA_RANK
    w_small = jnp.pad(jnp.concatenate([ff, glr], axis=1), ((0, 0), (0, pad))).astype(BF16)
    b_f_pad = jnp.pad(b_f, (0, LANES - ATT_HEADS)).reshape(1, LANES)
    w_lr_pad = jnp.pad(w_lr, ((ATT_HEADS, pad), (0, 0))).astype(BF16)
    place = np.zeros((3, LANES, ATT_HEADS * FOX_PAD), np.float32)
    for i in range(3):
        for h in range(ATT_HEADS):
            place[i, h, h * FOX_PAD + HEAD_DIM + i] = 1.0
    return (w_main, wq_t, wv_t, w_small, b_f_pad, w_lr_pad, b_lr.reshape(1, kw),
            jnp.asarray(place, BF16))


def _prep_odd(w_in):
    kw = REC_HEADS * HEAD_DIM
    vw = REC_HEADS * VAL_DIM
    dw = ATT_HEADS * HEAD_DIM
    sizes = (kw, kw, vw, dw, dw, dw, vw + dw)
    offs = np.cumsum((0,) + sizes)
    rq, rk, rv, dq, dk, dv, gate = (w_in[:, offs[i]:offs[i + 1]] for i in range(7))
    scale = HEAD_DIM ** -0.5
    return jnp.concatenate([gate, rv, dq * scale, dk, dv, rq * scale, rk], axis=1).astype(BF16)


def _rope_tables(S):
    half = HEAD_DIM // 2
    inv = jnp.power(ROPE_THETA, -jnp.arange(0, HEAD_DIM, 2, dtype=F32) / HEAD_DIM)
    ang = jnp.arange(S, dtype=F32)[:, None] * inv[None, :]
    cos, sin = jnp.cos(ang), jnp.sin(ang)
    reps = LANES // HEAD_DIM
    cos_t = jnp.tile(jnp.concatenate([cos, cos], axis=1), (1, reps))
    sin_t = jnp.tile(jnp.concatenate([-sin, sin], axis=1), (1, reps))
    return cos_t, sin_t


def _tile(S, pref):
    return min(S, pref)


def _even_layer(h, norm_g, w_in, b_f, w_lr, b_lr, gla_g, w_out, final_g, final):
    B, S, D = h.shape
    z, q_t, v_t, glog = _in_proj_even(h, norm_g.reshape(1, D), *_prep_even(w_in, b_f, w_lr, b_lr),
                                      _tile(S, 512))
    o_a = _fox_attention(z, q_t, v_t, _tile(S, 256))
    o_b = _gla(z, glog, gla_g.reshape(1, -1), _tile(S, 512))
    return _out_proj((o_a, o_b), (), z, h, w_out.astype(BF16), final_g.reshape(1, D), final,
                     _tile(S, 512))


def _odd_layer(h, norm_g, w_in, gn_w, gn_b, w_out, final_g, final, cos, sin):
    B, S, D = h.shape
    z = _in_proj_odd(h, norm_g.reshape(1, D), _prep_odd(w_in), cos, sin, _tile(S, 512))
    o_c = _retention(z, gn_w.reshape(1, -1), gn_b.reshape(1, -1), _tile(S, 512))
    outs, lses = zip(*[_dilated_branch(z, window, dil) for window, dil in DIL_PATTERNS])
    return _out_proj((o_c, *outs), lses, z, h, w_out.astype(BF16), final_g.reshape(1, D), final,
                     _tile(S, 512))


def kernel(x, norm_even, w_in_even, b_f_even, w_lr_even, b_lr_even, gla_norm_even, w_out_even,
           norm_odd, w_in_odd, ret_gn_w_odd, ret_gn_b_odd, w_out_odd, final_norm):
    depth = norm_even.shape[0] + norm_odd.shape[0]
    cos, sin = _rope_tables(x.shape[1])
    h = x
    for i in range(depth):
        j = i // 2
        final = i == depth - 1
        if i % 2 == 0:
            h = _even_layer(h, norm_even[j], w_in_even[j], b_f_even[j], w_lr_even[j],
                            b_lr_even[j], gla_norm_even[j], w_out_even[j], final_norm, final)
        else:
            h = _odd_layer(h, norm_odd[j], w_in_odd[j], ret_gn_w_odd[j], ret_gn_b_odd[j],
                           w_out_odd[j], final_norm, final, cos, sin)
    return h
```

```python
import functools
import math

import numpy as np
import jax
import jax.numpy as jnp
from jax import lax
from jax.experimental import pallas as pl
from jax.experimental.pallas import tpu as pltpu

F32 = jnp.float32
BF16 = jnp.bfloat16

EPS = 1e-6
ROPE_THETA = 10000.0
HEAD_DIM = 64
VAL_DIM = 128
ATT_HEADS = 8
REC_HEADS = 4
GLA_RANK = 16
GLA_TAU = 16.0
GLA_CHUNK = 64
RET_CHUNK = 128
DIL_BLOCK = 128
DIL_PATTERNS = ((128, 1), (512, 4), (2048, 16))

Z_WIDTH = 3584
DIL_TILE0 = 4
DIL_QBLOCKS = 2
ZE_WIDTH = 3072
FOX_PAD = 128
FOX_VROWS = 80
LOG2E = 1.4426950408889634
COL_TILE = 512
LANES = 128
V7X_VMEM_LIMIT = 56 * 1024 * 1024

NT_DIMS = (((1,), (1,)), ((), ()))
TN_DIMS = (((0,), (0,)), ((), ()))


def _params(*sem):
    return pltpu.CompilerParams(dimension_semantics=sem, vmem_limit_bytes=V7X_VMEM_LIMIT)


def _split2(x):
    hi = x.astype(BF16)
    lo = (x - hi.astype(F32)).astype(BF16)
    return hi, lo


def _split3(x):
    hi, lo = _split2(x)
    lo2 = (x - hi.astype(F32) - lo.astype(F32)).astype(BF16)
    return hi, lo, lo2


def _log_sigmoid(x):
    return jnp.minimum(x, 0.0) - jnp.log(1.0 + jnp.exp(-jnp.abs(x)))


def _rms_normed(x, g):
    ms = jnp.mean(x * x, axis=-1, keepdims=True)
    return x * lax.rsqrt(ms + EPS) * g


def _in_even_kernel(x_ref, g_ref, w_ref, wq_ref, wv_ref, ws_ref, bf_ref, wlr_ref, blr_ref, place_ref,
                    qaug_ref, vaug_ref, z_ref, qt_ref, vt_ref, glog_ref, u_ref, carry_ref):
    tm = x_ref.shape[1]

    @pl.when(pl.program_id(1) == 0)
    def _():
        carry_ref[...] = jnp.zeros_like(carry_ref)

    u_ref[...] = _rms_normed(x_ref[0], g_ref[...]).astype(BF16)
    u = u_ref[...]

    zs = jnp.dot(u, ws_ref[...], preferred_element_type=F32)
    logf = _log_sigmoid(zs + bf_ref[...])
    row = lax.broadcasted_iota(jnp.int32, (tm, tm), 0)
    col = lax.broadcasted_iota(jnp.int32, (tm, tm), 1)
    tril = jnp.where(row >= col, 1.0, 0.0).astype(BF16)
    csum = sum(jnp.dot(tril, t, preferred_element_type=F32) for t in _split3(logf)) + carry_ref[...]
    carry_ref[...] = csum[tm - 1:tm, :]
    c_cols = sum(jnp.dot(t, place_ref[i], preferred_element_type=F32)
                 for i, t in enumerate(_split3(csum * LOG2E)))

    glr = jnp.dot(zs.astype(BF16), wlr_ref[...], preferred_element_type=F32)
    glog_ref[0] = _log_sigmoid(glr + blr_ref[...]) * (1.0 / GLA_TAU)

    for j in range(ZE_WIDTH // COL_TILE):
        cols = slice(j * COL_TILE, (j + 1) * COL_TILE)
        r = jnp.dot(u, w_ref[:, cols], preferred_element_type=F32)
        if j in (2, 3):
            r = r + c_cols[:, (j - 2) * COL_TILE:(j - 1) * COL_TILE]
        z_ref[0, :, cols] = r.astype(BF16)

    def feature_major(w_t_ref, aug_ref, out_ref, tile):
        for j in range(w_t_ref.shape[0] // tile):
            rows = slice(j * tile, (j + 1) * tile)
            r = lax.dot_general(w_t_ref[rows, :], u, NT_DIMS, preferred_element_type=F32)
            aug = aug_ref[rows, :]
            for c in range(tm // LANES):
                lanes = slice(c * LANES, (c + 1) * LANES)
                out_ref[0, rows, lanes] = (r[:, lanes] + aug).astype(BF16)

    feature_major(wq_ref, qaug_ref, qt_ref, COL_TILE)
    feature_major(wv_ref, vaug_ref, vt_ref, ATT_HEADS * FOX_VROWS // 2)


def _rope_tile(x, cos, sin):
    lane = lax.broadcasted_iota(jnp.int32, (x.shape[0], LANES), 1)
    first_half = (lane % HEAD_DIM) < (HEAD_DIM // 2)
    outs = []
    for c in range(x.shape[1] // LANES):
        xc = x[:, c * LANES:(c + 1) * LANES]
        partner = jnp.where(first_half,
                            pltpu.roll(xc, LANES - HEAD_DIM // 2, 1),
                            pltpu.roll(xc, HEAD_DIM // 2, 1))
        outs.append(xc * cos + partner * sin)
    return jnp.concatenate(outs, axis=-1)


def _in_odd_kernel(x_ref, g_ref, w_ref, cos_ref, sin_ref, z_ref, zd4_ref, zd16_ref, u_ref, r_ref):
    tm = x_ref.shape[1]
    u_ref[...] = _rms_normed(x_ref[0], g_ref[...]).astype(BF16)
    u = u_ref[...]
    cos = cos_ref[...]
    sin = sin_ref[...]
    for j in range(Z_WIDTH // COL_TILE):
        cols = slice(j * COL_TILE, (j + 1) * COL_TILE)
        r = jnp.dot(u, w_ref[:, cols], preferred_element_type=F32)
        if j in (3, 4, 5):
            r = _rope_tile(r, cos, sin)
        z_ref[0, :, cols] = r.astype(BF16)
        if j >= DIL_TILE0:
            t = j - DIL_TILE0
            for c in range(COL_TILE // LANES):
                slot = t * (COL_TILE // LANES) + c
                r_ref[slot] = r[:, c * LANES:(c + 1) * LANES]
                dcols = slice(t * COL_TILE + c * LANES, t * COL_TILE + (c + 1) * LANES)
                for dil, ref in ((4, zd4_ref), (16, zd16_ref)):
                    for res in range(dil):
                        ref[0, res, :, dcols] = (
                            r_ref[slot, pl.ds(res, tm // dil, stride=dil), :].astype(BF16))


def _in_proj_even(h, g, w_main, wq_t, wv_t, w_small, b_f, w_lr, b_lr, place, q_aug, v_aug, tm):
    B, S, D = h.shape
    grid = (B, S // tm)
    const = lambda b, s: (0, 0)
    fox_rows = ATT_HEADS * FOX_PAD
    val_rows = ATT_HEADS * FOX_VROWS
    kw = w_lr.shape[1]
    return pl.pallas_call(
        _in_even_kernel,
        grid=grid,
        in_specs=[
            pl.BlockSpec((1, tm, D), lambda b, s: (b, s, 0)),
            pl.BlockSpec((1, D), const),
            pl.BlockSpec((D, ZE_WIDTH), const),
            pl.BlockSpec((fox_rows, D), const),
            pl.BlockSpec((val_rows, D), const),
            pl.BlockSpec((D, LANES), const),
            pl.BlockSpec((1, LANES), const),
            pl.BlockSpec((LANES, kw), const),
            pl.BlockSpec((1, kw), const),
            pl.BlockSpec((3, LANES, fox_rows), lambda b, s: (0, 0, 0)),
            pl.BlockSpec((fox_rows, LANES), const),
            pl.BlockSpec((val_rows, LANES), const),
        ],
        out_specs=[
            pl.BlockSpec((1, tm, ZE_WIDTH), lambda b, s: (b, s, 0)),
            pl.BlockSpec((1, fox_rows, tm), lambda b, s: (b, 0, s)),
            pl.BlockSpec((1, val_rows, tm), lambda b, s: (b, 0, s)),
            pl.BlockSpec((1, tm, kw), lambda b, s: (b, s, 0)),
        ],
        out_shape=[
            jax.ShapeDtypeStruct((B, S, ZE_WIDTH), BF16),
            jax.ShapeDtypeStruct((B, fox_rows, S), BF16),
            jax.ShapeDtypeStruct((B, val_rows, S), BF16),
            jax.ShapeDtypeStruct((B, S, kw), F32),
        ],
        scratch_shapes=[pltpu.VMEM((tm, D), BF16), pltpu.VMEM((1, LANES), F32)],
        compiler_params=_params("arbitrary", "arbitrary"),
        name="in_proj_even",
    )(h, g, w_main, wq_t, wv_t, w_small, b_f, w_lr, b_lr, place, q_aug, v_aug)


def _in_proj_odd(h, g, w_main, cos, sin, tm):
    B, S, D = h.shape
    grid = (B, S // tm)
    const = lambda b, s: (0, 0)
    dw = 3 * COL_TILE
    return pl.pallas_call(
        _in_odd_kernel,
        grid=grid,
        in_specs=[
            pl.BlockSpec((1, tm, D), lambda b, s: (b, s, 0)),
            pl.BlockSpec((1, D), const),
            pl.BlockSpec((D, Z_WIDTH), const),
            pl.BlockSpec((tm, LANES), lambda b, s: (s, 0)),
            pl.BlockSpec((tm, LANES), lambda b, s: (s, 0)),
        ],
        out_specs=[
            pl.BlockSpec((1, tm, Z_WIDTH), lambda b, s: (b, s, 0)),
            pl.BlockSpec((1, 4, tm // 4, dw), lambda b, s: (b, 0, s, 0)),
            pl.BlockSpec((1, 16, tm // 16, dw), lambda b, s: (b, 0, s, 0)),
        ],
        out_shape=[
            jax.ShapeDtypeStruct((B, S, Z_WIDTH), BF16),
            jax.ShapeDtypeStruct((B, 4, S // 4, dw), BF16),
            jax.ShapeDtypeStruct((B, 16, S // 16, dw), BF16),
        ],
        scratch_shapes=[pltpu.VMEM((tm, D), BF16), pltpu.VMEM((dw // LANES, tm, LANES), F32)],
        compiler_params=_params("arbitrary", "arbitrary"),
        name="in_proj_odd",
    )(h, g, w_main, cos, sin)


def _fox_kernel(qt_ref, kp_ref, vt_ref, o_ref, s_ref, acc_ref, mp_ref, mc_ref, ot_ref, *, blk):
    i = pl.program_id(1)
    key = lax.broadcasted_iota(jnp.int32, (blk, blk), 0)
    qry = lax.broadcasted_iota(jnp.int32, (blk, blk), 1)
    heads = range(ATT_HEADS)
    rows = lambda h: slice(h * FOX_PAD, (h + 1) * FOX_PAD)
    vrows = lambda h: slice(h * FOX_VROWS, (h + 1) * FOX_VROWS)

    def score_stage(h, j, mask):
        ks = pl.multiple_of(j * blk, blk)
        s = jnp.dot(kp_ref[0, pl.ds(ks, blk), rows(h)], qt_ref[0, rows(h), :],
                    preferred_element_type=F32)
        if mask is not None:
            s = jnp.where(mask, s, -jnp.inf)
        s_ref[h] = s
        m_old = mc_ref[h]
        mp_ref[h] = m_old
        mc_ref[h] = jnp.maximum(m_old, jnp.max(s, axis=0, keepdims=True))

    def value_stage(h, j):
        ks = pl.multiple_of(j * blk, blk)
        m_new = mc_ref[h]
        alpha = jnp.exp2(mp_ref[h] - m_new)
        p = jnp.exp2((s_ref[h] - m_new).astype(BF16))
        pv = jnp.dot(vt_ref[0, vrows(h), pl.ds(ks, blk)], p, preferred_element_type=F32)
        acc_ref[h] = alpha * acc_ref[h] + pv

    first_mask = key <= qry + jnp.where(i > 0, blk, 0)
    for h in heads:
        acc_ref[h] = jnp.zeros(acc_ref.shape[1:], F32)
        mc_ref[h] = jnp.full((1, blk), -jnp.inf, F32)
        score_stage(h, 0, first_mask)

    def trip(j, _):
        for h in heads:
            value_stage(h, j - 1)
            score_stage(h, j, None)
        return 0

    lax.fori_loop(1, i, trip, 0)

    @pl.when(i > 0)
    def _():
        for h in heads:
            value_stage(h, i - 1)
            score_stage(h, i, key <= qry)

    for h in heads:
        value_stage(h, i)
        acc = acc_ref[h]
        ot_ref[h * HEAD_DIM:(h + 1) * HEAD_DIM, :] = (
            acc[:HEAD_DIM, :] / acc[HEAD_DIM:HEAD_DIM + 1, :])
    o_ref[0] = ot_ref[...].T.astype(BF16)


def _fox_attention(z, q_t, v_t, blk):
    B, S, _ = z.shape
    fox_rows = ATT_HEADS * FOX_PAD
    val_rows = ATT_HEADS * FOX_VROWS
    return pl.pallas_call(
        functools.partial(_fox_kernel, blk=blk),
        grid=(B, S // blk),
        in_specs=[
            pl.BlockSpec((1, fox_rows, blk), lambda b, i: (b, 0, i)),
            pl.BlockSpec((1, S, fox_rows), lambda b, i: (b, 0, 1)),
            pl.BlockSpec((1, val_rows, S), lambda b, i: (b, 0, 0)),
        ],
        out_specs=pl.BlockSpec((1, blk, COL_TILE), lambda b, i: (b, i, 0)),
        out_shape=jax.ShapeDtypeStruct((B, S, COL_TILE), BF16),
        scratch_shapes=[pltpu.VMEM((ATT_HEADS, blk, blk), F32),
                        pltpu.VMEM((ATT_HEADS, FOX_VROWS, blk), F32),
                        pltpu.VMEM((ATT_HEADS, 1, blk), F32),
                        pltpu.VMEM((ATT_HEADS, 1, blk), F32),
                        pltpu.VMEM((ATT_HEADS * HEAD_DIM, blk), F32)],
        compiler_params=_params("arbitrary", "arbitrary"),
        name="fox_attention",
    )(q_t, z, v_t)


def _gla_kernel(q_ref, k_ref, v_ref, g_ref, gn_ref, o_ref,
                qt_ref, kt_ref, kd_ref, dec_ref, state_ref):
    tg = q_ref.shape[1]
    C = GLA_CHUNK
    n_chunks = tg // C

    @pl.when(pl.program_id(1) == 0)
    def _():
        state_ref[...] = jnp.zeros_like(state_ref)

    row = lax.broadcasted_iota(jnp.int32, (tg, tg), 0)
    col = lax.broadcasted_iota(jnp.int32, (tg, tg), 1)
    chunk_start = row & (-C)
    tril = jnp.where(col <= row, jnp.where(col >= chunk_start, 1.0, 0.0), 0.0).astype(BF16)
    ghi, glo = _split2(g_ref[0])
    b = (jnp.dot(tril, ghi, preferred_element_type=F32)
         + jnp.dot(tril, glo, preferred_element_type=F32))
    q = q_ref[0].astype(F32)
    k = k_ref[0].astype(F32)
    qt_ref[...] = (q * jnp.exp(b)).astype(BF16)
    kt_ref[...] = (k * jnp.exp(-b)).astype(BF16)
    for c in range(n_chunks):
        rows = slice(c * C, (c + 1) * C)
        b_last = b[(c + 1) * C - 1:(c + 1) * C, :]
        kd_ref[rows, :] = (k[rows, :] * jnp.exp(b_last - b[rows, :])).astype(BF16)
        dec_ref[c:c + 1, :] = jnp.exp(b_last)

    r64 = lax.broadcasted_iota(jnp.int32, (C, C), 0)
    c64 = lax.broadcasted_iota(jnp.int32, (C, C), 1)
    lower = c64 <= r64

    for c in range(n_chunks):
        rows = slice(c * C, (c + 1) * C)
        dec = dec_ref[c:c + 1, :]
        for h in range(REC_HEADS):
            ks = slice(h * HEAD_DIM, (h + 1) * HEAD_DIM)
            vs = slice(h * VAL_DIM, (h + 1) * VAL_DIM)
            qt = qt_ref[rows, ks]
            kt = kt_ref[rows, ks]
            kd = kd_ref[rows, ks]
            v = v_ref[0, rows, vs]
            att = lax.dot_general(qt, kt, NT_DIMS, preferred_element_type=F32)
            att = jnp.where(lower, att, 0.0).astype(BF16)
            st = state_ref[h]
            o = (jnp.dot(att, v, preferred_element_type=F32)
                 + lax.dot_general(qt, st.astype(BF16), NT_DIMS, preferred_element_type=F32))
            kv_t = lax.dot_general(v, kd, TN_DIMS, preferred_element_type=F32)
            state_ref[h] = st * dec[:, ks] + kv_t
            ms = jnp.mean(o * o, axis=-1, keepdims=True)
            o_ref[0, rows, vs] = (o * lax.rsqrt(ms + EPS) * gn_ref[:, vs]).astype(BF16)


def _gla(z, glog, gla_g, tg):
    B, S, _ = z.shape
    kw = REC_HEADS * HEAD_DIM
    vw = REC_HEADS * VAL_DIM
    return pl.pallas_call(
        _gla_kernel,
        grid=(B, S // tg),
        in_specs=[
            pl.BlockSpec((1, tg, kw), lambda b, s: (b, s, 10)),
            pl.BlockSpec((1, tg, kw), lambda b, s: (b, s, 11)),
            pl.BlockSpec((1, tg, vw), lambda b, s: (b, s, 4)),
            pl.BlockSpec((1, tg, kw), lambda b, s: (b, s, 0)),
            pl.BlockSpec((1, vw), lambda b, s: (0, 0)),
        ],
        out_specs=pl.BlockSpec((1, tg, vw), lambda b, s: (b, s, 0)),
        out_shape=jax.ShapeDtypeStruct((B, S, vw), BF16),
        scratch_shapes=[
            pltpu.VMEM((tg, kw), BF16), pltpu.VMEM((tg, kw), BF16), pltpu.VMEM((tg, kw), BF16),
            pltpu.VMEM((tg // GLA_CHUNK, kw), F32),
            pltpu.VMEM((REC_HEADS, VAL_DIM, HEAD_DIM), F32),
        ],
        compiler_params=_params("arbitrary", "arbitrary"),
        name="gla",
    )(z, z, z, glog, gla_g)


def _ret_kernel(q_ref, k_ref, v_ref, gw_ref, gb_ref, o_ref, state_ref, dmat_ref):
    tg = q_ref.shape[1]
    C = RET_CHUNK
    n_chunks = tg // C

    @pl.when(pl.program_id(1) == 0)
    def _():
        state_ref[...] = jnp.zeros_like(state_ref)

    ri = lax.broadcasted_iota(jnp.int32, (C, C), 0)
    ci = lax.broadcasted_iota(jnp.int32, (C, C), 1)
    diff = (ri - ci).astype(F32)
    idx = lax.broadcasted_iota(jnp.int32, (C, 1), 0).astype(F32)

    for h in range(REC_HEADS):
        log_g = math.log(1.0 - 2.0 ** (-5.0 - h))
        dmat_ref[h] = jnp.where(diff >= 0, jnp.exp(jnp.maximum(diff, 0.0) * log_g), 0.0)

    for c in range(n_chunks):
        rows = slice(c * C, (c + 1) * C)
        for h in range(REC_HEADS):
            log_g = math.log(1.0 - 2.0 ** (-5.0 - h))
            dmat = dmat_ref[h]
            xi = jnp.exp((idx + 1.0) * log_g)
            zeta = jnp.exp((C - 1.0 - idx) * log_g)
            ks = slice(h * HEAD_DIM, (h + 1) * HEAD_DIM)
            vs = slice(h * VAL_DIM, (h + 1) * VAL_DIM)
            q = q_ref[0, rows, ks]
            k = k_ref[0, rows, ks]
            v = v_ref[0, rows, vs]
            att = lax.dot_general(q, k, NT_DIMS, preferred_element_type=F32) * dmat
            st = state_ref[h]
            o = (jnp.dot(att.astype(BF16), v, preferred_element_type=F32)
                 + xi * lax.dot_general(q, st.astype(BF16), NT_DIMS, preferred_element_type=F32))
            vz = (v.astype(F32) * zeta).astype(BF16)
            kv_t = lax.dot_general(vz, k, TN_DIMS, preferred_element_type=F32)
            state_ref[h] = st * math.exp(C * log_g) + kv_t
            o = o - jnp.mean(o, axis=-1, keepdims=True)
            ms = jnp.mean(o * o, axis=-1, keepdims=True)
            o = o * lax.rsqrt(ms + EPS) * gw_ref[:, vs] + gb_ref[:, vs]
            o_ref[0, rows, vs] = o.astype(BF16)


def _retention(z, gn_w, gn_b, tg):
    B, S, _ = z.shape
    kw = REC_HEADS * HEAD_DIM
    vw = REC_HEADS * VAL_DIM
    return pl.pallas_call(
        _ret_kernel,
        grid=(B, S // tg),
        in_specs=[
            pl.BlockSpec((1, tg, kw), lambda b, s: (b, s, 6)),
            pl.BlockSpec((1, tg, kw), lambda b, s: (b, s, 7)),
            pl.BlockSpec((1, tg, vw), lambda b, s: (b, s, 2)),
            pl.BlockSpec((1, vw), lambda b, s: (0, 0)),
            pl.BlockSpec((1, vw), lambda b, s: (0, 0)),
        ],
        out_specs=pl.BlockSpec((1, tg, vw), lambda b, s: (b, s, 0)),
        out_shape=jax.ShapeDtypeStruct((B, S, vw), BF16),
        scratch_shapes=[pltpu.VMEM((REC_HEADS, VAL_DIM, HEAD_DIM), F32),
                        pltpu.VMEM((REC_HEADS, RET_CHUNK, RET_CHUNK), F32)],
        compiler_params=_params("arbitrary", "arbitrary"),
        name="retention",
    )(z, z, z, gn_w, gn_b)


def _dil_kernel(q_ref, k_ref, v_ref, o_ref, lse_ref, s_ref, m_ref, bias_ref, *, blk, nblk, span):
    n = pl.program_id(2)
    pairs = ATT_HEADS // 2
    lane = lax.broadcasted_iota(jnp.int32, (blk, LANES), 1)
    low = lane < HEAD_DIM
    low_bf = jnp.where(low, 1.0, 0.0).astype(BF16)
    high_bf = jnp.where(low, 0.0, 1.0).astype(BF16)
    qi = lax.broadcasted_iota(jnp.int32, (blk, 2 * blk), 0)
    ki = lax.broadcasted_iota(jnp.int32, (blk, 2 * blk), 1)

    for t in range(nblk):
        qb = n * nblk + t
        start = pl.multiple_of(jnp.maximum(qb - 1, 0) * blk, blk)
        rel = qb * blk + qi - start - ki
        bias_ref[...] = jnp.where(rel >= 0, jnp.where(rel <= span, 0.0, -jnp.inf), -jnp.inf)
        qrows = slice(t * blk, (t + 1) * blk)
        chains = [(g, half) for g in range(pairs) for half in range(2)]

        def score_stage(c):
            g, half = chains[c]
            gs = slice(g * LANES, (g + 1) * LANES)
            q = q_ref[0, 0, qrows, gs] * (high_bf if half else low_bf)
            k = k_ref[0, 0, pl.ds(start, 2 * blk), gs]
            s = lax.dot_general(q, k, NT_DIMS, preferred_element_type=F32) + bias_ref[...]
            s_ref[c] = s
            m_ref[c] = jnp.max(s, axis=-1, keepdims=True)

        def value_stage(c):
            g, half = chains[c]
            gs = slice(g * LANES, (g + 1) * LANES)
            m = m_ref[c]
            p = jnp.exp(s_ref[c] - m)
            den = jnp.sum(p, axis=-1, keepdims=True)
            v = v_ref[0, 0, pl.ds(start, 2 * blk), gs]
            pv = jnp.dot(p.astype(BF16), v, preferred_element_type=F32)
            return pv * (1.0 / den), m + jnp.log(den)

        lse_tile = jnp.zeros((blk, LANES), F32)
        outs = {}
        score_stage(0)
        for c in range(len(chains)):
            if c + 1 < len(chains):
                score_stage(c + 1)
            o, lse = value_stage(c)
            g, half = chains[c]
            lse_tile = jnp.where(lane == 2 * g + half, lse, lse_tile)
            if half == 0:
                outs[g] = o
            else:
                o_ref[0, 0, qrows, g * LANES:(g + 1) * LANES] = (
                    jnp.where(low, outs.pop(g), o).astype(BF16))
        lse_ref[0, 0, qrows, :] = lse_tile


def _dilated_branch(src, tile0, window, dil, nblk):
    B, _, m, _ = src.shape
    blk = DIL_BLOCK
    tq = nblk * blk
    return pl.pallas_call(
        functools.partial(_dil_kernel, blk=blk, nblk=nblk, span=window // dil),
        grid=(B, dil, m // tq),
        in_specs=[
            pl.BlockSpec((1, 1, tq, COL_TILE), lambda b, r, n: (b, r, n, tile0)),
            pl.BlockSpec((1, 1, m, COL_TILE), lambda b, r, n: (b, r, 0, tile0 + 1)),
            pl.BlockSpec((1, 1, m, COL_TILE), lambda b, r, n: (b, r, 0, tile0 + 2)),
        ],
        out_specs=[
            pl.BlockSpec((1, 1, tq, COL_TILE), lambda b, r, n: (b, r, n, 0)),
            pl.BlockSpec((1, 1, tq, LANES), lambda b, r, n: (b, r, n, 0)),
        ],
        out_shape=[
            jax.ShapeDtypeStruct((B, dil, m, COL_TILE), BF16),
            jax.ShapeDtypeStruct((B, dil, m, LANES), F32),
        ],
        scratch_shapes=[pltpu.VMEM((ATT_HEADS, blk, 2 * blk), F32),
                        pltpu.VMEM((ATT_HEADS, blk, 1), F32),
                        pltpu.VMEM((blk, 2 * blk), F32)],
        compiler_params=_params("arbitrary", "arbitrary", "arbitrary"),
        name=f"dilated_{dil}",
    )(src, src, src)


def _silu(x):
    return x / (1.0 + jnp.exp(-x))


def _out_tail(mix_a, mix_b, gate_ref, h_ref, w_ref, fg_ref, out_ref, final):
    half = mix_a.shape[1]
    gate = _silu(gate_ref[0].astype(F32))
    a = (mix_a * gate[:, :half]).astype(BF16)
    b = (mix_b * gate[:, half:]).astype(BF16)
    y = (jnp.dot(a, w_ref[:half, :], preferred_element_type=F32)
         + jnp.dot(b, w_ref[half:, :], preferred_element_type=F32))
    hn = h_ref[0] + y
    if final:
        hn = _rms_normed(hn, fg_ref[...])
    out_ref[0] = hn


def _out_even_kernel(oa_ref, ob_ref, gate_ref, h_ref, w_ref, fg_ref, out_ref, *, final):
    _out_tail(oa_ref[0].astype(F32), ob_ref[0].astype(F32), gate_ref, h_ref, w_ref, fg_ref,
              out_ref, final)


def _out_odd_kernel(oc_ref, d1_ref, d4_ref, d16_ref, l1_ref, l4_ref, l16_ref, e_ref,
                    gate_ref, h_ref, w_ref, fg_ref, out_ref, o4_ref, o16_ref, ls4_ref, ls16_ref,
                    *, final):
    tm = oc_ref.shape[1]
    chunks = oc_ref.shape[2] // LANES
    for dil, src, lsrc, dst, ldst in ((4, d4_ref, l4_ref, o4_ref, ls4_ref),
                                      (16, d16_ref, l16_ref, o16_ref, ls16_ref)):
        for res in range(dil):
            token_rows = pl.ds(res, tm // dil, stride=dil)
            ldst[token_rows, :] = lsrc[0, res]
            for c in range(chunks):
                dst[c, token_rows, :] = src[0, res, :, c * LANES:(c + 1) * LANES].astype(F32)
    l1, l2, l3 = l1_ref[0], ls4_ref[...], ls16_ref[...]
    mx = jnp.maximum(jnp.maximum(l1, l2), l3)
    e1, e2, e3 = jnp.exp(l1 - mx), jnp.exp(l2 - mx), jnp.exp(l3 - mx)
    inv = 1.0 / (e1 + e2 + e3)
    e = e_ref[...]
    expand = lambda w: jnp.dot(w, e, preferred_element_type=F32, precision=lax.Precision.HIGHEST)
    o4 = jnp.concatenate([o4_ref[c] for c in range(chunks)], axis=-1)
    o16 = jnp.concatenate([o16_ref[c] for c in range(chunks)], axis=-1)
    o_d = (expand(e1 * inv) * d1_ref[0].astype(F32)
           + expand(e2 * inv) * o4
           + expand(e3 * inv) * o16)
    _out_tail(oc_ref[0].astype(F32), o_d, gate_ref, h_ref, w_ref, fg_ref, out_ref, final)


def _out_proj(mix_parts, lses, z, h, w_out, final_g, final, tm):
    B, S, D = h.shape
    half = COL_TILE
    tok = lambda b, s: (b, s, 0)
    const = lambda b, s: (0, 0)
    part_spec = pl.BlockSpec((1, tm, half), tok)
    tail_specs = [
        pl.BlockSpec((1, tm, 2 * half), tok),
        pl.BlockSpec((1, tm, D), tok),
        pl.BlockSpec((2 * half, D), const),
        pl.BlockSpec((1, D), const),
    ]
    scratch = []
    if lses:
        expand = jnp.pad(jnp.repeat(jnp.eye(ATT_HEADS, dtype=F32), HEAD_DIM, axis=1),
                         ((0, LANES - ATT_HEADS), (0, 0)))
        kern = functools.partial(_out_odd_kernel, final=final)
        res_spec = lambda dil, c: pl.BlockSpec((1, dil, tm // dil, c), lambda b, s: (b, 0, s, 0))
        in_specs = ([part_spec] * 2 + [res_spec(4, half), res_spec(16, half)]
                    + [pl.BlockSpec((1, tm, LANES), tok), res_spec(4, LANES), res_spec(16, LANES)]
                    + [pl.BlockSpec((LANES, half), const)] + tail_specs)
        args = (*mix_parts, *lses, expand, z, h, w_out, final_g)
        scratch = [pltpu.VMEM((half // LANES, tm, LANES), F32),
                   pltpu.VMEM((half // LANES, tm, LANES), F32),
                   pltpu.VMEM((tm, LANES), F32), pltpu.VMEM((tm, LANES), F32)]
    else:
        kern = functools.partial(_out_even_kernel, final=final)
        in_specs = [part_spec] * 2 + tail_specs
        args = (*mix_parts, z, h, w_out, final_g)
    return pl.pallas_call(
        kern,
        grid=(B, S // tm),
        in_specs=in_specs,
        out_specs=pl.BlockSpec((1, tm, D), tok),
        out_shape=jax.ShapeDtypeStruct((B, S, D), F32),
        scratch_shapes=scratch,
        compiler_params=_params("arbitrary", "arbitrary"),
        name="out_proj_odd" if lses else "out_proj_even",
    )(*args)


def _prep_even(w_in, b_f, w_lr, b_lr):
    fw = ATT_HEADS * HEAD_DIM
    kw = REC_HEADS * HEAD_DIM
    vw = REC_HEADS * VAL_DIM
    sizes = (fw, fw, fw, ATT_HEADS, kw, kw, vw, GLA_RANK, fw + vw)
    offs = np.cumsum((0,) + sizes)
    fq, fk, fv, ff, gq, gk, gv, glr, gate = (w_in[:, offs[i]:offs[i + 1]] for i in range(9))
    scale = HEAD_DIM ** -0.5
    D = w_in.shape[0]

    def pad_heads(w, width):
        w = w.reshape(D, ATT_HEADS, HEAD_DIM)
        return jnp.pad(w, ((0, 0), (0, 0), (0, width - HEAD_DIM))).reshape(D, ATT_HEADS * width)

    w_main = jnp.concatenate([gate, pad_heads(fk, FOX_PAD), gv, gq * scale, gk], axis=1).astype(BF16)
    wq_t = pad_heads(fq * (scale * LOG2E), FOX_PAD).T.astype(BF16)
    wv_t = pad_heads(fv, FOX_VROWS).T.astype(BF16)
    q_aug = np.zeros((ATT_HEADS, FOX_PAD, LANES), np.float32)
    q_aug[:, HEAD_DIM:HEAD_DIM + 3, :] = -1.0
    v_aug = np.zeros((ATT_HEADS, FOX_VROWS, LANES), np.float32)
    v_aug[:, HEAD_DIM, :] = 1.0
    pad = LANES - ATT_HEADS - GLA_RANK
    w_small = jnp.pad(jnp.concatenate([ff, glr], axis=1), ((0, 0), (0, pad))).astype(BF16)
    b_f_pad = jnp.pad(b_f, (0, LANES - ATT_HEADS)).reshape(1, LANES)
    w_lr_pad = jnp.pad(w_lr, ((ATT_HEADS, pad), (0, 0))).astype(BF16)
    place = np.zeros((3, LANES, ATT_HEADS * FOX_PAD), np.float32)
    for i in range(3):
        for h in range(ATT_HEADS):
            place[i, h, h * FOX_PAD + HEAD_DIM + i] = 1.0
    return (w_main, wq_t, wv_t, w_small, b_f_pad, w_lr_pad, b_lr.reshape(1, kw),
            jnp.asarray(place, BF16), jnp.asarray(q_aug.reshape(-1, LANES)),
            jnp.asarray(v_aug.reshape(-1, LANES)))


def _prep_odd(w_in):
    kw = REC_HEADS * HEAD_DIM
    vw = REC_HEADS * VAL_DIM
    dw = ATT_HEADS * HEAD_DIM
    sizes = (kw, kw, vw, dw, dw, dw, vw + dw)
    offs = np.cumsum((0,) + sizes)
    rq, rk, rv, dq, dk, dv, gate = (w_in[:, offs[i]:offs[i + 1]] for i in range(7))
    scale = HEAD_DIM ** -0.5
    return jnp.concatenate([gate, rv, rq * scale, rk, dq * scale, dk, dv], axis=1).astype(BF16)


def _rope_tables(S):
    inv = jnp.power(ROPE_THETA, -jnp.arange(0, HEAD_DIM, 2, dtype=F32) / HEAD_DIM)
    ang = jnp.arange(S, dtype=F32)[:, None] * inv[None, :]
    cos, sin = jnp.cos(ang), jnp.sin(ang)
    reps = LANES // HEAD_DIM
    cos_t = jnp.tile(jnp.concatenate([cos, cos], axis=1), (1, reps))
    sin_t = jnp.tile(jnp.concatenate([-sin, sin], axis=1), (1, reps))
    return cos_t, sin_t


def _tile(S, pref):
    return min(S, pref)


def _even_layer(h, norm_g, w_in, b_f, w_lr, b_lr, gla_g, w_out, final_g, final):
    B, S, D = h.shape
    z, q_t, v_t, glog = _in_proj_even(h, norm_g.reshape(1, D), *_prep_even(w_in, b_f, w_lr, b_lr),
                                      _tile(S, 512))
    o_a = _fox_attention(z, q_t, v_t, _tile(S, 256))
    o_b = _gla(z, glog, gla_g.reshape(1, -1), _tile(S, 512))
    return _out_proj((o_a, o_b), (), z, h, w_out.astype(BF16), final_g.reshape(1, D), final,
                     _tile(S, 512))


def _odd_layer(h, norm_g, w_in, gn_w, gn_b, w_out, final_g, final, cos, sin):
    B, S, D = h.shape
    z, zd4, zd16 = _in_proj_odd(h, norm_g.reshape(1, D), _prep_odd(w_in), cos, sin, _tile(S, 512))
    o_c = _retention(z, gn_w.reshape(1, -1), gn_b.reshape(1, -1), _tile(S, 512))
    sources = {1: (z.reshape(B, 1, S, Z_WIDTH), DIL_TILE0), 4: (zd4, 0), 16: (zd16, 0)}
    outs, lses = zip(*[_dilated_branch(*sources[dil], window, dil, DIL_QBLOCKS)
                       for window, dil in DIL_PATTERNS])
    o1, l1 = outs[0].reshape(B, S, COL_TILE), lses[0].reshape(B, S, LANES)
    return _out_proj((o_c, o1, *outs[1:]), (l1, *lses[1:]), z, h, w_out.astype(BF16),
                     final_g.reshape(1, D), final, _tile(S, 512))


def kernel(x, norm_even, w_in_even, b_f_even, w_lr_even, b_lr_even, gla_norm_even, w_out_even,
           norm_odd, w_in_odd, ret_gn_w_odd, ret_gn_b_odd, w_out_odd, final_norm):
    depth = norm_even.shape[0] + norm_odd.shape[0]
    cos, sin = _rope_tables(x.shape[1])
    h = x
    for i in range(depth):
        j = i // 2
        final = i == depth - 1
        if i % 2 == 0:
            h = _even_layer(h, norm_even[j], w_in_even[j], b_f_even[j], w_lr_even[j],
                            b_lr_even[j], gla_norm_even[j], w_out_even[j], final_norm, final)
        else:
            h = _odd_layer(h, norm_odd[j], w_in_odd[j], ret_gn_w_odd[j], ret_gn_b_odd[j],
                           w_out_odd[j], final_norm, final, cos, sin)
    return h
```

```python
import functools
import math

import numpy as np
import jax
import jax.numpy as jnp
from jax import lax
from jax.experimental import pallas as pl
from jax.experimental.pallas import tpu as pltpu

F32 = jnp.float32
BF16 = jnp.bfloat16

EPS = 1e-6
ROPE_THETA = 10000.0
HEAD_DIM = 64
VAL_DIM = 128
ATT_HEADS = 8
REC_HEADS = 4
GLA_RANK = 16
GLA_TAU = 16.0
GLA_CHUNK = 64
RET_CHUNK = 128
DIL_BLOCK = 128
DIL_PATTERNS = ((128, 1), (512, 4), (2048, 16))

Z_WIDTH = 3584
DIL_TILE0 = 4
DIL_QBLOCKS = 2
DIL_VROWS = 144
ZE_WIDTH = 3072
FOX_PAD = 128
FOX_VROWS = 80
LOG2E = 1.4426950408889634
COL_TILE = 512
LANES = 128
V7X_VMEM_LIMIT = 56 * 1024 * 1024

NT_DIMS = (((1,), (1,)), ((), ()))
TN_DIMS = (((0,), (0,)), ((), ()))


def _params(*sem):
    return pltpu.CompilerParams(dimension_semantics=sem, vmem_limit_bytes=V7X_VMEM_LIMIT)


def _split2(x):
    hi = x.astype(BF16)
    lo = (x - hi.astype(F32)).astype(BF16)
    return hi, lo


def _split3(x):
    hi, lo = _split2(x)
    lo2 = (x - hi.astype(F32) - lo.astype(F32)).astype(BF16)
    return hi, lo, lo2


def _log_sigmoid(x):
    return jnp.minimum(x, 0.0) - jnp.log(1.0 + jnp.exp(-jnp.abs(x)))


def _rms_normed(x, g):
    ms = jnp.mean(x * x, axis=-1, keepdims=True)
    return x * lax.rsqrt(ms + EPS) * g


def _in_even_kernel(x_ref, g_ref, w_ref, wq_ref, wv_ref, ws_ref, bf_ref, wlr_ref, blr_ref, place_ref,
                    qaug_ref, vaug_ref, z_ref, qt_ref, vt_ref, glog_ref, u_ref, carry_ref):
    tm = x_ref.shape[1]

    @pl.when(pl.program_id(1) == 0)
    def _():
        carry_ref[...] = jnp.zeros_like(carry_ref)

    u_ref[...] = _rms_normed(x_ref[0], g_ref[...]).astype(BF16)
    u = u_ref[...]

    zs = jnp.dot(u, ws_ref[...], preferred_element_type=F32)
    logf = _log_sigmoid(zs + bf_ref[...])
    row = lax.broadcasted_iota(jnp.int32, (tm, tm), 0)
    col = lax.broadcasted_iota(jnp.int32, (tm, tm), 1)
    tril = jnp.where(row >= col, 1.0, 0.0).astype(BF16)
    csum = sum(jnp.dot(tril, t, preferred_element_type=F32) for t in _split3(logf)) + carry_ref[...]
    carry_ref[...] = csum[tm - 1:tm, :]
    c_cols = sum(jnp.dot(t, place_ref[i], preferred_element_type=F32)
                 for i, t in enumerate(_split3(csum * LOG2E)))

    glr = jnp.dot(zs.astype(BF16), wlr_ref[...], preferred_element_type=F32)
    glog_ref[0] = _log_sigmoid(glr + blr_ref[...]) * (1.0 / GLA_TAU)

    for j in range(ZE_WIDTH // COL_TILE):
        cols = slice(j * COL_TILE, (j + 1) * COL_TILE)
        r = jnp.dot(u, w_ref[:, cols], preferred_element_type=F32)
        if j in (2, 3):
            r = r + c_cols[:, (j - 2) * COL_TILE:(j - 1) * COL_TILE]
        z_ref[0, :, cols] = r.astype(BF16)

    def feature_major(w_t_ref, aug_ref, out_ref, tile):
        for j in range(w_t_ref.shape[0] // tile):
            rows = slice(j * tile, (j + 1) * tile)
            r = lax.dot_general(w_t_ref[rows, :], u, NT_DIMS, preferred_element_type=F32)
            aug = aug_ref[rows, :]
            for c in range(tm // LANES):
                lanes = slice(c * LANES, (c + 1) * LANES)
                out_ref[0, rows, lanes] = (r[:, lanes] + aug).astype(BF16)

    feature_major(wq_ref, qaug_ref, qt_ref, COL_TILE)
    feature_major(wv_ref, vaug_ref, vt_ref, ATT_HEADS * FOX_VROWS // 2)


def _rope_tile(x, cos, sin):
    lane = lax.broadcasted_iota(jnp.int32, (x.shape[0], LANES), 1)
    first_half = (lane % HEAD_DIM) < (HEAD_DIM // 2)
    outs = []
    for c in range(x.shape[1] // LANES):
        xc = x[:, c * LANES:(c + 1) * LANES]
        partner = jnp.where(first_half,
                            pltpu.roll(xc, LANES - HEAD_DIM // 2, 1),
                            pltpu.roll(xc, HEAD_DIM // 2, 1))
        outs.append(xc * cos + partner * sin)
    return jnp.concatenate(outs, axis=-1)


def _in_odd_kernel(x_ref, g_ref, w_ref, cos_ref, sin_ref, z_ref, zd4_ref, zd16_ref, u_ref, r_ref):
    tm = x_ref.shape[1]
    u_ref[...] = _rms_normed(x_ref[0], g_ref[...]).astype(BF16)
    u = u_ref[...]
    cos = cos_ref[...]
    sin = sin_ref[...]
    for j in range(Z_WIDTH // COL_TILE):
        cols = slice(j * COL_TILE, (j + 1) * COL_TILE)
        r = jnp.dot(u, w_ref[:, cols], preferred_element_type=F32)
        if j in (3, 4, 5):
            r = _rope_tile(r, cos, sin)
        z_ref[0, :, cols] = r.astype(BF16)
        if j >= DIL_TILE0:
            t = j - DIL_TILE0
            for c in range(COL_TILE // LANES):
                slot = t * (COL_TILE // LANES) + c
                r_ref[slot] = r[:, c * LANES:(c + 1) * LANES]
                dcols = slice(t * COL_TILE + c * LANES, t * COL_TILE + (c + 1) * LANES)
                for dil, ref in ((4, zd4_ref), (16, zd16_ref)):
                    for res in range(dil):
                        ref[0, res, :, dcols] = (
                            r_ref[slot, pl.ds(res, tm // dil, stride=dil), :].astype(BF16))


def _in_proj_even(h, g, w_main, wq_t, wv_t, w_small, b_f, w_lr, b_lr, place, q_aug, v_aug, tm):
    B, S, D = h.shape
    grid = (B, S // tm)
    const = lambda b, s: (0, 0)
    fox_rows = ATT_HEADS * FOX_PAD
    val_rows = ATT_HEADS * FOX_VROWS
    kw = w_lr.shape[1]
    return pl.pallas_call(
        _in_even_kernel,
        grid=grid,
        in_specs=[
            pl.BlockSpec((1, tm, D), lambda b, s: (b, s, 0)),
            pl.BlockSpec((1, D), const),
            pl.BlockSpec((D, ZE_WIDTH), const),
            pl.BlockSpec((fox_rows, D), const),
            pl.BlockSpec((val_rows, D), const),
            pl.BlockSpec((D, LANES), const),
            pl.BlockSpec((1, LANES), const),
            pl.BlockSpec((LANES, kw), const),
            pl.BlockSpec((1, kw), const),
            pl.BlockSpec((3, LANES, fox_rows), lambda b, s: (0, 0, 0)),
            pl.BlockSpec((fox_rows, LANES), const),
            pl.BlockSpec((val_rows, LANES), const),
        ],
        out_specs=[
            pl.BlockSpec((1, tm, ZE_WIDTH), lambda b, s: (b, s, 0)),
            pl.BlockSpec((1, fox_rows, tm), lambda b, s: (b, 0, s)),
            pl.BlockSpec((1, val_rows, tm), lambda b, s: (b, 0, s)),
            pl.BlockSpec((1, tm, kw), lambda b, s: (b, s, 0)),
        ],
        out_shape=[
            jax.ShapeDtypeStruct((B, S, ZE_WIDTH), BF16),
            jax.ShapeDtypeStruct((B, fox_rows, S), BF16),
            jax.ShapeDtypeStruct((B, val_rows, S), BF16),
            jax.ShapeDtypeStruct((B, S, kw), F32),
        ],
        scratch_shapes=[pltpu.VMEM((tm, D), BF16), pltpu.VMEM((1, LANES), F32)],
        compiler_params=_params("arbitrary", "arbitrary"),
        name="in_proj_even",
    )(h, g, w_main, wq_t, wv_t, w_small, b_f, w_lr, b_lr, place, q_aug, v_aug)


def _in_proj_odd(h, g, w_main, cos, sin, tm):
    B, S, D = h.shape
    grid = (B, S // tm)
    const = lambda b, s: (0, 0)
    dw = 3 * COL_TILE
    return pl.pallas_call(
        _in_odd_kernel,
        grid=grid,
        in_specs=[
            pl.BlockSpec((1, tm, D), lambda b, s: (b, s, 0)),
            pl.BlockSpec((1, D), const),
            pl.BlockSpec((D, Z_WIDTH), const),
            pl.BlockSpec((tm, LANES), lambda b, s: (s, 0)),
            pl.BlockSpec((tm, LANES), lambda b, s: (s, 0)),
        ],
        out_specs=[
            pl.BlockSpec((1, tm, Z_WIDTH), lambda b, s: (b, s, 0)),
            pl.BlockSpec((1, 4, tm // 4, dw), lambda b, s: (b, 0, s, 0)),
            pl.BlockSpec((1, 16, tm // 16, dw), lambda b, s: (b, 0, s, 0)),
        ],
        out_shape=[
            jax.ShapeDtypeStruct((B, S, Z_WIDTH), BF16),
            jax.ShapeDtypeStruct((B, 4, S // 4, dw), BF16),
            jax.ShapeDtypeStruct((B, 16, S // 16, dw), BF16),
        ],
        scratch_shapes=[pltpu.VMEM((tm, D), BF16), pltpu.VMEM((dw // LANES, tm, LANES), F32)],
        compiler_params=_params("arbitrary", "arbitrary"),
        name="in_proj_odd",
    )(h, g, w_main, cos, sin)


def _fox_kernel(qt_ref, kp_ref, vt_ref, o_ref, s_ref, acc_ref, mp_ref, mc_ref, ot_ref, mask_ref, *, blk):
    i = pl.program_id(1)
    heads = range(ATT_HEADS)
    rows = lambda h: slice(h * FOX_PAD, (h + 1) * FOX_PAD)
    vrows = lambda h: slice(h * FOX_VROWS, (h + 1) * FOX_VROWS)

    @pl.when(i == 0)
    def _():
        key = lax.broadcasted_iota(jnp.int32, (blk, blk), 0)
        qry = lax.broadcasted_iota(jnp.int32, (blk, blk), 1)
        mask_ref[0] = jnp.zeros((blk, blk), F32)
        mask_ref[1] = jnp.where(key <= qry, 0.0, -jnp.inf)

    def score_stage(h, j):
        ks = pl.multiple_of(j * blk, blk)
        s = jnp.dot(kp_ref[0, pl.ds(ks, blk), rows(h)], qt_ref[0, rows(h), :],
                    preferred_element_type=F32)
        s = s + mask_ref[jnp.where(j == i, 1, 0)]
        s_ref[h] = s
        m_old = mc_ref[h]
        mp_ref[h] = m_old
        mc_ref[h] = jnp.maximum(m_old, jnp.max(s, axis=0, keepdims=True))

    def value_stage(h, j):
        ks = pl.multiple_of(j * blk, blk)
        m_new = mc_ref[h]
        alpha = jnp.exp2(mp_ref[h] - m_new)
        p = jnp.exp2((s_ref[h] - m_new).astype(BF16))
        pv = jnp.dot(vt_ref[0, vrows(h), pl.ds(ks, blk)], p, preferred_element_type=F32)
        acc_ref[h] = alpha * acc_ref[h] + pv

    def trip(j):
        for h in heads:
            value_stage(h, j - 1)
            score_stage(h, j)

    for h in heads:
        acc_ref[h] = jnp.zeros(acc_ref.shape[1:], F32)
        mc_ref[h] = jnp.full((1, blk), -jnp.inf, F32)
        score_stage(h, 0)

    odd = i % 2

    @pl.when(odd == 1)
    def _():
        trip(1)

    def pair(t, _):
        j = 1 + odd + 2 * t
        trip(j)
        trip(j + 1)
        return 0

    lax.fori_loop(0, i // 2, pair, 0)

    for h in heads:
        value_stage(h, i)
        acc = acc_ref[h]
        ot_ref[h * HEAD_DIM:(h + 1) * HEAD_DIM, :] = (
            acc[:HEAD_DIM, :] / acc[HEAD_DIM:HEAD_DIM + 1, :])
    o_ref[0] = ot_ref[...].T.astype(BF16)


def _fox_attention(z, q_t, v_t, blk):
    B, S, _ = z.shape
    fox_rows = ATT_HEADS * FOX_PAD
    val_rows = ATT_HEADS * FOX_VROWS
    return pl.pallas_call(
        functools.partial(_fox_kernel, blk=blk),
        grid=(B, S // blk),
        in_specs=[
            pl.BlockSpec((1, fox_rows, blk), lambda b, i: (b, 0, i)),
            pl.BlockSpec((1, S, fox_rows), lambda b, i: (b, 0, 1)),
            pl.BlockSpec((1, val_rows, S), lambda b, i: (b, 0, 0)),
        ],
        out_specs=pl.BlockSpec((1, blk, COL_TILE), lambda b, i: (b, i, 0)),
        out_shape=jax.ShapeDtypeStruct((B, S, COL_TILE), BF16),
        scratch_shapes=[pltpu.VMEM((ATT_HEADS, blk, blk), F32),
                        pltpu.VMEM((ATT_HEADS, FOX_VROWS, blk), F32),
                        pltpu.VMEM((ATT_HEADS, 1, blk), F32),
                        pltpu.VMEM((ATT_HEADS, 1, blk), F32),
                        pltpu.VMEM((ATT_HEADS * HEAD_DIM, blk), F32),
                        pltpu.VMEM((2, blk, blk), F32)],
        compiler_params=_params("arbitrary", "arbitrary"),
        name="fox_attention",
    )(q_t, z, v_t)


def _gla_kernel(q_ref, k_ref, v_ref, g_ref, gn_ref, o_ref,
                qt_ref, kt_ref, kd_ref, dec_ref, state_ref):
    tg = q_ref.shape[1]
    C = GLA_CHUNK
    n_chunks = tg // C

    @pl.when(pl.program_id(1) == 0)
    def _():
        state_ref[...] = jnp.zeros_like(state_ref)

    row = lax.broadcasted_iota(jnp.int32, (tg, tg), 0)
    col = lax.broadcasted_iota(jnp.int32, (tg, tg), 1)
    chunk_start = row & (-C)
    tril = jnp.where(col <= row, jnp.where(col >= chunk_start, 1.0, 0.0), 0.0).astype(BF16)
    ghi, glo = _split2(g_ref[0])
    b = (jnp.dot(tril, ghi, preferred_element_type=F32)
         + jnp.dot(tril, glo, preferred_element_type=F32))
    q = q_ref[0].astype(F32)
    k = k_ref[0].astype(F32)
    qt_ref[...] = (q * jnp.exp(b)).astype(BF16)
    kt_ref[...] = (k * jnp.exp(-b)).astype(BF16)
    for c in range(n_chunks):
        rows = slice(c * C, (c + 1) * C)
        b_last = b[(c + 1) * C - 1:(c + 1) * C, :]
        kd_ref[rows, :] = (k[rows, :] * jnp.exp(b_last - b[rows, :])).astype(BF16)
        dec_ref[c:c + 1, :] = jnp.exp(b_last)

    r64 = lax.broadcasted_iota(jnp.int32, (C, C), 0)
    c64 = lax.broadcasted_iota(jnp.int32, (C, C), 1)
    lower = c64 <= r64

    for c in range(n_chunks):
        rows = slice(c * C, (c + 1) * C)
        dec = dec_ref[c:c + 1, :]
        for h in range(REC_HEADS):
            ks = slice(h * HEAD_DIM, (h + 1) * HEAD_DIM)
            vs = slice(h * VAL_DIM, (h + 1) * VAL_DIM)
            qt = qt_ref[rows, ks]
            kt = kt_ref[rows, ks]
            kd = kd_ref[rows, ks]
            v = v_ref[0, rows, vs]
            att = lax.dot_general(qt, kt, NT_DIMS, preferred_element_type=F32)
            att = jnp.where(lower, att, 0.0).astype(BF16)
            st = state_ref[h]
            o = (jnp.dot(att, v, preferred_element_type=F32)
                 + lax.dot_general(qt, st.astype(BF16), NT_DIMS, preferred_element_type=F32))
            kv_t = lax.dot_general(v, kd, TN_DIMS, preferred_element_type=F32)
            state_ref[h] = st * dec[:, ks] + kv_t
            ms = jnp.mean(o * o, axis=-1, keepdims=True)
            o_ref[0, rows, vs] = (o * lax.rsqrt(ms + EPS) * gn_ref[:, vs]).astype(BF16)


def _gla(z, glog, gla_g, tg):
    B, S, _ = z.shape
    kw = REC_HEADS * HEAD_DIM
    vw = REC_HEADS * VAL_DIM
    return pl.pallas_call(
        _gla_kernel,
        grid=(B, S // tg),
        in_specs=[
            pl.BlockSpec((1, tg, kw), lambda b, s: (b, s, 10)),
            pl.BlockSpec((1, tg, kw), lambda b, s: (b, s, 11)),
            pl.BlockSpec((1, tg, vw), lambda b, s: (b, s, 4)),
            pl.BlockSpec((1, tg, kw), lambda b, s: (b, s, 0)),
            pl.BlockSpec((1, vw), lambda b, s: (0, 0)),
        ],
        out_specs=pl.BlockSpec((1, tg, vw), lambda b, s: (b, s, 0)),
        out_shape=jax.ShapeDtypeStruct((B, S, vw), BF16),
        scratch_shapes=[
            pltpu.VMEM((tg, kw), BF16), pltpu.VMEM((tg, kw), BF16), pltpu.VMEM((tg, kw), BF16),
            pltpu.VMEM((tg // GLA_CHUNK, kw), F32),
            pltpu.VMEM((REC_HEADS, VAL_DIM, HEAD_DIM), F32),
        ],
        compiler_params=_params("arbitrary", "arbitrary"),
        name="gla",
    )(z, z, z, glog, gla_g)


def _ret_kernel(q_ref, k_ref, v_ref, gw_ref, gb_ref, o_ref, state_ref, dmat_ref):
    tg = q_ref.shape[1]
    C = RET_CHUNK
    n_chunks = tg // C

    @pl.when(pl.program_id(1) == 0)
    def _():
        state_ref[...] = jnp.zeros_like(state_ref)

    ri = lax.broadcasted_iota(jnp.int32, (C, C), 0)
    ci = lax.broadcasted_iota(jnp.int32, (C, C), 1)
    diff = (ri - ci).astype(F32)
    idx = lax.broadcasted_iota(jnp.int32, (C, 1), 0).astype(F32)

    for h in range(REC_HEADS):
        log_g = math.log(1.0 - 2.0 ** (-5.0 - h))
        dmat_ref[h] = jnp.where(diff >= 0, jnp.exp(jnp.maximum(diff, 0.0) * log_g), 0.0)

    for c in range(n_chunks):
        rows = slice(c * C, (c + 1) * C)
        for h in range(REC_HEADS):
            log_g = math.log(1.0 - 2.0 ** (-5.0 - h))
            dmat = dmat_ref[h]
            xi = jnp.exp((idx + 1.0) * log_g)
            zeta = jnp.exp((C - 1.0 - idx) * log_g)
            ks = slice(h * HEAD_DIM, (h + 1) * HEAD_DIM)
            vs = slice(h * VAL_DIM, (h + 1) * VAL_DIM)
            q = q_ref[0, rows, ks]
            k = k_ref[0, rows, ks]
            v = v_ref[0, rows, vs]
            att = lax.dot_general(q, k, NT_DIMS, preferred_element_type=F32) * dmat
            st = state_ref[h]
            o = (jnp.dot(att.astype(BF16), v, preferred_element_type=F32)
                 + xi * lax.dot_general(q, st.astype(BF16), NT_DIMS, preferred_element_type=F32))
            vz = (v.astype(F32) * zeta).astype(BF16)
            kv_t = lax.dot_general(vz, k, TN_DIMS, preferred_element_type=F32)
            state_ref[h] = st * math.exp(C * log_g) + kv_t
            o = o - jnp.mean(o, axis=-1, keepdims=True)
            ms = jnp.mean(o * o, axis=-1, keepdims=True)
            o = o * lax.rsqrt(ms + EPS) * gw_ref[:, vs] + gb_ref[:, vs]
            o_ref[0, rows, vs] = o.astype(BF16)


def _retention(z, gn_w, gn_b, tg):
    B, S, _ = z.shape
    kw = REC_HEADS * HEAD_DIM
    vw = REC_HEADS * VAL_DIM
    return pl.pallas_call(
        _ret_kernel,
        grid=(B, S // tg),
        in_specs=[
            pl.BlockSpec((1, tg, kw), lambda b, s: (b, s, 6)),
            pl.BlockSpec((1, tg, kw), lambda b, s: (b, s, 7)),
            pl.BlockSpec((1, tg, vw), lambda b, s: (b, s, 2)),
            pl.BlockSpec((1, vw), lambda b, s: (0, 0)),
            pl.BlockSpec((1, vw), lambda b, s: (0, 0)),
        ],
        out_specs=pl.BlockSpec((1, tg, vw), lambda b, s: (b, s, 0)),
        out_shape=jax.ShapeDtypeStruct((B, S, vw), BF16),
        scratch_shapes=[pltpu.VMEM((REC_HEADS, VAL_DIM, HEAD_DIM), F32),
                        pltpu.VMEM((REC_HEADS, RET_CHUNK, RET_CHUNK), F32)],
        compiler_params=_params("arbitrary", "arbitrary"),
        name="retention",
    )(z, z, z, gn_w, gn_b)


def _dil_kernel(q_ref, k_ref, v_ref, o_ref, lse_ref,
                vt_ref, qt_ref, s_ref, mx_ref, bias_ref, ot_ref, lt_ref, *, blk, nblk, span):
    n = pl.program_id(2)
    m = k_ref.shape[2]
    pairs = ATT_HEADS // 2
    chains = [(g, half) for g in range(pairs) for half in range(2)]
    feat = lax.broadcasted_iota(jnp.int32, (LANES, blk), 0)
    ki = lax.broadcasted_iota(jnp.int32, (2 * blk, blk), 0)
    qi = lax.broadcasted_iota(jnp.int32, (2 * blk, blk), 1)

    @pl.when(n == 0)
    def _():
        for g in range(pairs):
            vt_ref[g * DIL_VROWS + LANES:(g + 1) * DIL_VROWS, :] = jnp.ones(
                (DIL_VROWS - LANES, m), BF16)

        def build(kb, _):
            r0 = pl.multiple_of(kb * blk, blk)
            for g in range(pairs):
                v = v_ref[0, 0, pl.ds(r0, blk), g * LANES:(g + 1) * LANES].astype(F32)
                vt_ref[g * DIL_VROWS:g * DIL_VROWS + LANES, pl.ds(r0, blk)] = v.T.astype(BF16)
            return 0

        lax.fori_loop(0, m // blk, build, 0)

    starts = []
    for t in range(nblk):
        qb = n * nblk + t
        start = pl.multiple_of(jnp.maximum(qb - 1, 0) * blk, blk)
        starts.append(start)
        rel = qb * blk + qi - start - ki
        bias_ref[t] = jnp.where(rel >= 0, jnp.where(rel <= span, 0.0, -jnp.inf), -jnp.inf)
        for g in range(pairs):
            q_t = q_ref[0, 0, t * blk:(t + 1) * blk, g * LANES:(g + 1) * LANES].astype(F32).T
            qt_ref[t, 2 * g] = jnp.where(feat < HEAD_DIM, q_t, 0.0).astype(BF16)
            qt_ref[t, 2 * g + 1] = jnp.where(feat < HEAD_DIM, 0.0, q_t).astype(BF16)

    for t in range(nblk):
        for c, (g, half) in enumerate(chains):
            k = k_ref[0, 0, pl.ds(starts[t], 2 * blk), g * LANES:(g + 1) * LANES]
            s = jnp.dot(k, qt_ref[t, c], preferred_element_type=F32) + bias_ref[t]
            s_ref[t, c] = s
            mx_ref[t, c] = jnp.max(s, axis=0, keepdims=True)

    for t in range(nblk):
        lt_ref[t] = jnp.zeros(lt_ref.shape[1:], F32)
        for c, (g, half) in enumerate(chains):
            mx = mx_ref[t, c]
            p = jnp.exp2((s_ref[t, c] - mx).astype(BF16))
            acc = jnp.dot(vt_ref[g * DIL_VROWS:(g + 1) * DIL_VROWS, pl.ds(starts[t], 2 * blk)], p,
                          preferred_element_type=F32)
            den = acc[LANES:LANES + 1, :]
            ot_ref[t, c * HEAD_DIM:(c + 1) * HEAD_DIM, :] = (
                acc[half * HEAD_DIM:(half + 1) * HEAD_DIM, :] * (1.0 / den))
            lt_ref[t, c:c + 1, :] = (mx + jnp.log2(den)) * (1.0 / LOG2E)
        qrows = slice(t * blk, (t + 1) * blk)
        o_ref[0, 0, qrows, :] = ot_ref[t].T.astype(BF16)
        lse_ref[0, 0, qrows, :] = lt_ref[t].T


def _dilated_branch(src, tile0, window, dil, nblk):
    B, _, m, _ = src.shape
    blk = DIL_BLOCK
    tq = nblk * blk
    return pl.pallas_call(
        functools.partial(_dil_kernel, blk=blk, nblk=nblk, span=window // dil),
        grid=(B, dil, m // tq),
        in_specs=[
            pl.BlockSpec((1, 1, tq, COL_TILE), lambda b, r, n: (b, r, n, tile0)),
            pl.BlockSpec((1, 1, m, COL_TILE), lambda b, r, n: (b, r, 0, tile0 + 1)),
            pl.BlockSpec((1, 1, m, COL_TILE), lambda b, r, n: (b, r, 0, tile0 + 2)),
        ],
        out_specs=[
            pl.BlockSpec((1, 1, tq, COL_TILE), lambda b, r, n: (b, r, n, 0)),
            pl.BlockSpec((1, 1, tq, LANES), lambda b, r, n: (b, r, n, 0)),
        ],
        out_shape=[
            jax.ShapeDtypeStruct((B, dil, m, COL_TILE), BF16),
            jax.ShapeDtypeStruct((B, dil, m, LANES), F32),
        ],
        scratch_shapes=[pltpu.VMEM((ATT_HEADS // 2 * DIL_VROWS, m), BF16),
                        pltpu.VMEM((nblk, ATT_HEADS, LANES, blk), BF16),
                        pltpu.VMEM((nblk, ATT_HEADS, 2 * blk, blk), F32),
                        pltpu.VMEM((nblk, ATT_HEADS, 1, blk), F32),
                        pltpu.VMEM((nblk, 2 * blk, blk), F32),
                        pltpu.VMEM((nblk, ATT_HEADS * HEAD_DIM, blk), F32),
                        pltpu.VMEM((nblk, LANES, blk), F32)],
        compiler_params=_params("arbitrary", "arbitrary", "arbitrary"),
        name=f"dilated_{dil}",
    )(src, src, src)


def _silu(x):
    return x / (1.0 + jnp.exp(-x))


def _out_tail(mix_a, mix_b, gate_ref, h_ref, w_ref, fg_ref, out_ref, final):
    half = mix_a.shape[1]
    gate = _silu(gate_ref[0].astype(F32))
    a = (mix_a * gate[:, :half]).astype(BF16)
    b = (mix_b * gate[:, half:]).astype(BF16)
    y = (jnp.dot(a, w_ref[:half, :], preferred_element_type=F32)
         + jnp.dot(b, w_ref[half:, :], preferred_element_type=F32))
    hn = h_ref[0] + y
    if final:
        hn = _rms_normed(hn, fg_ref[...])
    out_ref[0] = hn


def _out_even_kernel(oa_ref, ob_ref, gate_ref, h_ref, w_ref, fg_ref, out_ref, *, final):
    _out_tail(oa_ref[0].astype(F32), ob_ref[0].astype(F32), gate_ref, h_ref, w_ref, fg_ref,
              out_ref, final)


def _out_odd_kernel(oc_ref, d1_ref, d4_ref, d16_ref, l1_ref, l4_ref, l16_ref, e_ref,
                    gate_ref, h_ref, w_ref, fg_ref, out_ref, o4_ref, o16_ref, ls4_ref, ls16_ref,
                    *, final):
    tm = oc_ref.shape[1]
    chunks = oc_ref.shape[2] // LANES
    for dil, src, lsrc, dst, ldst in ((4, d4_ref, l4_ref, o4_ref, ls4_ref),
                                      (16, d16_ref, l16_ref, o16_ref, ls16_ref)):
        for res in range(dil):
            token_rows = pl.ds(res, tm // dil, stride=dil)
            ldst[token_rows, :] = lsrc[0, res]
            for c in range(chunks):
                dst[c, token_rows, :] = src[0, res, :, c * LANES:(c + 1) * LANES].astype(F32)
    l1, l2, l3 = l1_ref[0], ls4_ref[...], ls16_ref[...]
    mx = jnp.maximum(jnp.maximum(l1, l2), l3)
    e1, e2, e3 = jnp.exp(l1 - mx), jnp.exp(l2 - mx), jnp.exp(l3 - mx)
    inv = 1.0 / (e1 + e2 + e3)
    e = e_ref[...]
    expand = lambda w: sum(jnp.dot(t, e, preferred_element_type=F32) for t in _split2(w))
    o4 = jnp.concatenate([o4_ref[c] for c in range(chunks)], axis=-1)
    o16 = jnp.concatenate([o16_ref[c] for c in range(chunks)], axis=-1)
    o_d = (expand(e1 * inv) * d1_ref[0].astype(F32)
           + expand(e2 * inv) * o4
           + expand(e3 * inv) * o16)
    _out_tail(oc_ref[0].astype(F32), o_d, gate_ref, h_ref, w_ref, fg_ref, out_ref, final)


def _out_proj(mix_parts, lses, z, h, w_out, final_g, final, tm):
    B, S, D = h.shape
    half = COL_TILE
    tok = lambda b, s: (b, s, 0)
    const = lambda b, s: (0, 0)
    part_spec = pl.BlockSpec((1, tm, half), tok)
    tail_specs = [
        pl.BlockSpec((1, tm, 2 * half), tok),
        pl.BlockSpec((1, tm, D), tok),
        pl.BlockSpec((2 * half, D), const),
        pl.BlockSpec((1, D), const),
    ]
    scratch = []
    if lses:
        expand = jnp.pad(jnp.repeat(jnp.eye(ATT_HEADS, dtype=BF16), HEAD_DIM, axis=1),
                         ((0, LANES - ATT_HEADS), (0, 0)))
        kern = functools.partial(_out_odd_kernel, final=final)
        res_spec = lambda dil, c: pl.BlockSpec((1, dil, tm // dil, c), lambda b, s: (b, 0, s, 0))
        in_specs = ([part_spec] * 2 + [res_spec(4, half), res_spec(16, half)]
                    + [pl.BlockSpec((1, tm, LANES), tok), res_spec(4, LANES), res_spec(16, LANES)]
                    + [pl.BlockSpec((LANES, half), const)] + tail_specs)
        args = (*mix_parts, *lses, expand, z, h, w_out, final_g)
        scratch = [pltpu.VMEM((half // LANES, tm, LANES), F32),
                   pltpu.VMEM((half // LANES, tm, LANES), F32),
                   pltpu.VMEM((tm, LANES), F32), pltpu.VMEM((tm, LANES), F32)]
    else:
        kern = functools.partial(_out_even_kernel, final=final)
        in_specs = [part_spec] * 2 + tail_specs
        args = (*mix_parts, z, h, w_out, final_g)
    return pl.pallas_call(
        kern,
        grid=(B, S // tm),
        in_specs=in_specs,
        out_specs=pl.BlockSpec((1, tm, D), tok),
        out_shape=jax.ShapeDtypeStruct((B, S, D), F32),
        scratch_shapes=scratch,
        compiler_params=_params("arbitrary", "arbitrary"),
        name="out_proj_odd" if lses else "out_proj_even",
    )(*args)


def _prep_even(w_in, b_f, w_lr, b_lr):
    fw = ATT_HEADS * HEAD_DIM
    kw = REC_HEADS * HEAD_DIM
    vw = REC_HEADS * VAL_DIM
    sizes = (fw, fw, fw, ATT_HEADS, kw, kw, vw, GLA_RANK, fw + vw)
    offs = np.cumsum((0,) + sizes)
    fq, fk, fv, ff, gq, gk, gv, glr, gate = (w_in[:, offs[i]:offs[i + 1]] for i in range(9))
    scale = HEAD_DIM ** -0.5
    D = w_in.shape[0]

    def pad_heads(w, width):
        w = w.reshape(D, ATT_HEADS, HEAD_DIM)
        return jnp.pad(w, ((0, 0), (0, 0), (0, width - HEAD_DIM))).reshape(D, ATT_HEADS * width)

    w_main = jnp.concatenate([gate, pad_heads(fk, FOX_PAD), gv, gq * scale, gk], axis=1).astype(BF16)
    wq_t = pad_heads(fq * (scale * LOG2E), FOX_PAD).T.astype(BF16)
    wv_t = pad_heads(fv, FOX_VROWS).T.astype(BF16)
    q_aug = np.zeros((ATT_HEADS, FOX_PAD, LANES), np.float32)
    q_aug[:, HEAD_DIM:HEAD_DIM + 3, :] = -1.0
    v_aug = np.zeros((ATT_HEADS, FOX_VROWS, LANES), np.float32)
    v_aug[:, HEAD_DIM, :] = 1.0
    pad = LANES - ATT_HEADS - GLA_RANK
    w_small = jnp.pad(jnp.concatenate([ff, glr], axis=1), ((0, 0), (0, pad))).astype(BF16)
    b_f_pad = jnp.pad(b_f, (0, LANES - ATT_HEADS)).reshape(1, LANES)
    w_lr_pad = jnp.pad(w_lr, ((ATT_HEADS, pad), (0, 0))).astype(BF16)
    place = np.zeros((3, LANES, ATT_HEADS * FOX_PAD), np.float32)
    for i in range(3):
        for h in range(ATT_HEADS):
            place[i, h, h * FOX_PAD + HEAD_DIM + i] = 1.0
    return (w_main, wq_t, wv_t, w_small, b_f_pad, w_lr_pad, b_lr.reshape(1, kw),
            jnp.asarray(place, BF16), jnp.asarray(q_aug.reshape(-1, LANES)),
            jnp.asarray(v_aug.reshape(-1, LANES)))


def _prep_odd(w_in):
    kw = REC_HEADS * HEAD_DIM
    vw = REC_HEADS * VAL_DIM
    dw = ATT_HEADS * HEAD_DIM
    sizes = (kw, kw, vw, dw, dw, dw, vw + dw)
    offs = np.cumsum((0,) + sizes)
    rq, rk, rv, dq, dk, dv, gate = (w_in[:, offs[i]:offs[i + 1]] for i in range(7))
    scale = HEAD_DIM ** -0.5
    return jnp.concatenate([gate, rv, rq * scale, rk, dq * (scale * LOG2E), dk, dv], axis=1).astype(BF16)


def _rope_tables(S):
    inv = jnp.power(ROPE_THETA, -jnp.arange(0, HEAD_DIM, 2, dtype=F32) / HEAD_DIM)
    ang = jnp.arange(S, dtype=F32)[:, None] * inv[None, :]
    cos, sin = jnp.cos(ang), jnp.sin(ang)
    reps = LANES // HEAD_DIM
    cos_t = jnp.tile(jnp.concatenate([cos, cos], axis=1), (1, reps))
    sin_t = jnp.tile(jnp.concatenate([-sin, sin], axis=1), (1, reps))
    return cos_t, sin_t


def _tile(S, pref):
    return min(S, pref)


def _even_layer(h, norm_g, w_in, b_f, w_lr, b_lr, gla_g, w_out, final_g, final):
    B, S, D = h.shape
    z, q_t, v_t, glog = _in_proj_even(h, norm_g.reshape(1, D), *_prep_even(w_in, b_f, w_lr, b_lr),
                                      _tile(S, 512))
    o_a = _fox_attention(z, q_t, v_t, _tile(S, 256))
    o_b = _gla(z, glog, gla_g.reshape(1, -1), _tile(S, 512))
    return _out_proj((o_a, o_b), (), z, h, w_out.astype(BF16), final_g.reshape(1, D), final,
                     _tile(S, 512))


def _odd_layer(h, norm_g, w_in, gn_w, gn_b, w_out, final_g, final, cos, sin):
    B, S, D = h.shape
    z, zd4, zd16 = _in_proj_odd(h, norm_g.reshape(1, D), _prep_odd(w_in), cos, sin, _tile(S, 512))
    o_c = _retention(z, gn_w.reshape(1, -1), gn_b.reshape(1, -1), _tile(S, 512))
    sources = {1: (z.reshape(B, 1, S, Z_WIDTH), DIL_TILE0), 4: (zd4, 0), 16: (zd16, 0)}
    outs, lses = zip(*[_dilated_branch(*sources[dil], window, dil, DIL_QBLOCKS)
                       for window, dil in DIL_PATTERNS])
    o1, l1 = outs[0].reshape(B, S, COL_TILE), lses[0].reshape(B, S, LANES)
    return _out_proj((o_c, o1, *outs[1:]), (l1, *lses[1:]), z, h, w_out.astype(BF16),
                     final_g.reshape(1, D), final, _tile(S, 512))


def kernel(x, norm_even, w_in_even, b_f_even, w_lr_even, b_lr_even, gla_norm_even, w_out_even,
           norm_odd, w_in_odd, ret_gn_w_odd, ret_gn_b_odd, w_out_odd, final_norm):
    depth = norm_even.shape[0] + norm_odd.shape[0]
    cos, sin = _rope_tables(x.shape[1])
    h = x
    for i in range(depth):
        j = i // 2
        final = i == depth - 1
        if i % 2 == 0:
            h = _even_layer(h, norm_even[j], w_in_even[j], b_f_even[j], w_lr_even[j],
                            b_lr_even[j], gla_norm_even[j], w_out_even[j], final_norm, final)
        else:
            h = _odd_layer(h, norm_odd[j], w_in_odd[j], ret_gn_w_odd[j], ret_gn_b_odd[j],
                           w_out_odd[j], final_norm, final, cos, sin)
    return h
```

```python
import functools
import math

import numpy as np
import jax
import jax.numpy as jnp
from jax import lax
from jax.experimental import pallas as pl
from jax.experimental.pallas import tpu as pltpu

F32 = jnp.float32
BF16 = jnp.bfloat16

EPS = 1e-6
ROPE_THETA = 10000.0
HEAD_DIM = 64
VAL_DIM = 128
ATT_HEADS = 8
REC_HEADS = 4
GLA_RANK = 16
GLA_TAU = 16.0
GLA_CHUNK = 64
RET_CHUNK = 128
DIL_BLOCK = 128
DIL_PATTERNS = ((128, 1), (512, 4), (2048, 16))

Z_WIDTH = 3584
DIL_TILE0 = 4
DIL_QBLOCKS = 4
DIL_VROWS = 144
ZE_WIDTH = 3072
FOX_PAD = 128
FOX_VROWS = 80
LOG2E = 1.4426950408889634
COL_TILE = 512
LANES = 128
V7X_VMEM_LIMIT = 56 * 1024 * 1024

NT_DIMS = (((1,), (1,)), ((), ()))
TN_DIMS = (((0,), (0,)), ((), ()))


def _params(*sem):
    return pltpu.CompilerParams(dimension_semantics=sem, vmem_limit_bytes=V7X_VMEM_LIMIT)


def _split2(x):
    hi = x.astype(BF16)
    lo = (x - hi.astype(F32)).astype(BF16)
    return hi, lo


def _split3(x):
    hi, lo = _split2(x)
    lo2 = (x - hi.astype(F32) - lo.astype(F32)).astype(BF16)
    return hi, lo, lo2


def _log_sigmoid(x):
    return jnp.minimum(x, 0.0) - jnp.log(1.0 + jnp.exp(-jnp.abs(x)))


def _rms_normed(x, g):
    ms = jnp.mean(x * x, axis=-1, keepdims=True)
    return x * lax.rsqrt(ms + EPS) * g


def _in_even_body(x, g_ref, w_ref, wq_ref, wv_ref, ws_ref, bf_ref, wlr_ref, blr_ref, place_ref,
                  qaug_ref, vaug_ref, z_ref, qt_ref, vt_ref, glog_ref, u_ref, carry_ref):
    tm = x.shape[0]

    @pl.when(pl.program_id(1) == 0)
    def _():
        carry_ref[...] = jnp.zeros_like(carry_ref)

    u_ref[...] = _rms_normed(x, g_ref[...]).astype(BF16)
    u = u_ref[...]

    zs = jnp.dot(u, ws_ref[...], preferred_element_type=F32)
    logf = _log_sigmoid(zs + bf_ref[...])
    row = lax.broadcasted_iota(jnp.int32, (tm, tm), 0)
    col = lax.broadcasted_iota(jnp.int32, (tm, tm), 1)
    tril = jnp.where(row >= col, 1.0, 0.0).astype(BF16)
    csum = sum(jnp.dot(tril, t, preferred_element_type=F32) for t in _split3(logf)) + carry_ref[...]
    carry_ref[...] = csum[tm - 1:tm, :]
    hi, lo, lo2 = (t.astype(F32) for t in _split3(csum * LOG2E))
    lane = lax.broadcasted_iota(jnp.int32, (tm, LANES), 1)
    packed = jnp.where(lane < ATT_HEADS, hi,
                       jnp.where(lane < 2 * ATT_HEADS, pltpu.roll(lo, ATT_HEADS, 1),
                                 jnp.where(lane < 3 * ATT_HEADS, pltpu.roll(lo2, 2 * ATT_HEADS, 1), 0.0)))
    c_cols = jnp.dot(packed.astype(BF16), place_ref[...], preferred_element_type=F32)

    glr = jnp.dot(zs.astype(BF16), wlr_ref[...], preferred_element_type=F32)
    glog_ref[0] = _log_sigmoid(glr + blr_ref[...]) * (1.0 / GLA_TAU)

    for j in range(ZE_WIDTH // COL_TILE):
        cols = slice(j * COL_TILE, (j + 1) * COL_TILE)
        r = jnp.dot(u, w_ref[:, cols], preferred_element_type=F32)
        if j in (2, 3):
            r = r + c_cols[:, (j - 2) * COL_TILE:(j - 1) * COL_TILE]
        z_ref[0, :, cols] = r.astype(BF16)

    def feature_major(w_t_ref, aug_ref, out_ref, tile):
        for j in range(w_t_ref.shape[0] // tile):
            rows = slice(j * tile, (j + 1) * tile)
            r = lax.dot_general(w_t_ref[rows, :], u, NT_DIMS, preferred_element_type=F32)
            aug = aug_ref[rows, :]
            for c in range(tm // LANES):
                lanes = slice(c * LANES, (c + 1) * LANES)
                out_ref[0, rows, lanes] = (r[:, lanes] + aug).astype(BF16)

    feature_major(wq_ref, qaug_ref, qt_ref, COL_TILE)
    feature_major(wv_ref, vaug_ref, vt_ref, ATT_HEADS * FOX_VROWS // 2)


def _rope_tile(x, cos, sin):
    lane = lax.broadcasted_iota(jnp.int32, (x.shape[0], LANES), 1)
    first_half = (lane % HEAD_DIM) < (HEAD_DIM // 2)
    outs = []
    for c in range(x.shape[1] // LANES):
        xc = x[:, c * LANES:(c + 1) * LANES]
        partner = jnp.where(first_half,
                            pltpu.roll(xc, LANES - HEAD_DIM // 2, 1),
                            pltpu.roll(xc, HEAD_DIM // 2, 1))
        outs.append(xc * cos + partner * sin)
    return jnp.concatenate(outs, axis=-1)


def _in_odd_body(x, g_ref, w_ref, cos_ref, sin_ref, z_ref, zd4_ref, zd16_ref, u_ref, r_ref):
    tm = x.shape[0]
    u_ref[...] = _rms_normed(x, g_ref[...]).astype(BF16)
    u = u_ref[...]
    cos = cos_ref[...]
    sin = sin_ref[...]
    for j in range(Z_WIDTH // COL_TILE):
        cols = slice(j * COL_TILE, (j + 1) * COL_TILE)
        r = jnp.dot(u, w_ref[:, cols], preferred_element_type=F32)
        if j in (3, 4, 5):
            r = _rope_tile(r, cos, sin)
        z_ref[0, :, cols] = r.astype(BF16)
        if j >= DIL_TILE0:
            t = j - DIL_TILE0
            for c in range(COL_TILE // LANES):
                slot = t * (COL_TILE // LANES) + c
                r_ref[slot] = r[:, c * LANES:(c + 1) * LANES]
                dcols = slice(t * COL_TILE + c * LANES, t * COL_TILE + (c + 1) * LANES)
                for dil, ref in ((4, zd4_ref), (16, zd16_ref)):
                    for res in range(dil):
                        ref[0, res, :, dcols] = (
                            r_ref[slot, pl.ds(res, tm // dil, stride=dil), :].astype(BF16))


def _fox_kernel(qt_ref, kp_ref, vt_ref, o_ref, s_ref, acc_ref, mp_ref, mc_ref, ot_ref, mask_ref, *, blk):
    i = pl.program_id(1)
    heads = range(ATT_HEADS)
    rows = lambda h: slice(h * FOX_PAD, (h + 1) * FOX_PAD)
    vrows = lambda h: slice(h * FOX_VROWS, (h + 1) * FOX_VROWS)

    @pl.when(i == 0)
    def _():
        key = lax.broadcasted_iota(jnp.int32, (blk, blk), 0)
        qry = lax.broadcasted_iota(jnp.int32, (blk, blk), 1)
        mask_ref[0] = jnp.zeros((blk, blk), F32)
        mask_ref[1] = jnp.where(key <= qry, 0.0, -jnp.inf)

    def score_stage(h, j):
        ks = pl.multiple_of(j * blk, blk)
        s = jnp.dot(kp_ref[0, pl.ds(ks, blk), rows(h)], qt_ref[0, rows(h), :],
                    preferred_element_type=F32)
        s = s + mask_ref[jnp.where(j == i, 1, 0)]
        s_ref[h] = s
        m_old = mc_ref[h]
        mp_ref[h] = m_old
        mc_ref[h] = jnp.maximum(m_old, jnp.max(s, axis=0, keepdims=True))

    def value_stage(h, j):
        ks = pl.multiple_of(j * blk, blk)
        m_new = mc_ref[h]
        alpha = jnp.exp2(mp_ref[h] - m_new)
        p = jnp.exp2((s_ref[h] - m_new).astype(BF16))
        pv = jnp.dot(vt_ref[0, vrows(h), pl.ds(ks, blk)], p, preferred_element_type=F32)
        acc_ref[h] = alpha * acc_ref[h] + pv

    def trip(j):
        for h in heads:
            value_stage(h, j - 1)
            score_stage(h, j)

    for h in heads:
        acc_ref[h] = jnp.zeros(acc_ref.shape[1:], F32)
        mc_ref[h] = jnp.full((1, blk), -jnp.inf, F32)
        score_stage(h, 0)

    odd = i % 2

    @pl.when(odd == 1)
    def _():
        trip(1)

    def pair(t, _):
        j = 1 + odd + 2 * t
        trip(j)
        trip(j + 1)
        return 0

    lax.fori_loop(0, i // 2, pair, 0)

    for h in heads:
        value_stage(h, i)
        acc = acc_ref[h]
        ot_ref[h * HEAD_DIM:(h + 1) * HEAD_DIM, :] = (
            acc[:HEAD_DIM, :] / acc[HEAD_DIM:HEAD_DIM + 1, :])
    o_ref[0] = ot_ref[...].T.astype(BF16)


def _fox_attention(z, q_t, v_t, blk):
    B, S, _ = z.shape
    fox_rows = ATT_HEADS * FOX_PAD
    val_rows = ATT_HEADS * FOX_VROWS
    return pl.pallas_call(
        functools.partial(_fox_kernel, blk=blk),
        grid=(B, S // blk),
        in_specs=[
            pl.BlockSpec((1, fox_rows, blk), lambda b, i: (b, 0, i)),
            pl.BlockSpec((1, S, fox_rows), lambda b, i: (b, 0, 1)),
            pl.BlockSpec((1, val_rows, S), lambda b, i: (b, 0, 0)),
        ],
        out_specs=pl.BlockSpec((1, blk, COL_TILE), lambda b, i: (b, i, 0)),
        out_shape=jax.ShapeDtypeStruct((B, S, COL_TILE), BF16),
        scratch_shapes=[pltpu.VMEM((ATT_HEADS, blk, blk), F32),
                        pltpu.VMEM((ATT_HEADS, FOX_VROWS, blk), F32),
                        pltpu.VMEM((ATT_HEADS, 1, blk), F32),
                        pltpu.VMEM((ATT_HEADS, 1, blk), F32),
                        pltpu.VMEM((ATT_HEADS * HEAD_DIM, blk), F32),
                        pltpu.VMEM((2, blk, blk), F32)],
        compiler_params=_params("arbitrary", "arbitrary"),
        name="fox_attention",
    )(q_t, z, v_t)


def _gla_kernel(q_ref, k_ref, v_ref, g_ref, gn_ref, o_ref,
                qt_ref, kt_ref, kd_ref, dec_ref, state_ref):
    tg = q_ref.shape[1]
    C = GLA_CHUNK
    n_chunks = tg // C

    @pl.when(pl.program_id(1) == 0)
    def _():
        state_ref[...] = jnp.zeros_like(state_ref)

    row = lax.broadcasted_iota(jnp.int32, (tg, tg), 0)
    col = lax.broadcasted_iota(jnp.int32, (tg, tg), 1)
    chunk_start = row & (-C)
    tril = jnp.where(col <= row, jnp.where(col >= chunk_start, 1.0, 0.0), 0.0).astype(BF16)
    ghi, glo = _split2(g_ref[0])
    b = (jnp.dot(tril, ghi, preferred_element_type=F32)
         + jnp.dot(tril, glo, preferred_element_type=F32))
    q = q_ref[0].astype(F32)
    k = k_ref[0].astype(F32)
    qt_ref[...] = (q * jnp.exp(b)).astype(BF16)
    kt_ref[...] = (k * jnp.exp(-b)).astype(BF16)
    for c in range(n_chunks):
        rows = slice(c * C, (c + 1) * C)
        b_last = b[(c + 1) * C - 1:(c + 1) * C, :]
        kd_ref[rows, :] = (k[rows, :] * jnp.exp(b_last - b[rows, :])).astype(BF16)
        dec_ref[c:c + 1, :] = jnp.exp(b_last)

    r64 = lax.broadcasted_iota(jnp.int32, (C, C), 0)
    c64 = lax.broadcasted_iota(jnp.int32, (C, C), 1)
    lower = c64 <= r64

    for c in range(n_chunks):
        rows = slice(c * C, (c + 1) * C)
        dec = dec_ref[c:c + 1, :]
        for h in range(REC_HEADS):
            ks = slice(h * HEAD_DIM, (h + 1) * HEAD_DIM)
            vs = slice(h * VAL_DIM, (h + 1) * VAL_DIM)
            qt = qt_ref[rows, ks]
            kt = kt_ref[rows, ks]
            kd = kd_ref[rows, ks]
            v = v_ref[0, rows, vs]
            att = lax.dot_general(qt, kt, NT_DIMS, preferred_element_type=F32)
            att = jnp.where(lower, att, 0.0).astype(BF16)
            st = state_ref[h]
            o = (jnp.dot(att, v, preferred_element_type=F32)
                 + lax.dot_general(qt, st.astype(BF16), NT_DIMS, preferred_element_type=F32))
            kv_t = lax.dot_general(v, kd, TN_DIMS, preferred_element_type=F32)
            state_ref[h] = st * dec[:, ks] + kv_t
            ms = jnp.mean(o * o, axis=-1, keepdims=True)
            o_ref[0, rows, vs] = (o * lax.rsqrt(ms + EPS) * gn_ref[:, vs]).astype(BF16)


def _gla(z, glog, gla_g, tg):
    B, S, _ = z.shape
    kw = REC_HEADS * HEAD_DIM
    vw = REC_HEADS * VAL_DIM
    return pl.pallas_call(
        _gla_kernel,
        grid=(B, S // tg),
        in_specs=[
            pl.BlockSpec((1, tg, kw), lambda b, s: (b, s, 10)),
            pl.BlockSpec((1, tg, kw), lambda b, s: (b, s, 11)),
            pl.BlockSpec((1, tg, vw), lambda b, s: (b, s, 4)),
            pl.BlockSpec((1, tg, kw), lambda b, s: (b, s, 0)),
            pl.BlockSpec((1, vw), lambda b, s: (0, 0)),
        ],
        out_specs=pl.BlockSpec((1, tg, vw), lambda b, s: (b, s, 0)),
        out_shape=jax.ShapeDtypeStruct((B, S, vw), BF16),
        scratch_shapes=[
            pltpu.VMEM((tg, kw), BF16), pltpu.VMEM((tg, kw), BF16), pltpu.VMEM((tg, kw), BF16),
            pltpu.VMEM((tg // GLA_CHUNK, kw), F32),
            pltpu.VMEM((REC_HEADS, VAL_DIM, HEAD_DIM), F32),
        ],
        compiler_params=_params("arbitrary", "arbitrary"),
        name="gla",
    )(z, z, z, glog, gla_g)


def _ret_kernel(q_ref, k_ref, v_ref, gw_ref, gb_ref, o_ref, state_ref, dmat_ref, zeta_ref):
    tg = q_ref.shape[1]
    C = RET_CHUNK
    n_chunks = tg // C
    pairs = REC_HEADS // 2
    log_gamma = [math.log(1.0 - 2.0 ** (-5.0 - h)) for h in range(REC_HEADS)]

    @pl.when(pl.program_id(1) == 0)
    def _():
        state_ref[...] = jnp.zeros_like(state_ref)

    ri = lax.broadcasted_iota(jnp.int32, (C, C), 0)
    ci = lax.broadcasted_iota(jnp.int32, (C, C), 1)
    diff = (ri - ci).astype(F32)
    idx = lax.broadcasted_iota(jnp.int32, (C, 1), 0).astype(F32)
    low_lanes = ci < HEAD_DIM
    low_rows = ri < HEAD_DIM
    for h in range(REC_HEADS):
        dmat_ref[h] = jnp.where(diff >= 0, jnp.exp(jnp.maximum(diff, 0.0) * log_gamma[h]), 0.0)
    for g in range(pairs):
        zeta_ref[g] = jnp.where(low_lanes, jnp.exp((C - 1.0 - idx) * log_gamma[2 * g]),
                                jnp.exp((C - 1.0 - idx) * log_gamma[2 * g + 1]))

    for c in range(n_chunks):
        rows = slice(c * C, (c + 1) * C)
        for g in range(pairs):
            gs = slice(g * LANES, (g + 1) * LANES)
            q_pair = q_ref[0, rows, gs]
            k_f32 = k_ref[0, rows, gs].astype(F32)
            k_t = k_f32.T.astype(BF16)
            kz_t = (k_f32 * zeta_ref[g]).T.astype(BF16)
            st = state_ref[g]
            rhs = jnp.concatenate([k_t, st.astype(BF16)], axis=1)
            zero = jnp.zeros_like(q_pair)
            inc = []
            for half in range(2):
                h = 2 * g + half
                vs = slice(h * VAL_DIM, (h + 1) * VAL_DIM)
                xi = jnp.exp((idx + 1.0) * log_gamma[h])
                q = jnp.where(low_lanes, q_pair, zero) if half == 0 else jnp.where(low_lanes, zero, q_pair)
                v = v_ref[0, rows, vs]
                sc = jnp.dot(q, rhs, preferred_element_type=F32)
                att = (sc[:, :C] * dmat_ref[h]).astype(BF16)
                both = jnp.dot(jnp.concatenate([att, kz_t], axis=0), v, preferred_element_type=F32)
                o = both[:C, :] + xi * sc[:, C:]
                inc.append(both[C:, :])
                o = o - jnp.mean(o, axis=-1, keepdims=True)
                ms = jnp.mean(o * o, axis=-1, keepdims=True)
                o = o * lax.rsqrt(ms + EPS) * gw_ref[:, vs] + gb_ref[:, vs]
                o_ref[0, rows, vs] = o.astype(BF16)
            decay = jnp.where(low_rows, math.exp(C * log_gamma[2 * g]), math.exp(C * log_gamma[2 * g + 1]))
            state_ref[g] = st * decay + jnp.where(low_rows, inc[0], inc[1])


def _retention(z, gn_w, gn_b, tg):
    B, S, _ = z.shape
    kw = REC_HEADS * HEAD_DIM
    vw = REC_HEADS * VAL_DIM
    return pl.pallas_call(
        _ret_kernel,
        grid=(B, S // tg),
        in_specs=[
            pl.BlockSpec((1, tg, kw), lambda b, s: (b, s, 6)),
            pl.BlockSpec((1, tg, kw), lambda b, s: (b, s, 7)),
            pl.BlockSpec((1, tg, vw), lambda b, s: (b, s, 2)),
            pl.BlockSpec((1, vw), lambda b, s: (0, 0)),
            pl.BlockSpec((1, vw), lambda b, s: (0, 0)),
        ],
        out_specs=pl.BlockSpec((1, tg, vw), lambda b, s: (b, s, 0)),
        out_shape=jax.ShapeDtypeStruct((B, S, vw), BF16),
        scratch_shapes=[pltpu.VMEM((REC_HEADS // 2, 2 * HEAD_DIM, VAL_DIM), F32),
                        pltpu.VMEM((REC_HEADS, RET_CHUNK, RET_CHUNK), F32),
                        pltpu.VMEM((REC_HEADS // 2, RET_CHUNK, LANES), F32)],
        compiler_params=_params("arbitrary", "arbitrary"),
        name="retention",
    )(z, z, z, gn_w, gn_b)


def _dil_kernel(q_ref, k_ref, v_ref, o_ref, lse_ref,
                vt_ref, qt_ref, s_ref, mx_ref, bias_ref, ot_ref, lt_ref, *, blk, nblk, span):
    n = pl.program_id(2)
    m = k_ref.shape[2]
    pairs = ATT_HEADS // 2
    chains = [(g, half) for g in range(pairs) for half in range(2)]
    feat = lax.broadcasted_iota(jnp.int32, (LANES, blk), 0)
    ki = lax.broadcasted_iota(jnp.int32, (2 * blk, blk), 0)
    qi = lax.broadcasted_iota(jnp.int32, (2 * blk, blk), 1)

    @pl.when(n == 0)
    def _():
        for g in range(pairs):
            vt_ref[g * DIL_VROWS + LANES:(g + 1) * DIL_VROWS, :] = jnp.ones(
                (DIL_VROWS - LANES, m), BF16)

        def build(kb, _):
            r0 = pl.multiple_of(kb * blk, blk)
            for g in range(pairs):
                v = v_ref[0, 0, pl.ds(r0, blk), g * LANES:(g + 1) * LANES].astype(F32)
                vt_ref[g * DIL_VROWS:g * DIL_VROWS + LANES, pl.ds(r0, blk)] = v.T.astype(BF16)
            return 0

        lax.fori_loop(0, m // blk, build, 0)

    starts = []
    for t in range(nblk):
        qb = n * nblk + t
        start = pl.multiple_of(jnp.maximum(qb - 1, 0) * blk, blk)
        starts.append(start)
        rel = qb * blk + qi - start - ki
        bias_ref[t] = jnp.where(rel >= 0, jnp.where(rel <= span, 0.0, -jnp.inf), -jnp.inf)
        for g in range(pairs):
            q_t = q_ref[0, 0, t * blk:(t + 1) * blk, g * LANES:(g + 1) * LANES].astype(F32).T
            qt_ref[t, 2 * g] = jnp.where(feat < HEAD_DIM, q_t, 0.0).astype(BF16)
            qt_ref[t, 2 * g + 1] = jnp.where(feat < HEAD_DIM, 0.0, q_t).astype(BF16)

    for t in range(nblk):
        for c, (g, half) in enumerate(chains):
            k = k_ref[0, 0, pl.ds(starts[t], 2 * blk), g * LANES:(g + 1) * LANES]
            s = jnp.dot(k, qt_ref[t, c], preferred_element_type=F32) + bias_ref[t]
            s_ref[t, c] = s
            mx_ref[t, c] = jnp.max(s, axis=0, keepdims=True)

    for t in range(nblk):
        lt_ref[t] = jnp.zeros(lt_ref.shape[1:], F32)
        for c, (g, half) in enumerate(chains):
            mx = mx_ref[t, c]
            p = jnp.exp2((s_ref[t, c] - mx).astype(BF16))
            acc = jnp.dot(vt_ref[g * DIL_VROWS:(g + 1) * DIL_VROWS, pl.ds(starts[t], 2 * blk)], p,
                          preferred_element_type=F32)
            den = acc[LANES:LANES + 1, :]
            ot_ref[t, c * HEAD_DIM:(c + 1) * HEAD_DIM, :] = (
                acc[half * HEAD_DIM:(half + 1) * HEAD_DIM, :] * (1.0 / den))
            lt_ref[t, c:c + 1, :] = (mx + jnp.log2(den)) * (1.0 / LOG2E)
        qrows = slice(t * blk, (t + 1) * blk)
        o_ref[0, 0, qrows, :] = ot_ref[t].T.astype(BF16)
        lse_ref[0, 0, qrows, :] = lt_ref[t].T


def _dilated_branch(src, tile0, window, dil, nblk):
    B, _, m, _ = src.shape
    blk = DIL_BLOCK
    nblk = min(nblk, m // blk)
    tq = nblk * blk
    return pl.pallas_call(
        functools.partial(_dil_kernel, blk=blk, nblk=nblk, span=window // dil),
        grid=(B, dil, m // tq),
        in_specs=[
            pl.BlockSpec((1, 1, tq, COL_TILE), lambda b, r, n: (b, r, n, tile0)),
            pl.BlockSpec((1, 1, m, COL_TILE), lambda b, r, n: (b, r, 0, tile0 + 1)),
            pl.BlockSpec((1, 1, m, COL_TILE), lambda b, r, n: (b, r, 0, tile0 + 2)),
        ],
        out_specs=[
            pl.BlockSpec((1, 1, tq, COL_TILE), lambda b, r, n: (b, r, n, 0)),
            pl.BlockSpec((1, 1, tq, LANES), lambda b, r, n: (b, r, n, 0)),
        ],
        out_shape=[
            jax.ShapeDtypeStruct((B, dil, m, COL_TILE), BF16),
            jax.ShapeDtypeStruct((B, dil, m, LANES), F32),
        ],
        scratch_shapes=[pltpu.VMEM((ATT_HEADS // 2 * DIL_VROWS, m), BF16),
                        pltpu.VMEM((nblk, ATT_HEADS, LANES, blk), BF16),
                        pltpu.VMEM((nblk, ATT_HEADS, 2 * blk, blk), F32),
                        pltpu.VMEM((nblk, ATT_HEADS, 1, blk), F32),
                        pltpu.VMEM((nblk, 2 * blk, blk), F32),
                        pltpu.VMEM((nblk, ATT_HEADS * HEAD_DIM, blk), F32),
                        pltpu.VMEM((nblk, LANES, blk), F32)],
        compiler_params=_params("arbitrary", "arbitrary", "arbitrary"),
        name=f"dilated_{dil}",
    )(src, src, src)


def _silu(x):
    return x / (1.0 + jnp.exp(-x))


def _mix_out(mix_a, mix_b, gate_ref, h_ref, w_ref):
    half = mix_a.shape[1]
    gate = _silu(gate_ref[0].astype(F32))
    a = (mix_a * gate[:, :half]).astype(BF16)
    b = (mix_b * gate[:, half:]).astype(BF16)
    y = (jnp.dot(a, w_ref[:half, :], preferred_element_type=F32)
         + jnp.dot(b, w_ref[half:, :], preferred_element_type=F32))
    return h_ref[0] + y


def _dilated_mix(d1_ref, d4_ref, d16_ref, l1_ref, l4_ref, l16_ref, e_ref,
                 o4_ref, o16_ref, ls4_ref, ls16_ref):
    tm = d1_ref.shape[1]
    chunks = d1_ref.shape[2] // LANES
    for dil, src, lsrc, dst, ldst in ((4, d4_ref, l4_ref, o4_ref, ls4_ref),
                                      (16, d16_ref, l16_ref, o16_ref, ls16_ref)):
        for res in range(dil):
            token_rows = pl.ds(res, tm // dil, stride=dil)
            ldst[token_rows, :] = lsrc[0, res]
            for c in range(chunks):
                dst[c, token_rows, :] = src[0, res, :, c * LANES:(c + 1) * LANES].astype(F32)
    l1, l2, l3 = l1_ref[0], ls4_ref[...], ls16_ref[...]
    mx = jnp.maximum(jnp.maximum(l1, l2), l3)
    e1, e2, e3 = jnp.exp(l1 - mx), jnp.exp(l2 - mx), jnp.exp(l3 - mx)
    inv = 1.0 / (e1 + e2 + e3)
    e = e_ref[...]
    expand = lambda w: sum(jnp.dot(t, e, preferred_element_type=F32) for t in _split2(w))
    o4 = jnp.concatenate([o4_ref[c] for c in range(chunks)], axis=-1)
    o16 = jnp.concatenate([o16_ref[c] for c in range(chunks)], axis=-1)
    return (expand(e1 * inv) * d1_ref[0].astype(F32) + expand(e2 * inv) * o4
            + expand(e3 * inv) * o16)


N_IN = {"even": 11, "odd": 4}
N_OUT = {"even": 4, "odd": 3}
N_SCRATCH = {"even": 2, "odd": 2}


def _proj_kernel(*refs, prev, nxt, final):
    refs = list(refs)
    take = lambda n: [refs.pop(0) for _ in range(n)]
    if prev is None:
        (x_ref,) = take(1)
    elif prev == "even":
        oa_ref, ob_ref, gate_ref, h_ref, wout_ref = take(5)
    else:
        oc_ref, d1_ref, d4_ref, d16_ref, l1_ref, l4_ref, l16_ref, e_ref, gate_ref, h_ref, wout_ref = take(11)
    if final:
        (fg_ref,) = take(1)
    in_params = take(N_IN[nxt]) if nxt else []
    if prev is not None:
        (hout_ref,) = take(1)
    in_outs = take(N_OUT[nxt]) if nxt else []
    mix_scratch = take(4) if prev == "odd" else []
    in_scratch = take(N_SCRATCH[nxt]) if nxt else []

    if prev is None:
        x = x_ref[0]
    elif prev == "even":
        x = _mix_out(oa_ref[0].astype(F32), ob_ref[0].astype(F32), gate_ref, h_ref, wout_ref)
    else:
        o_d = _dilated_mix(d1_ref, d4_ref, d16_ref, l1_ref, l4_ref, l16_ref, e_ref, *mix_scratch)
        x = _mix_out(oc_ref[0].astype(F32), o_d, gate_ref, h_ref, wout_ref)
    if final:
        x = _rms_normed(x, fg_ref[...])
    if prev is not None:
        hout_ref[0] = x
    if nxt == "even":
        _in_even_body(x, *in_params, *in_outs, *in_scratch)
    elif nxt == "odd":
        _in_odd_body(x, *in_params, *in_outs, *in_scratch)


def _proj_call(h, prev, prev_args, nxt, nxt_args, final_g, tm):
    B, S, D = h.shape
    half = COL_TILE
    tok = lambda b, s: (b, s, 0)
    const2 = lambda b, s: (0, 0)
    once = dict(pipeline_mode=pl.Buffered(1))
    weight = lambda shape: pl.BlockSpec(shape, (lambda b, s: (0,) * len(shape)), **once)
    res_spec = lambda dil, c: pl.BlockSpec((1, dil, tm // dil, c), lambda b, s: (b, 0, s, 0))
    part = pl.BlockSpec((1, tm, half), tok)
    h_spec = pl.BlockSpec((1, tm, D), tok)
    args, in_specs, out_specs, out_shape, scratch = [], [], [], [], []

    if prev is None:
        args += [h]
        in_specs += [h_spec]
    elif prev == "even":
        o_a, o_b, z, w_out = prev_args
        args += [o_a, o_b, z, h, w_out]
        in_specs += [part, part, pl.BlockSpec((1, tm, 2 * half), tok), h_spec, weight((2 * half, D))]
    else:
        o_c, outs, lses, z, w_out = prev_args
        expand = jnp.pad(jnp.repeat(jnp.eye(ATT_HEADS, dtype=BF16), HEAD_DIM, axis=1),
                         ((0, LANES - ATT_HEADS), (0, 0)))
        args += [o_c, *outs, *lses, expand, z, h, w_out]
        in_specs += [part, part, res_spec(4, half), res_spec(16, half),
                     pl.BlockSpec((1, tm, LANES), tok), res_spec(4, LANES), res_spec(16, LANES),
                     weight((LANES, half)), pl.BlockSpec((1, tm, 2 * half), tok), h_spec,
                     weight((2 * half, D))]
    final = nxt is None
    if final:
        args += [final_g]
        in_specs += [weight((1, D))]

    if nxt == "even":
        g, w_main, wq_t, wv_t, w_small, b_f, w_lr, b_lr, place, q_aug, v_aug = nxt_args
        fox_rows, val_rows, kw = ATT_HEADS * FOX_PAD, ATT_HEADS * FOX_VROWS, w_lr.shape[1]
        args += list(nxt_args)
        in_specs += [weight(a.shape) for a in nxt_args]
    elif nxt == "odd":
        g, w_main, cos, sin = nxt_args
        args += list(nxt_args)
        in_specs += [weight(g.shape), weight(w_main.shape),
                     pl.BlockSpec((tm, LANES), lambda b, s: (s, 0)),
                     pl.BlockSpec((tm, LANES), lambda b, s: (s, 0))]

    if prev is not None:
        out_specs += [h_spec]
        out_shape += [jax.ShapeDtypeStruct((B, S, D), F32)]
    if nxt == "even":
        out_specs += [pl.BlockSpec((1, tm, ZE_WIDTH), tok),
                      pl.BlockSpec((1, fox_rows, tm), lambda b, s: (b, 0, s)),
                      pl.BlockSpec((1, val_rows, tm), lambda b, s: (b, 0, s)),
                      pl.BlockSpec((1, tm, kw), tok)]
        out_shape += [jax.ShapeDtypeStruct((B, S, ZE_WIDTH), BF16),
                      jax.ShapeDtypeStruct((B, fox_rows, S), BF16),
                      jax.ShapeDtypeStruct((B, val_rows, S), BF16),
                      jax.ShapeDtypeStruct((B, S, kw), F32)]
    elif nxt == "odd":
        dw = 3 * COL_TILE
        out_specs += [pl.BlockSpec((1, tm, Z_WIDTH), tok), res_spec(4, dw), res_spec(16, dw)]
        out_shape += [jax.ShapeDtypeStruct((B, S, Z_WIDTH), BF16),
                      jax.ShapeDtypeStruct((B, 4, S // 4, dw), BF16),
                      jax.ShapeDtypeStruct((B, 16, S // 16, dw), BF16)]

    if prev == "odd":
        scratch += [pltpu.VMEM((half // LANES, tm, LANES), F32),
                    pltpu.VMEM((half // LANES, tm, LANES), F32),
                    pltpu.VMEM((tm, LANES), F32), pltpu.VMEM((tm, LANES), F32)]
    if nxt == "even":
        scratch += [pltpu.VMEM((tm, D), BF16), pltpu.VMEM((1, LANES), F32)]
    elif nxt == "odd":
        scratch += [pltpu.VMEM((tm, D), BF16), pltpu.VMEM((3 * COL_TILE // LANES, tm, LANES), F32)]

    outs = pl.pallas_call(
        functools.partial(_proj_kernel, prev=prev, nxt=nxt, final=final),
        grid=(B, S // tm),
        in_specs=in_specs,
        out_specs=out_specs,
        out_shape=out_shape,
        scratch_shapes=scratch,
        compiler_params=_params("arbitrary", "arbitrary"),
        name=f"proj_{prev}_{nxt}",
    )(*args)
    return outs


def _prep_even(w_in, b_f, w_lr, b_lr):
    fw = ATT_HEADS * HEAD_DIM
    kw = REC_HEADS * HEAD_DIM
    vw = REC_HEADS * VAL_DIM
    sizes = (fw, fw, fw, ATT_HEADS, kw, kw, vw, GLA_RANK, fw + vw)
    offs = np.cumsum((0,) + sizes)
    fq, fk, fv, ff, gq, gk, gv, glr, gate = (w_in[:, offs[i]:offs[i + 1]] for i in range(9))
    scale = HEAD_DIM ** -0.5
    D = w_in.shape[0]

    def pad_heads(w, width):
        w = w.reshape(D, ATT_HEADS, HEAD_DIM)
        return jnp.pad(w, ((0, 0), (0, 0), (0, width - HEAD_DIM))).reshape(D, ATT_HEADS * width)

    w_main = jnp.concatenate([gate, pad_heads(fk, FOX_PAD), gv, gq * scale, gk], axis=1).astype(BF16)
    wq_t = pad_heads(fq * (scale * LOG2E), FOX_PAD).T.astype(BF16)
    wv_t = pad_heads(fv, FOX_VROWS).T.astype(BF16)
    q_aug = np.zeros((ATT_HEADS, FOX_PAD, LANES), np.float32)
    q_aug[:, HEAD_DIM:HEAD_DIM + 3, :] = -1.0
    v_aug = np.zeros((ATT_HEADS, FOX_VROWS, LANES), np.float32)
    v_aug[:, HEAD_DIM, :] = 1.0
    pad = LANES - ATT_HEADS - GLA_RANK
    w_small = jnp.pad(jnp.concatenate([ff, glr], axis=1), ((0, 0), (0, pad))).astype(BF16)
    b_f_pad = jnp.pad(b_f, (0, LANES - ATT_HEADS)).reshape(1, LANES)
    w_lr_pad = jnp.pad(w_lr, ((ATT_HEADS, pad), (0, 0))).astype(BF16)
    place = np.zeros((LANES, ATT_HEADS * FOX_PAD), np.float32)
    for i in range(3):
        for h in range(ATT_HEADS):
            place[ATT_HEADS * i + h, h * FOX_PAD + HEAD_DIM + i] = 1.0
    return (w_main, wq_t, wv_t, w_small, b_f_pad, w_lr_pad, b_lr.reshape(1, kw),
            jnp.asarray(place, BF16), jnp.asarray(q_aug.reshape(-1, LANES)),
            jnp.asarray(v_aug.reshape(-1, LANES)))


def _prep_odd(w_in):
    kw = REC_HEADS * HEAD_DIM
    vw = REC_HEADS * VAL_DIM
    dw = ATT_HEADS * HEAD_DIM
    sizes = (kw, kw, vw, dw, dw, dw, vw + dw)
    offs = np.cumsum((0,) + sizes)
    rq, rk, rv, dq, dk, dv, gate = (w_in[:, offs[i]:offs[i + 1]] for i in range(7))
    scale = HEAD_DIM ** -0.5
    return jnp.concatenate([gate, rv, rq * scale, rk, dq * (scale * LOG2E), dk, dv], axis=1).astype(BF16)


def _rope_tables(S):
    inv = jnp.power(ROPE_THETA, -jnp.arange(0, HEAD_DIM, 2, dtype=F32) / HEAD_DIM)
    ang = jnp.arange(S, dtype=F32)[:, None] * inv[None, :]
    cos, sin = jnp.cos(ang), jnp.sin(ang)
    reps = LANES // HEAD_DIM
    cos_t = jnp.tile(jnp.concatenate([cos, cos], axis=1), (1, reps))
    sin_t = jnp.tile(jnp.concatenate([-sin, sin], axis=1), (1, reps))
    return cos_t, sin_t


def _tile(S, pref):
    return min(S, pref)


def _even_mixers(z, q_t, v_t, glog, gla_g):
    S = z.shape[1]
    o_a = _fox_attention(z, q_t, v_t, _tile(S, 256))
    o_b = _gla(z, glog, gla_g.reshape(1, -1), _tile(S, 512))
    return o_a, o_b


def _odd_mixers(z, zd4, zd16, gn_w, gn_b):
    B, S, _ = z.shape
    o_c = _retention(z, gn_w.reshape(1, -1), gn_b.reshape(1, -1), _tile(S, 512))
    sources = {1: (z.reshape(B, 1, S, Z_WIDTH), DIL_TILE0), 4: (zd4, 0), 16: (zd16, 0)}
    outs, lses = zip(*[_dilated_branch(*sources[dil], window, dil, DIL_QBLOCKS)
                       for window, dil in DIL_PATTERNS])
    outs = (outs[0].reshape(B, S, COL_TILE), *outs[1:])
    lses = (lses[0].reshape(B, S, LANES), *lses[1:])
    return o_c, outs, lses


def kernel(x, norm_even, w_in_even, b_f_even, w_lr_even, b_lr_even, gla_norm_even, w_out_even,
           norm_odd, w_in_odd, ret_gn_w_odd, ret_gn_b_odd, w_out_odd, final_norm):
    depth = norm_even.shape[0] + norm_odd.shape[0]
    B, S, D = x.shape
    tm = _tile(S, 512)
    cos, sin = _rope_tables(S)

    def in_args(i):
        j = i // 2
        if i % 2 == 0:
            return "even", (norm_even[j].reshape(1, D),
                            *_prep_even(w_in_even[j], b_f_even[j], w_lr_even[j], b_lr_even[j]))
        return "odd", (norm_odd[j].reshape(1, D), _prep_odd(w_in_odd[j]), cos, sin)

    h = x
    kind, args = in_args(0)
    proj = _proj_call(h, None, None, kind, args, None, tm)
    for i in range(depth):
        j = i // 2
        if i % 2 == 0:
            z = proj[0]
            o_a, o_b = _even_mixers(*proj, gla_norm_even[j])
            prev_args = (o_a, o_b, z, w_out_even[j].astype(BF16))
        else:
            z = proj[0]
            o_c, outs, lses = _odd_mixers(*proj, ret_gn_w_odd[j], ret_gn_b_odd[j])
            prev_args = (o_c, outs, lses, z, w_out_odd[j].astype(BF16))
        nxt, nxt_args = in_args(i + 1) if i + 1 < depth else (None, None)
        h, *proj = _proj_call(h, kind, prev_args, nxt, nxt_args, final_norm.reshape(1, D), tm)
        kind = nxt
    return h
```

```python
import functools
import math

import numpy as np
import jax
import jax.numpy as jnp
from jax import lax
from jax.experimental import pallas as pl
from jax.experimental.pallas import tpu as pltpu

F32 = jnp.float32
BF16 = jnp.bfloat16

EPS = 1e-6
ROPE_THETA = 10000.0
HEAD_DIM = 64
VAL_DIM = 128
ATT_HEADS = 8
REC_HEADS = 4
GLA_RANK = 16
GLA_TAU = 16.0
GLA_CHUNK = 64
RET_CHUNK = 128
DIL_BLOCK = 128
DIL_PATTERNS = ((128, 1), (512, 4), (2048, 16))

Z_WIDTH = 3584
DIL_TILE0 = 4
DIL_QBLOCKS = 4
DIL_VROWS = 144
ZE_WIDTH = 3072
FOX_PAD = 128
FOX_VROWS = 80
LOG2E = 1.4426950408889634
COL_TILE = 512
LANES = 128
V7X_VMEM_LIMIT = 56 * 1024 * 1024

NT_DIMS = (((1,), (1,)), ((), ()))
TN_DIMS = (((0,), (0,)), ((), ()))


def _params(*sem):
    return pltpu.CompilerParams(dimension_semantics=sem, vmem_limit_bytes=V7X_VMEM_LIMIT)


def _split2(x):
    hi = x.astype(BF16)
    lo = (x - hi.astype(F32)).astype(BF16)
    return hi, lo


def _split3(x):
    hi, lo = _split2(x)
    lo2 = (x - hi.astype(F32) - lo.astype(F32)).astype(BF16)
    return hi, lo, lo2


def _log_sigmoid(x):
    return jnp.minimum(x, 0.0) - jnp.log(1.0 + jnp.exp(-jnp.abs(x)))


def _rms_normed(x, g):
    ms = jnp.mean(x * x, axis=-1, keepdims=True)
    return x * lax.rsqrt(ms + EPS) * g


def _in_even_body(x, g_ref, w_ref, wq_ref, wv_ref, ws_ref, bf_ref, wlr_ref, blr_ref, place_ref,
                  qaug_ref, vaug_ref, z_ref, qt_ref, vt_ref, glog_ref, u_ref, carry_ref):
    tm = x.shape[0]

    @pl.when(pl.program_id(1) == 0)
    def _():
        carry_ref[...] = jnp.zeros_like(carry_ref)

    u_ref[...] = _rms_normed(x, g_ref[...]).astype(BF16)
    u = u_ref[...]

    zs = jnp.dot(u, ws_ref[...], preferred_element_type=F32)
    logf = _log_sigmoid(zs + bf_ref[...])
    row = lax.broadcasted_iota(jnp.int32, (tm, tm), 0)
    col = lax.broadcasted_iota(jnp.int32, (tm, tm), 1)
    tril = jnp.where(row >= col, 1.0, 0.0).astype(BF16)
    csum = sum(jnp.dot(tril, t, preferred_element_type=F32) for t in _split3(logf)) + carry_ref[...]
    carry_ref[...] = csum[tm - 1:tm, :]
    hi, lo, lo2 = (t.astype(F32) for t in _split3(csum * LOG2E))
    lane = lax.broadcasted_iota(jnp.int32, (tm, LANES), 1)
    packed = jnp.where(lane < ATT_HEADS, hi,
                       jnp.where(lane < 2 * ATT_HEADS, pltpu.roll(lo, ATT_HEADS, 1),
                                 jnp.where(lane < 3 * ATT_HEADS, pltpu.roll(lo2, 2 * ATT_HEADS, 1), 0.0)))
    c_cols = jnp.dot(packed.astype(BF16), place_ref[...], preferred_element_type=F32)

    glr = jnp.dot(zs.astype(BF16), wlr_ref[...], preferred_element_type=F32)
    glog_ref[0] = _log_sigmoid(glr + blr_ref[...]) * (1.0 / GLA_TAU)

    for j in range(ZE_WIDTH // COL_TILE):
        cols = slice(j * COL_TILE, (j + 1) * COL_TILE)
        r = jnp.dot(u, w_ref[:, cols], preferred_element_type=F32)
        if j in (2, 3):
            r = r + c_cols[:, (j - 2) * COL_TILE:(j - 1) * COL_TILE]
        z_ref[0, :, cols] = r.astype(BF16)

    def feature_major(w_t_ref, aug_ref, out_ref, tile):
        for j in range(w_t_ref.shape[0] // tile):
            rows = slice(j * tile, (j + 1) * tile)
            r = lax.dot_general(w_t_ref[rows, :], u, NT_DIMS, preferred_element_type=F32)
            aug = aug_ref[rows, :]
            for c in range(tm // LANES):
                lanes = slice(c * LANES, (c + 1) * LANES)
                out_ref[0, rows, lanes] = (r[:, lanes] + aug).astype(BF16)

    feature_major(wv_ref, vaug_ref, vt_ref, ATT_HEADS * FOX_VROWS // 2)
    qtok = jnp.dot(u, wq_ref[...], preferred_element_type=F32)
    aug = qaug_ref[...].astype(BF16)
    for g in range(ATT_HEADS // 2):
        q_t = qtok[:, g * LANES:(g + 1) * LANES].T
        for half in range(2):
            r0 = (2 * g + half) * FOX_PAD
            qt_ref[0, r0:r0 + HEAD_DIM, :] = q_t[half * HEAD_DIM:(half + 1) * HEAD_DIM, :].astype(BF16)
            qt_ref[0, r0 + HEAD_DIM:r0 + FOX_PAD, :] = aug


def _rope_tile(x, cos, sin):
    lane = lax.broadcasted_iota(jnp.int32, (x.shape[0], LANES), 1)
    first_half = (lane % HEAD_DIM) < (HEAD_DIM // 2)
    outs = []
    for c in range(x.shape[1] // LANES):
        xc = x[:, c * LANES:(c + 1) * LANES]
        partner = jnp.where(first_half,
                            pltpu.roll(xc, LANES - HEAD_DIM // 2, 1),
                            pltpu.roll(xc, HEAD_DIM // 2, 1))
        outs.append(xc * cos + partner * sin)
    return jnp.concatenate(outs, axis=-1)


def _in_odd_body(x, g_ref, w_ref, cos_ref, sin_ref, z_ref, zd4_ref, zd16_ref, u_ref, r_ref):
    tm = x.shape[0]
    u_ref[...] = _rms_normed(x, g_ref[...]).astype(BF16)
    u = u_ref[...]
    cos = cos_ref[...]
    sin = sin_ref[...]
    for j in range(Z_WIDTH // COL_TILE):
        cols = slice(j * COL_TILE, (j + 1) * COL_TILE)
        r = jnp.dot(u, w_ref[:, cols], preferred_element_type=F32)
        if j in (3, 4, 5):
            r = _rope_tile(r, cos, sin)
        z_ref[0, :, cols] = r.astype(BF16)
        if j >= DIL_TILE0:
            t = j - DIL_TILE0
            for c in range(COL_TILE // LANES):
                slot = t * (COL_TILE // LANES) + c
                r_ref[slot] = r[:, c * LANES:(c + 1) * LANES]
                dcols = slice(t * COL_TILE + c * LANES, t * COL_TILE + (c + 1) * LANES)
                for dil, ref in ((4, zd4_ref), (16, zd16_ref)):
                    for res in range(dil):
                        ref[0, res, :, dcols] = (
                            r_ref[slot, pl.ds(res, tm // dil, stride=dil), :].astype(BF16))


def _fox_kernel(qt_ref, kp_ref, vt_ref, o_ref, s_ref, acc_ref, mp_ref, mc_ref, ot_ref, mask_ref, *, blk):
    i = pl.program_id(1)
    heads = range(ATT_HEADS)
    rows = lambda h: slice(h * FOX_PAD, (h + 1) * FOX_PAD)
    vrows = lambda h: slice(h * FOX_VROWS, (h + 1) * FOX_VROWS)

    @pl.when(i == 0)
    def _():
        key = lax.broadcasted_iota(jnp.int32, (blk, blk), 0)
        qry = lax.broadcasted_iota(jnp.int32, (blk, blk), 1)
        mask_ref[0] = jnp.zeros((blk, blk), F32)
        mask_ref[1] = jnp.where(key <= qry, 0.0, -jnp.inf)

    def score_stage(h, j):
        ks = pl.multiple_of(j * blk, blk)
        s = jnp.dot(kp_ref[0, pl.ds(ks, blk), rows(h)], qt_ref[0, rows(h), :],
                    preferred_element_type=F32)
        s = s + mask_ref[jnp.where(j == i, 1, 0)]
        s_ref[h] = s
        m_old = mc_ref[h]
        mp_ref[h] = m_old
        mc_ref[h] = jnp.maximum(m_old, jnp.max(s, axis=0, keepdims=True))

    def value_stage(h, j):
        ks = pl.multiple_of(j * blk, blk)
        m_new = mc_ref[h]
        alpha = jnp.exp2(mp_ref[h] - m_new)
        p = jnp.exp2((s_ref[h] - m_new).astype(BF16))
        pv = jnp.dot(vt_ref[0, vrows(h), pl.ds(ks, blk)], p, preferred_element_type=F32)
        acc_ref[h] = alpha * acc_ref[h] + pv

    def trip(j):
        for h in heads:
            value_stage(h, j - 1)
            score_stage(h, j)

    for h in heads:
        acc_ref[h] = jnp.zeros(acc_ref.shape[1:], F32)
        mc_ref[h] = jnp.full((1, blk), -jnp.inf, F32)
        score_stage(h, 0)

    odd = i % 2

    @pl.when(odd == 1)
    def _():
        trip(1)

    def pair(t, _):
        j = 1 + odd + 2 * t
        trip(j)
        trip(j + 1)
        return 0

    lax.fori_loop(0, i // 2, pair, 0)

    for h in heads:
        value_stage(h, i)
        acc = acc_ref[h]
        ot_ref[h * HEAD_DIM:(h + 1) * HEAD_DIM, :] = (
            acc[:HEAD_DIM, :] / acc[HEAD_DIM:HEAD_DIM + 1, :])
    o_ref[0] = ot_ref[...].T.astype(BF16)


def _fox_attention(z, q_t, v_t, blk):
    B, S, _ = z.shape
    fox_rows = ATT_HEADS * FOX_PAD
    val_rows = ATT_HEADS * FOX_VROWS
    return pl.pallas_call(
        functools.partial(_fox_kernel, blk=blk),
        grid=(B, S // blk),
        in_specs=[
            pl.BlockSpec((1, fox_rows, blk), lambda b, i: (b, 0, i)),
            pl.BlockSpec((1, S, fox_rows), lambda b, i: (b, 0, 1)),
            pl.BlockSpec((1, val_rows, S), lambda b, i: (b, 0, 0)),
        ],
        out_specs=pl.BlockSpec((1, blk, COL_TILE), lambda b, i: (b, i, 0)),
        out_shape=jax.ShapeDtypeStruct((B, S, COL_TILE), BF16),
        scratch_shapes=[pltpu.VMEM((ATT_HEADS, blk, blk), F32),
                        pltpu.VMEM((ATT_HEADS, FOX_VROWS, blk), F32),
                        pltpu.VMEM((ATT_HEADS, 1, blk), F32),
                        pltpu.VMEM((ATT_HEADS, 1, blk), F32),
                        pltpu.VMEM((ATT_HEADS * HEAD_DIM, blk), F32),
                        pltpu.VMEM((2, blk, blk), F32)],
        compiler_params=_params("arbitrary", "arbitrary"),
        name="fox_attention",
    )(q_t, z, v_t)


def _gla_kernel(q_ref, k_ref, v_ref, g_ref, gn_ref, o_ref,
                qhat_ref, top_ref, bot_ref, kinc_ref, dec_ref, dect_ref, state_ref):
    tg = q_ref.shape[1]
    C = GLA_CHUNK
    blk = 2 * C
    n_chunks = tg // C
    pairs = REC_HEADS // 2

    @pl.when(pl.program_id(1) == 0)
    def _():
        state_ref[...] = jnp.zeros_like(state_ref)

    row = lax.broadcasted_iota(jnp.int32, (tg, tg), 0)
    col = lax.broadcasted_iota(jnp.int32, (tg, tg), 1)
    chunk_start = row & (-C)
    tril = jnp.where(col <= row, jnp.where(col >= chunk_start, 1.0, 0.0), 0.0).astype(BF16)
    ghi, glo = _split2(g_ref[0])
    b = (jnp.dot(tril, ghi, preferred_element_type=F32)
         + jnp.dot(tril, glo, preferred_element_type=F32))
    q = q_ref[0].astype(F32)
    k = k_ref[0].astype(F32)
    tok = lax.broadcasted_iota(jnp.int32, (tg, LANES), 0)
    lane = lax.broadcasted_iota(jnp.int32, (tg, LANES), 1)
    in_a = (tok & C) == 0
    first = lane < HEAD_DIM
    qt = q * jnp.exp(b)
    kt = k * jnp.exp(-b)
    top_ref[...] = kt
    dec_ref[...] = jnp.zeros_like(dec_ref)
    kd_rows, dec_rows = [], []
    for c in range(n_chunks):
        rows = slice(c * C, (c + 1) * C)
        b_last = b[(c + 1) * C - 1:(c + 1) * C, :]
        kd_rows.append(k[rows, :] * jnp.exp(b_last - b[rows, :]))
        dec_rows.append(jnp.exp(b_last))
        dec_ref[c:c + 1, :] = dec_rows[-1]
    for c in range(0, n_chunks, 2):
        bot_ref[c * C:(c + 1) * C, :] = kd_rows[c]
        bot_ref[(c + 1) * C:(c + 2) * C, :] = kt[(c + 1) * C:(c + 2) * C, :]
        kinc_ref[c * C:(c + 1) * C, :] = kd_rows[c] * dec_rows[c + 1]
        kinc_ref[(c + 1) * C:(c + 2) * C, :] = kd_rows[c + 1]
    dect_ref[...] = dec_ref[...].T
    for half in range(2):
        for g in range(pairs):
            gs = slice(g * LANES, (g + 1) * LANES)
            qh = jnp.where(first, qt[:, gs], 0.0) if half == 0 else jnp.where(first, 0.0, qt[:, gs])
            qhat_ref[half, :, 2 * g * LANES:(2 * g + 1) * LANES] = jnp.where(in_a, qh, 0.0).astype(BF16)
            qhat_ref[half, :, (2 * g + 1) * LANES:(2 * g + 2) * LANES] = jnp.where(in_a, 0.0, qh).astype(BF16)

    rb = lax.broadcasted_iota(jnp.int32, (blk, blk), 0)
    cb = lax.broadcasted_iota(jnp.int32, (blk, blk), 1)
    lower = cb <= rb
    low_rows = rb < HEAD_DIM

    for n in range(tg // blk):
        rows = slice(n * blk, (n + 1) * blk)
        for g in range(pairs):
            gs = slice(g * LANES, (g + 1) * LANES)
            top_t = top_ref[rows, gs].T.astype(BF16)
            bot_t = bot_ref[rows, gs].T.astype(BF16)
            kinc_t = kinc_ref[rows, gs].T.astype(BF16)
            st = state_ref[g]
            dec_a = dect_ref[gs, 2 * n:2 * n + 1]
            dec_b = dect_ref[gs, 2 * n + 1:2 * n + 2]
            rhs = jnp.concatenate([jnp.concatenate([top_t, st.astype(BF16)], axis=1),
                                   jnp.concatenate([bot_t, (st * dec_a).astype(BF16)], axis=1)], axis=0)
            inc = []
            for half in range(2):
                h = 2 * g + half
                vs = slice(h * VAL_DIM, (h + 1) * VAL_DIM)
                v = v_ref[0, rows, vs]
                sc = jnp.dot(qhat_ref[half, rows, 2 * g * LANES:(2 * g + 2) * LANES], rhs,
                             preferred_element_type=F32)
                att = jnp.where(lower, sc[:, :blk], 0.0).astype(BF16)
                both = jnp.dot(jnp.concatenate([att, kinc_t], axis=0), v, preferred_element_type=F32)
                o = both[:blk, :] + sc[:, blk:]
                inc.append(both[blk:, :])
                ms = jnp.mean(o * o, axis=-1, keepdims=True)
                o_ref[0, rows, vs] = (o * lax.rsqrt(ms + EPS) * gn_ref[:, vs]).astype(BF16)
            state_ref[g] = st * (dec_a * dec_b) + jnp.where(low_rows, inc[0], inc[1])


def _gla(z, glog, gla_g, tg):
    B, S, _ = z.shape
    kw = REC_HEADS * HEAD_DIM
    vw = REC_HEADS * VAL_DIM
    return pl.pallas_call(
        _gla_kernel,
        grid=(B, S // tg),
        in_specs=[
            pl.BlockSpec((1, tg, kw), lambda b, s: (b, s, 10)),
            pl.BlockSpec((1, tg, kw), lambda b, s: (b, s, 11)),
            pl.BlockSpec((1, tg, vw), lambda b, s: (b, s, 4)),
            pl.BlockSpec((1, tg, kw), lambda b, s: (b, s, 0)),
            pl.BlockSpec((1, vw), lambda b, s: (0, 0)),
        ],
        out_specs=pl.BlockSpec((1, tg, vw), lambda b, s: (b, s, 0)),
        out_shape=jax.ShapeDtypeStruct((B, S, vw), BF16),
        scratch_shapes=[
            pltpu.VMEM((2, tg, 2 * kw), BF16),
            pltpu.VMEM((tg, kw), F32), pltpu.VMEM((tg, kw), F32), pltpu.VMEM((tg, kw), F32),
            pltpu.VMEM((LANES, kw), F32), pltpu.VMEM((kw, LANES), F32),
            pltpu.VMEM((REC_HEADS // 2, 2 * HEAD_DIM, VAL_DIM), F32),
        ],
        compiler_params=_params("arbitrary", "arbitrary"),
        name="gla",
    )(z, z, z, glog, gla_g)


def _ret_kernel(q_ref, k_ref, v_ref, gw_ref, gb_ref, o_ref, state_ref, dmat_ref, zeta_ref):
    tg = q_ref.shape[1]
    C = RET_CHUNK
    n_chunks = tg // C
    pairs = REC_HEADS // 2
    log_gamma = [math.log(1.0 - 2.0 ** (-5.0 - h)) for h in range(REC_HEADS)]

    @pl.when(pl.program_id(1) == 0)
    def _():
        state_ref[...] = jnp.zeros_like(state_ref)

    ri = lax.broadcasted_iota(jnp.int32, (C, C), 0)
    ci = lax.broadcasted_iota(jnp.int32, (C, C), 1)
    diff = (ri - ci).astype(F32)
    idx = lax.broadcasted_iota(jnp.int32, (C, 1), 0).astype(F32)
    low_lanes = ci < HEAD_DIM
    low_rows = ri < HEAD_DIM
    for h in range(REC_HEADS):
        dmat_ref[h] = jnp.where(diff >= 0, jnp.exp(jnp.maximum(diff, 0.0) * log_gamma[h]), 0.0)
    for g in range(pairs):
        zeta_ref[g] = jnp.where(low_lanes, jnp.exp((C - 1.0 - idx) * log_gamma[2 * g]),
                                jnp.exp((C - 1.0 - idx) * log_gamma[2 * g + 1]))

    for c in range(n_chunks):
        rows = slice(c * C, (c + 1) * C)
        for g in range(pairs):
            gs = slice(g * LANES, (g + 1) * LANES)
            q_pair = q_ref[0, rows, gs]
            k_f32 = k_ref[0, rows, gs].astype(F32)
            k_t = k_f32.T.astype(BF16)
            kz_t = (k_f32 * zeta_ref[g]).T.astype(BF16)
            st = state_ref[g]
            rhs = jnp.concatenate([k_t, st.astype(BF16)], axis=1)
            zero = jnp.zeros_like(q_pair)
            inc = []
            for half in range(2):
                h = 2 * g + half
                vs = slice(h * VAL_DIM, (h + 1) * VAL_DIM)
                xi = jnp.exp((idx + 1.0) * log_gamma[h])
                q = jnp.where(low_lanes, q_pair, zero) if half == 0 else jnp.where(low_lanes, zero, q_pair)
                v = v_ref[0, rows, vs]
                sc = jnp.dot(q, rhs, preferred_element_type=F32)
                att = (sc[:, :C] * dmat_ref[h]).astype(BF16)
                both = jnp.dot(jnp.concatenate([att, kz_t], axis=0), v, preferred_element_type=F32)
                o = both[:C, :] + xi * sc[:, C:]
                inc.append(both[C:, :])
                o = o - jnp.mean(o, axis=-1, keepdims=True)
                ms = jnp.mean(o * o, axis=-1, keepdims=True)
                o = o * lax.rsqrt(ms + EPS) * gw_ref[:, vs] + gb_ref[:, vs]
                o_ref[0, rows, vs] = o.astype(BF16)
            decay = jnp.where(low_rows, math.exp(C * log_gamma[2 * g]), math.exp(C * log_gamma[2 * g + 1]))
            state_ref[g] = st * decay + jnp.where(low_rows, inc[0], inc[1])


def _retention(z, gn_w, gn_b, tg):
    B, S, _ = z.shape
    kw = REC_HEADS * HEAD_DIM
    vw = REC_HEADS * VAL_DIM
    return pl.pallas_call(
        _ret_kernel,
        grid=(B, S // tg),
        in_specs=[
            pl.BlockSpec((1, tg, kw), lambda b, s: (b, s, 6)),
            pl.BlockSpec((1, tg, kw), lambda b, s: (b, s, 7)),
            pl.BlockSpec((1, tg, vw), lambda b, s: (b, s, 2)),
            pl.BlockSpec((1, vw), lambda b, s: (0, 0)),
            pl.BlockSpec((1, vw), lambda b, s: (0, 0)),
        ],
        out_specs=pl.BlockSpec((1, tg, vw), lambda b, s: (b, s, 0)),
        out_shape=jax.ShapeDtypeStruct((B, S, vw), BF16),
        scratch_shapes=[pltpu.VMEM((REC_HEADS // 2, 2 * HEAD_DIM, VAL_DIM), F32),
                        pltpu.VMEM((REC_HEADS, RET_CHUNK, RET_CHUNK), F32),
                        pltpu.VMEM((REC_HEADS // 2, RET_CHUNK, LANES), F32)],
        compiler_params=_params("arbitrary", "arbitrary"),
        name="retention",
    )(z, z, z, gn_w, gn_b)


def _dil_kernel(q_ref, k_ref, v_ref, o_ref, lse_ref,
                vt_ref, qt_ref, s_ref, mx_ref, bias_ref, ot_ref, lt_ref, *, blk, nblk, span):
    n = pl.program_id(2)
    m = k_ref.shape[2]
    pairs = ATT_HEADS // 2
    chains = [(g, half) for g in range(pairs) for half in range(2)]
    feat = lax.broadcasted_iota(jnp.int32, (LANES, blk), 0)
    ki = lax.broadcasted_iota(jnp.int32, (2 * blk, blk), 0)
    qi = lax.broadcasted_iota(jnp.int32, (2 * blk, blk), 1)

    @pl.when(n == 0)
    def _():
        for g in range(pairs):
            vt_ref[g * DIL_VROWS + LANES:(g + 1) * DIL_VROWS, :] = jnp.ones(
                (DIL_VROWS - LANES, m), BF16)

        def build(kb, _):
            r0 = pl.multiple_of(kb * blk, blk)
            for g in range(pairs):
                v = v_ref[0, 0, pl.ds(r0, blk), g * LANES:(g + 1) * LANES].astype(F32)
                vt_ref[g * DIL_VROWS:g * DIL_VROWS + LANES, pl.ds(r0, blk)] = v.T.astype(BF16)
            return 0

        lax.fori_loop(0, m // blk, build, 0)

    starts = []
    for t in range(nblk):
        qb = n * nblk + t
        start = pl.multiple_of(jnp.maximum(qb - 1, 0) * blk, blk)
        starts.append(start)
        rel = qb * blk + qi - start - ki
        bias_ref[t] = jnp.where(rel >= 0, jnp.where(rel <= span, 0.0, -jnp.inf), -jnp.inf)
        for g in range(pairs):
            q_t = q_ref[0, 0, t * blk:(t + 1) * blk, g * LANES:(g + 1) * LANES].astype(F32).T
            qt_ref[t, 2 * g] = jnp.where(feat < HEAD_DIM, q_t, 0.0).astype(BF16)
            qt_ref[t, 2 * g + 1] = jnp.where(feat < HEAD_DIM, 0.0, q_t).astype(BF16)

    for t in range(nblk):
        for c, (g, half) in enumerate(chains):
            k = k_ref[0, 0, pl.ds(starts[t], 2 * blk), g * LANES:(g + 1) * LANES]
            s = jnp.dot(k, qt_ref[t, c], preferred_element_type=F32) + bias_ref[t]
            s_ref[t, c] = s
            mx_ref[t, c] = jnp.max(s, axis=0, keepdims=True)

    for t in range(nblk):
        lt_ref[t] = jnp.zeros(lt_ref.shape[1:], F32)
        for c, (g, half) in enumerate(chains):
            mx = mx_ref[t, c]
            p = jnp.exp2((s_ref[t, c] - mx).astype(BF16))
            acc = jnp.dot(vt_ref[g * DIL_VROWS:(g + 1) * DIL_VROWS, pl.ds(starts[t], 2 * blk)], p,
                          preferred_element_type=F32)
            den = acc[LANES:LANES + 1, :]
            ot_ref[t, c * HEAD_DIM:(c + 1) * HEAD_DIM, :] = (
                acc[half * HEAD_DIM:(half + 1) * HEAD_DIM, :] * (1.0 / den))
            lt_ref[t, c:c + 1, :] = (mx + jnp.log2(den)) * (1.0 / LOG2E)
        qrows = slice(t * blk, (t + 1) * blk)
        o_ref[0, 0, qrows, :] = ot_ref[t].T.astype(BF16)
        lse_ref[0, 0, qrows, :] = lt_ref[t].T


def _dilated_branch(src, tile0, window, dil, nblk):
    B, _, m, _ = src.shape
    blk = DIL_BLOCK
    nblk = min(nblk, m // blk)
    tq = nblk * blk
    return pl.pallas_call(
        functools.partial(_dil_kernel, blk=blk, nblk=nblk, span=window // dil),
        grid=(B, dil, m // tq),
        in_specs=[
            pl.BlockSpec((1, 1, tq, COL_TILE), lambda b, r, n: (b, r, n, tile0)),
            pl.BlockSpec((1, 1, m, COL_TILE), lambda b, r, n: (b, r, 0, tile0 + 1)),
            pl.BlockSpec((1, 1, m, COL_TILE), lambda b, r, n: (b, r, 0, tile0 + 2)),
        ],
        out_specs=[
            pl.BlockSpec((1, 1, tq, COL_TILE), lambda b, r, n: (b, r, n, 0)),
            pl.BlockSpec((1, 1, tq, LANES), lambda b, r, n: (b, r, n, 0)),
        ],
        out_shape=[
            jax.ShapeDtypeStruct((B, dil, m, COL_TILE), BF16),
            jax.ShapeDtypeStruct((B, dil, m, LANES), F32),
        ],
        scratch_shapes=[pltpu.VMEM((ATT_HEADS // 2 * DIL_VROWS, m), BF16),
                        pltpu.VMEM((nblk, ATT_HEADS, LANES, blk), BF16),
                        pltpu.VMEM((nblk, ATT_HEADS, 2 * blk, blk), F32),
                        pltpu.VMEM((nblk, ATT_HEADS, 1, blk), F32),
                        pltpu.VMEM((nblk, 2 * blk, blk), F32),
                        pltpu.VMEM((nblk, ATT_HEADS * HEAD_DIM, blk), F32),
                        pltpu.VMEM((nblk, LANES, blk), F32)],
        compiler_params=_params("arbitrary", "arbitrary", "arbitrary"),
        name=f"dilated_{dil}",
    )(src, src, src)


def _silu(x):
    return x / (1.0 + jnp.exp(-x))


def _mix_out(mix_a, mix_b, gate_ref, h_ref, w_ref):
    half = mix_a.shape[1]
    gate = _silu(gate_ref[0].astype(F32))
    a = (mix_a * gate[:, :half]).astype(BF16)
    b = (mix_b * gate[:, half:]).astype(BF16)
    y = (jnp.dot(a, w_ref[:half, :], preferred_element_type=F32)
         + jnp.dot(b, w_ref[half:, :], preferred_element_type=F32))
    return h_ref[0] + y


def _dilated_mix(d1_ref, d4_ref, d16_ref, l1_ref, l4_ref, l16_ref, e_ref,
                 o4_ref, o16_ref, ls4_ref, ls16_ref):
    tm = d1_ref.shape[1]
    chunks = d1_ref.shape[2] // LANES
    for dil, src, lsrc, dst, ldst in ((4, d4_ref, l4_ref, o4_ref, ls4_ref),
                                      (16, d16_ref, l16_ref, o16_ref, ls16_ref)):
        for res in range(dil):
            token_rows = pl.ds(res, tm // dil, stride=dil)
            ldst[token_rows, :] = lsrc[0, res]
            for c in range(chunks):
                dst[c, token_rows, :] = src[0, res, :, c * LANES:(c + 1) * LANES].astype(F32)
    l1, l2, l3 = l1_ref[0], ls4_ref[...], ls16_ref[...]
    mx = jnp.maximum(jnp.maximum(l1, l2), l3)
    e1, e2, e3 = jnp.exp(l1 - mx), jnp.exp(l2 - mx), jnp.exp(l3 - mx)
    inv = 1.0 / (e1 + e2 + e3)
    e = e_ref[...]
    expand = lambda w: sum(jnp.dot(t, e, preferred_element_type=F32) for t in _split2(w))
    o4 = jnp.concatenate([o4_ref[c] for c in range(chunks)], axis=-1)
    o16 = jnp.concatenate([o16_ref[c] for c in range(chunks)], axis=-1)
    return (expand(e1 * inv) * d1_ref[0].astype(F32) + expand(e2 * inv) * o4
            + expand(e3 * inv) * o16)


N_IN = {"even": 11, "odd": 4}
N_OUT = {"even": 4, "odd": 3}
N_SCRATCH = {"even": 2, "odd": 2}


def _proj_kernel(*refs, prev, nxt, final):
    refs = list(refs)
    take = lambda n: [refs.pop(0) for _ in range(n)]
    if prev is None:
        (x_ref,) = take(1)
    elif prev == "even":
        oa_ref, ob_ref, gate_ref, h_ref, wout_ref = take(5)
    else:
        oc_ref, d1_ref, d4_ref, d16_ref, l1_ref, l4_ref, l16_ref, e_ref, gate_ref, h_ref, wout_ref = take(11)
    if final:
        (fg_ref,) = take(1)
    in_params = take(N_IN[nxt]) if nxt else []
    if prev is not None:
        (hout_ref,) = take(1)
    in_outs = take(N_OUT[nxt]) if nxt else []
    mix_scratch = take(4) if prev == "odd" else []
    in_scratch = take(N_SCRATCH[nxt]) if nxt else []

    if prev is None:
        x = x_ref[0]
    elif prev == "even":
        x = _mix_out(oa_ref[0].astype(F32), ob_ref[0].astype(F32), gate_ref, h_ref, wout_ref)
    else:
        o_d = _dilated_mix(d1_ref, d4_ref, d16_ref, l1_ref, l4_ref, l16_ref, e_ref, *mix_scratch)
        x = _mix_out(oc_ref[0].astype(F32), o_d, gate_ref, h_ref, wout_ref)
    if final:
        x = _rms_normed(x, fg_ref[...])
    if prev is not None:
        hout_ref[0] = x
    if nxt == "even":
        _in_even_body(x, *in_params, *in_outs, *in_scratch)
    elif nxt == "odd":
        _in_odd_body(x, *in_params, *in_outs, *in_scratch)


def _proj_call(h, prev, prev_args, nxt, nxt_args, final_g, tm):
    B, S, D = h.shape
    half = COL_TILE
    tok = lambda b, s: (b, s, 0)
    const2 = lambda b, s: (0, 0)
    once = dict(pipeline_mode=pl.Buffered(1))
    weight = lambda shape: pl.BlockSpec(shape, (lambda b, s: (0,) * len(shape)), **once)
    res_spec = lambda dil, c: pl.BlockSpec((1, dil, tm // dil, c), lambda b, s: (b, 0, s, 0))
    part = pl.BlockSpec((1, tm, half), tok)
    h_spec = pl.BlockSpec((1, tm, D), tok)
    args, in_specs, out_specs, out_shape, scratch = [], [], [], [], []

    if prev is None:
        args += [h]
        in_specs += [h_spec]
    elif prev == "even":
        o_a, o_b, z, w_out = prev_args
        args += [o_a, o_b, z, h, w_out]
        in_specs += [part, part, pl.BlockSpec((1, tm, 2 * half), tok), h_spec, weight((2 * half, D))]
    else:
        o_c, outs, lses, z, w_out = prev_args
        expand = jnp.pad(jnp.repeat(jnp.eye(ATT_HEADS, dtype=BF16), HEAD_DIM, axis=1),
                         ((0, LANES - ATT_HEADS), (0, 0)))
        args += [o_c, *outs, *lses, expand, z, h, w_out]
        in_specs += [part, part, res_spec(4, half), res_spec(16, half),
                     pl.BlockSpec((1, tm, LANES), tok), res_spec(4, LANES), res_spec(16, LANES),
                     weight((LANES, half)), pl.BlockSpec((1, tm, 2 * half), tok), h_spec,
                     weight((2 * half, D))]
    final = nxt is None
    if final:
        args += [final_g]
        in_specs += [weight((1, D))]

    if nxt == "even":
        g, w_main, wq_t, wv_t, w_small, b_f, w_lr, b_lr, place, q_aug, v_aug = nxt_args
        fox_rows, val_rows, kw = ATT_HEADS * FOX_PAD, ATT_HEADS * FOX_VROWS, w_lr.shape[1]
        args += list(nxt_args)
        in_specs += [weight(a.shape) for a in nxt_args]
    elif nxt == "odd":
        g, w_main, cos, sin = nxt_args
        args += list(nxt_args)
        in_specs += [weight(g.shape), weight(w_main.shape),
                     pl.BlockSpec((tm, LANES), lambda b, s: (s, 0)),
                     pl.BlockSpec((tm, LANES), lambda b, s: (s, 0))]

    if prev is not None:
        out_specs += [h_spec]
        out_shape += [jax.ShapeDtypeStruct((B, S, D), F32)]
    if nxt == "even":
        out_specs += [pl.BlockSpec((1, tm, ZE_WIDTH), tok),
                      pl.BlockSpec((1, fox_rows, tm), lambda b, s: (b, 0, s)),
                      pl.BlockSpec((1, val_rows, tm), lambda b, s: (b, 0, s)),
                      pl.BlockSpec((1, tm, kw), tok)]
        out_shape += [jax.ShapeDtypeStruct((B, S, ZE_WIDTH), BF16),
                      jax.ShapeDtypeStruct((B, fox_rows, S), BF16),
                      jax.ShapeDtypeStruct((B, val_rows, S), BF16),
                      jax.ShapeDtypeStruct((B, S, kw), F32)]
    elif nxt == "odd":
        dw = 3 * COL_TILE
        out_specs += [pl.BlockSpec((1, tm, Z_WIDTH), tok), res_spec(4, dw), res_spec(16, dw)]
        out_shape += [jax.ShapeDtypeStruct((B, S, Z_WIDTH), BF16),
                      jax.ShapeDtypeStruct((B, 4, S // 4, dw), BF16),
                      jax.ShapeDtypeStruct((B, 16, S // 16, dw), BF16)]

    if prev == "odd":
        scratch += [pltpu.VMEM((half // LANES, tm, LANES), F32),
                    pltpu.VMEM((half // LANES, tm, LANES), F32),
                    pltpu.VMEM((tm, LANES), F32), pltpu.VMEM((tm, LANES), F32)]
    if nxt == "even":
        scratch += [pltpu.VMEM((tm, D), BF16), pltpu.VMEM((1, LANES), F32)]
    elif nxt == "odd":
        scratch += [pltpu.VMEM((tm, D), BF16), pltpu.VMEM((3 * COL_TILE // LANES, tm, LANES), F32)]

    outs = pl.pallas_call(
        functools.partial(_proj_kernel, prev=prev, nxt=nxt, final=final),
        grid=(B, S // tm),
        in_specs=in_specs,
        out_specs=out_specs,
        out_shape=out_shape,
        scratch_shapes=scratch,
        compiler_params=_params("arbitrary", "arbitrary"),
        name=f"proj_{prev}_{nxt}",
    )(*args)
    return outs


def _prep_even(w_in, b_f, w_lr, b_lr):
    fw = ATT_HEADS * HEAD_DIM
    kw = REC_HEADS * HEAD_DIM
    vw = REC_HEADS * VAL_DIM
    sizes = (fw, fw, fw, ATT_HEADS, kw, kw, vw, GLA_RANK, fw + vw)
    offs = np.cumsum((0,) + sizes)
    fq, fk, fv, ff, gq, gk, gv, glr, gate = (w_in[:, offs[i]:offs[i + 1]] for i in range(9))
    scale = HEAD_DIM ** -0.5
    D = w_in.shape[0]

    def pad_heads(w, width):
        w = w.reshape(D, ATT_HEADS, HEAD_DIM)
        return jnp.pad(w, ((0, 0), (0, 0), (0, width - HEAD_DIM))).reshape(D, ATT_HEADS * width)

    w_main = jnp.concatenate([gate, pad_heads(fk, FOX_PAD), gv, gq * scale, gk], axis=1).astype(BF16)
    wq = (fq * (scale * LOG2E)).astype(BF16)
    wv_t = pad_heads(fv, FOX_VROWS).T.astype(BF16)
    q_aug = np.zeros((FOX_PAD - HEAD_DIM, COL_TILE), np.float32)
    q_aug[:3, :] = -1.0
    v_aug = np.zeros((ATT_HEADS, FOX_VROWS, LANES), np.float32)
    v_aug[:, HEAD_DIM, :] = 1.0
    pad = LANES - ATT_HEADS - GLA_RANK
    w_small = jnp.pad(jnp.concatenate([ff, glr], axis=1), ((0, 0), (0, pad))).astype(BF16)
    b_f_pad = jnp.pad(b_f, (0, LANES - ATT_HEADS)).reshape(1, LANES)
    w_lr_pad = jnp.pad(w_lr, ((ATT_HEADS, pad), (0, 0))).astype(BF16)
    place = np.zeros((LANES, ATT_HEADS * FOX_PAD), np.float32)
    for i in range(3):
        for h in range(ATT_HEADS):
            place[ATT_HEADS * i + h, h * FOX_PAD + HEAD_DIM + i] = 1.0
    return (w_main, wq, wv_t, w_small, b_f_pad, w_lr_pad, b_lr.reshape(1, kw),
            jnp.asarray(place, BF16), jnp.asarray(q_aug), jnp.asarray(v_aug.reshape(-1, LANES)))


def _prep_odd(w_in):
    kw = REC_HEADS * HEAD_DIM
    vw = REC_HEADS * VAL_DIM
    dw = ATT_HEADS * HEAD_DIM
    sizes = (kw, kw, vw, dw, dw, dw, vw + dw)
    offs = np.cumsum((0,) + sizes)
    rq, rk, rv, dq, dk, dv, gate = (w_in[:, offs[i]:offs[i + 1]] for i in range(7))
    scale = HEAD_DIM ** -0.5
    return jnp.concatenate([gate, rv, rq * scale, rk, dq * (scale * LOG2E), dk, dv], axis=1).astype(BF16)


def _rope_tables(S):
    inv = jnp.power(ROPE_THETA, -jnp.arange(0, HEAD_DIM, 2, dtype=F32) / HEAD_DIM)
    ang = jnp.arange(S, dtype=F32)[:, None] * inv[None, :]
    cos, sin = jnp.cos(ang), jnp.sin(ang)
    reps = LANES // HEAD_DIM
    cos_t = jnp.tile(jnp.concatenate([cos, cos], axis=1), (1, reps))
    sin_t = jnp.tile(jnp.concatenate([-sin, sin], axis=1), (1, reps))
    return cos_t, sin_t


def _tile(S, pref):
    return min(S, pref)


def _even_mixers(z, q_t, v_t, glog, gla_g):
    S = z.shape[1]
    o_a = _fox_attention(z, q_t, v_t, _tile(S, 256))
    o_b = _gla(z, glog, gla_g.reshape(1, -1), _tile(S, 512))
    return o_a, o_b


def _odd_mixers(z, zd4, zd16, gn_w, gn_b):
    B, S, _ = z.shape
    o_c = _retention(z, gn_w.reshape(1, -1), gn_b.reshape(1, -1), _tile(S, 512))
    sources = {1: (z.reshape(B, 1, S, Z_WIDTH), DIL_TILE0), 4: (zd4, 0), 16: (zd16, 0)}
    outs, lses = zip(*[_dilated_branch(*sources[dil], window, dil, DIL_QBLOCKS)
                       for window, dil in DIL_PATTERNS])
    outs = (outs[0].reshape(B, S, COL_TILE), *outs[1:])
    lses = (lses[0].reshape(B, S, LANES), *lses[1:])
    return o_c, outs, lses


def kernel(x, norm_even, w_in_even, b_f_even, w_lr_even, b_lr_even, gla_norm_even, w_out_even,
           norm_odd, w_in_odd, ret_gn_w_odd, ret_gn_b_odd, w_out_odd, final_norm):
    depth = norm_even.shape[0] + norm_odd.shape[0]
    B, S, D = x.shape
    tm = _tile(S, 512)
    cos, sin = _rope_tables(S)

    def in_args(i):
        j = i // 2
        if i % 2 == 0:
            return "even", (norm_even[j].reshape(1, D),
                            *_prep_even(w_in_even[j], b_f_even[j], w_lr_even[j], b_lr_even[j]))
        return "odd", (norm_odd[j].reshape(1, D), _prep_odd(w_in_odd[j]), cos, sin)

    h = x
    kind, args = in_args(0)
    proj = _proj_call(h, None, None, kind, args, None, tm)
    for i in range(depth):
        j = i // 2
        if i % 2 == 0:
            z = proj[0]
            o_a, o_b = _even_mixers(*proj, gla_norm_even[j])
            prev_args = (o_a, o_b, z, w_out_even[j].astype(BF16))
        else:
            z = proj[0]
            o_c, outs, lses = _odd_mixers(*proj, ret_gn_w_odd[j], ret_gn_b_odd[j])
            prev_args = (o_c, outs, lses, z, w_out_odd[j].astype(BF16))
        nxt, nxt_args = in_args(i + 1) if i + 1 < depth else (None, None)
        h, *proj = _proj_call(h, kind, prev_args, nxt, nxt_args, final_norm.reshape(1, D), tm)
        kind = nxt
    return h
```

```python
import functools
import math

import numpy as np
import jax
import jax.numpy as jnp
from jax import lax
from jax.experimental import pallas as pl
from jax.experimental.pallas import tpu as pltpu

F32 = jnp.float32
BF16 = jnp.bfloat16

EPS = 1e-6
ROPE_THETA = 10000.0
HEAD_DIM = 64
VAL_DIM = 128
ATT_HEADS = 8
REC_HEADS = 4
GLA_RANK = 16
GLA_TAU = 16.0
GLA_CHUNK = 64
RET_CHUNK = 128
DIL_BLOCK = 128
DIL_PATTERNS = ((128, 1), (512, 4), (2048, 16))

Z_WIDTH = 3584
DIL_TILE0 = 4
DIL_QBLOCKS = 4
DIL_VROWS = 144
ZE_WIDTH = 3072
FOX_PAD = 128
FOX_VROWS = 80
LOG2E = 1.4426950408889634
COL_TILE = 512
TOKEN_TILE = 512
FOX_BLOCK = 256
LANES = 128
V7X_VMEM_LIMIT = 56 * 1024 * 1024

NT_DIMS = (((1,), (1,)), ((), ()))
TN_DIMS = (((0,), (0,)), ((), ()))


def _params(*sem):
    return pltpu.CompilerParams(dimension_semantics=sem, vmem_limit_bytes=V7X_VMEM_LIMIT)


def _split2(x):
    hi = x.astype(BF16)
    lo = (x - hi.astype(F32)).astype(BF16)
    return hi, lo


def _split3(x):
    hi, lo = _split2(x)
    lo2 = (x - hi.astype(F32) - lo.astype(F32)).astype(BF16)
    return hi, lo, lo2


def _log_sigmoid(x):
    return jnp.minimum(x, 0.0) - jnp.log(1.0 + jnp.exp(-jnp.abs(x)))


def _rms_normed(x, g):
    ms = jnp.mean(x * x, axis=-1, keepdims=True)
    return x * lax.rsqrt(ms + EPS) * g


def _in_even_body(x, g_ref, w_ref, wq_ref, wv_ref, ws_ref, bf_ref, wlr_ref, blr_ref, place_ref,
                  qaug_ref, vaug_ref, z_ref, qt_ref, vt_ref, glog_ref, u_ref, carry_ref):
    tm = x.shape[0]

    @pl.when(pl.program_id(1) == 0)
    def _():
        carry_ref[...] = jnp.zeros_like(carry_ref)

    u_ref[...] = _rms_normed(x, g_ref[...]).astype(BF16)
    u = u_ref[...]

    zs = jnp.dot(u, ws_ref[...], preferred_element_type=F32)
    logf = _log_sigmoid(zs + bf_ref[...])
    row = lax.broadcasted_iota(jnp.int32, (LANES, LANES), 0)
    col = lax.broadcasted_iota(jnp.int32, (LANES, LANES), 1)
    tril = jnp.where(row >= col, 1.0, 0.0).astype(BF16)
    carry = carry_ref[...]
    blocks = []
    for r0 in range(0, tm, LANES):
        blk_sum = sum(jnp.dot(tril, t[r0:r0 + LANES, :], preferred_element_type=F32)
                      for t in _split3(logf)) + carry
        carry = blk_sum[LANES - 1:LANES, :]
        blocks.append(blk_sum)
    csum = jnp.concatenate(blocks, axis=0)
    carry_ref[...] = carry
    hi, lo, lo2 = (t.astype(F32) for t in _split3(csum * LOG2E))
    lane = lax.broadcasted_iota(jnp.int32, (tm, LANES), 1)
    packed = jnp.where(lane < ATT_HEADS, hi,
                       jnp.where(lane < 2 * ATT_HEADS, pltpu.roll(lo, ATT_HEADS, 1),
                                 jnp.where(lane < 3 * ATT_HEADS, pltpu.roll(lo2, 2 * ATT_HEADS, 1), 0.0)))
    c_cols = jnp.dot(packed.astype(BF16), place_ref[...], preferred_element_type=F32)

    glr = jnp.dot(zs.astype(BF16), wlr_ref[...], preferred_element_type=F32)
    glog_ref[0] = _log_sigmoid(glr + blr_ref[...]) * (1.0 / GLA_TAU)

    for j in range(ZE_WIDTH // COL_TILE):
        cols = slice(j * COL_TILE, (j + 1) * COL_TILE)
        r = jnp.dot(u, w_ref[:, cols], preferred_element_type=F32)
        if j in (2, 3):
            r = r + c_cols[:, (j - 2) * COL_TILE:(j - 1) * COL_TILE]
        z_ref[0, :, cols] = r.astype(BF16)

    def feature_major(w_t_ref, aug_ref, out_ref, tile):
        for j in range(w_t_ref.shape[0] // tile):
            rows = slice(j * tile, (j + 1) * tile)
            r = lax.dot_general(w_t_ref[rows, :], u, NT_DIMS, preferred_element_type=F32)
            aug = aug_ref[rows, :]
            for c in range(tm // LANES):
                lanes = slice(c * LANES, (c + 1) * LANES)
                out_ref[0, rows, lanes] = (r[:, lanes] + aug).astype(BF16)

    feature_major(wv_ref, vaug_ref, vt_ref, ATT_HEADS * FOX_VROWS // 2)
    qtok = jnp.dot(u, wq_ref[...], preferred_element_type=F32)
    aug = qaug_ref[...].astype(BF16)
    for g in range(ATT_HEADS // 2):
        q_t = qtok[:, g * LANES:(g + 1) * LANES].T
        for half in range(2):
            r0 = (2 * g + half) * FOX_PAD
            qt_ref[0, r0:r0 + HEAD_DIM, :] = q_t[half * HEAD_DIM:(half + 1) * HEAD_DIM, :].astype(BF16)
            qt_ref[0, r0 + HEAD_DIM:r0 + FOX_PAD, :] = aug


def _rope_tile(x, cos, sin):
    lane = lax.broadcasted_iota(jnp.int32, (x.shape[0], LANES), 1)
    first_half = (lane % HEAD_DIM) < (HEAD_DIM // 2)
    outs = []
    for c in range(x.shape[1] // LANES):
        xc = x[:, c * LANES:(c + 1) * LANES]
        partner = jnp.where(first_half,
                            pltpu.roll(xc, LANES - HEAD_DIM // 2, 1),
                            pltpu.roll(xc, HEAD_DIM // 2, 1))
        outs.append(xc * cos + partner * sin)
    return jnp.concatenate(outs, axis=-1)


def _in_odd_body(x, g_ref, w_ref, cos_ref, sin_ref, z_ref, zd4_ref, zd16_ref, u_ref, r_ref):
    tm = x.shape[0]
    u_ref[...] = _rms_normed(x, g_ref[...]).astype(BF16)
    u = u_ref[...]
    cos = cos_ref[...]
    sin = sin_ref[...]
    for j in range(Z_WIDTH // COL_TILE):
        cols = slice(j * COL_TILE, (j + 1) * COL_TILE)
        r = jnp.dot(u, w_ref[:, cols], preferred_element_type=F32)
        if j in (3, 4, 5):
            r = _rope_tile(r, cos, sin)
        z_ref[0, :, cols] = r.astype(BF16)
        if j >= DIL_TILE0:
            t = j - DIL_TILE0
            for c in range(COL_TILE // LANES):
                slot = t * (COL_TILE // LANES) + c
                r_ref[slot] = r[:, c * LANES:(c + 1) * LANES]
                dcols = slice(t * COL_TILE + c * LANES, t * COL_TILE + (c + 1) * LANES)
                for dil, ref in ((4, zd4_ref), (16, zd16_ref)):
                    for res in range(dil):
                        ref[0, res, :, dcols] = (
                            r_ref[slot, pl.ds(res, tm // dil, stride=dil), :].astype(BF16))


def _fox_kernel(qt_ref, kp_ref, vt_ref, o_ref, s_ref, acc_ref, mp_ref, mc_ref, ot_ref, mask_ref, *, blk):
    i = pl.program_id(1)
    heads = range(ATT_HEADS)
    rows = lambda h: slice(h * FOX_PAD, (h + 1) * FOX_PAD)
    vrows = lambda h: slice(h * FOX_VROWS, (h + 1) * FOX_VROWS)

    @pl.when(i == 0)
    def _():
        key = lax.broadcasted_iota(jnp.int32, (blk, blk), 0)
        qry = lax.broadcasted_iota(jnp.int32, (blk, blk), 1)
        mask_ref[0] = jnp.zeros((blk, blk), F32)
        mask_ref[1] = jnp.where(key <= qry, 0.0, -jnp.inf)

    def score_stage(h, j):
        ks = pl.multiple_of(j * blk, blk)
        s = jnp.dot(kp_ref[0, pl.ds(ks, blk), rows(h)], qt_ref[0, rows(h), :],
                    preferred_element_type=F32)
        s = s + mask_ref[jnp.where(j == i, 1, 0)]
        s_ref[h] = s
        m_old = mc_ref[h]
        mp_ref[h] = m_old
        mc_ref[h] = jnp.maximum(m_old, jnp.max(s, axis=0, keepdims=True))

    def value_stage(h, j):
        ks = pl.multiple_of(j * blk, blk)
        m_new = mc_ref[h]
        alpha = jnp.exp2(mp_ref[h] - m_new)
        p = jnp.exp2((s_ref[h] - m_new).astype(BF16))
        pv = jnp.dot(vt_ref[0, vrows(h), pl.ds(ks, blk)], p, preferred_element_type=F32)
        acc_ref[h] = alpha * acc_ref[h] + pv

    def trip(j):
        for h in heads:
            value_stage(h, j - 1)
            score_stage(h, j)

    for h in heads:
        acc_ref[h] = jnp.zeros(acc_ref.shape[1:], F32)
        mc_ref[h] = jnp.full((1, blk), -jnp.inf, F32)
        score_stage(h, 0)

    odd = i % 2

    @pl.when(odd == 1)
    def _():
        trip(1)

    def pair(t, _):
        j = 1 + odd + 2 * t
        trip(j)
        trip(j + 1)
        return 0

    lax.fori_loop(0, i // 2, pair, 0)

    for h in heads:
        value_stage(h, i)
        acc = acc_ref[h]
        ot_ref[h * HEAD_DIM:(h + 1) * HEAD_DIM, :] = (
            acc[:HEAD_DIM, :] / acc[HEAD_DIM:HEAD_DIM + 1, :])
    o_ref[0] = ot_ref[...].T.astype(BF16)


def _fox_attention(z, q_t, v_t, blk):
    B, S, _ = z.shape
    fox_rows = ATT_HEADS * FOX_PAD
    val_rows = ATT_HEADS * FOX_VROWS
    return pl.pallas_call(
        functools.partial(_fox_kernel, blk=blk),
        grid=(B, S // blk),
        in_specs=[
            pl.BlockSpec((1, fox_rows, blk), lambda b, i: (b, 0, i)),
            pl.BlockSpec((1, S, fox_rows), lambda b, i: (b, 0, 1)),
            pl.BlockSpec((1, val_rows, S), lambda b, i: (b, 0, 0)),
        ],
        out_specs=pl.BlockSpec((1, blk, COL_TILE), lambda b, i: (b, i, 0)),
        out_shape=jax.ShapeDtypeStruct((B, S, COL_TILE), BF16),
        scratch_shapes=[pltpu.VMEM((ATT_HEADS, blk, blk), F32),
                        pltpu.VMEM((ATT_HEADS, FOX_VROWS, blk), F32),
                        pltpu.VMEM((ATT_HEADS, 1, blk), F32),
                        pltpu.VMEM((ATT_HEADS, 1, blk), F32),
                        pltpu.VMEM((ATT_HEADS * HEAD_DIM, blk), F32),
                        pltpu.VMEM((2, blk, blk), F32)],
        compiler_params=_params("arbitrary", "arbitrary"),
        name="fox_attention",
    )(q_t, z, v_t)


def _gla_kernel(q_ref, k_ref, v_ref, g_ref, gn_ref, o_ref,
                qhat_ref, top_ref, bot_ref, kinc_ref, dec_ref, dect_ref, state_ref):
    tg = q_ref.shape[1]
    C = GLA_CHUNK
    blk = 2 * C
    n_chunks = tg // C
    pairs = REC_HEADS // 2

    @pl.when(pl.program_id(1) == 0)
    def _():
        state_ref[...] = jnp.zeros_like(state_ref)

    row = lax.broadcasted_iota(jnp.int32, (blk, blk), 0)
    col = lax.broadcasted_iota(jnp.int32, (blk, blk), 1)
    chunk_start = row & (-C)
    tril = jnp.where(col <= row, jnp.where(col >= chunk_start, 1.0, 0.0), 0.0).astype(BF16)
    ghi, glo = _split2(g_ref[0])
    b = jnp.concatenate(
        [jnp.dot(tril, ghi[r0:r0 + blk, :], preferred_element_type=F32)
         + jnp.dot(tril, glo[r0:r0 + blk, :], preferred_element_type=F32)
         for r0 in range(0, tg, blk)], axis=0)
    q = q_ref[0].astype(F32)
    k = k_ref[0].astype(F32)
    tok = lax.broadcasted_iota(jnp.int32, (tg, LANES), 0)
    lane = lax.broadcasted_iota(jnp.int32, (tg, LANES), 1)
    in_a = (tok & C) == 0
    first = lane < HEAD_DIM
    qt = q * jnp.exp(b)
    kt = k * jnp.exp(-b)
    top_ref[...] = kt
    dec_ref[...] = jnp.zeros_like(dec_ref)
    kd_rows, dec_rows = [], []
    for c in range(n_chunks):
        rows = slice(c * C, (c + 1) * C)
        b_last = b[(c + 1) * C - 1:(c + 1) * C, :]
        kd_rows.append(k[rows, :] * jnp.exp(b_last - b[rows, :]))
        dec_rows.append(jnp.exp(b_last))
        dec_ref[c:c + 1, :] = dec_rows[-1]
    for c in range(0, n_chunks, 2):
        bot_ref[c * C:(c + 1) * C, :] = kd_rows[c]
        bot_ref[(c + 1) * C:(c + 2) * C, :] = kt[(c + 1) * C:(c + 2) * C, :]
        kinc_ref[c * C:(c + 1) * C, :] = kd_rows[c] * dec_rows[c + 1]
        kinc_ref[(c + 1) * C:(c + 2) * C, :] = kd_rows[c + 1]
    dect_ref[...] = dec_ref[...].T
    for half in range(2):
        for g in range(pairs):
            gs = slice(g * LANES, (g + 1) * LANES)
            qh = jnp.where(first, qt[:, gs], 0.0) if half == 0 else jnp.where(first, 0.0, qt[:, gs])
            qhat_ref[half, :, 2 * g * LANES:(2 * g + 1) * LANES] = jnp.where(in_a, qh, 0.0).astype(BF16)
            qhat_ref[half, :, (2 * g + 1) * LANES:(2 * g + 2) * LANES] = jnp.where(in_a, 0.0, qh).astype(BF16)

    rb = lax.broadcasted_iota(jnp.int32, (blk, blk), 0)
    cb = lax.broadcasted_iota(jnp.int32, (blk, blk), 1)
    lower = cb <= rb
    low_rows = rb < HEAD_DIM

    for n in range(tg // blk):
        rows = slice(n * blk, (n + 1) * blk)
        for g in range(pairs):
            gs = slice(g * LANES, (g + 1) * LANES)
            top_t = top_ref[rows, gs].T.astype(BF16)
            bot_t = bot_ref[rows, gs].T.astype(BF16)
            kinc_t = kinc_ref[rows, gs].T.astype(BF16)
            st = state_ref[g]
            dec_a = dect_ref[gs, 2 * n:2 * n + 1]
            dec_b = dect_ref[gs, 2 * n + 1:2 * n + 2]
            rhs = jnp.concatenate([jnp.concatenate([top_t, st.astype(BF16)], axis=1),
                                   jnp.concatenate([bot_t, (st * dec_a).astype(BF16)], axis=1)], axis=0)
            inc = []
            for half in range(2):
                h = 2 * g + half
                vs = slice(h * VAL_DIM, (h + 1) * VAL_DIM)
                v = v_ref[0, rows, vs]
                sc = jnp.dot(qhat_ref[half, rows, 2 * g * LANES:(2 * g + 2) * LANES], rhs,
                             preferred_element_type=F32)
                att = jnp.where(lower, sc[:, :blk], 0.0).astype(BF16)
                both = jnp.dot(jnp.concatenate([att, kinc_t], axis=0), v, preferred_element_type=F32)
                o = both[:blk, :] + sc[:, blk:]
                inc.append(both[blk:, :])
                ms = jnp.mean(o * o, axis=-1, keepdims=True)
                o_ref[0, rows, vs] = (o * lax.rsqrt(ms + EPS) * gn_ref[:, vs]).astype(BF16)
            state_ref[g] = st * (dec_a * dec_b) + jnp.where(low_rows, inc[0], inc[1])


def _gla(z, glog, gla_g, tg):
    B, S, _ = z.shape
    kw = REC_HEADS * HEAD_DIM
    vw = REC_HEADS * VAL_DIM
    return pl.pallas_call(
        _gla_kernel,
        grid=(B, S // tg),
        in_specs=[
            pl.BlockSpec((1, tg, kw), lambda b, s: (b, s, 10)),
            pl.BlockSpec((1, tg, kw), lambda b, s: (b, s, 11)),
            pl.BlockSpec((1, tg, vw), lambda b, s: (b, s, 4)),
            pl.BlockSpec((1, tg, kw), lambda b, s: (b, s, 0)),
            pl.BlockSpec((1, vw), lambda b, s: (0, 0)),
        ],
        out_specs=pl.BlockSpec((1, tg, vw), lambda b, s: (b, s, 0)),
        out_shape=jax.ShapeDtypeStruct((B, S, vw), BF16),
        scratch_shapes=[
            pltpu.VMEM((2, tg, 2 * kw), BF16),
            pltpu.VMEM((tg, kw), F32), pltpu.VMEM((tg, kw), F32), pltpu.VMEM((tg, kw), F32),
            pltpu.VMEM((LANES, kw), F32), pltpu.VMEM((kw, LANES), F32),
            pltpu.VMEM((REC_HEADS // 2, 2 * HEAD_DIM, VAL_DIM), F32),
        ],
        compiler_params=_params("arbitrary", "arbitrary"),
        name="gla",
    )(z, z, z, glog, gla_g)


def _ret_kernel(q_ref, k_ref, v_ref, gw_ref, gb_ref, o_ref, state_ref, dmat_ref, zeta_ref):
    tg = q_ref.shape[1]
    C = RET_CHUNK
    n_chunks = tg // C
    pairs = REC_HEADS // 2
    log_gamma = [math.log(1.0 - 2.0 ** (-5.0 - h)) for h in range(REC_HEADS)]

    @pl.when(pl.program_id(1) == 0)
    def _():
        state_ref[...] = jnp.zeros_like(state_ref)

    ri = lax.broadcasted_iota(jnp.int32, (C, C), 0)
    ci = lax.broadcasted_iota(jnp.int32, (C, C), 1)
    diff = (ri - ci).astype(F32)
    idx = lax.broadcasted_iota(jnp.int32, (C, 1), 0).astype(F32)
    low_lanes = ci < HEAD_DIM
    low_rows = ri < HEAD_DIM
    for h in range(REC_HEADS):
        dmat_ref[h] = jnp.where(diff >= 0, jnp.exp(jnp.maximum(diff, 0.0) * log_gamma[h]), 0.0)
    for g in range(pairs):
        zeta_ref[g] = jnp.where(low_lanes, jnp.exp((C - 1.0 - idx) * log_gamma[2 * g]),
                                jnp.exp((C - 1.0 - idx) * log_gamma[2 * g + 1]))

    for c in range(n_chunks):
        rows = slice(c * C, (c + 1) * C)
        for g in range(pairs):
            gs = slice(g * LANES, (g + 1) * LANES)
            q_pair = q_ref[0, rows, gs]
            k_f32 = k_ref[0, rows, gs].astype(F32)
            k_t = k_f32.T.astype(BF16)
            kz_t = (k_f32 * zeta_ref[g]).T.astype(BF16)
            st = state_ref[g]
            rhs = jnp.concatenate([k_t, st.astype(BF16)], axis=1)
            zero = jnp.zeros_like(q_pair)
            inc = []
            for half in range(2):
                h = 2 * g + half
                vs = slice(h * VAL_DIM, (h + 1) * VAL_DIM)
                xi = jnp.exp((idx + 1.0) * log_gamma[h])
                q = jnp.where(low_lanes, q_pair, zero) if half == 0 else jnp.where(low_lanes, zero, q_pair)
                v = v_ref[0, rows, vs]
                sc = jnp.dot(q, rhs, preferred_element_type=F32)
                att = (sc[:, :C] * dmat_ref[h]).astype(BF16)
                both = jnp.dot(jnp.concatenate([att, kz_t], axis=0), v, preferred_element_type=F32)
                o = both[:C, :] + xi * sc[:, C:]
                inc.append(both[C:, :])
                o = o - jnp.mean(o, axis=-1, keepdims=True)
                ms = jnp.mean(o * o, axis=-1, keepdims=True)
                o = o * lax.rsqrt(ms + EPS) * gw_ref[:, vs] + gb_ref[:, vs]
                o_ref[0, rows, vs] = o.astype(BF16)
            decay = jnp.where(low_rows, math.exp(C * log_gamma[2 * g]), math.exp(C * log_gamma[2 * g + 1]))
            state_ref[g] = st * decay + jnp.where(low_rows, inc[0], inc[1])


def _retention(z, gn_w, gn_b, tg):
    B, S, _ = z.shape
    kw = REC_HEADS * HEAD_DIM
    vw = REC_HEADS * VAL_DIM
    return pl.pallas_call(
        _ret_kernel,
        grid=(B, S // tg),
        in_specs=[
            pl.BlockSpec((1, tg, kw), lambda b, s: (b, s, 6)),
            pl.BlockSpec((1, tg, kw), lambda b, s: (b, s, 7)),
            pl.BlockSpec((1, tg, vw), lambda b, s: (b, s, 2)),
            pl.BlockSpec((1, vw), lambda b, s: (0, 0)),
            pl.BlockSpec((1, vw), lambda b, s: (0, 0)),
        ],
        out_specs=pl.BlockSpec((1, tg, vw), lambda b, s: (b, s, 0)),
        out_shape=jax.ShapeDtypeStruct((B, S, vw), BF16),
        scratch_shapes=[pltpu.VMEM((REC_HEADS // 2, 2 * HEAD_DIM, VAL_DIM), F32),
                        pltpu.VMEM((REC_HEADS, RET_CHUNK, RET_CHUNK), F32),
                        pltpu.VMEM((REC_HEADS // 2, RET_CHUNK, LANES), F32)],
        compiler_params=_params("arbitrary", "arbitrary"),
        name="retention",
    )(z, z, z, gn_w, gn_b)


def _dil_kernel(q_ref, k_ref, v_ref, o_ref, lse_ref,
                vt_ref, qt_ref, s_ref, mx_ref, bias_ref, ot_ref, lt_ref, *, blk, nblk, span):
    n = pl.program_id(2)
    m = k_ref.shape[2]
    pairs = ATT_HEADS // 2
    chains = [(g, half) for g in range(pairs) for half in range(2)]
    feat = lax.broadcasted_iota(jnp.int32, (LANES, blk), 0)
    ki = lax.broadcasted_iota(jnp.int32, (2 * blk, blk), 0)
    qi = lax.broadcasted_iota(jnp.int32, (2 * blk, blk), 1)

    @pl.when(n == 0)
    def _():
        for g in range(pairs):
            vt_ref[g * DIL_VROWS + LANES:(g + 1) * DIL_VROWS, :] = jnp.ones(
                (DIL_VROWS - LANES, m), BF16)

        def build(kb, _):
            r0 = pl.multiple_of(kb * blk, blk)
            for g in range(pairs):
                v = v_ref[0, 0, pl.ds(r0, blk), g * LANES:(g + 1) * LANES].astype(F32)
                vt_ref[g * DIL_VROWS:g * DIL_VROWS + LANES, pl.ds(r0, blk)] = v.T.astype(BF16)
            return 0

        lax.fori_loop(0, m // blk, build, 0)

    starts = []
    for t in range(nblk):
        qb = n * nblk + t
        start = pl.multiple_of(jnp.maximum(qb - 1, 0) * blk, blk)
        starts.append(start)
        rel = qb * blk + qi - start - ki
        bias_ref[t] = jnp.where(rel >= 0, jnp.where(rel <= span, 0.0, -jnp.inf), -jnp.inf)
        for g in range(pairs):
            q_t = q_ref[0, 0, t * blk:(t + 1) * blk, g * LANES:(g + 1) * LANES].astype(F32).T
            qt_ref[t, 2 * g] = jnp.where(feat < HEAD_DIM, q_t, 0.0).astype(BF16)
            qt_ref[t, 2 * g + 1] = jnp.where(feat < HEAD_DIM, 0.0, q_t).astype(BF16)

    for t in range(nblk):
        for c, (g, half) in enumerate(chains):
            k = k_ref[0, 0, pl.ds(starts[t], 2 * blk), g * LANES:(g + 1) * LANES]
            s = jnp.dot(k, qt_ref[t, c], preferred_element_type=F32) + bias_ref[t]
            s_ref[t, c] = s
            mx_ref[t, c] = jnp.max(s, axis=0, keepdims=True)

    for t in range(nblk):
        lt_ref[t] = jnp.zeros(lt_ref.shape[1:], F32)
        for c, (g, half) in enumerate(chains):
            mx = mx_ref[t, c]
            p = jnp.exp2((s_ref[t, c] - mx).astype(BF16))
            acc = jnp.dot(vt_ref[g * DIL_VROWS:(g + 1) * DIL_VROWS, pl.ds(starts[t], 2 * blk)], p,
                          preferred_element_type=F32)
            den = acc[LANES:LANES + 1, :]
            ot_ref[t, c * HEAD_DIM:(c + 1) * HEAD_DIM, :] = (
                acc[half * HEAD_DIM:(half + 1) * HEAD_DIM, :] * (1.0 / den))
            lt_ref[t, c:c + 1, :] = (mx + jnp.log2(den)) * (1.0 / LOG2E)
        qrows = slice(t * blk, (t + 1) * blk)
        o_ref[0, 0, qrows, :] = ot_ref[t].T.astype(BF16)
        lse_ref[0, 0, qrows, :] = lt_ref[t].T


def _dilated_branch(src, tile0, window, dil, nblk):
    B, _, m, _ = src.shape
    blk = DIL_BLOCK
    nblk = min(nblk, m // blk)
    tq = nblk * blk
    return pl.pallas_call(
        functools.partial(_dil_kernel, blk=blk, nblk=nblk, span=window // dil),
        grid=(B, dil, m // tq),
        in_specs=[
            pl.BlockSpec((1, 1, tq, COL_TILE), lambda b, r, n: (b, r, n, tile0)),
            pl.BlockSpec((1, 1, m, COL_TILE), lambda b, r, n: (b, r, 0, tile0 + 1)),
            pl.BlockSpec((1, 1, m, COL_TILE), lambda b, r, n: (b, r, 0, tile0 + 2)),
        ],
        out_specs=[
            pl.BlockSpec((1, 1, tq, COL_TILE), lambda b, r, n: (b, r, n, 0)),
            pl.BlockSpec((1, 1, tq, LANES), lambda b, r, n: (b, r, n, 0)),
        ],
        out_shape=[
            jax.ShapeDtypeStruct((B, dil, m, COL_TILE), BF16),
            jax.ShapeDtypeStruct((B, dil, m, LANES), F32),
        ],
        scratch_shapes=[pltpu.VMEM((ATT_HEADS // 2 * DIL_VROWS, m), BF16),
                        pltpu.VMEM((nblk, ATT_HEADS, LANES, blk), BF16),
                        pltpu.VMEM((nblk, ATT_HEADS, 2 * blk, blk), F32),
                        pltpu.VMEM((nblk, ATT_HEADS, 1, blk), F32),
                        pltpu.VMEM((nblk, 2 * blk, blk), F32),
                        pltpu.VMEM((nblk, ATT_HEADS * HEAD_DIM, blk), F32),
                        pltpu.VMEM((nblk, LANES, blk), F32)],
        compiler_params=_params("arbitrary", "arbitrary", "arbitrary"),
        name=f"dilated_{dil}",
    )(src, src, src)


def _silu(x):
    return x / (1.0 + jnp.exp(-x))


def _mix_out(mix_a, mix_b, gate_ref, h_ref, w_ref):
    half = mix_a.shape[1]
    gate = _silu(gate_ref[0].astype(F32))
    a = (mix_a * gate[:, :half]).astype(BF16)
    b = (mix_b * gate[:, half:]).astype(BF16)
    y = (jnp.dot(a, w_ref[:half, :], preferred_element_type=F32)
         + jnp.dot(b, w_ref[half:, :], preferred_element_type=F32))
    return h_ref[0] + y


def _dilated_mix(d1_ref, d4_ref, d16_ref, l1_ref, l4_ref, l16_ref, e_ref,
                 o4_ref, o16_ref, ls4_ref, ls16_ref):
    tm = d1_ref.shape[1]
    chunks = d1_ref.shape[2] // LANES
    for dil, src, lsrc, dst, ldst in ((4, d4_ref, l4_ref, o4_ref, ls4_ref),
                                      (16, d16_ref, l16_ref, o16_ref, ls16_ref)):
        for res in range(dil):
            token_rows = pl.ds(res, tm // dil, stride=dil)
            ldst[token_rows, :] = lsrc[0, res]
            for c in range(chunks):
                dst[c, token_rows, :] = src[0, res, :, c * LANES:(c + 1) * LANES].astype(F32)
    l1, l2, l3 = l1_ref[0], ls4_ref[...], ls16_ref[...]
    mx = jnp.maximum(jnp.maximum(l1, l2), l3)
    e1, e2, e3 = jnp.exp(l1 - mx), jnp.exp(l2 - mx), jnp.exp(l3 - mx)
    inv = 1.0 / (e1 + e2 + e3)
    e = e_ref[...]
    expand = lambda w: sum(jnp.dot(t, e, preferred_element_type=F32) for t in _split2(w))
    o4 = jnp.concatenate([o4_ref[c] for c in range(chunks)], axis=-1)
    o16 = jnp.concatenate([o16_ref[c] for c in range(chunks)], axis=-1)
    return (expand(e1 * inv) * d1_ref[0].astype(F32) + expand(e2 * inv) * o4
            + expand(e3 * inv) * o16)


N_IN = {"even": 11, "odd": 4}
N_OUT = {"even": 4, "odd": 3}
N_SCRATCH = {"even": 2, "odd": 2}


def _proj_kernel(*refs, prev, nxt, final):
    refs = list(refs)
    take = lambda n: [refs.pop(0) for _ in range(n)]
    if prev is None:
        (x_ref,) = take(1)
    elif prev == "even":
        oa_ref, ob_ref, gate_ref, h_ref, wout_ref = take(5)
    else:
        oc_ref, d1_ref, d4_ref, d16_ref, l1_ref, l4_ref, l16_ref, e_ref, gate_ref, h_ref, wout_ref = take(11)
    if final:
        (fg_ref,) = take(1)
    in_params = take(N_IN[nxt]) if nxt else []
    if prev is not None:
        (hout_ref,) = take(1)
    in_outs = take(N_OUT[nxt]) if nxt else []
    mix_scratch = take(4) if prev == "odd" else []
    in_scratch = take(N_SCRATCH[nxt]) if nxt else []

    if prev is None:
        x = x_ref[0]
    elif prev == "even":
        x = _mix_out(oa_ref[0].astype(F32), ob_ref[0].astype(F32), gate_ref, h_ref, wout_ref)
    else:
        o_d = _dilated_mix(d1_ref, d4_ref, d16_ref, l1_ref, l4_ref, l16_ref, e_ref, *mix_scratch)
        x = _mix_out(oc_ref[0].astype(F32), o_d, gate_ref, h_ref, wout_ref)
    if final:
        x = _rms_normed(x, fg_ref[...])
    if prev is not None:
        hout_ref[0] = x
    if nxt == "even":
        _in_even_body(x, *in_params, *in_outs, *in_scratch)
    elif nxt == "odd":
        _in_odd_body(x, *in_params, *in_outs, *in_scratch)


def _proj_call(h, prev, prev_args, nxt, nxt_args, final_g, tm):
    B, S, D = h.shape
    half = COL_TILE
    tok = lambda b, s: (b, s, 0)
    const2 = lambda b, s: (0, 0)
    once = dict(pipeline_mode=pl.Buffered(1))
    weight = lambda shape: pl.BlockSpec(shape, (lambda b, s: (0,) * len(shape)), **once)
    res_spec = lambda dil, c: pl.BlockSpec((1, dil, tm // dil, c), lambda b, s: (b, 0, s, 0))
    part = pl.BlockSpec((1, tm, half), tok)
    h_spec = pl.BlockSpec((1, tm, D), tok)
    args, in_specs, out_specs, out_shape, scratch = [], [], [], [], []

    if prev is None:
        args += [h]
        in_specs += [h_spec]
    elif prev == "even":
        o_a, o_b, z, w_out = prev_args
        args += [o_a, o_b, z, h, w_out]
        in_specs += [part, part, pl.BlockSpec((1, tm, 2 * half), tok), h_spec, weight((2 * half, D))]
    else:
        o_c, outs, lses, z, w_out = prev_args
        expand = jnp.pad(jnp.repeat(jnp.eye(ATT_HEADS, dtype=BF16), HEAD_DIM, axis=1),
                         ((0, LANES - ATT_HEADS), (0, 0)))
        args += [o_c, *outs, *lses, expand, z, h, w_out]
        in_specs += [part, part, res_spec(4, half), res_spec(16, half),
                     pl.BlockSpec((1, tm, LANES), tok), res_spec(4, LANES), res_spec(16, LANES),
                     weight((LANES, half)), pl.BlockSpec((1, tm, 2 * half), tok), h_spec,
                     weight((2 * half, D))]
    final = nxt is None
    if final:
        args += [final_g]
        in_specs += [weight((1, D))]

    if nxt == "even":
        g, w_main, wq_t, wv_t, w_small, b_f, w_lr, b_lr, place, q_aug, v_aug = nxt_args
        fox_rows, val_rows, kw = ATT_HEADS * FOX_PAD, ATT_HEADS * FOX_VROWS, w_lr.shape[1]
        args += list(nxt_args)
        in_specs += [weight(a.shape) for a in nxt_args]
    elif nxt == "odd":
        g, w_main, cos, sin = nxt_args
        args += list(nxt_args)
        in_specs += [weight(g.shape), weight(w_main.shape),
                     pl.BlockSpec((tm, LANES), lambda b, s: (s, 0)),
                     pl.BlockSpec((tm, LANES), lambda b, s: (s, 0))]

    if prev is not None:
        out_specs += [h_spec]
        out_shape += [jax.ShapeDtypeStruct((B, S, D), F32)]
    if nxt == "even":
        out_specs += [pl.BlockSpec((1, tm, ZE_WIDTH), tok),
                      pl.BlockSpec((1, fox_rows, tm), lambda b, s: (b, 0, s)),
                      pl.BlockSpec((1, val_rows, tm), lambda b, s: (b, 0, s)),
                      pl.BlockSpec((1, tm, kw), tok)]
        out_shape += [jax.ShapeDtypeStruct((B, S, ZE_WIDTH), BF16),
                      jax.ShapeDtypeStruct((B, fox_rows, S), BF16),
                      jax.ShapeDtypeStruct((B, val_rows, S), BF16),
                      jax.ShapeDtypeStruct((B, S, kw), F32)]
    elif nxt == "odd":
        dw = 3 * COL_TILE
        out_specs += [pl.BlockSpec((1, tm, Z_WIDTH), tok), res_spec(4, dw), res_spec(16, dw)]
        out_shape += [jax.ShapeDtypeStruct((B, S, Z_WIDTH), BF16),
                      jax.ShapeDtypeStruct((B, 4, S // 4, dw), BF16),
                      jax.ShapeDtypeStruct((B, 16, S // 16, dw), BF16)]

    if prev == "odd":
        scratch += [pltpu.VMEM((half // LANES, tm, LANES), F32),
                    pltpu.VMEM((half // LANES, tm, LANES), F32),
                    pltpu.VMEM((tm, LANES), F32), pltpu.VMEM((tm, LANES), F32)]
    if nxt == "even":
        scratch += [pltpu.VMEM((tm, D), BF16), pltpu.VMEM((1, LANES), F32)]
    elif nxt == "odd":
        scratch += [pltpu.VMEM((tm, D), BF16), pltpu.VMEM((3 * COL_TILE // LANES, tm, LANES), F32)]

    outs = pl.pallas_call(
        functools.partial(_proj_kernel, prev=prev, nxt=nxt, final=final),
        grid=(B, S // tm),
        in_specs=in_specs,
        out_specs=out_specs,
        out_shape=out_shape,
        scratch_shapes=scratch,
        compiler_params=_params("arbitrary", "arbitrary"),
        name=f"proj_{prev}_{nxt}",
    )(*args)
    return outs


def _prep_even(w_in, b_f, w_lr, b_lr):
    fw = ATT_HEADS * HEAD_DIM
    kw = REC_HEADS * HEAD_DIM
    vw = REC_HEADS * VAL_DIM
    sizes = (fw, fw, fw, ATT_HEADS, kw, kw, vw, GLA_RANK, fw + vw)
    offs = np.cumsum((0,) + sizes)
    fq, fk, fv, ff, gq, gk, gv, glr, gate = (w_in[:, offs[i]:offs[i + 1]] for i in range(9))
    scale = HEAD_DIM ** -0.5
    D = w_in.shape[0]

    def pad_heads(w, width):
        w = w.reshape(D, ATT_HEADS, HEAD_DIM)
        return jnp.pad(w, ((0, 0), (0, 0), (0, width - HEAD_DIM))).reshape(D, ATT_HEADS * width)

    w_main = jnp.concatenate([gate, pad_heads(fk, FOX_PAD), gv, gq * scale, gk], axis=1).astype(BF16)
    wq = (fq * (scale * LOG2E)).astype(BF16)
    wv_t = pad_heads(fv, FOX_VROWS).T.astype(BF16)
    q_aug = np.zeros((FOX_PAD - HEAD_DIM, TOKEN_TILE), np.float32)
    q_aug[:3, :] = -1.0
    v_aug = np.zeros((ATT_HEADS, FOX_VROWS, LANES), np.float32)
    v_aug[:, HEAD_DIM, :] = 1.0
    pad = LANES - ATT_HEADS - GLA_RANK
    w_small = jnp.pad(jnp.concatenate([ff, glr], axis=1), ((0, 0), (0, pad))).astype(BF16)
    b_f_pad = jnp.pad(b_f, (0, LANES - ATT_HEADS)).reshape(1, LANES)
    w_lr_pad = jnp.pad(w_lr, ((ATT_HEADS, pad), (0, 0))).astype(BF16)
    place = np.zeros((LANES, ATT_HEADS * FOX_PAD), np.float32)
    for i in range(3):
        for h in range(ATT_HEADS):
            place[ATT_HEADS * i + h, h * FOX_PAD + HEAD_DIM + i] = 1.0
    return (w_main, wq, wv_t, w_small, b_f_pad, w_lr_pad, b_lr.reshape(1, kw),
            jnp.asarray(place, BF16), jnp.asarray(q_aug), jnp.asarray(v_aug.reshape(-1, LANES)))


def _prep_odd(w_in):
    kw = REC_HEADS * HEAD_DIM
    vw = REC_HEADS * VAL_DIM
    dw = ATT_HEADS * HEAD_DIM
    sizes = (kw, kw, vw, dw, dw, dw, vw + dw)
    offs = np.cumsum((0,) + sizes)
    rq, rk, rv, dq, dk, dv, gate = (w_in[:, offs[i]:offs[i + 1]] for i in range(7))
    scale = HEAD_DIM ** -0.5
    return jnp.concatenate([gate, rv, rq * scale, rk, dq * (scale * LOG2E), dk, dv], axis=1).astype(BF16)


def _rope_tables(S):
    inv = jnp.power(ROPE_THETA, -jnp.arange(0, HEAD_DIM, 2, dtype=F32) / HEAD_DIM)
    ang = jnp.arange(S, dtype=F32)[:, None] * inv[None, :]
    cos, sin = jnp.cos(ang), jnp.sin(ang)
    reps = LANES // HEAD_DIM
    cos_t = jnp.tile(jnp.concatenate([cos, cos], axis=1), (1, reps))
    sin_t = jnp.tile(jnp.concatenate([-sin, sin], axis=1), (1, reps))
    return cos_t, sin_t


def _tile(S, pref):
    return min(S, pref)


def _even_mixers(z, q_t, v_t, glog, gla_g):
    S = z.shape[1]
    o_a = _fox_attention(z, q_t, v_t, _tile(S, FOX_BLOCK))
    o_b = _gla(z, glog, gla_g.reshape(1, -1), _tile(S, TOKEN_TILE))
    return o_a, o_b


def _odd_mixers(z, zd4, zd16, gn_w, gn_b):
    B, S, _ = z.shape
    o_c = _retention(z, gn_w.reshape(1, -1), gn_b.reshape(1, -1), _tile(S, TOKEN_TILE))
    sources = {1: (z.reshape(B, 1, S, Z_WIDTH), DIL_TILE0), 4: (zd4, 0), 16: (zd16, 0)}
    outs, lses = zip(*[_dilated_branch(*sources[dil], window, dil, DIL_QBLOCKS)
                       for window, dil in DIL_PATTERNS])
    outs = (outs[0].reshape(B, S, COL_TILE), *outs[1:])
    lses = (lses[0].reshape(B, S, LANES), *lses[1:])
    return o_c, outs, lses


def kernel(x, norm_even, w_in_even, b_f_even, w_lr_even, b_lr_even, gla_norm_even, w_out_even,
           norm_odd, w_in_odd, ret_gn_w_odd, ret_gn_b_odd, w_out_odd, final_norm):
    depth = norm_even.shape[0] + norm_odd.shape[0]
    B, S, D = x.shape
    tm = _tile(S, TOKEN_TILE)
    cos, sin = _rope_tables(S)

    def in_args(i):
        j = i // 2
        if i % 2 == 0:
            return "even", (norm_even[j].reshape(1, D),
                            *_prep_even(w_in_even[j], b_f_even[j], w_lr_even[j], b_lr_even[j]))
        return "odd", (norm_odd[j].reshape(1, D), _prep_odd(w_in_odd[j]), cos, sin)

    h = x
    kind, args = in_args(0)
    proj = _proj_call(h, None, None, kind, args, None, tm)
    for i in range(depth):
        j = i // 2
        if i % 2 == 0:
            z = proj[0]
            o_a, o_b = _even_mixers(*proj, gla_norm_even[j])
            prev_args = (o_a, o_b, z, w_out_even[j].astype(BF16))
        else:
            z = proj[0]
            o_c, outs, lses = _odd_mixers(*proj, ret_gn_w_odd[j], ret_gn_b_odd[j])
            prev_args = (o_c, outs, lses, z, w_out_odd[j].astype(BF16))
        nxt, nxt_args = in_args(i + 1) if i + 1 < depth else (None, None)
        h, *proj = _proj_call(h, kind, prev_args, nxt, nxt_args, final_norm.reshape(1, D), tm)
        kind = nxt
    return h
```

```python
import functools
import math

import numpy as np
import jax
import jax.numpy as jnp
from jax import lax
from jax.experimental import pallas as pl
from jax.experimental.pallas import tpu as pltpu

F32 = jnp.float32
BF16 = jnp.bfloat16

EPS = 1e-6
ROPE_THETA = 10000.0
HEAD_DIM = 64
VAL_DIM = 128
ATT_HEADS = 8
REC_HEADS = 4
GLA_RANK = 16
GLA_TAU = 16.0
GLA_CHUNK = 64
RET_CHUNK = 128
DIL_BLOCK = 128
DIL_PATTERNS = ((128, 1), (512, 4), (2048, 16))

Z_WIDTH = 3584
DIL_TILE0 = 4
DIL_QBLOCKS = 4
DIL_VROWS = 144
ZE_WIDTH = 3072
FOX_PAD = 128
FOX_VROWS = 80
LOG2E = 1.4426950408889634
COL_TILE = 512
TOKEN_TILE = 512
FOX_BLOCK = 256
LANES = 128
V7X_VMEM_LIMIT = 56 * 1024 * 1024

NT_DIMS = (((1,), (1,)), ((), ()))
TN_DIMS = (((0,), (0,)), ((), ()))


def _params(*sem):
    return pltpu.CompilerParams(dimension_semantics=sem, vmem_limit_bytes=V7X_VMEM_LIMIT)


def _split2(x):
    hi = x.astype(BF16)
    lo = (x - hi.astype(F32)).astype(BF16)
    return hi, lo


def _split3(x):
    hi, lo = _split2(x)
    lo2 = (x - hi.astype(F32) - lo.astype(F32)).astype(BF16)
    return hi, lo, lo2


def _log_sigmoid(x):
    return jnp.minimum(x, 0.0) - jnp.log(1.0 + jnp.exp(-jnp.abs(x)))


def _rms_normed(x, g):
    ms = jnp.mean(x * x, axis=-1, keepdims=True)
    return x * lax.rsqrt(ms + EPS) * g


def _in_even_body(x, g_ref, w_ref, wq_ref, wv_ref, ws_ref, bf_ref, wlr_ref, blr_ref, place_ref,
                  qaug_ref, vaug_ref, z_ref, qt_ref, vt_ref, glog_ref, u_ref, carry_ref):
    tm = x.shape[0]

    @pl.when(pl.program_id(1) == 0)
    def _():
        carry_ref[...] = jnp.zeros_like(carry_ref)

    u_ref[...] = _rms_normed(x, g_ref[...]).astype(BF16)
    u = u_ref[...]

    zs = jnp.dot(u, ws_ref[...], preferred_element_type=F32)
    logf = _log_sigmoid(zs + bf_ref[...])
    row = lax.broadcasted_iota(jnp.int32, (LANES, LANES), 0)
    col = lax.broadcasted_iota(jnp.int32, (LANES, LANES), 1)
    tril = jnp.where(row >= col, 1.0, 0.0).astype(BF16)
    carry = carry_ref[...]
    blocks = []
    for r0 in range(0, tm, LANES):
        blk_sum = sum(jnp.dot(tril, t[r0:r0 + LANES, :], preferred_element_type=F32)
                      for t in _split3(logf)) + carry
        carry = blk_sum[LANES - 1:LANES, :]
        blocks.append(blk_sum)
    csum = jnp.concatenate(blocks, axis=0)
    carry_ref[...] = carry
    hi, lo, lo2 = (t.astype(F32) for t in _split3(csum * LOG2E))
    lane = lax.broadcasted_iota(jnp.int32, (tm, LANES), 1)
    packed = jnp.where(lane < ATT_HEADS, hi,
                       jnp.where(lane < 2 * ATT_HEADS, pltpu.roll(lo, ATT_HEADS, 1),
                                 jnp.where(lane < 3 * ATT_HEADS, pltpu.roll(lo2, 2 * ATT_HEADS, 1), 0.0)))
    c_cols = jnp.dot(packed.astype(BF16), place_ref[...], preferred_element_type=F32)

    glr = jnp.dot(zs.astype(BF16), wlr_ref[...], preferred_element_type=F32)
    glog_ref[0] = _log_sigmoid(glr + blr_ref[...]) * (1.0 / GLA_TAU)

    for j in range(ZE_WIDTH // COL_TILE):
        cols = slice(j * COL_TILE, (j + 1) * COL_TILE)
        r = jnp.dot(u, w_ref[:, cols], preferred_element_type=F32)
        if j in (2, 3):
            r = r + c_cols[:, (j - 2) * COL_TILE:(j - 1) * COL_TILE]
        z_ref[0, :, cols] = r.astype(BF16)

    def feature_major(w_t_ref, aug_ref, out_ref, tile):
        for j in range(w_t_ref.shape[0] // tile):
            rows = slice(j * tile, (j + 1) * tile)
            r = lax.dot_general(w_t_ref[rows, :], u, NT_DIMS, preferred_element_type=F32)
            aug = aug_ref[rows, :]
            for c in range(tm // LANES):
                lanes = slice(c * LANES, (c + 1) * LANES)
                out_ref[0, rows, lanes] = (r[:, lanes] + aug).astype(BF16)

    feature_major(wv_ref, vaug_ref, vt_ref, ATT_HEADS * FOX_VROWS // 2)
    qtok = jnp.dot(u, wq_ref[...], preferred_element_type=F32)
    aug = qaug_ref[...].astype(BF16)
    for g in range(ATT_HEADS // 2):
        q_t = qtok[:, g * LANES:(g + 1) * LANES].T
        for half in range(2):
            r0 = (2 * g + half) * FOX_PAD
            qt_ref[0, r0:r0 + HEAD_DIM, :] = q_t[half * HEAD_DIM:(half + 1) * HEAD_DIM, :].astype(BF16)
            qt_ref[0, r0 + HEAD_DIM:r0 + FOX_PAD, :] = aug


def _rope_tile(x, cos, sin):
    lane = lax.broadcasted_iota(jnp.int32, (x.shape[0], LANES), 1)
    first_half = (lane % HEAD_DIM) < (HEAD_DIM // 2)
    outs = []
    for c in range(x.shape[1] // LANES):
        xc = x[:, c * LANES:(c + 1) * LANES]
        partner = jnp.where(first_half,
                            pltpu.roll(xc, LANES - HEAD_DIM // 2, 1),
                            pltpu.roll(xc, HEAD_DIM // 2, 1))
        outs.append(xc * cos + partner * sin)
    return jnp.concatenate(outs, axis=-1)


def _in_odd_body(x, g_ref, w_ref, cos_ref, sin_ref, z_ref, zd4_ref, zd16_ref, u_ref, r_ref):
    tm = x.shape[0]
    u_ref[...] = _rms_normed(x, g_ref[...]).astype(BF16)
    u = u_ref[...]
    cos = cos_ref[...]
    sin = sin_ref[...]
    for j in range(Z_WIDTH // COL_TILE):
        cols = slice(j * COL_TILE, (j + 1) * COL_TILE)
        r = jnp.dot(u, w_ref[:, cols], preferred_element_type=F32)
        if j in (3, 4, 5):
            r = _rope_tile(r, cos, sin)
        z_ref[0, :, cols] = r.astype(BF16)
        if j >= DIL_TILE0:
            t = j - DIL_TILE0
            for c in range(COL_TILE // LANES):
                slot = t * (COL_TILE // LANES) + c
                r_ref[slot] = r[:, c * LANES:(c + 1) * LANES]
                dcols = slice(t * COL_TILE + c * LANES, t * COL_TILE + (c + 1) * LANES)
                for dil, ref in ((4, zd4_ref), (16, zd16_ref)):
                    for res in range(dil):
                        ref[0, res, :, dcols] = (
                            r_ref[slot, pl.ds(res, tm // dil, stride=dil), :].astype(BF16))


def _fox_kernel(qt_ref, kp_ref, vt_ref, o_ref, s_ref, acc_ref, mp_ref, mc_ref, ot_ref, mask_ref, *, blk):
    i = pl.program_id(1)
    heads = range(ATT_HEADS)
    rows = lambda h: slice(h * FOX_PAD, (h + 1) * FOX_PAD)
    vrows = lambda h: slice(h * FOX_VROWS, (h + 1) * FOX_VROWS)

    @pl.when(i == 0)
    def _():
        key = lax.broadcasted_iota(jnp.int32, (blk, blk), 0)
        qry = lax.broadcasted_iota(jnp.int32, (blk, blk), 1)
        mask_ref[0] = jnp.zeros((blk, blk), F32)
        mask_ref[1] = jnp.where(key <= qry, 0.0, -jnp.inf)

    def score_stage(h, j):
        ks = pl.multiple_of(j * blk, blk)
        s = jnp.dot(kp_ref[0, pl.ds(ks, blk), rows(h)], qt_ref[0, rows(h), :],
                    preferred_element_type=F32)
        s = s + mask_ref[jnp.where(j == i, 1, 0)]
        s_ref[h] = s
        m_old = mc_ref[h]
        mp_ref[h] = m_old
        mc_ref[h] = jnp.maximum(m_old, jnp.max(s, axis=0, keepdims=True))

    def value_stage(h, j):
        ks = pl.multiple_of(j * blk, blk)
        m_new = mc_ref[h]
        alpha = jnp.exp2(mp_ref[h] - m_new)
        p = jnp.exp2((s_ref[h] - m_new).astype(BF16))
        pv = jnp.dot(vt_ref[0, vrows(h), pl.ds(ks, blk)], p, preferred_element_type=F32)
        acc_ref[h] = alpha * acc_ref[h] + pv

    def trip(j):
        for h in heads:
            value_stage(h, j - 1)
            score_stage(h, j)

    for h in heads:
        acc_ref[h] = jnp.zeros(acc_ref.shape[1:], F32)
        mc_ref[h] = jnp.full((1, blk), -jnp.inf, F32)
        score_stage(h, 0)

    odd = i % 2

    @pl.when(odd == 1)
    def _():
        trip(1)

    def pair(t, _):
        j = 1 + odd + 2 * t
        trip(j)
        trip(j + 1)
        return 0

    lax.fori_loop(0, i // 2, pair, 0)

    for h in heads:
        value_stage(h, i)
        acc = acc_ref[h]
        ot_ref[h * HEAD_DIM:(h + 1) * HEAD_DIM, :] = (
            acc[:HEAD_DIM, :] / acc[HEAD_DIM:HEAD_DIM + 1, :])
    o_ref[0] = ot_ref[...].T.astype(BF16)


def _fox_attention(z, q_t, v_t, blk):
    B, S, _ = z.shape
    fox_rows = ATT_HEADS * FOX_PAD
    val_rows = ATT_HEADS * FOX_VROWS
    return pl.pallas_call(
        functools.partial(_fox_kernel, blk=blk),
        grid=(B, S // blk),
        in_specs=[
            pl.BlockSpec((1, fox_rows, blk), lambda b, i: (b, 0, i)),
            pl.BlockSpec((1, S, fox_rows), lambda b, i: (b, 0, 1)),
            pl.BlockSpec((1, val_rows, S), lambda b, i: (b, 0, 0)),
        ],
        out_specs=pl.BlockSpec((1, blk, COL_TILE), lambda b, i: (b, i, 0)),
        out_shape=jax.ShapeDtypeStruct((B, S, COL_TILE), BF16),
        scratch_shapes=[pltpu.VMEM((ATT_HEADS, blk, blk), F32),
                        pltpu.VMEM((ATT_HEADS, FOX_VROWS, blk), F32),
                        pltpu.VMEM((ATT_HEADS, 1, blk), F32),
                        pltpu.VMEM((ATT_HEADS, 1, blk), F32),
                        pltpu.VMEM((ATT_HEADS * HEAD_DIM, blk), F32),
                        pltpu.VMEM((2, blk, blk), F32)],
        compiler_params=_params("arbitrary", "arbitrary"),
        name="fox_attention",
    )(q_t, z, v_t)


def _gla_kernel(q_ref, k_ref, v_ref, g_ref, gn_ref, o_ref,
                qhat_ref, top_ref, bot_ref, kinc_ref, dec_ref, dect_ref, state_ref):
    tg = q_ref.shape[1]
    C = GLA_CHUNK
    blk = 2 * C
    n_chunks = tg // C
    pairs = REC_HEADS // 2

    @pl.when(pl.program_id(1) == 0)
    def _():
        state_ref[...] = jnp.zeros_like(state_ref)

    row = lax.broadcasted_iota(jnp.int32, (blk, blk), 0)
    col = lax.broadcasted_iota(jnp.int32, (blk, blk), 1)
    chunk_start = row & (-C)
    tril = jnp.where(col <= row, jnp.where(col >= chunk_start, 1.0, 0.0), 0.0).astype(BF16)
    ghi, glo = _split2(g_ref[0])
    b = jnp.concatenate(
        [jnp.dot(tril, ghi[r0:r0 + blk, :], preferred_element_type=F32)
         + jnp.dot(tril, glo[r0:r0 + blk, :], preferred_element_type=F32)
         for r0 in range(0, tg, blk)], axis=0)
    q = q_ref[0].astype(F32)
    k = k_ref[0].astype(F32)
    tok = lax.broadcasted_iota(jnp.int32, (tg, LANES), 0)
    lane = lax.broadcasted_iota(jnp.int32, (tg, LANES), 1)
    in_a = (tok & C) == 0
    first = lane < HEAD_DIM
    qt = q * jnp.exp(b)
    kt = k * jnp.exp(-b)
    top_ref[...] = kt
    dec_ref[...] = jnp.zeros_like(dec_ref)
    kd_rows, dec_rows = [], []
    for c in range(n_chunks):
        rows = slice(c * C, (c + 1) * C)
        b_last = b[(c + 1) * C - 1:(c + 1) * C, :]
        kd_rows.append(k[rows, :] * jnp.exp(b_last - b[rows, :]))
        dec_rows.append(jnp.exp(b_last))
        dec_ref[c:c + 1, :] = dec_rows[-1]
    for c in range(0, n_chunks, 2):
        bot_ref[c * C:(c + 1) * C, :] = kd_rows[c]
        bot_ref[(c + 1) * C:(c + 2) * C, :] = kt[(c + 1) * C:(c + 2) * C, :]
        kinc_ref[c * C:(c + 1) * C, :] = kd_rows[c] * dec_rows[c + 1]
        kinc_ref[(c + 1) * C:(c + 2) * C, :] = kd_rows[c + 1]
    dect_ref[...] = dec_ref[...].T
    for half in range(2):
        for g in range(pairs):
            gs = slice(g * LANES, (g + 1) * LANES)
            qh = jnp.where(first, qt[:, gs], 0.0) if half == 0 else jnp.where(first, 0.0, qt[:, gs])
            qhat_ref[half, :, 2 * g * LANES:(2 * g + 1) * LANES] = jnp.where(in_a, qh, 0.0).astype(BF16)
            qhat_ref[half, :, (2 * g + 1) * LANES:(2 * g + 2) * LANES] = jnp.where(in_a, 0.0, qh).astype(BF16)

    rb = lax.broadcasted_iota(jnp.int32, (blk, blk), 0)
    cb = lax.broadcasted_iota(jnp.int32, (blk, blk), 1)
    lower = cb <= rb
    low_rows = rb < HEAD_DIM

    for n in range(tg // blk):
        rows = slice(n * blk, (n + 1) * blk)
        for g in range(pairs):
            gs = slice(g * LANES, (g + 1) * LANES)
            top_t = top_ref[rows, gs].T.astype(BF16)
            bot_t = bot_ref[rows, gs].T.astype(BF16)
            kinc_t = kinc_ref[rows, gs].T.astype(BF16)
            st = state_ref[g]
            dec_a = dect_ref[gs, 2 * n:2 * n + 1]
            dec_b = dect_ref[gs, 2 * n + 1:2 * n + 2]
            rhs = jnp.concatenate([jnp.concatenate([top_t, st.astype(BF16)], axis=1),
                                   jnp.concatenate([bot_t, (st * dec_a).astype(BF16)], axis=1)], axis=0)
            inc = []
            for half in range(2):
                h = 2 * g + half
                vs = slice(h * VAL_DIM, (h + 1) * VAL_DIM)
                v = v_ref[0, rows, vs]
                sc = jnp.dot(qhat_ref[half, rows, 2 * g * LANES:(2 * g + 2) * LANES], rhs,
                             preferred_element_type=F32)
                att = jnp.where(lower, sc[:, :blk], 0.0).astype(BF16)
                both = jnp.dot(jnp.concatenate([att, kinc_t], axis=0), v, preferred_element_type=F32)
                o = both[:blk, :] + sc[:, blk:]
                inc.append(both[blk:, :])
                ms = jnp.mean(o * o, axis=-1, keepdims=True)
                o_ref[0, rows, vs] = (o * lax.rsqrt(ms + EPS) * gn_ref[:, vs]).astype(BF16)
            state_ref[g] = st * (dec_a * dec_b) + jnp.where(low_rows, inc[0], inc[1])


def _gla(z, glog, gla_g, tg):
    B, S, _ = z.shape
    kw = REC_HEADS * HEAD_DIM
    vw = REC_HEADS * VAL_DIM
    return pl.pallas_call(
        _gla_kernel,
        grid=(B, S // tg),
        in_specs=[
            pl.BlockSpec((1, tg, kw), lambda b, s: (b, s, 10)),
            pl.BlockSpec((1, tg, kw), lambda b, s: (b, s, 11)),
            pl.BlockSpec((1, tg, vw), lambda b, s: (b, s, 4)),
            pl.BlockSpec((1, tg, kw), lambda b, s: (b, s, 0)),
            pl.BlockSpec((1, vw), lambda b, s: (0, 0)),
        ],
        out_specs=pl.BlockSpec((1, tg, vw), lambda b, s: (b, s, 0)),
        out_shape=jax.ShapeDtypeStruct((B, S, vw), BF16),
        scratch_shapes=[
            pltpu.VMEM((2, tg, 2 * kw), BF16),
            pltpu.VMEM((tg, kw), F32), pltpu.VMEM((tg, kw), F32), pltpu.VMEM((tg, kw), F32),
            pltpu.VMEM((LANES, kw), F32), pltpu.VMEM((kw, LANES), F32),
            pltpu.VMEM((REC_HEADS // 2, 2 * HEAD_DIM, VAL_DIM), F32),
        ],
        compiler_params=_params("arbitrary", "arbitrary"),
        name="gla",
    )(z, z, z, glog, gla_g)


def _ret_kernel(q_ref, k_ref, v_ref, gw_ref, gb_ref, o_ref, state_ref, dmat_ref, zeta_ref):
    tg = q_ref.shape[1]
    C = RET_CHUNK
    n_chunks = tg // C
    pairs = REC_HEADS // 2
    log_gamma = [math.log(1.0 - 2.0 ** (-5.0 - h)) for h in range(REC_HEADS)]

    @pl.when(pl.program_id(1) == 0)
    def _():
        state_ref[...] = jnp.zeros_like(state_ref)

    ri = lax.broadcasted_iota(jnp.int32, (C, C), 0)
    ci = lax.broadcasted_iota(jnp.int32, (C, C), 1)
    diff = (ri - ci).astype(F32)
    idx = lax.broadcasted_iota(jnp.int32, (C, 1), 0).astype(F32)
    low_lanes = ci < HEAD_DIM
    low_rows = ri < HEAD_DIM
    for h in range(REC_HEADS):
        dmat_ref[h] = jnp.where(diff >= 0, jnp.exp(jnp.maximum(diff, 0.0) * log_gamma[h]), 0.0)
    for g in range(pairs):
        zeta_ref[g] = jnp.where(low_lanes, jnp.exp((C - 1.0 - idx) * log_gamma[2 * g]),
                                jnp.exp((C - 1.0 - idx) * log_gamma[2 * g + 1]))

    for c in range(n_chunks):
        rows = slice(c * C, (c + 1) * C)
        for g in range(pairs):
            gs = slice(g * LANES, (g + 1) * LANES)
            q_pair = q_ref[0, rows, gs]
            k_f32 = k_ref[0, rows, gs].astype(F32)
            k_t = k_f32.T.astype(BF16)
            kz_t = (k_f32 * zeta_ref[g]).T.astype(BF16)
            st = state_ref[g]
            rhs = jnp.concatenate([k_t, st.astype(BF16)], axis=1)
            zero = jnp.zeros_like(q_pair)
            inc = []
            for half in range(2):
                h = 2 * g + half
                vs = slice(h * VAL_DIM, (h + 1) * VAL_DIM)
                xi = jnp.exp((idx + 1.0) * log_gamma[h])
                q = jnp.where(low_lanes, q_pair, zero) if half == 0 else jnp.where(low_lanes, zero, q_pair)
                v = v_ref[0, rows, vs]
                sc = jnp.dot(q, rhs, preferred_element_type=F32)
                att = (sc[:, :C] * dmat_ref[h]).astype(BF16)
                both = jnp.dot(jnp.concatenate([att, kz_t], axis=0), v, preferred_element_type=F32)
                o = both[:C, :] + xi * sc[:, C:]
                inc.append(both[C:, :])
                o = o - jnp.mean(o, axis=-1, keepdims=True)
                ms = jnp.mean(o * o, axis=-1, keepdims=True)
                o = o * lax.rsqrt(ms + EPS) * gw_ref[:, vs] + gb_ref[:, vs]
                o_ref[0, rows, vs] = o.astype(BF16)
            decay = jnp.where(low_rows, math.exp(C * log_gamma[2 * g]), math.exp(C * log_gamma[2 * g + 1]))
            state_ref[g] = st * decay + jnp.where(low_rows, inc[0], inc[1])


def _retention(z, gn_w, gn_b, tg):
    B, S, _ = z.shape
    kw = REC_HEADS * HEAD_DIM
    vw = REC_HEADS * VAL_DIM
    return pl.pallas_call(
        _ret_kernel,
        grid=(B, S // tg),
        in_specs=[
            pl.BlockSpec((1, tg, kw), lambda b, s: (b, s, 6)),
            pl.BlockSpec((1, tg, kw), lambda b, s: (b, s, 7)),
            pl.BlockSpec((1, tg, vw), lambda b, s: (b, s, 2)),
            pl.BlockSpec((1, vw), lambda b, s: (0, 0)),
            pl.BlockSpec((1, vw), lambda b, s: (0, 0)),
        ],
        out_specs=pl.BlockSpec((1, tg, vw), lambda b, s: (b, s, 0)),
        out_shape=jax.ShapeDtypeStruct((B, S, vw), BF16),
        scratch_shapes=[pltpu.VMEM((REC_HEADS // 2, 2 * HEAD_DIM, VAL_DIM), F32),
                        pltpu.VMEM((REC_HEADS, RET_CHUNK, RET_CHUNK), F32),
                        pltpu.VMEM((REC_HEADS // 2, RET_CHUNK, LANES), F32)],
        compiler_params=_params("arbitrary", "arbitrary"),
        name="retention",
    )(z, z, z, gn_w, gn_b)


def _dil_kernel(q_ref, k_ref, v_ref, o_ref, lse_ref,
                vt_ref, qt_ref, s_ref, mx_ref, bias_ref, ot_ref, lt_ref, *, blk, nblk, ncls, span):
    n = pl.program_id(2)
    m = k_ref.shape[2]
    pairs = ATT_HEADS // 2
    chains = [(g, half) for g in range(pairs) for half in range(2)]
    units = [(rc, t) for rc in range(ncls) for t in range(nblk)]
    feat = lax.broadcasted_iota(jnp.int32, (LANES, blk), 0)
    ki = lax.broadcasted_iota(jnp.int32, (2 * blk, blk), 0)
    qi = lax.broadcasted_iota(jnp.int32, (2 * blk, blk), 1)

    @pl.when(n == 0)
    def _():
        for rc in range(ncls):
            for g in range(pairs):
                vt_ref[rc, g * DIL_VROWS + LANES:(g + 1) * DIL_VROWS, :] = jnp.ones(
                    (DIL_VROWS - LANES, m), BF16)

        def build(kb, _):
            r0 = pl.multiple_of(kb * blk, blk)
            for rc in range(ncls):
                for g in range(pairs):
                    v = v_ref[0, rc, pl.ds(r0, blk), g * LANES:(g + 1) * LANES].astype(F32)
                    vt_ref[rc, g * DIL_VROWS:g * DIL_VROWS + LANES, pl.ds(r0, blk)] = v.T.astype(BF16)
            return 0

        lax.fori_loop(0, m // blk, build, 0)

    starts = []
    for t in range(nblk):
        qb = n * nblk + t
        start = pl.multiple_of(jnp.maximum(qb - 1, 0) * blk, blk)
        starts.append(start)
        rel = qb * blk + qi - start - ki
        bias_ref[t] = jnp.where(rel >= 0, jnp.where(rel <= span, 0.0, -jnp.inf), -jnp.inf)
    for u, (rc, t) in enumerate(units):
        for g in range(pairs):
            q_t = q_ref[0, rc, t * blk:(t + 1) * blk, g * LANES:(g + 1) * LANES].astype(F32).T
            qt_ref[u, 2 * g] = jnp.where(feat < HEAD_DIM, q_t, 0.0).astype(BF16)
            qt_ref[u, 2 * g + 1] = jnp.where(feat < HEAD_DIM, 0.0, q_t).astype(BF16)

    for u, (rc, t) in enumerate(units):
        for c, (g, half) in enumerate(chains):
            k = k_ref[0, rc, pl.ds(starts[t], 2 * blk), g * LANES:(g + 1) * LANES]
            s = jnp.dot(k, qt_ref[u, c], preferred_element_type=F32) + bias_ref[t]
            s_ref[u, c] = s
            mx_ref[u, c] = jnp.max(s, axis=0, keepdims=True)

    for u, (rc, t) in enumerate(units):
        lt_ref[u] = jnp.zeros(lt_ref.shape[1:], F32)
        for c, (g, half) in enumerate(chains):
            mx = mx_ref[u, c]
            p = jnp.exp2((s_ref[u, c] - mx).astype(BF16))
            acc = jnp.dot(vt_ref[rc, g * DIL_VROWS:(g + 1) * DIL_VROWS, pl.ds(starts[t], 2 * blk)], p,
                          preferred_element_type=F32)
            den = acc[LANES:LANES + 1, :]
            ot_ref[u, c * HEAD_DIM:(c + 1) * HEAD_DIM, :] = (
                acc[half * HEAD_DIM:(half + 1) * HEAD_DIM, :] * (1.0 / den))
            lt_ref[u, c:c + 1, :] = (mx + jnp.log2(den)) * (1.0 / LOG2E)
        qrows = slice(t * blk, (t + 1) * blk)
        o_ref[0, rc, qrows, :] = ot_ref[u].T.astype(BF16)
        lse_ref[0, rc, qrows, :] = lt_ref[u].T


def _dilated_branch(src, tile0, window, dil, units):
    B, _, m, _ = src.shape
    blk = DIL_BLOCK
    nblk = min(units, m // blk)
    ncls = min(dil, units // nblk)
    tq = nblk * blk
    n_units = ncls * nblk
    return pl.pallas_call(
        functools.partial(_dil_kernel, blk=blk, nblk=nblk, ncls=ncls, span=window // dil),
        grid=(B, dil // ncls, m // tq),
        in_specs=[
            pl.BlockSpec((1, ncls, tq, COL_TILE), lambda b, r, n: (b, r, n, tile0)),
            pl.BlockSpec((1, ncls, m, COL_TILE), lambda b, r, n: (b, r, 0, tile0 + 1)),
            pl.BlockSpec((1, ncls, m, COL_TILE), lambda b, r, n: (b, r, 0, tile0 + 2)),
        ],
        out_specs=[
            pl.BlockSpec((1, ncls, tq, COL_TILE), lambda b, r, n: (b, r, n, 0)),
            pl.BlockSpec((1, ncls, tq, LANES), lambda b, r, n: (b, r, n, 0)),
        ],
        out_shape=[
            jax.ShapeDtypeStruct((B, dil, m, COL_TILE), BF16),
            jax.ShapeDtypeStruct((B, dil, m, LANES), F32),
        ],
        scratch_shapes=[pltpu.VMEM((ncls, ATT_HEADS // 2 * DIL_VROWS, m), BF16),
                        pltpu.VMEM((n_units, ATT_HEADS, LANES, blk), BF16),
                        pltpu.VMEM((n_units, ATT_HEADS, 2 * blk, blk), F32),
                        pltpu.VMEM((n_units, ATT_HEADS, 1, blk), F32),
                        pltpu.VMEM((nblk, 2 * blk, blk), F32),
                        pltpu.VMEM((n_units, ATT_HEADS * HEAD_DIM, blk), F32),
                        pltpu.VMEM((n_units, LANES, blk), F32)],
        compiler_params=_params("arbitrary", "arbitrary", "arbitrary"),
        name=f"dilated_{dil}",
    )(src, src, src)


def _silu(x):
    return x / (1.0 + jnp.exp(-x))


def _mix_out(mix_a, mix_b, gate_ref, h_ref, w_ref):
    half = mix_a.shape[1]
    gate = _silu(gate_ref[0].astype(F32))
    a = (mix_a * gate[:, :half]).astype(BF16)
    b = (mix_b * gate[:, half:]).astype(BF16)
    y = (jnp.dot(a, w_ref[:half, :], preferred_element_type=F32)
         + jnp.dot(b, w_ref[half:, :], preferred_element_type=F32))
    return h_ref[0] + y


def _dilated_mix(d1_ref, d4_ref, d16_ref, l1_ref, l4_ref, l16_ref, e_ref,
                 o4_ref, o16_ref, ls4_ref, ls16_ref):
    tm = d1_ref.shape[1]
    chunks = d1_ref.shape[2] // LANES
    for dil, src, lsrc, dst, ldst in ((4, d4_ref, l4_ref, o4_ref, ls4_ref),
                                      (16, d16_ref, l16_ref, o16_ref, ls16_ref)):
        for res in range(dil):
            token_rows = pl.ds(res, tm // dil, stride=dil)
            ldst[token_rows, :] = lsrc[0, res]
            for c in range(chunks):
                dst[c, token_rows, :] = src[0, res, :, c * LANES:(c + 1) * LANES].astype(F32)
    l1, l2, l3 = l1_ref[0], ls4_ref[...], ls16_ref[...]
    mx = jnp.maximum(jnp.maximum(l1, l2), l3)
    e1, e2, e3 = jnp.exp(l1 - mx), jnp.exp(l2 - mx), jnp.exp(l3 - mx)
    inv = 1.0 / (e1 + e2 + e3)
    e = e_ref[...]
    expand = lambda w: sum(jnp.dot(t, e, preferred_element_type=F32) for t in _split2(w))
    o4 = jnp.concatenate([o4_ref[c] for c in range(chunks)], axis=-1)
    o16 = jnp.concatenate([o16_ref[c] for c in range(chunks)], axis=-1)
    return (expand(e1 * inv) * d1_ref[0].astype(F32) + expand(e2 * inv) * o4
            + expand(e3 * inv) * o16)


N_IN = {"even": 11, "odd": 4}
N_OUT = {"even": 4, "odd": 3}
N_SCRATCH = {"even": 2, "odd": 2}


def _proj_kernel(*refs, prev, nxt, final):
    refs = list(refs)
    take = lambda n: [refs.pop(0) for _ in range(n)]
    if prev is None:
        (x_ref,) = take(1)
    elif prev == "even":
        oa_ref, ob_ref, gate_ref, h_ref, wout_ref = take(5)
    else:
        oc_ref, d1_ref, d4_ref, d16_ref, l1_ref, l4_ref, l16_ref, e_ref, gate_ref, h_ref, wout_ref = take(11)
    if final:
        (fg_ref,) = take(1)
    in_params = take(N_IN[nxt]) if nxt else []
    if prev is not None:
        (hout_ref,) = take(1)
    in_outs = take(N_OUT[nxt]) if nxt else []
    mix_scratch = take(4) if prev == "odd" else []
    in_scratch = take(N_SCRATCH[nxt]) if nxt else []

    if prev is None:
        x = x_ref[0]
    elif prev == "even":
        x = _mix_out(oa_ref[0].astype(F32), ob_ref[0].astype(F32), gate_ref, h_ref, wout_ref)
    else:
        o_d = _dilated_mix(d1_ref, d4_ref, d16_ref, l1_ref, l4_ref, l16_ref, e_ref, *mix_scratch)
        x = _mix_out(oc_ref[0].astype(F32), o_d, gate_ref, h_ref, wout_ref)
    if final:
        x = _rms_normed(x, fg_ref[...])
    if prev is not None:
        hout_ref[0] = x
    if nxt == "even":
        _in_even_body(x, *in_params, *in_outs, *in_scratch)
    elif nxt == "odd":
        _in_odd_body(x, *in_params, *in_outs, *in_scratch)


def _proj_call(h, prev, prev_args, nxt, nxt_args, final_g, tm):
    B, S, D = h.shape
    half = COL_TILE
    tok = lambda b, s: (b, s, 0)
    const2 = lambda b, s: (0, 0)
    once = dict(pipeline_mode=pl.Buffered(1))
    weight = lambda shape: pl.BlockSpec(shape, (lambda b, s: (0,) * len(shape)), **once)

    def layer_weight(arr, j):
        return pl.BlockSpec((None, *arr.shape[1:]), (lambda b, s: (j,) + (0,) * (arr.ndim - 1)), **once)
    res_spec = lambda dil, c: pl.BlockSpec((1, dil, tm // dil, c), lambda b, s: (b, 0, s, 0))
    part = pl.BlockSpec((1, tm, half), tok)
    h_spec = pl.BlockSpec((1, tm, D), tok)
    args, in_specs, out_specs, out_shape, scratch = [], [], [], [], []

    if prev is None:
        args += [h]
        in_specs += [h_spec]
    elif prev == "even":
        o_a, o_b, z, w_out, jp = prev_args
        args += [o_a, o_b, z, h, w_out]
        in_specs += [part, part, pl.BlockSpec((1, tm, 2 * half), tok), h_spec, layer_weight(w_out, jp)]
    else:
        o_c, outs, lses, z, w_out, jp = prev_args
        expand = jnp.pad(jnp.repeat(jnp.eye(ATT_HEADS, dtype=BF16), HEAD_DIM, axis=1),
                         ((0, LANES - ATT_HEADS), (0, 0)))
        args += [o_c, *outs, *lses, expand, z, h, w_out]
        in_specs += [part, part, res_spec(4, half), res_spec(16, half),
                     pl.BlockSpec((1, tm, LANES), tok), res_spec(4, LANES), res_spec(16, LANES),
                     weight((LANES, half)), pl.BlockSpec((1, tm, 2 * half), tok), h_spec,
                     layer_weight(w_out, jp)]
    final = nxt is None
    if final:
        args += [final_g]
        in_specs += [weight((1, D))]

    if nxt == "even":
        jn, layered, shared = nxt_args
        fox_rows, val_rows, kw = ATT_HEADS * FOX_PAD, ATT_HEADS * FOX_VROWS, REC_HEADS * HEAD_DIM
        args += [*layered, *shared]
        in_specs += [layer_weight(a, jn) for a in layered] + [weight(a.shape) for a in shared]
    elif nxt == "odd":
        jn, g, w_main, cos, sin = nxt_args
        args += [g, w_main, cos, sin]
        in_specs += [layer_weight(g, jn), layer_weight(w_main, jn),
                     pl.BlockSpec((tm, LANES), lambda b, s: (s, 0)),
                     pl.BlockSpec((tm, LANES), lambda b, s: (s, 0))]

    if prev is not None:
        out_specs += [h_spec]
        out_shape += [jax.ShapeDtypeStruct((B, S, D), F32)]
    if nxt == "even":
        out_specs += [pl.BlockSpec((1, tm, ZE_WIDTH), tok),
                      pl.BlockSpec((1, fox_rows, tm), lambda b, s: (b, 0, s)),
                      pl.BlockSpec((1, val_rows, tm), lambda b, s: (b, 0, s)),
                      pl.BlockSpec((1, tm, kw), tok)]
        out_shape += [jax.ShapeDtypeStruct((B, S, ZE_WIDTH), BF16),
                      jax.ShapeDtypeStruct((B, fox_rows, S), BF16),
                      jax.ShapeDtypeStruct((B, val_rows, S), BF16),
                      jax.ShapeDtypeStruct((B, S, kw), F32)]
    elif nxt == "odd":
        dw = 3 * COL_TILE
        out_specs += [pl.BlockSpec((1, tm, Z_WIDTH), tok), res_spec(4, dw), res_spec(16, dw)]
        out_shape += [jax.ShapeDtypeStruct((B, S, Z_WIDTH), BF16),
                      jax.ShapeDtypeStruct((B, 4, S // 4, dw), BF16),
                      jax.ShapeDtypeStruct((B, 16, S // 16, dw), BF16)]

    if prev == "odd":
        scratch += [pltpu.VMEM((half // LANES, tm, LANES), F32),
                    pltpu.VMEM((half // LANES, tm, LANES), F32),
                    pltpu.VMEM((tm, LANES), F32), pltpu.VMEM((tm, LANES), F32)]
    if nxt == "even":
        scratch += [pltpu.VMEM((tm, D), BF16), pltpu.VMEM((1, LANES), F32)]
    elif nxt == "odd":
        scratch += [pltpu.VMEM((tm, D), BF16), pltpu.VMEM((3 * COL_TILE // LANES, tm, LANES), F32)]

    outs = pl.pallas_call(
        functools.partial(_proj_kernel, prev=prev, nxt=nxt, final=final),
        grid=(B, S // tm),
        in_specs=in_specs,
        out_specs=out_specs,
        out_shape=out_shape,
        scratch_shapes=scratch,
        compiler_params=_params("arbitrary", "arbitrary"),
        name=f"proj_{prev}_{nxt}",
    )(*args)
    return outs


def _prep_even(w_in, b_f, w_lr, b_lr):
    fw = ATT_HEADS * HEAD_DIM
    kw = REC_HEADS * HEAD_DIM
    vw = REC_HEADS * VAL_DIM
    sizes = (fw, fw, fw, ATT_HEADS, kw, kw, vw, GLA_RANK, fw + vw)
    offs = np.cumsum((0,) + sizes)
    fq, fk, fv, ff, gq, gk, gv, glr, gate = (w_in[..., offs[i]:offs[i + 1]] for i in range(9))
    scale = HEAD_DIM ** -0.5
    L, D = w_in.shape[:2]

    def pad_heads(w, width):
        w = w.reshape(L, D, ATT_HEADS, HEAD_DIM)
        w = jnp.pad(w, ((0, 0), (0, 0), (0, 0), (0, width - HEAD_DIM)))
        return w.reshape(L, D, ATT_HEADS * width)

    w_main = jnp.concatenate([gate, pad_heads(fk, FOX_PAD), gv, gq * scale, gk], axis=-1).astype(BF16)
    wq = (fq * (scale * LOG2E)).astype(BF16)
    wv_t = jnp.swapaxes(pad_heads(fv, FOX_VROWS), 1, 2).astype(BF16)
    pad = LANES - ATT_HEADS - GLA_RANK
    w_small = jnp.pad(jnp.concatenate([ff, glr], axis=-1), ((0, 0), (0, 0), (0, pad))).astype(BF16)
    b_f_pad = jnp.pad(b_f, ((0, 0), (0, LANES - ATT_HEADS))).reshape(L, 1, LANES)
    w_lr_pad = jnp.pad(w_lr, ((0, 0), (ATT_HEADS, pad), (0, 0))).astype(BF16)
    q_aug = np.zeros((FOX_PAD - HEAD_DIM, TOKEN_TILE), np.float32)
    q_aug[:3, :] = -1.0
    v_aug = np.zeros((ATT_HEADS, FOX_VROWS, LANES), np.float32)
    v_aug[:, HEAD_DIM, :] = 1.0
    place = np.zeros((LANES, ATT_HEADS * FOX_PAD), np.float32)
    for i in range(3):
        for h in range(ATT_HEADS):
            place[ATT_HEADS * i + h, h * FOX_PAD + HEAD_DIM + i] = 1.0
    layered = (w_main, wq, wv_t, w_small, b_f_pad, w_lr_pad, b_lr.reshape(L, 1, kw))
    shared = (jnp.asarray(place, BF16), jnp.asarray(q_aug), jnp.asarray(v_aug.reshape(-1, LANES)))
    return layered, shared


def _prep_odd(w_in):
    kw = REC_HEADS * HEAD_DIM
    vw = REC_HEADS * VAL_DIM
    dw = ATT_HEADS * HEAD_DIM
    sizes = (kw, kw, vw, dw, dw, dw, vw + dw)
    offs = np.cumsum((0,) + sizes)
    rq, rk, rv, dq, dk, dv, gate = (w_in[..., offs[i]:offs[i + 1]] for i in range(7))
    scale = HEAD_DIM ** -0.5
    return jnp.concatenate([gate, rv, rq * scale, rk, dq * (scale * LOG2E), dk, dv], axis=-1).astype(BF16)


def _rope_tables(S):
    inv = jnp.power(ROPE_THETA, -jnp.arange(0, HEAD_DIM, 2, dtype=F32) / HEAD_DIM)
    ang = jnp.arange(S, dtype=F32)[:, None] * inv[None, :]
    cos, sin = jnp.cos(ang), jnp.sin(ang)
    reps = LANES // HEAD_DIM
    cos_t = jnp.tile(jnp.concatenate([cos, cos], axis=1), (1, reps))
    sin_t = jnp.tile(jnp.concatenate([-sin, sin], axis=1), (1, reps))
    return cos_t, sin_t


def _tile(S, pref):
    return min(S, pref)


def _even_mixers(z, q_t, v_t, glog, gla_g):
    S = z.shape[1]
    o_a = _fox_attention(z, q_t, v_t, _tile(S, FOX_BLOCK))
    o_b = _gla(z, glog, gla_g.reshape(1, -1), _tile(S, TOKEN_TILE))
    return o_a, o_b


def _odd_mixers(z, zd4, zd16, gn_w, gn_b):
    B, S, _ = z.shape
    o_c = _retention(z, gn_w.reshape(1, -1), gn_b.reshape(1, -1), _tile(S, TOKEN_TILE))
    sources = {1: (z.reshape(B, 1, S, Z_WIDTH), DIL_TILE0), 4: (zd4, 0), 16: (zd16, 0)}
    outs, lses = zip(*[_dilated_branch(*sources[dil], window, dil, DIL_QBLOCKS)
                       for window, dil in DIL_PATTERNS])
    outs = (outs[0].reshape(B, S, COL_TILE), *outs[1:])
    lses = (lses[0].reshape(B, S, LANES), *lses[1:])
    return o_c, outs, lses


def kernel(x, norm_even, w_in_even, b_f_even, w_lr_even, b_lr_even, gla_norm_even, w_out_even,
           norm_odd, w_in_odd, ret_gn_w_odd, ret_gn_b_odd, w_out_odd, final_norm):
    depth = norm_even.shape[0] + norm_odd.shape[0]
    B, S, D = x.shape
    tm = _tile(S, TOKEN_TILE)
    cos, sin = _rope_tables(S)
    even_layered, even_shared = _prep_even(w_in_even, b_f_even, w_lr_even, b_lr_even)
    even_layered = (norm_even.reshape(-1, 1, D), *even_layered)
    odd_w = _prep_odd(w_in_odd)
    odd_g = norm_odd.reshape(-1, 1, D)
    w_out = {"even": w_out_even.astype(BF16), "odd": w_out_odd.astype(BF16)}

    def in_args(i):
        if i % 2 == 0:
            return "even", (i // 2, even_layered, even_shared)
        return "odd", (i // 2, odd_g, odd_w, cos, sin)

    h = x
    kind, args = in_args(0)
    proj = _proj_call(h, None, None, kind, args, None, tm)
    for i in range(depth):
        j = i // 2
        z = proj[0]
        if kind == "even":
            o_a, o_b = _even_mixers(*proj, gla_norm_even[j])
            prev_args = (o_a, o_b, z, w_out[kind], j)
        else:
            o_c, outs, lses = _odd_mixers(*proj, ret_gn_w_odd[j], ret_gn_b_odd[j])
            prev_args = (o_c, outs, lses, z, w_out[kind], j)
        nxt, nxt_args = in_args(i + 1) if i + 1 < depth else (None, None)
        h, *proj = _proj_call(h, kind, prev_args, nxt, nxt_args, final_norm.reshape(1, D), tm)
        kind = nxt
    return h
```

```python
import functools
import math

import numpy as np
import jax
import jax.numpy as jnp
from jax import lax
from jax.experimental import pallas as pl
from jax.experimental.pallas import tpu as pltpu

F32 = jnp.float32
BF16 = jnp.bfloat16

EPS = 1e-6
ROPE_THETA = 10000.0
HEAD_DIM = 64
VAL_DIM = 128
ATT_HEADS = 8
REC_HEADS = 4
GLA_RANK = 16
GLA_TAU = 16.0
GLA_CHUNK = 64
RET_CHUNK = 128
DIL_BLOCK = 128
DIL_PATTERNS = ((128, 1), (512, 4), (2048, 16))

Z_WIDTH = 3584
DIL_TILE0 = 4
DIL_QBLOCKS = 8
DIL_VROWS = 144
ZE_WIDTH = 3072
FOX_PAD = 128
FOX_VROWS = 80
LOG2E = 1.4426950408889634
COL_TILE = 512
TOKEN_TILE = 512
FOX_BLOCK = 256
LANES = 128
V7X_VMEM_LIMIT = 56 * 1024 * 1024

NT_DIMS = (((1,), (1,)), ((), ()))
TN_DIMS = (((0,), (0,)), ((), ()))


def _params(*sem):
    return pltpu.CompilerParams(dimension_semantics=sem, vmem_limit_bytes=V7X_VMEM_LIMIT)


def _split2(x):
    hi = x.astype(BF16)
    lo = (x - hi.astype(F32)).astype(BF16)
    return hi, lo


def _split3(x):
    hi, lo = _split2(x)
    lo2 = (x - hi.astype(F32) - lo.astype(F32)).astype(BF16)
    return hi, lo, lo2


def _log_sigmoid(x):
    return jnp.minimum(x, 0.0) - jnp.log(1.0 + jnp.exp(-jnp.abs(x)))


def _rms_normed(x, g):
    ms = jnp.mean(x * x, axis=-1, keepdims=True)
    return x * lax.rsqrt(ms + EPS) * g


def _in_even_body(x, g_ref, w_ref, wq_ref, wv_ref, ws_ref, bf_ref, wlr_ref, blr_ref, place_ref,
                  qaug_ref, vaug_ref, z_ref, qt_ref, vt_ref, glog_ref, u_ref, carry_ref):
    tm = x.shape[0]

    @pl.when(pl.program_id(1) == 0)
    def _():
        carry_ref[...] = jnp.zeros_like(carry_ref)

    u_ref[...] = _rms_normed(x, g_ref[...]).astype(BF16)
    u = u_ref[...]

    zs = jnp.dot(u, ws_ref[...], preferred_element_type=F32)
    logf = _log_sigmoid(zs + bf_ref[...])
    row = lax.broadcasted_iota(jnp.int32, (LANES, LANES), 0)
    col = lax.broadcasted_iota(jnp.int32, (LANES, LANES), 1)
    tril = jnp.where(row >= col, 1.0, 0.0).astype(BF16)
    carry = carry_ref[...]
    blocks = []
    for r0 in range(0, tm, LANES):
        blk_sum = sum(jnp.dot(tril, t[r0:r0 + LANES, :], preferred_element_type=F32)
                      for t in _split3(logf)) + carry
        carry = blk_sum[LANES - 1:LANES, :]
        blocks.append(blk_sum)
    csum = jnp.concatenate(blocks, axis=0)
    carry_ref[...] = carry
    hi, lo, lo2 = (t.astype(F32) for t in _split3(csum * LOG2E))
    lane = lax.broadcasted_iota(jnp.int32, (tm, LANES), 1)
    packed = jnp.where(lane < ATT_HEADS, hi,
                       jnp.where(lane < 2 * ATT_HEADS, pltpu.roll(lo, ATT_HEADS, 1),
                                 jnp.where(lane < 3 * ATT_HEADS, pltpu.roll(lo2, 2 * ATT_HEADS, 1), 0.0)))
    c_cols = jnp.dot(packed.astype(BF16), place_ref[...], preferred_element_type=F32)

    glr = jnp.dot(zs.astype(BF16), wlr_ref[...], preferred_element_type=F32)
    glog_ref[0] = _log_sigmoid(glr + blr_ref[...]) * (1.0 / GLA_TAU)

    for j in range(ZE_WIDTH // COL_TILE):
        cols = slice(j * COL_TILE, (j + 1) * COL_TILE)
        r = jnp.dot(u, w_ref[:, cols], preferred_element_type=F32)
        if j in (2, 3):
            r = r + c_cols[:, (j - 2) * COL_TILE:(j - 1) * COL_TILE]
        z_ref[0, :, cols] = r.astype(BF16)

    def feature_major(w_t_ref, aug_ref, out_ref, tile):
        for j in range(w_t_ref.shape[0] // tile):
            rows = slice(j * tile, (j + 1) * tile)
            r = lax.dot_general(w_t_ref[rows, :], u, NT_DIMS, preferred_element_type=F32)
            aug = aug_ref[rows, :]
            for c in range(tm // LANES):
                lanes = slice(c * LANES, (c + 1) * LANES)
                out_ref[0, rows, lanes] = (r[:, lanes] + aug).astype(BF16)

    feature_major(wv_ref, vaug_ref, vt_ref, ATT_HEADS * FOX_VROWS // 2)
    qtok = jnp.dot(u, wq_ref[...], preferred_element_type=F32)
    aug = qaug_ref[...].astype(BF16)
    for g in range(ATT_HEADS // 2):
        q_t = qtok[:, g * LANES:(g + 1) * LANES].T
        for half in range(2):
            r0 = (2 * g + half) * FOX_PAD
            qt_ref[0, r0:r0 + HEAD_DIM, :] = q_t[half * HEAD_DIM:(half + 1) * HEAD_DIM, :].astype(BF16)
            qt_ref[0, r0 + HEAD_DIM:r0 + FOX_PAD, :] = aug


def _rope_tile(x, cos, sin):
    lane = lax.broadcasted_iota(jnp.int32, (x.shape[0], LANES), 1)
    first_half = (lane % HEAD_DIM) < (HEAD_DIM // 2)
    outs = []
    for c in range(x.shape[1] // LANES):
        xc = x[:, c * LANES:(c + 1) * LANES]
        partner = jnp.where(first_half,
                            pltpu.roll(xc, LANES - HEAD_DIM // 2, 1),
                            pltpu.roll(xc, HEAD_DIM // 2, 1))
        outs.append(xc * cos + partner * sin)
    return jnp.concatenate(outs, axis=-1)


def _in_odd_body(x, g_ref, w_ref, cos_ref, sin_ref, z_ref, zd4_ref, zd16_ref, u_ref, r_ref):
    tm = x.shape[0]
    u_ref[...] = _rms_normed(x, g_ref[...]).astype(BF16)
    u = u_ref[...]
    cos = cos_ref[...]
    sin = sin_ref[...]
    for j in range(Z_WIDTH // COL_TILE):
        cols = slice(j * COL_TILE, (j + 1) * COL_TILE)
        r = jnp.dot(u, w_ref[:, cols], preferred_element_type=F32)
        if j in (3, 4, 5):
            r = _rope_tile(r, cos, sin)
        z_ref[0, :, cols] = r.astype(BF16)
        if j >= DIL_TILE0:
            t = j - DIL_TILE0
            for c in range(COL_TILE // LANES):
                slot = t * (COL_TILE // LANES) + c
                r_ref[slot] = r[:, c * LANES:(c + 1) * LANES]
                dcols = slice(t * COL_TILE + c * LANES, t * COL_TILE + (c + 1) * LANES)
                for dil, ref in ((4, zd4_ref), (16, zd16_ref)):
                    for res in range(dil):
                        ref[0, res, :, dcols] = (
                            r_ref[slot, pl.ds(res, tm // dil, stride=dil), :].astype(BF16))


def _fox_kernel(qt_ref, kp_ref, vt_ref, o_ref, s_ref, acc_ref, mp_ref, mc_ref, ot_ref, mask_ref, *, blk):
    i = pl.program_id(1)
    heads = range(ATT_HEADS)
    rows = lambda h: slice(h * FOX_PAD, (h + 1) * FOX_PAD)
    vrows = lambda h: slice(h * FOX_VROWS, (h + 1) * FOX_VROWS)

    @pl.when(i == 0)
    def _():
        key = lax.broadcasted_iota(jnp.int32, (blk, blk), 0)
        qry = lax.broadcasted_iota(jnp.int32, (blk, blk), 1)
        mask_ref[0] = jnp.zeros((blk, blk), F32)
        mask_ref[1] = jnp.where(key <= qry, 0.0, -jnp.inf)

    def score_stage(h, j):
        ks = pl.multiple_of(j * blk, blk)
        s = jnp.dot(kp_ref[0, pl.ds(ks, blk), rows(h)], qt_ref[0, rows(h), :],
                    preferred_element_type=F32)
        s = s + mask_ref[jnp.where(j == i, 1, 0)]
        s_ref[h] = s
        m_old = mc_ref[h]
        mp_ref[h] = m_old
        mc_ref[h] = jnp.maximum(m_old, jnp.max(s, axis=0, keepdims=True))

    def value_stage(h, j):
        ks = pl.multiple_of(j * blk, blk)
        m_new = mc_ref[h]
        alpha = jnp.exp2(mp_ref[h] - m_new)
        p = jnp.exp2((s_ref[h] - m_new).astype(BF16))
        pv = jnp.dot(vt_ref[0, vrows(h), pl.ds(ks, blk)], p, preferred_element_type=F32)
        acc_ref[h] = alpha * acc_ref[h] + pv

    def trip(j):
        for h in heads:
            value_stage(h, j - 1)
            score_stage(h, j)

    for h in heads:
        acc_ref[h] = jnp.zeros(acc_ref.shape[1:], F32)
        mc_ref[h] = jnp.full((1, blk), -jnp.inf, F32)
        score_stage(h, 0)

    odd = i % 2

    @pl.when(odd == 1)
    def _():
        trip(1)

    def pair(t, _):
        j = 1 + odd + 2 * t
        trip(j)
        trip(j + 1)
        return 0

    lax.fori_loop(0, i // 2, pair, 0)

    for h in heads:
        value_stage(h, i)
        acc = acc_ref[h]
        ot_ref[h * HEAD_DIM:(h + 1) * HEAD_DIM, :] = (
            acc[:HEAD_DIM, :] / acc[HEAD_DIM:HEAD_DIM + 1, :])
    o_ref[0] = ot_ref[...].T.astype(BF16)


def _fox_attention(z, q_t, v_t, blk):
    B, S, _ = z.shape
    fox_rows = ATT_HEADS * FOX_PAD
    val_rows = ATT_HEADS * FOX_VROWS
    return pl.pallas_call(
        functools.partial(_fox_kernel, blk=blk),
        grid=(B, S // blk),
        in_specs=[
            pl.BlockSpec((1, fox_rows, blk), lambda b, i: (b, 0, i)),
            pl.BlockSpec((1, S, fox_rows), lambda b, i: (b, 0, 1)),
            pl.BlockSpec((1, val_rows, S), lambda b, i: (b, 0, 0)),
        ],
        out_specs=pl.BlockSpec((1, blk, COL_TILE), lambda b, i: (b, i, 0)),
        out_shape=jax.ShapeDtypeStruct((B, S, COL_TILE), BF16),
        scratch_shapes=[pltpu.VMEM((ATT_HEADS, blk, blk), F32),
                        pltpu.VMEM((ATT_HEADS, FOX_VROWS, blk), F32),
                        pltpu.VMEM((ATT_HEADS, 1, blk), F32),
                        pltpu.VMEM((ATT_HEADS, 1, blk), F32),
                        pltpu.VMEM((ATT_HEADS * HEAD_DIM, blk), F32),
                        pltpu.VMEM((2, blk, blk), F32)],
        compiler_params=_params("arbitrary", "arbitrary"),
        name="fox_attention",
    )(q_t, z, v_t)


def _gla_kernel(q_ref, k_ref, v_ref, g_ref, gn_ref, o_ref,
                qhat_ref, top_ref, bot_ref, kinc_ref, dec_ref, dect_ref, state_ref):
    tg = q_ref.shape[1]
    C = GLA_CHUNK
    blk = 2 * C
    n_chunks = tg // C
    pairs = REC_HEADS // 2

    @pl.when(pl.program_id(1) == 0)
    def _():
        state_ref[...] = jnp.zeros_like(state_ref)

    row = lax.broadcasted_iota(jnp.int32, (blk, blk), 0)
    col = lax.broadcasted_iota(jnp.int32, (blk, blk), 1)
    chunk_start = row & (-C)
    tril = jnp.where(col <= row, jnp.where(col >= chunk_start, 1.0, 0.0), 0.0).astype(BF16)
    ghi, glo = _split2(g_ref[0])
    b = jnp.concatenate(
        [jnp.dot(tril, ghi[r0:r0 + blk, :], preferred_element_type=F32)
         + jnp.dot(tril, glo[r0:r0 + blk, :], preferred_element_type=F32)
         for r0 in range(0, tg, blk)], axis=0)
    q = q_ref[0].astype(F32)
    k = k_ref[0].astype(F32)
    tok = lax.broadcasted_iota(jnp.int32, (tg, LANES), 0)
    lane = lax.broadcasted_iota(jnp.int32, (tg, LANES), 1)
    in_a = (tok & C) == 0
    first = lane < HEAD_DIM
    qt = q * jnp.exp(b)
    kt = k * jnp.exp(-b)
    top_ref[...] = kt
    dec_ref[...] = jnp.zeros_like(dec_ref)
    kd_rows, dec_rows = [], []
    for c in range(n_chunks):
        rows = slice(c * C, (c + 1) * C)
        b_last = b[(c + 1) * C - 1:(c + 1) * C, :]
        kd_rows.append(k[rows, :] * jnp.exp(b_last - b[rows, :]))
        dec_rows.append(jnp.exp(b_last))
        dec_ref[c:c + 1, :] = dec_rows[-1]
    for c in range(0, n_chunks, 2):
        bot_ref[c * C:(c + 1) * C, :] = kd_rows[c]
        bot_ref[(c + 1) * C:(c + 2) * C, :] = kt[(c + 1) * C:(c + 2) * C, :]
        kinc_ref[c * C:(c + 1) * C, :] = kd_rows[c] * dec_rows[c + 1]
        kinc_ref[(c + 1) * C:(c + 2) * C, :] = kd_rows[c + 1]
    dect_ref[...] = dec_ref[...].T
    for half in range(2):
        for g in range(pairs):
            gs = slice(g * LANES, (g + 1) * LANES)
            qh = jnp.where(first, qt[:, gs], 0.0) if half == 0 else jnp.where(first, 0.0, qt[:, gs])
            qhat_ref[half, :, 2 * g * LANES:(2 * g + 1) * LANES] = jnp.where(in_a, qh, 0.0).astype(BF16)
            qhat_ref[half, :, (2 * g + 1) * LANES:(2 * g + 2) * LANES] = jnp.where(in_a, 0.0, qh).astype(BF16)

    rb = lax.broadcasted_iota(jnp.int32, (blk, blk), 0)
    cb = lax.broadcasted_iota(jnp.int32, (blk, blk), 1)
    lower = cb <= rb
    low_rows = rb < HEAD_DIM

    for n in range(tg // blk):
        rows = slice(n * blk, (n + 1) * blk)
        for g in range(pairs):
            gs = slice(g * LANES, (g + 1) * LANES)
            top_t = top_ref[rows, gs].T.astype(BF16)
            bot_t = bot_ref[rows, gs].T.astype(BF16)
            kinc_t = kinc_ref[rows, gs].T.astype(BF16)
            st = state_ref[g]
            dec_a = dect_ref[gs, 2 * n:2 * n + 1]
            dec_b = dect_ref[gs, 2 * n + 1:2 * n + 2]
            rhs = jnp.concatenate([jnp.concatenate([top_t, st.astype(BF16)], axis=1),
                                   jnp.concatenate([bot_t, (st * dec_a).astype(BF16)], axis=1)], axis=0)
            inc = []
            for half in range(2):
                h = 2 * g + half
                vs = slice(h * VAL_DIM, (h + 1) * VAL_DIM)
                v = v_ref[0, rows, vs]
                sc = jnp.dot(qhat_ref[half, rows, 2 * g * LANES:(2 * g + 2) * LANES], rhs,
                             preferred_element_type=F32)
                att = jnp.where(lower, sc[:, :blk], 0.0).astype(BF16)
                both = jnp.dot(jnp.concatenate([att, kinc_t], axis=0), v, preferred_element_type=F32)
                o = both[:blk, :] + sc[:, blk:]
                inc.append(both[blk:, :])
                ms = jnp.mean(o * o, axis=-1, keepdims=True)
                o_ref[0, rows, vs] = (o * lax.rsqrt(ms + EPS) * gn_ref[:, vs]).astype(BF16)
            state_ref[g] = st * (dec_a * dec_b) + jnp.where(low_rows, inc[0], inc[1])


def _gla(z, glog, gla_g, tg):
    B, S, _ = z.shape
    kw = REC_HEADS * HEAD_DIM
    vw = REC_HEADS * VAL_DIM
    return pl.pallas_call(
        _gla_kernel,
        grid=(B, S // tg),
        in_specs=[
            pl.BlockSpec((1, tg, kw), lambda b, s: (b, s, 10)),
            pl.BlockSpec((1, tg, kw), lambda b, s: (b, s, 11)),
            pl.BlockSpec((1, tg, vw), lambda b, s: (b, s, 4)),
            pl.BlockSpec((1, tg, kw), lambda b, s: (b, s, 0)),
            pl.BlockSpec((1, vw), lambda b, s: (0, 0)),
        ],
        out_specs=pl.BlockSpec((1, tg, vw), lambda b, s: (b, s, 0)),
        out_shape=jax.ShapeDtypeStruct((B, S, vw), BF16),
        scratch_shapes=[
            pltpu.VMEM((2, tg, 2 * kw), BF16),
            pltpu.VMEM((tg, kw), F32), pltpu.VMEM((tg, kw), F32), pltpu.VMEM((tg, kw), F32),
            pltpu.VMEM((LANES, kw), F32), pltpu.VMEM((kw, LANES), F32),
            pltpu.VMEM((REC_HEADS // 2, 2 * HEAD_DIM, VAL_DIM), F32),
        ],
        compiler_params=_params("arbitrary", "arbitrary"),
        name="gla",
    )(z, z, z, glog, gla_g)


def _ret_kernel(q_ref, k_ref, v_ref, gw_ref, gb_ref, o_ref, state_ref, dmat_ref, zeta_ref):
    tg = q_ref.shape[1]
    C = RET_CHUNK
    n_chunks = tg // C
    pairs = REC_HEADS // 2
    log_gamma = [math.log(1.0 - 2.0 ** (-5.0 - h)) for h in range(REC_HEADS)]

    @pl.when(pl.program_id(1) == 0)
    def _():
        state_ref[...] = jnp.zeros_like(state_ref)

    ri = lax.broadcasted_iota(jnp.int32, (C, C), 0)
    ci = lax.broadcasted_iota(jnp.int32, (C, C), 1)
    diff = (ri - ci).astype(F32)
    idx = lax.broadcasted_iota(jnp.int32, (C, 1), 0).astype(F32)
    low_lanes = ci < HEAD_DIM
    low_rows = ri < HEAD_DIM
    for h in range(REC_HEADS):
        dmat_ref[h] = jnp.where(diff >= 0, jnp.exp(jnp.maximum(diff, 0.0) * log_gamma[h]), 0.0)
    for g in range(pairs):
        zeta_ref[g] = jnp.where(low_lanes, jnp.exp((C - 1.0 - idx) * log_gamma[2 * g]),
                                jnp.exp((C - 1.0 - idx) * log_gamma[2 * g + 1]))

    for c in range(n_chunks):
        rows = slice(c * C, (c + 1) * C)
        for g in range(pairs):
            gs = slice(g * LANES, (g + 1) * LANES)
            q_pair = q_ref[0, rows, gs]
            k_f32 = k_ref[0, rows, gs].astype(F32)
            k_t = k_f32.T.astype(BF16)
            kz_t = (k_f32 * zeta_ref[g]).T.astype(BF16)
            st = state_ref[g]
            rhs = jnp.concatenate([k_t, st.astype(BF16)], axis=1)
            zero = jnp.zeros_like(q_pair)
            inc = []
            for half in range(2):
                h = 2 * g + half
                vs = slice(h * VAL_DIM, (h + 1) * VAL_DIM)
                xi = jnp.exp((idx + 1.0) * log_gamma[h])
                q = jnp.where(low_lanes, q_pair, zero) if half == 0 else jnp.where(low_lanes, zero, q_pair)
                v = v_ref[0, rows, vs]
                sc = jnp.dot(q, rhs, preferred_element_type=F32)
                att = (sc[:, :C] * dmat_ref[h]).astype(BF16)
                both = jnp.dot(jnp.concatenate([att, kz_t], axis=0), v, preferred_element_type=F32)
                o = both[:C, :] + xi * sc[:, C:]
                inc.append(both[C:, :])
                o = o - jnp.mean(o, axis=-1, keepdims=True)
                ms = jnp.mean(o * o, axis=-1, keepdims=True)
                o = o * lax.rsqrt(ms + EPS) * gw_ref[:, vs] + gb_ref[:, vs]
                o_ref[0, rows, vs] = o.astype(BF16)
            decay = jnp.where(low_rows, math.exp(C * log_gamma[2 * g]), math.exp(C * log_gamma[2 * g + 1]))
            state_ref[g] = st * decay + jnp.where(low_rows, inc[0], inc[1])


def _retention(z, gn_w, gn_b, tg):
    B, S, _ = z.shape
    kw = REC_HEADS * HEAD_DIM
    vw = REC_HEADS * VAL_DIM
    return pl.pallas_call(
        _ret_kernel,
        grid=(B, S // tg),
        in_specs=[
            pl.BlockSpec((1, tg, kw), lambda b, s: (b, s, 6)),
            pl.BlockSpec((1, tg, kw), lambda b, s: (b, s, 7)),
            pl.BlockSpec((1, tg, vw), lambda b, s: (b, s, 2)),
            pl.BlockSpec((1, vw), lambda b, s: (0, 0)),
            pl.BlockSpec((1, vw), lambda b, s: (0, 0)),
        ],
        out_specs=pl.BlockSpec((1, tg, vw), lambda b, s: (b, s, 0)),
        out_shape=jax.ShapeDtypeStruct((B, S, vw), BF16),
        scratch_shapes=[pltpu.VMEM((REC_HEADS // 2, 2 * HEAD_DIM, VAL_DIM), F32),
                        pltpu.VMEM((REC_HEADS, RET_CHUNK, RET_CHUNK), F32),
                        pltpu.VMEM((REC_HEADS // 2, RET_CHUNK, LANES), F32)],
        compiler_params=_params("arbitrary", "arbitrary"),
        name="retention",
    )(z, z, z, gn_w, gn_b)


def _dil_kernel(q_ref, k_ref, v_ref, o_ref, lse_ref,
                vt_ref, qt_ref, s_ref, mx_ref, bias_ref, ot_ref, lt_ref, *, blk, nblk, ncls, span):
    n = pl.program_id(2)
    m = k_ref.shape[2]
    pairs = ATT_HEADS // 2
    chains = [(g, half) for g in range(pairs) for half in range(2)]
    units = [(rc, t) for rc in range(ncls) for t in range(nblk)]
    feat = lax.broadcasted_iota(jnp.int32, (LANES, blk), 0)

    @pl.when(n == 0)
    def _():
        for rc in range(ncls):
            for g in range(pairs):
                vt_ref[rc, g * DIL_VROWS + LANES:(g + 1) * DIL_VROWS, :] = jnp.ones(
                    (DIL_VROWS - LANES, m), BF16)

        def build(kb, _):
            r0 = pl.multiple_of(kb * blk, blk)
            for rc in range(ncls):
                for g in range(pairs):
                    v = v_ref[0, rc, pl.ds(r0, blk), g * LANES:(g + 1) * LANES].astype(F32)
                    vt_ref[rc, g * DIL_VROWS:g * DIL_VROWS + LANES, pl.ds(r0, blk)] = v.T.astype(BF16)
            return 0

        lax.fori_loop(0, m // blk, build, 0)

    single_step = m == nblk * blk
    starts, wins = [], []
    for t in range(nblk):
        qb = n * nblk + t
        first = single_step and t == 0
        win = blk if first else 2 * blk
        start = 0 if first else pl.multiple_of(jnp.maximum(qb - 1, 0) * blk, blk)
        starts.append(start)
        wins.append(win)
        ki_w = lax.broadcasted_iota(jnp.int32, (win, blk), 0)
        qi_w = lax.broadcasted_iota(jnp.int32, (win, blk), 1)
        rel = qb * blk + qi_w - start - ki_w
        bias_ref[t, :win] = jnp.where(rel >= 0, jnp.where(rel <= span, 0.0, -jnp.inf), -jnp.inf)
    for u, (rc, t) in enumerate(units):
        for g in range(pairs):
            q_t = q_ref[0, rc, t * blk:(t + 1) * blk, g * LANES:(g + 1) * LANES].astype(F32).T
            qt_ref[u, 2 * g] = jnp.where(feat < HEAD_DIM, q_t, 0.0).astype(BF16)
            qt_ref[u, 2 * g + 1] = jnp.where(feat < HEAD_DIM, 0.0, q_t).astype(BF16)

    for u, (rc, t) in enumerate(units):
        for c, (g, half) in enumerate(chains):
            k = k_ref[0, rc, pl.ds(starts[t], wins[t]), g * LANES:(g + 1) * LANES]
            s = jnp.dot(k, qt_ref[u, c], preferred_element_type=F32) + bias_ref[t, :wins[t]]
            s_ref[u, c, :wins[t]] = s
            mx_ref[u, c] = jnp.max(s, axis=0, keepdims=True)

    for u, (rc, t) in enumerate(units):
        lt_ref[u] = jnp.zeros(lt_ref.shape[1:], F32)
        for c, (g, half) in enumerate(chains):
            mx = mx_ref[u, c]
            p = jnp.exp2((s_ref[u, c, :wins[t]] - mx).astype(BF16))
            acc = jnp.dot(vt_ref[rc, g * DIL_VROWS:(g + 1) * DIL_VROWS, pl.ds(starts[t], wins[t])], p,
                          preferred_element_type=F32)
            den = acc[LANES:LANES + 1, :]
            ot_ref[u, c * HEAD_DIM:(c + 1) * HEAD_DIM, :] = (
                acc[half * HEAD_DIM:(half + 1) * HEAD_DIM, :] * (1.0 / den))
            lt_ref[u, c:c + 1, :] = (mx + jnp.log2(den)) * (1.0 / LOG2E)
        qrows = slice(t * blk, (t + 1) * blk)
        o_ref[0, rc, qrows, :] = ot_ref[u].T.astype(BF16)
        lse_ref[0, rc, qrows, :] = lt_ref[u].T


def _dilated_branch(src, tile0, window, dil, units):
    B, _, m, _ = src.shape
    blk = DIL_BLOCK
    nblk = min(units, m // blk)
    ncls = min(dil, units // nblk)
    tq = nblk * blk
    n_units = ncls * nblk
    return pl.pallas_call(
        functools.partial(_dil_kernel, blk=blk, nblk=nblk, ncls=ncls, span=window // dil),
        grid=(B, dil // ncls, m // tq),
        in_specs=[
            pl.BlockSpec((1, ncls, tq, COL_TILE), lambda b, r, n: (b, r, n, tile0)),
            pl.BlockSpec((1, ncls, m, COL_TILE), lambda b, r, n: (b, r, 0, tile0 + 1)),
            pl.BlockSpec((1, ncls, m, COL_TILE), lambda b, r, n: (b, r, 0, tile0 + 2)),
        ],
        out_specs=[
            pl.BlockSpec((1, ncls, tq, COL_TILE), lambda b, r, n: (b, r, n, 0)),
            pl.BlockSpec((1, ncls, tq, LANES), lambda b, r, n: (b, r, n, 0)),
        ],
        out_shape=[
            jax.ShapeDtypeStruct((B, dil, m, COL_TILE), BF16),
            jax.ShapeDtypeStruct((B, dil, m, LANES), F32),
        ],
        scratch_shapes=[pltpu.VMEM((ncls, ATT_HEADS // 2 * DIL_VROWS, m), BF16),
                        pltpu.VMEM((n_units, ATT_HEADS, LANES, blk), BF16),
                        pltpu.VMEM((n_units, ATT_HEADS, 2 * blk, blk), F32),
                        pltpu.VMEM((n_units, ATT_HEADS, 1, blk), F32),
                        pltpu.VMEM((nblk, 2 * blk, blk), F32),
                        pltpu.VMEM((n_units, ATT_HEADS * HEAD_DIM, blk), F32),
                        pltpu.VMEM((n_units, LANES, blk), F32)],
        compiler_params=_params("arbitrary", "arbitrary", "arbitrary"),
        name=f"dilated_{dil}",
    )(src, src, src)


def _silu(x):
    return x / (1.0 + jnp.exp(-x))


def _mix_out(mix_a, mix_b, gate_ref, h_ref, w_ref):
    half = mix_a.shape[1]
    gate = _silu(gate_ref[0].astype(F32))
    a = (mix_a * gate[:, :half]).astype(BF16)
    b = (mix_b * gate[:, half:]).astype(BF16)
    y = (jnp.dot(a, w_ref[:half, :], preferred_element_type=F32)
         + jnp.dot(b, w_ref[half:, :], preferred_element_type=F32))
    return h_ref[0] + y


def _dilated_mix(d1_ref, d4_ref, d16_ref, l1_ref, l4_ref, l16_ref, e_ref,
                 o4_ref, o16_ref, ls4_ref, ls16_ref):
    tm = d1_ref.shape[1]
    chunks = d1_ref.shape[2] // LANES
    for dil, src, lsrc, dst, ldst in ((4, d4_ref, l4_ref, o4_ref, ls4_ref),
                                      (16, d16_ref, l16_ref, o16_ref, ls16_ref)):
        for res in range(dil):
            token_rows = pl.ds(res, tm // dil, stride=dil)
            ldst[token_rows, :] = lsrc[0, res]
            for c in range(chunks):
                dst[c, token_rows, :] = src[0, res, :, c * LANES:(c + 1) * LANES].astype(F32)
    l1, l2, l3 = l1_ref[0], ls4_ref[...], ls16_ref[...]
    mx = jnp.maximum(jnp.maximum(l1, l2), l3)
    e1, e2, e3 = jnp.exp(l1 - mx), jnp.exp(l2 - mx), jnp.exp(l3 - mx)
    inv = 1.0 / (e1 + e2 + e3)
    e = e_ref[...]
    expand = lambda w: sum(jnp.dot(t, e, preferred_element_type=F32) for t in _split2(w))
    o4 = jnp.concatenate([o4_ref[c] for c in range(chunks)], axis=-1)
    o16 = jnp.concatenate([o16_ref[c] for c in range(chunks)], axis=-1)
    return (expand(e1 * inv) * d1_ref[0].astype(F32) + expand(e2 * inv) * o4
            + expand(e3 * inv) * o16)


N_IN = {"even": 11, "odd": 4}
N_OUT = {"even": 4, "odd": 3}
N_SCRATCH = {"even": 2, "odd": 2}


def _proj_kernel(*refs, prev, nxt, final):
    refs = list(refs)
    take = lambda n: [refs.pop(0) for _ in range(n)]
    if prev is None:
        (x_ref,) = take(1)
    elif prev == "even":
        oa_ref, ob_ref, gate_ref, h_ref, wout_ref = take(5)
    else:
        oc_ref, d1_ref, d4_ref, d16_ref, l1_ref, l4_ref, l16_ref, e_ref, gate_ref, h_ref, wout_ref = take(11)
    if final:
        (fg_ref,) = take(1)
    in_params = take(N_IN[nxt]) if nxt else []
    if prev is not None:
        (hout_ref,) = take(1)
    in_outs = take(N_OUT[nxt]) if nxt else []
    mix_scratch = take(4) if prev == "odd" else []
    in_scratch = take(N_SCRATCH[nxt]) if nxt else []

    if prev is None:
        x = x_ref[0]
    elif prev == "even":
        x = _mix_out(oa_ref[0].astype(F32), ob_ref[0].astype(F32), gate_ref, h_ref, wout_ref)
    else:
        o_d = _dilated_mix(d1_ref, d4_ref, d16_ref, l1_ref, l4_ref, l16_ref, e_ref, *mix_scratch)
        x = _mix_out(oc_ref[0].astype(F32), o_d, gate_ref, h_ref, wout_ref)
    if final:
        x = _rms_normed(x, fg_ref[...])
    if prev is not None:
        hout_ref[0] = x
    if nxt == "even":
        _in_even_body(x, *in_params, *in_outs, *in_scratch)
    elif nxt == "odd":
        _in_odd_body(x, *in_params, *in_outs, *in_scratch)


def _proj_call(h, prev, prev_args, nxt, nxt_args, final_g, tm):
    B, S, D = h.shape
    half = COL_TILE
    tok = lambda b, s: (b, s, 0)
    const2 = lambda b, s: (0, 0)
    once = dict(pipeline_mode=pl.Buffered(1))
    weight = lambda shape: pl.BlockSpec(shape, (lambda b, s: (0,) * len(shape)), **once)

    def layer_weight(arr, j):
        return pl.BlockSpec((None, *arr.shape[1:]), (lambda b, s: (j,) + (0,) * (arr.ndim - 1)), **once)
    res_spec = lambda dil, c: pl.BlockSpec((1, dil, tm // dil, c), lambda b, s: (b, 0, s, 0))
    part = pl.BlockSpec((1, tm, half), tok)
    h_spec = pl.BlockSpec((1, tm, D), tok)
    args, in_specs, out_specs, out_shape, scratch = [], [], [], [], []

    if prev is None:
        args += [h]
        in_specs += [h_spec]
    elif prev == "even":
        o_a, o_b, z, w_out, jp = prev_args
        args += [o_a, o_b, z, h, w_out]
        in_specs += [part, part, pl.BlockSpec((1, tm, 2 * half), tok), h_spec, layer_weight(w_out, jp)]
    else:
        o_c, outs, lses, z, w_out, jp = prev_args
        expand = jnp.pad(jnp.repeat(jnp.eye(ATT_HEADS, dtype=BF16), HEAD_DIM, axis=1),
                         ((0, LANES - ATT_HEADS), (0, 0)))
        args += [o_c, *outs, *lses, expand, z, h, w_out]
        in_specs += [part, part, res_spec(4, half), res_spec(16, half),
                     pl.BlockSpec((1, tm, LANES), tok), res_spec(4, LANES), res_spec(16, LANES),
                     weight((LANES, half)), pl.BlockSpec((1, tm, 2 * half), tok), h_spec,
                     layer_weight(w_out, jp)]
    final = nxt is None
    if final:
        args += [final_g]
        in_specs += [weight((1, D))]

    if nxt == "even":
        jn, layered, shared = nxt_args
        fox_rows, val_rows, kw = ATT_HEADS * FOX_PAD, ATT_HEADS * FOX_VROWS, REC_HEADS * HEAD_DIM
        args += [*layered, *shared]
        in_specs += [layer_weight(a, jn) for a in layered] + [weight(a.shape) for a in shared]
    elif nxt == "odd":
        jn, g, w_main, cos, sin = nxt_args
        args += [g, w_main, cos, sin]
        in_specs += [layer_weight(g, jn), layer_weight(w_main, jn),
                     pl.BlockSpec((tm, LANES), lambda b, s: (s, 0)),
                     pl.BlockSpec((tm, LANES), lambda b, s: (s, 0))]

    if prev is not None:
        out_specs += [h_spec]
        out_shape += [jax.ShapeDtypeStruct((B, S, D), F32)]
    if nxt == "even":
        out_specs += [pl.BlockSpec((1, tm, ZE_WIDTH), tok),
                      pl.BlockSpec((1, fox_rows, tm), lambda b, s: (b, 0, s)),
                      pl.BlockSpec((1, val_rows, tm), lambda b, s: (b, 0, s)),
                      pl.BlockSpec((1, tm, kw), tok)]
        out_shape += [jax.ShapeDtypeStruct((B, S, ZE_WIDTH), BF16),
                      jax.ShapeDtypeStruct((B, fox_rows, S), BF16),
                      jax.ShapeDtypeStruct((B, val_rows, S), BF16),
                      jax.ShapeDtypeStruct((B, S, kw), F32)]
    elif nxt == "odd":
        dw = 3 * COL_TILE
        out_specs += [pl.BlockSpec((1, tm, Z_WIDTH), tok), res_spec(4, dw), res_spec(16, dw)]
        out_shape += [jax.ShapeDtypeStruct((B, S, Z_WIDTH), BF16),
                      jax.ShapeDtypeStruct((B, 4, S // 4, dw), BF16),
                      jax.ShapeDtypeStruct((B, 16, S // 16, dw), BF16)]

    if prev == "odd":
        scratch += [pltpu.VMEM((half // LANES, tm, LANES), F32),
                    pltpu.VMEM((half // LANES, tm, LANES), F32),
                    pltpu.VMEM((tm, LANES), F32), pltpu.VMEM((tm, LANES), F32)]
    if nxt == "even":
        scratch += [pltpu.VMEM((tm, D), BF16), pltpu.VMEM((1, LANES), F32)]
    elif nxt == "odd":
        scratch += [pltpu.VMEM((tm, D), BF16), pltpu.VMEM((3 * COL_TILE // LANES, tm, LANES), F32)]

    outs = pl.pallas_call(
        functools.partial(_proj_kernel, prev=prev, nxt=nxt, final=final),
        grid=(B, S // tm),
        in_specs=in_specs,
        out_specs=out_specs,
        out_shape=out_shape,
        scratch_shapes=scratch,
        compiler_params=_params("arbitrary", "arbitrary"),
        name=f"proj_{prev}_{nxt}",
    )(*args)
    return outs


def _prep_even(w_in, b_f, w_lr, b_lr):
    fw = ATT_HEADS * HEAD_DIM
    kw = REC_HEADS * HEAD_DIM
    vw = REC_HEADS * VAL_DIM
    sizes = (fw, fw, fw, ATT_HEADS, kw, kw, vw, GLA_RANK, fw + vw)
    offs = np.cumsum((0,) + sizes)
    fq, fk, fv, ff, gq, gk, gv, glr, gate = (w_in[..., offs[i]:offs[i + 1]] for i in range(9))
    scale = HEAD_DIM ** -0.5
    L, D = w_in.shape[:2]

    def pad_heads(w, width):
        w = w.reshape(L, D, ATT_HEADS, HEAD_DIM)
        w = jnp.pad(w, ((0, 0), (0, 0), (0, 0), (0, width - HEAD_DIM)))
        return w.reshape(L, D, ATT_HEADS * width)

    w_main = jnp.concatenate([gate, pad_heads(fk, FOX_PAD), gv, gq * scale, gk], axis=-1).astype(BF16)
    wq = (fq * (scale * LOG2E)).astype(BF16)
    wv_t = jnp.swapaxes(pad_heads(fv, FOX_VROWS), 1, 2).astype(BF16)
    pad = LANES - ATT_HEADS - GLA_RANK
    w_small = jnp.pad(jnp.concatenate([ff, glr], axis=-1), ((0, 0), (0, 0), (0, pad))).astype(BF16)
    b_f_pad = jnp.pad(b_f, ((0, 0), (0, LANES - ATT_HEADS))).reshape(L, 1, LANES)
    w_lr_pad = jnp.pad(w_lr, ((0, 0), (ATT_HEADS, pad), (0, 0))).astype(BF16)
    q_aug = np.zeros((FOX_PAD - HEAD_DIM, TOKEN_TILE), np.float32)
    q_aug[:3, :] = -1.0
    v_aug = np.zeros((ATT_HEADS, FOX_VROWS, LANES), np.float32)
    v_aug[:, HEAD_DIM, :] = 1.0
    place = np.zeros((LANES, ATT_HEADS * FOX_PAD), np.float32)
    for i in range(3):
        for h in range(ATT_HEADS):
            place[ATT_HEADS * i + h, h * FOX_PAD + HEAD_DIM + i] = 1.0
    layered = (w_main, wq, wv_t, w_small, b_f_pad, w_lr_pad, b_lr.reshape(L, 1, kw))
    shared = (jnp.asarray(place, BF16), jnp.asarray(q_aug), jnp.asarray(v_aug.reshape(-1, LANES)))
    return layered, shared


def _prep_odd(w_in):
    kw = REC_HEADS * HEAD_DIM
    vw = REC_HEADS * VAL_DIM
    dw = ATT_HEADS * HEAD_DIM
    sizes = (kw, kw, vw, dw, dw, dw, vw + dw)
    offs = np.cumsum((0,) + sizes)
    rq, rk, rv, dq, dk, dv, gate = (w_in[..., offs[i]:offs[i + 1]] for i in range(7))
    scale = HEAD_DIM ** -0.5
    return jnp.concatenate([gate, rv, rq * scale, rk, dq * (scale * LOG2E), dk, dv], axis=-1).astype(BF16)


def _rope_tables(S):
    inv = jnp.power(ROPE_THETA, -jnp.arange(0, HEAD_DIM, 2, dtype=F32) / HEAD_DIM)
    ang = jnp.arange(S, dtype=F32)[:, None] * inv[None, :]
    cos, sin = jnp.cos(ang), jnp.sin(ang)
    reps = LANES // HEAD_DIM
    cos_t = jnp.tile(jnp.concatenate([cos, cos], axis=1), (1, reps))
    sin_t = jnp.tile(jnp.concatenate([-sin, sin], axis=1), (1, reps))
    return cos_t, sin_t


def _tile(S, pref):
    return min(S, pref)


def _even_mixers(z, q_t, v_t, glog, gla_g):
    S = z.shape[1]
    o_a = _fox_attention(z, q_t, v_t, _tile(S, FOX_BLOCK))
    o_b = _gla(z, glog, gla_g.reshape(1, -1), _tile(S, TOKEN_TILE))
    return o_a, o_b


def _odd_mixers(z, zd4, zd16, gn_w, gn_b):
    B, S, _ = z.shape
    o_c = _retention(z, gn_w.reshape(1, -1), gn_b.reshape(1, -1), _tile(S, TOKEN_TILE))
    sources = {1: (z.reshape(B, 1, S, Z_WIDTH), DIL_TILE0), 4: (zd4, 0), 16: (zd16, 0)}
    outs, lses = zip(*[_dilated_branch(*sources[dil], window, dil, DIL_QBLOCKS)
                       for window, dil in DIL_PATTERNS])
    outs = (outs[0].reshape(B, S, COL_TILE), *outs[1:])
    lses = (lses[0].reshape(B, S, LANES), *lses[1:])
    return o_c, outs, lses


def kernel(x, norm_even, w_in_even, b_f_even, w_lr_even, b_lr_even, gla_norm_even, w_out_even,
           norm_odd, w_in_odd, ret_gn_w_odd, ret_gn_b_odd, w_out_odd, final_norm):
    depth = norm_even.shape[0] + norm_odd.shape[0]
    B, S, D = x.shape
    tm = _tile(S, TOKEN_TILE)
    cos, sin = _rope_tables(S)
    even_layered, even_shared = _prep_even(w_in_even, b_f_even, w_lr_even, b_lr_even)
    even_layered = (norm_even.reshape(-1, 1, D), *even_layered)
    odd_w = _prep_odd(w_in_odd)
    odd_g = norm_odd.reshape(-1, 1, D)
    w_out = {"even": w_out_even.astype(BF16), "odd": w_out_odd.astype(BF16)}

    def in_args(i):
        if i % 2 == 0:
            return "even", (i // 2, even_layered, even_shared)
        return "odd", (i // 2, odd_g, odd_w, cos, sin)

    h = x
    kind, args = in_args(0)
    proj = _proj_call(h, None, None, kind, args, None, tm)
    for i in range(depth):
        j = i // 2
        z = proj[0]
        if kind == "even":
            o_a, o_b = _even_mixers(*proj, gla_norm_even[j])
            prev_args = (o_a, o_b, z, w_out[kind], j)
        else:
            o_c, outs, lses = _odd_mixers(*proj, ret_gn_w_odd[j], ret_gn_b_odd[j])
            prev_args = (o_c, outs, lses, z, w_out[kind], j)
        nxt, nxt_args = in_args(i + 1) if i + 1 < depth else (None, None)
        h, *proj = _proj_call(h, kind, prev_args, nxt, nxt_args, final_norm.reshape(1, D), tm)
        kind = nxt
    return h
```

```python
import functools
import math

import numpy as np
import jax
import jax.numpy as jnp
from jax import lax
from jax.experimental import pallas as pl
from jax.experimental.pallas import tpu as pltpu

F32 = jnp.float32
BF16 = jnp.bfloat16

EPS = 1e-6
ROPE_THETA = 10000.0
HEAD_DIM = 64
VAL_DIM = 128
ATT_HEADS = 8
REC_HEADS = 4
GLA_RANK = 16
GLA_TAU = 16.0
GLA_CHUNK = 64
RET_CHUNK = 256
DIL_BLOCK = 128
DIL_PATTERNS = ((128, 1), (512, 4), (2048, 16))

Z_WIDTH = 3584
DIL_TILE0 = 4
DIL_QBLOCKS = 8
DIL_VROWS = 144
ZE_WIDTH = 3072
FOX_PAD = 128
FOX_VROWS = 80
LOG2E = 1.4426950408889634
COL_TILE = 512
TOKEN_TILE = 512
SCAN_TILE = 1024
FOX_BLOCK = 256
LANES = 128
V7X_VMEM_LIMIT = 56 * 1024 * 1024

NT_DIMS = (((1,), (1,)), ((), ()))
TN_DIMS = (((0,), (0,)), ((), ()))


def _params(*sem):
    return pltpu.CompilerParams(dimension_semantics=sem, vmem_limit_bytes=V7X_VMEM_LIMIT)


def _split2(x):
    hi = x.astype(BF16)
    lo = (x - hi.astype(F32)).astype(BF16)
    return hi, lo


def _split3(x):
    hi, lo = _split2(x)
    lo2 = (x - hi.astype(F32) - lo.astype(F32)).astype(BF16)
    return hi, lo, lo2


def _log_sigmoid(x):
    return jnp.minimum(x, 0.0) - jnp.log(1.0 + jnp.exp(-jnp.abs(x)))


def _rms_normed(x, g):
    ms = jnp.mean(x * x, axis=-1, keepdims=True)
    return x * lax.rsqrt(ms + EPS) * g


def _in_even_body(x, g_ref, w_ref, wq_ref, wv_ref, ws_ref, bf_ref, wlr_ref, blr_ref, place_ref,
                  qaug_ref, vaug_ref, z_ref, qt_ref, vt_ref, glog_ref, u_ref, carry_ref):
    tm = x.shape[0]

    @pl.when(pl.program_id(1) == 0)
    def _():
        carry_ref[...] = jnp.zeros_like(carry_ref)

    u_ref[...] = _rms_normed(x, g_ref[...]).astype(BF16)
    u = u_ref[...]

    zs = jnp.dot(u, ws_ref[...], preferred_element_type=F32)
    logf = _log_sigmoid(zs + bf_ref[...])
    row = lax.broadcasted_iota(jnp.int32, (LANES, LANES), 0)
    col = lax.broadcasted_iota(jnp.int32, (LANES, LANES), 1)
    tril = jnp.where(row >= col, 1.0, 0.0).astype(BF16)
    carry = carry_ref[...]
    blocks = []
    for r0 in range(0, tm, LANES):
        blk_sum = sum(jnp.dot(tril, t[r0:r0 + LANES, :], preferred_element_type=F32)
                      for t in _split3(logf)) + carry
        carry = blk_sum[LANES - 1:LANES, :]
        blocks.append(blk_sum)
    csum = jnp.concatenate(blocks, axis=0)
    carry_ref[...] = carry
    hi, lo, lo2 = (t.astype(F32) for t in _split3(csum * LOG2E))
    lane = lax.broadcasted_iota(jnp.int32, (tm, LANES), 1)
    packed = jnp.where(lane < ATT_HEADS, hi,
                       jnp.where(lane < 2 * ATT_HEADS, pltpu.roll(lo, ATT_HEADS, 1),
                                 jnp.where(lane < 3 * ATT_HEADS, pltpu.roll(lo2, 2 * ATT_HEADS, 1), 0.0)))
    c_cols = jnp.dot(packed.astype(BF16), place_ref[...], preferred_element_type=F32)

    glr = jnp.dot(zs.astype(BF16), wlr_ref[...], preferred_element_type=F32)
    glog_ref[0] = _log_sigmoid(glr + blr_ref[...]) * (1.0 / GLA_TAU)

    for j in range(ZE_WIDTH // COL_TILE):
        cols = slice(j * COL_TILE, (j + 1) * COL_TILE)
        r = jnp.dot(u, w_ref[:, cols], preferred_element_type=F32)
        if j in (2, 3):
            r = r + c_cols[:, (j - 2) * COL_TILE:(j - 1) * COL_TILE]
        z_ref[0, :, cols] = r.astype(BF16)

    def feature_major(w_t_ref, aug_ref, out_ref, tile):
        for j in range(w_t_ref.shape[0] // tile):
            rows = slice(j * tile, (j + 1) * tile)
            r = lax.dot_general(w_t_ref[rows, :], u, NT_DIMS, preferred_element_type=F32)
            aug = aug_ref[rows, :]
            for c in range(tm // LANES):
                lanes = slice(c * LANES, (c + 1) * LANES)
                out_ref[0, rows, lanes] = (r[:, lanes] + aug).astype(BF16)

    feature_major(wv_ref, vaug_ref, vt_ref, ATT_HEADS * FOX_VROWS // 2)
    qtok = jnp.dot(u, wq_ref[...], preferred_element_type=F32)
    aug = qaug_ref[...].astype(BF16)
    for g in range(ATT_HEADS // 2):
        q_t = qtok[:, g * LANES:(g + 1) * LANES].T
        for half in range(2):
            r0 = (2 * g + half) * FOX_PAD
            qt_ref[0, r0:r0 + HEAD_DIM, :] = q_t[half * HEAD_DIM:(half + 1) * HEAD_DIM, :].astype(BF16)
            qt_ref[0, r0 + HEAD_DIM:r0 + FOX_PAD, :] = aug


def _rope_tile(x, cos, sin):
    lane = lax.broadcasted_iota(jnp.int32, (x.shape[0], LANES), 1)
    first_half = (lane % HEAD_DIM) < (HEAD_DIM // 2)
    outs = []
    for c in range(x.shape[1] // LANES):
        xc = x[:, c * LANES:(c + 1) * LANES]
        partner = jnp.where(first_half,
                            pltpu.roll(xc, LANES - HEAD_DIM // 2, 1),
                            pltpu.roll(xc, HEAD_DIM // 2, 1))
        outs.append(xc * cos + partner * sin)
    return jnp.concatenate(outs, axis=-1)


def _in_odd_body(x, g_ref, w_ref, cos_ref, sin_ref, z_ref, zd4_ref, zd16_ref, u_ref, r_ref):
    tm = x.shape[0]
    u_ref[...] = _rms_normed(x, g_ref[...]).astype(BF16)
    u = u_ref[...]
    cos = cos_ref[...]
    sin = sin_ref[...]
    for j in range(Z_WIDTH // COL_TILE):
        cols = slice(j * COL_TILE, (j + 1) * COL_TILE)
        r = jnp.dot(u, w_ref[:, cols], preferred_element_type=F32)
        if j in (3, 4, 5):
            r = _rope_tile(r, cos, sin)
        z_ref[0, :, cols] = r.astype(BF16)
        if j >= DIL_TILE0:
            t = j - DIL_TILE0
            for c in range(COL_TILE // LANES):
                slot = t * (COL_TILE // LANES) + c
                r_ref[slot] = r[:, c * LANES:(c + 1) * LANES]
                dcols = slice(t * COL_TILE + c * LANES, t * COL_TILE + (c + 1) * LANES)
                for dil, ref in ((4, zd4_ref), (16, zd16_ref)):
                    for res in range(dil):
                        ref[0, res, :, dcols] = (
                            r_ref[slot, pl.ds(res, tm // dil, stride=dil), :].astype(BF16))


def _fox_kernel(qt_ref, kp_ref, vt_ref, o_ref, s_ref, acc_ref, mp_ref, mc_ref, ot_ref, mask_ref, *, blk):
    i = pl.program_id(1)
    heads = range(ATT_HEADS)
    rows = lambda h: slice(h * FOX_PAD, (h + 1) * FOX_PAD)
    vrows = lambda h: slice(h * FOX_VROWS, (h + 1) * FOX_VROWS)

    @pl.when(i == 0)
    def _():
        key = lax.broadcasted_iota(jnp.int32, (blk, blk), 0)
        qry = lax.broadcasted_iota(jnp.int32, (blk, blk), 1)
        mask_ref[0] = jnp.zeros((blk, blk), F32)
        mask_ref[1] = jnp.where(key <= qry, 0.0, -jnp.inf)

    def score_stage(h, j):
        ks = pl.multiple_of(j * blk, blk)
        s = jnp.dot(kp_ref[0, pl.ds(ks, blk), rows(h)], qt_ref[0, rows(h), :],
                    preferred_element_type=F32)
        s = s + mask_ref[jnp.where(j == i, 1, 0)]
        s_ref[h] = s
        m_old = mc_ref[h]
        mp_ref[h] = m_old
        mc_ref[h] = jnp.maximum(m_old, jnp.max(s, axis=0, keepdims=True))

    def value_stage(h, j):
        ks = pl.multiple_of(j * blk, blk)
        m_new = mc_ref[h]
        alpha = jnp.exp2(mp_ref[h] - m_new)
        p = jnp.exp2((s_ref[h] - m_new).astype(BF16))
        pv = jnp.dot(vt_ref[0, vrows(h), pl.ds(ks, blk)], p, preferred_element_type=F32)
        acc_ref[h] = alpha * acc_ref[h] + pv

    def trip(j):
        for h in heads:
            value_stage(h, j - 1)
            score_stage(h, j)

    for h in heads:
        acc_ref[h] = jnp.zeros(acc_ref.shape[1:], F32)
        mc_ref[h] = jnp.full((1, blk), -jnp.inf, F32)
        score_stage(h, 0)

    odd = i % 2

    @pl.when(odd == 1)
    def _():
        trip(1)

    def pair(t, _):
        j = 1 + odd + 2 * t
        trip(j)
        trip(j + 1)
        return 0

    lax.fori_loop(0, i // 2, pair, 0)

    for h in heads:
        value_stage(h, i)
        acc = acc_ref[h]
        ot_ref[h * HEAD_DIM:(h + 1) * HEAD_DIM, :] = (
            acc[:HEAD_DIM, :] / acc[HEAD_DIM:HEAD_DIM + 1, :])
    o_ref[0] = ot_ref[...].T.astype(BF16)


def _fox_attention(z, q_t, v_t, blk):
    B, S, _ = z.shape
    fox_rows = ATT_HEADS * FOX_PAD
    val_rows = ATT_HEADS * FOX_VROWS
    return pl.pallas_call(
        functools.partial(_fox_kernel, blk=blk),
        grid=(B, S // blk),
        in_specs=[
            pl.BlockSpec((1, fox_rows, blk), lambda b, i: (b, 0, i)),
            pl.BlockSpec((1, S, fox_rows), lambda b, i: (b, 0, 1)),
            pl.BlockSpec((1, val_rows, S), lambda b, i: (b, 0, 0)),
        ],
        out_specs=pl.BlockSpec((1, blk, COL_TILE), lambda b, i: (b, i, 0)),
        out_shape=jax.ShapeDtypeStruct((B, S, COL_TILE), BF16),
        scratch_shapes=[pltpu.VMEM((ATT_HEADS, blk, blk), F32),
                        pltpu.VMEM((ATT_HEADS, FOX_VROWS, blk), F32),
                        pltpu.VMEM((ATT_HEADS, 1, blk), F32),
                        pltpu.VMEM((ATT_HEADS, 1, blk), F32),
                        pltpu.VMEM((ATT_HEADS * HEAD_DIM, blk), F32),
                        pltpu.VMEM((2, blk, blk), F32)],
        compiler_params=_params("arbitrary", "arbitrary"),
        name="fox_attention",
    )(q_t, z, v_t)


def _gla_kernel(q_ref, k_ref, v_ref, g_ref, gn_ref, o_ref,
                qhat_ref, top_ref, bot_ref, kinc_ref, dec_ref, dect_ref, state_ref):
    tg = q_ref.shape[1]
    C = GLA_CHUNK
    blk = 2 * C
    n_chunks = tg // C
    pairs = REC_HEADS // 2

    @pl.when(pl.program_id(1) == 0)
    def _():
        state_ref[...] = jnp.zeros_like(state_ref)

    row = lax.broadcasted_iota(jnp.int32, (blk, blk), 0)
    col = lax.broadcasted_iota(jnp.int32, (blk, blk), 1)
    chunk_start = row & (-C)
    tril = jnp.where(col <= row, jnp.where(col >= chunk_start, 1.0, 0.0), 0.0).astype(BF16)
    ghi, glo = _split2(g_ref[0])
    b = jnp.concatenate(
        [jnp.dot(tril, ghi[r0:r0 + blk, :], preferred_element_type=F32)
         + jnp.dot(tril, glo[r0:r0 + blk, :], preferred_element_type=F32)
         for r0 in range(0, tg, blk)], axis=0)
    q = q_ref[0].astype(F32)
    k = k_ref[0].astype(F32)
    tok = lax.broadcasted_iota(jnp.int32, (tg, LANES), 0)
    lane = lax.broadcasted_iota(jnp.int32, (tg, LANES), 1)
    in_a = (tok & C) == 0
    first = lane < HEAD_DIM
    qt = q * jnp.exp(b)
    kt = k * jnp.exp(-b)
    top_ref[...] = kt
    dec_ref[...] = jnp.zeros_like(dec_ref)
    kd_rows, dec_rows = [], []
    for c in range(n_chunks):
        rows = slice(c * C, (c + 1) * C)
        b_last = b[(c + 1) * C - 1:(c + 1) * C, :]
        kd_rows.append(k[rows, :] * jnp.exp(b_last - b[rows, :]))
        dec_rows.append(jnp.exp(b_last))
        dec_ref[c:c + 1, :] = dec_rows[-1]
    for c in range(0, n_chunks, 2):
        bot_ref[c * C:(c + 1) * C, :] = kd_rows[c]
        bot_ref[(c + 1) * C:(c + 2) * C, :] = kt[(c + 1) * C:(c + 2) * C, :]
        kinc_ref[c * C:(c + 1) * C, :] = kd_rows[c] * dec_rows[c + 1]
        kinc_ref[(c + 1) * C:(c + 2) * C, :] = kd_rows[c + 1]
    dect_ref[...] = dec_ref[...].T
    for half in range(2):
        for g in range(pairs):
            gs = slice(g * LANES, (g + 1) * LANES)
            qh = jnp.where(first, qt[:, gs], 0.0) if half == 0 else jnp.where(first, 0.0, qt[:, gs])
            qhat_ref[half, :, 2 * g * LANES:(2 * g + 1) * LANES] = jnp.where(in_a, qh, 0.0).astype(BF16)
            qhat_ref[half, :, (2 * g + 1) * LANES:(2 * g + 2) * LANES] = jnp.where(in_a, 0.0, qh).astype(BF16)

    rb = lax.broadcasted_iota(jnp.int32, (blk, blk), 0)
    cb = lax.broadcasted_iota(jnp.int32, (blk, blk), 1)
    lower = cb <= rb
    low_rows = rb < HEAD_DIM

    for n in range(tg // blk):
        rows = slice(n * blk, (n + 1) * blk)
        for g in range(pairs):
            gs = slice(g * LANES, (g + 1) * LANES)
            top_t = top_ref[rows, gs].T.astype(BF16)
            bot_t = bot_ref[rows, gs].T.astype(BF16)
            kinc_t = kinc_ref[rows, gs].T.astype(BF16)
            st = state_ref[g]
            dec_a = dect_ref[gs, 2 * n:2 * n + 1]
            dec_b = dect_ref[gs, 2 * n + 1:2 * n + 2]
            rhs = jnp.concatenate([jnp.concatenate([top_t, st.astype(BF16)], axis=1),
                                   jnp.concatenate([bot_t, (st * dec_a).astype(BF16)], axis=1)], axis=0)
            inc = []
            for half in range(2):
                h = 2 * g + half
                vs = slice(h * VAL_DIM, (h + 1) * VAL_DIM)
                v = v_ref[0, rows, vs]
                sc = jnp.dot(qhat_ref[half, rows, 2 * g * LANES:(2 * g + 2) * LANES], rhs,
                             preferred_element_type=F32)
                att = jnp.where(lower, sc[:, :blk], 0.0).astype(BF16)
                both = jnp.dot(jnp.concatenate([att, kinc_t], axis=0), v, preferred_element_type=F32)
                o = both[:blk, :] + sc[:, blk:]
                inc.append(both[blk:, :])
                ms = jnp.mean(o * o, axis=-1, keepdims=True)
                o_ref[0, rows, vs] = (o * lax.rsqrt(ms + EPS) * gn_ref[:, vs]).astype(BF16)
            state_ref[g] = st * (dec_a * dec_b) + jnp.where(low_rows, inc[0], inc[1])


def _gla(z, glog, gla_g, tg):
    B, S, _ = z.shape
    kw = REC_HEADS * HEAD_DIM
    vw = REC_HEADS * VAL_DIM
    return pl.pallas_call(
        _gla_kernel,
        grid=(B, S // tg),
        in_specs=[
            pl.BlockSpec((1, tg, kw), lambda b, s: (b, s, 10)),
            pl.BlockSpec((1, tg, kw), lambda b, s: (b, s, 11)),
            pl.BlockSpec((1, tg, vw), lambda b, s: (b, s, 4)),
            pl.BlockSpec((1, tg, kw), lambda b, s: (b, s, 0)),
            pl.BlockSpec((1, vw), lambda b, s: (0, 0)),
        ],
        out_specs=pl.BlockSpec((1, tg, vw), lambda b, s: (b, s, 0)),
        out_shape=jax.ShapeDtypeStruct((B, S, vw), BF16),
        scratch_shapes=[
            pltpu.VMEM((2, tg, 2 * kw), BF16),
            pltpu.VMEM((tg, kw), F32), pltpu.VMEM((tg, kw), F32), pltpu.VMEM((tg, kw), F32),
            pltpu.VMEM((LANES, kw), F32), pltpu.VMEM((kw, LANES), F32),
            pltpu.VMEM((REC_HEADS // 2, 2 * HEAD_DIM, VAL_DIM), F32),
        ],
        compiler_params=_params("arbitrary", "arbitrary"),
        name="gla",
    )(z, z, z, glog, gla_g)


def _ret_kernel(q_ref, k_ref, v_ref, gw_ref, gb_ref, o_ref, state_ref, dmat_ref, zeta_ref):
    tg = q_ref.shape[1]
    C = RET_CHUNK
    n_chunks = tg // C
    pairs = REC_HEADS // 2
    log_gamma = [math.log(1.0 - 2.0 ** (-5.0 - h)) for h in range(REC_HEADS)]

    @pl.when(pl.program_id(1) == 0)
    def _():
        state_ref[...] = jnp.zeros_like(state_ref)

    ri = lax.broadcasted_iota(jnp.int32, (C, C), 0)
    ci = lax.broadcasted_iota(jnp.int32, (C, C), 1)
    diff = (ri - ci).astype(F32)
    idx = lax.broadcasted_iota(jnp.int32, (C, 1), 0).astype(F32)
    low_lanes = lax.broadcasted_iota(jnp.int32, (C, LANES), 1) < HEAD_DIM
    low_rows = lax.broadcasted_iota(jnp.int32, (LANES, VAL_DIM), 0) < HEAD_DIM
    for h in range(REC_HEADS):
        dmat_ref[h] = jnp.where(diff >= 0, jnp.exp(jnp.maximum(diff, 0.0) * log_gamma[h]), 0.0)
    for g in range(pairs):
        zeta_ref[g] = jnp.where(low_lanes, jnp.exp((C - 1.0 - idx) * log_gamma[2 * g]),
                                jnp.exp((C - 1.0 - idx) * log_gamma[2 * g + 1]))

    for c in range(n_chunks):
        rows = slice(c * C, (c + 1) * C)
        for g in range(pairs):
            gs = slice(g * LANES, (g + 1) * LANES)
            q_pair = q_ref[0, rows, gs]
            k_f32 = k_ref[0, rows, gs].astype(F32)
            k_t = k_f32.T.astype(BF16)
            kz_t = (k_f32 * zeta_ref[g]).T.astype(BF16)
            st = state_ref[g]
            rhs = jnp.concatenate([k_t, st.astype(BF16)], axis=1)
            zero = jnp.zeros_like(q_pair)
            inc = []
            for half in range(2):
                h = 2 * g + half
                vs = slice(h * VAL_DIM, (h + 1) * VAL_DIM)
                xi = jnp.exp((idx + 1.0) * log_gamma[h])
                q = jnp.where(low_lanes, q_pair, zero) if half == 0 else jnp.where(low_lanes, zero, q_pair)
                v = v_ref[0, rows, vs]
                sc = jnp.dot(q, rhs, preferred_element_type=F32)
                att = (sc[:, :C] * dmat_ref[h]).astype(BF16)
                both = jnp.dot(jnp.concatenate([att, kz_t], axis=0), v, preferred_element_type=F32)
                o = both[:C, :] + xi * sc[:, C:]
                inc.append(both[C:, :])
                o = o - jnp.mean(o, axis=-1, keepdims=True)
                ms = jnp.mean(o * o, axis=-1, keepdims=True)
                o = o * lax.rsqrt(ms + EPS) * gw_ref[:, vs] + gb_ref[:, vs]
                o_ref[0, rows, vs] = o.astype(BF16)
            decay = jnp.where(low_rows, math.exp(C * log_gamma[2 * g]), math.exp(C * log_gamma[2 * g + 1]))
            state_ref[g] = st * decay + jnp.where(low_rows, inc[0], inc[1])


def _retention(z, gn_w, gn_b, tg):
    B, S, _ = z.shape
    kw = REC_HEADS * HEAD_DIM
    vw = REC_HEADS * VAL_DIM
    return pl.pallas_call(
        _ret_kernel,
        grid=(B, S // tg),
        in_specs=[
            pl.BlockSpec((1, tg, kw), lambda b, s: (b, s, 6)),
            pl.BlockSpec((1, tg, kw), lambda b, s: (b, s, 7)),
            pl.BlockSpec((1, tg, vw), lambda b, s: (b, s, 2)),
            pl.BlockSpec((1, vw), lambda b, s: (0, 0)),
            pl.BlockSpec((1, vw), lambda b, s: (0, 0)),
        ],
        out_specs=pl.BlockSpec((1, tg, vw), lambda b, s: (b, s, 0)),
        out_shape=jax.ShapeDtypeStruct((B, S, vw), BF16),
        scratch_shapes=[pltpu.VMEM((REC_HEADS // 2, 2 * HEAD_DIM, VAL_DIM), F32),
                        pltpu.VMEM((REC_HEADS, RET_CHUNK, RET_CHUNK), F32),
                        pltpu.VMEM((REC_HEADS // 2, RET_CHUNK, LANES), F32)],
        compiler_params=_params("arbitrary", "arbitrary"),
        name="retention",
    )(z, z, z, gn_w, gn_b)


def _dil_kernel(q_ref, k_ref, v_ref, o_ref, lse_ref,
                vt_ref, qt_ref, s_ref, mx_ref, bias_ref, ot_ref, lt_ref, *, blk, nblk, ncls, span):
    n = pl.program_id(2)
    m = k_ref.shape[2]
    pairs = ATT_HEADS // 2
    chains = [(g, half) for g in range(pairs) for half in range(2)]
    units = [(rc, t) for rc in range(ncls) for t in range(nblk)]
    feat = lax.broadcasted_iota(jnp.int32, (LANES, blk), 0)

    @pl.when(n == 0)
    def _():
        for rc in range(ncls):
            for g in range(pairs):
                vt_ref[rc, g * DIL_VROWS + LANES:(g + 1) * DIL_VROWS, :] = jnp.ones(
                    (DIL_VROWS - LANES, m), BF16)

        def build(kb, _):
            r0 = pl.multiple_of(kb * blk, blk)
            for rc in range(ncls):
                for g in range(pairs):
                    v = v_ref[0, rc, pl.ds(r0, blk), g * LANES:(g + 1) * LANES].astype(F32)
                    vt_ref[rc, g * DIL_VROWS:g * DIL_VROWS + LANES, pl.ds(r0, blk)] = v.T.astype(BF16)
            return 0

        lax.fori_loop(0, m // blk, build, 0)

    single_step = m == nblk * blk
    starts, wins = [], []
    for t in range(nblk):
        qb = n * nblk + t
        first = single_step and t == 0
        win = blk if first else 2 * blk
        start = 0 if first else pl.multiple_of(jnp.maximum(qb - 1, 0) * blk, blk)
        starts.append(start)
        wins.append(win)
        ki_w = lax.broadcasted_iota(jnp.int32, (win, blk), 0)
        qi_w = lax.broadcasted_iota(jnp.int32, (win, blk), 1)
        rel = qb * blk + qi_w - start - ki_w
        bias_ref[t, :win] = jnp.where(rel >= 0, jnp.where(rel <= span, 0.0, -jnp.inf), -jnp.inf)
    for u, (rc, t) in enumerate(units):
        for g in range(pairs):
            q_t = q_ref[0, rc, t * blk:(t + 1) * blk, g * LANES:(g + 1) * LANES].astype(F32).T
            qt_ref[u, 2 * g] = jnp.where(feat < HEAD_DIM, q_t, 0.0).astype(BF16)
            qt_ref[u, 2 * g + 1] = jnp.where(feat < HEAD_DIM, 0.0, q_t).astype(BF16)

    for u, (rc, t) in enumerate(units):
        for c, (g, half) in enumerate(chains):
            k = k_ref[0, rc, pl.ds(starts[t], wins[t]), g * LANES:(g + 1) * LANES]
            s = jnp.dot(k, qt_ref[u, c], preferred_element_type=F32) + bias_ref[t, :wins[t]]
            s_ref[u, c, :wins[t]] = s
            mx_ref[u, c] = jnp.max(s, axis=0, keepdims=True)

    for u, (rc, t) in enumerate(units):
        lt_ref[u] = jnp.zeros(lt_ref.shape[1:], F32)
        for c, (g, half) in enumerate(chains):
            mx = mx_ref[u, c]
            p = jnp.exp2((s_ref[u, c, :wins[t]] - mx).astype(BF16))
            acc = jnp.dot(vt_ref[rc, g * DIL_VROWS:(g + 1) * DIL_VROWS, pl.ds(starts[t], wins[t])], p,
                          preferred_element_type=F32)
            den = acc[LANES:LANES + 1, :]
            ot_ref[u, c * HEAD_DIM:(c + 1) * HEAD_DIM, :] = (
                acc[half * HEAD_DIM:(half + 1) * HEAD_DIM, :] * (1.0 / den))
            lt_ref[u, c:c + 1, :] = (mx + jnp.log2(den)) * (1.0 / LOG2E)
        qrows = slice(t * blk, (t + 1) * blk)
        o_ref[0, rc, qrows, :] = ot_ref[u].T.astype(BF16)
        lse_ref[0, rc, qrows, :] = lt_ref[u].T


def _dilated_branch(src, tile0, window, dil, units):
    B, _, m, _ = src.shape
    blk = DIL_BLOCK
    nblk = min(units, m // blk)
    ncls = min(dil, units // nblk)
    tq = nblk * blk
    n_units = ncls * nblk
    return pl.pallas_call(
        functools.partial(_dil_kernel, blk=blk, nblk=nblk, ncls=ncls, span=window // dil),
        grid=(B, dil // ncls, m // tq),
        in_specs=[
            pl.BlockSpec((1, ncls, tq, COL_TILE), lambda b, r, n: (b, r, n, tile0)),
            pl.BlockSpec((1, ncls, m, COL_TILE), lambda b, r, n: (b, r, 0, tile0 + 1)),
            pl.BlockSpec((1, ncls, m, COL_TILE), lambda b, r, n: (b, r, 0, tile0 + 2)),
        ],
        out_specs=[
            pl.BlockSpec((1, ncls, tq, COL_TILE), lambda b, r, n: (b, r, n, 0)),
            pl.BlockSpec((1, ncls, tq, LANES), lambda b, r, n: (b, r, n, 0)),
        ],
        out_shape=[
            jax.ShapeDtypeStruct((B, dil, m, COL_TILE), BF16),
            jax.ShapeDtypeStruct((B, dil, m, LANES), F32),
        ],
        scratch_shapes=[pltpu.VMEM((ncls, ATT_HEADS // 2 * DIL_VROWS, m), BF16),
                        pltpu.VMEM((n_units, ATT_HEADS, LANES, blk), BF16),
                        pltpu.VMEM((n_units, ATT_HEADS, 2 * blk, blk), F32),
                        pltpu.VMEM((n_units, ATT_HEADS, 1, blk), F32),
                        pltpu.VMEM((nblk, 2 * blk, blk), F32),
                        pltpu.VMEM((n_units, ATT_HEADS * HEAD_DIM, blk), F32),
                        pltpu.VMEM((n_units, LANES, blk), F32)],
        compiler_params=_params("arbitrary", "arbitrary", "arbitrary"),
        name=f"dilated_{dil}",
    )(src, src, src)


def _silu(x):
    return x / (1.0 + jnp.exp(-x))


def _mix_out(mix_a, mix_b, gate_ref, h_ref, w_ref):
    half = mix_a.shape[1]
    gate = _silu(gate_ref[0].astype(F32))
    a = (mix_a * gate[:, :half]).astype(BF16)
    b = (mix_b * gate[:, half:]).astype(BF16)
    y = (jnp.dot(a, w_ref[:half, :], preferred_element_type=F32)
         + jnp.dot(b, w_ref[half:, :], preferred_element_type=F32))
    return h_ref[0] + y


def _dilated_mix(d1_ref, d4_ref, d16_ref, l1_ref, l4_ref, l16_ref, e_ref,
                 o4_ref, o16_ref, ls4_ref, ls16_ref):
    tm = d1_ref.shape[1]
    chunks = d1_ref.shape[2] // LANES
    for dil, src, lsrc, dst, ldst in ((4, d4_ref, l4_ref, o4_ref, ls4_ref),
                                      (16, d16_ref, l16_ref, o16_ref, ls16_ref)):
        for res in range(dil):
            token_rows = pl.ds(res, tm // dil, stride=dil)
            ldst[token_rows, :] = lsrc[0, res]
            for c in range(chunks):
                dst[c, token_rows, :] = src[0, res, :, c * LANES:(c + 1) * LANES].astype(F32)
    l1, l2, l3 = l1_ref[0], ls4_ref[...], ls16_ref[...]
    mx = jnp.maximum(jnp.maximum(l1, l2), l3)
    e1, e2, e3 = jnp.exp(l1 - mx), jnp.exp(l2 - mx), jnp.exp(l3 - mx)
    inv = 1.0 / (e1 + e2 + e3)
    e = e_ref[...]
    expand = lambda w: sum(jnp.dot(t, e, preferred_element_type=F32) for t in _split2(w))
    o4 = jnp.concatenate([o4_ref[c] for c in range(chunks)], axis=-1)
    o16 = jnp.concatenate([o16_ref[c] for c in range(chunks)], axis=-1)
    return (expand(e1 * inv) * d1_ref[0].astype(F32) + expand(e2 * inv) * o4
            + expand(e3 * inv) * o16)


N_IN = {"even": 11, "odd": 4}
N_OUT = {"even": 4, "odd": 3}
N_SCRATCH = {"even": 2, "odd": 2}


def _proj_kernel(*refs, prev, nxt, final):
    refs = list(refs)
    take = lambda n: [refs.pop(0) for _ in range(n)]
    if prev is None:
        (x_ref,) = take(1)
    elif prev == "even":
        oa_ref, ob_ref, gate_ref, h_ref, wout_ref = take(5)
    else:
        oc_ref, d1_ref, d4_ref, d16_ref, l1_ref, l4_ref, l16_ref, e_ref, gate_ref, h_ref, wout_ref = take(11)
    if final:
        (fg_ref,) = take(1)
    in_params = take(N_IN[nxt]) if nxt else []
    if prev is not None:
        (hout_ref,) = take(1)
    in_outs = take(N_OUT[nxt]) if nxt else []
    mix_scratch = take(4) if prev == "odd" else []
    in_scratch = take(N_SCRATCH[nxt]) if nxt else []

    if prev is None:
        x = x_ref[0]
    elif prev == "even":
        x = _mix_out(oa_ref[0].astype(F32), ob_ref[0].astype(F32), gate_ref, h_ref, wout_ref)
    else:
        o_d = _dilated_mix(d1_ref, d4_ref, d16_ref, l1_ref, l4_ref, l16_ref, e_ref, *mix_scratch)
        x = _mix_out(oc_ref[0].astype(F32), o_d, gate_ref, h_ref, wout_ref)
    if final:
        x = _rms_normed(x, fg_ref[...])
    if prev is not None:
        hout_ref[0] = x
    if nxt == "even":
        _in_even_body(x, *in_params, *in_outs, *in_scratch)
    elif nxt == "odd":
        _in_odd_body(x, *in_params, *in_outs, *in_scratch)


def _proj_call(h, prev, prev_args, nxt, nxt_args, final_g, tm):
    B, S, D = h.shape
    half = COL_TILE
    tok = lambda b, s: (b, s, 0)
    const2 = lambda b, s: (0, 0)
    once = dict(pipeline_mode=pl.Buffered(1))
    weight = lambda shape: pl.BlockSpec(shape, (lambda b, s: (0,) * len(shape)), **once)

    def layer_weight(arr, j):
        return pl.BlockSpec((None, *arr.shape[1:]), (lambda b, s: (j,) + (0,) * (arr.ndim - 1)), **once)
    res_spec = lambda dil, c: pl.BlockSpec((1, dil, tm // dil, c), lambda b, s: (b, 0, s, 0))
    part = pl.BlockSpec((1, tm, half), tok)
    h_spec = pl.BlockSpec((1, tm, D), tok)
    args, in_specs, out_specs, out_shape, scratch = [], [], [], [], []

    if prev is None:
        args += [h]
        in_specs += [h_spec]
    elif prev == "even":
        o_a, o_b, z, w_out, jp = prev_args
        args += [o_a, o_b, z, h, w_out]
        in_specs += [part, part, pl.BlockSpec((1, tm, 2 * half), tok), h_spec, layer_weight(w_out, jp)]
    else:
        o_c, outs, lses, z, w_out, jp = prev_args
        expand = jnp.pad(jnp.repeat(jnp.eye(ATT_HEADS, dtype=BF16), HEAD_DIM, axis=1),
                         ((0, LANES - ATT_HEADS), (0, 0)))
        args += [o_c, *outs, *lses, expand, z, h, w_out]
        in_specs += [part, part, res_spec(4, half), res_spec(16, half),
                     pl.BlockSpec((1, tm, LANES), tok), res_spec(4, LANES), res_spec(16, LANES),
                     weight((LANES, half)), pl.BlockSpec((1, tm, 2 * half), tok), h_spec,
                     layer_weight(w_out, jp)]
    final = nxt is None
    if final:
        args += [final_g]
        in_specs += [weight((1, D))]

    if nxt == "even":
        jn, layered, shared = nxt_args
        fox_rows, val_rows, kw = ATT_HEADS * FOX_PAD, ATT_HEADS * FOX_VROWS, REC_HEADS * HEAD_DIM
        args += [*layered, *shared]
        in_specs += [layer_weight(a, jn) for a in layered] + [weight(a.shape) for a in shared]
    elif nxt == "odd":
        jn, g, w_main, cos, sin = nxt_args
        args += [g, w_main, cos, sin]
        in_specs += [layer_weight(g, jn), layer_weight(w_main, jn),
                     pl.BlockSpec((tm, LANES), lambda b, s: (s, 0)),
                     pl.BlockSpec((tm, LANES), lambda b, s: (s, 0))]

    if prev is not None:
        out_specs += [h_spec]
        out_shape += [jax.ShapeDtypeStruct((B, S, D), F32)]
    if nxt == "even":
        out_specs += [pl.BlockSpec((1, tm, ZE_WIDTH), tok),
                      pl.BlockSpec((1, fox_rows, tm), lambda b, s: (b, 0, s)),
                      pl.BlockSpec((1, val_rows, tm), lambda b, s: (b, 0, s)),
                      pl.BlockSpec((1, tm, kw), tok)]
        out_shape += [jax.ShapeDtypeStruct((B, S, ZE_WIDTH), BF16),
                      jax.ShapeDtypeStruct((B, fox_rows, S), BF16),
                      jax.ShapeDtypeStruct((B, val_rows, S), BF16),
                      jax.ShapeDtypeStruct((B, S, kw), F32)]
    elif nxt == "odd":
        dw = 3 * COL_TILE
        out_specs += [pl.BlockSpec((1, tm, Z_WIDTH), tok), res_spec(4, dw), res_spec(16, dw)]
        out_shape += [jax.ShapeDtypeStruct((B, S, Z_WIDTH), BF16),
                      jax.ShapeDtypeStruct((B, 4, S // 4, dw), BF16),
                      jax.ShapeDtypeStruct((B, 16, S // 16, dw), BF16)]

    if prev == "odd":
        scratch += [pltpu.VMEM((half // LANES, tm, LANES), F32),
                    pltpu.VMEM((half // LANES, tm, LANES), F32),
                    pltpu.VMEM((tm, LANES), F32), pltpu.VMEM((tm, LANES), F32)]
    if nxt == "even":
        scratch += [pltpu.VMEM((tm, D), BF16), pltpu.VMEM((1, LANES), F32)]
    elif nxt == "odd":
        scratch += [pltpu.VMEM((tm, D), BF16), pltpu.VMEM((3 * COL_TILE // LANES, tm, LANES), F32)]

    outs = pl.pallas_call(
        functools.partial(_proj_kernel, prev=prev, nxt=nxt, final=final),
        grid=(B, S // tm),
        in_specs=in_specs,
        out_specs=out_specs,
        out_shape=out_shape,
        scratch_shapes=scratch,
        compiler_params=_params("arbitrary", "arbitrary"),
        name=f"proj_{prev}_{nxt}",
    )(*args)
    return outs


def _prep_even(w_in, b_f, w_lr, b_lr):
    fw = ATT_HEADS * HEAD_DIM
    kw = REC_HEADS * HEAD_DIM
    vw = REC_HEADS * VAL_DIM
    sizes = (fw, fw, fw, ATT_HEADS, kw, kw, vw, GLA_RANK, fw + vw)
    offs = np.cumsum((0,) + sizes)
    fq, fk, fv, ff, gq, gk, gv, glr, gate = (w_in[..., offs[i]:offs[i + 1]] for i in range(9))
    scale = HEAD_DIM ** -0.5
    L, D = w_in.shape[:2]

    def pad_heads(w, width):
        w = w.reshape(L, D, ATT_HEADS, HEAD_DIM)
        w = jnp.pad(w, ((0, 0), (0, 0), (0, 0), (0, width - HEAD_DIM)))
        return w.reshape(L, D, ATT_HEADS * width)

    w_main = jnp.concatenate([gate, pad_heads(fk, FOX_PAD), gv, gq * scale, gk], axis=-1).astype(BF16)
    wq = (fq * (scale * LOG2E)).astype(BF16)
    wv_t = jnp.swapaxes(pad_heads(fv, FOX_VROWS), 1, 2).astype(BF16)
    pad = LANES - ATT_HEADS - GLA_RANK
    w_small = jnp.pad(jnp.concatenate([ff, glr], axis=-1), ((0, 0), (0, 0), (0, pad))).astype(BF16)
    b_f_pad = jnp.pad(b_f, ((0, 0), (0, LANES - ATT_HEADS))).reshape(L, 1, LANES)
    w_lr_pad = jnp.pad(w_lr, ((0, 0), (ATT_HEADS, pad), (0, 0))).astype(BF16)
    q_aug = np.zeros((FOX_PAD - HEAD_DIM, TOKEN_TILE), np.float32)
    q_aug[:3, :] = -1.0
    v_aug = np.zeros((ATT_HEADS, FOX_VROWS, LANES), np.float32)
    v_aug[:, HEAD_DIM, :] = 1.0
    place = np.zeros((LANES, ATT_HEADS * FOX_PAD), np.float32)
    for i in range(3):
        for h in range(ATT_HEADS):
            place[ATT_HEADS * i + h, h * FOX_PAD + HEAD_DIM + i] = 1.0
    layered = (w_main, wq, wv_t, w_small, b_f_pad, w_lr_pad, b_lr.reshape(L, 1, kw))
    shared = (jnp.asarray(place, BF16), jnp.asarray(q_aug), jnp.asarray(v_aug.reshape(-1, LANES)))
    return layered, shared


def _prep_odd(w_in):
    kw = REC_HEADS * HEAD_DIM
    vw = REC_HEADS * VAL_DIM
    dw = ATT_HEADS * HEAD_DIM
    sizes = (kw, kw, vw, dw, dw, dw, vw + dw)
    offs = np.cumsum((0,) + sizes)
    rq, rk, rv, dq, dk, dv, gate = (w_in[..., offs[i]:offs[i + 1]] for i in range(7))
    scale = HEAD_DIM ** -0.5
    return jnp.concatenate([gate, rv, rq * scale, rk, dq * (scale * LOG2E), dk, dv], axis=-1).astype(BF16)


def _rope_tables(S):
    inv = jnp.power(ROPE_THETA, -jnp.arange(0, HEAD_DIM, 2, dtype=F32) / HEAD_DIM)
    ang = jnp.arange(S, dtype=F32)[:, None] * inv[None, :]
    cos, sin = jnp.cos(ang), jnp.sin(ang)
    reps = LANES // HEAD_DIM
    cos_t = jnp.tile(jnp.concatenate([cos, cos], axis=1), (1, reps))
    sin_t = jnp.tile(jnp.concatenate([-sin, sin], axis=1), (1, reps))
    return cos_t, sin_t


def _tile(S, pref):
    return min(S, pref)


def _even_mixers(z, q_t, v_t, glog, gla_g):
    S = z.shape[1]
    o_a = _fox_attention(z, q_t, v_t, _tile(S, FOX_BLOCK))
    o_b = _gla(z, glog, gla_g.reshape(1, -1), _tile(S, SCAN_TILE))
    return o_a, o_b


def _odd_mixers(z, zd4, zd16, gn_w, gn_b):
    B, S, _ = z.shape
    o_c = _retention(z, gn_w.reshape(1, -1), gn_b.reshape(1, -1), _tile(S, SCAN_TILE))
    sources = {1: (z.reshape(B, 1, S, Z_WIDTH), DIL_TILE0), 4: (zd4, 0), 16: (zd16, 0)}
    outs, lses = zip(*[_dilated_branch(*sources[dil], window, dil, DIL_QBLOCKS)
                       for window, dil in DIL_PATTERNS])
    outs = (outs[0].reshape(B, S, COL_TILE), *outs[1:])
    lses = (lses[0].reshape(B, S, LANES), *lses[1:])
    return o_c, outs, lses


def kernel(x, norm_even, w_in_even, b_f_even, w_lr_even, b_lr_even, gla_norm_even, w_out_even,
           norm_odd, w_in_odd, ret_gn_w_odd, ret_gn_b_odd, w_out_odd, final_norm):
    depth = norm_even.shape[0] + norm_odd.shape[0]
    B, S, D = x.shape
    tm = _tile(S, TOKEN_TILE)
    cos, sin = _rope_tables(S)
    even_layered, even_shared = _prep_even(w_in_even, b_f_even, w_lr_even, b_lr_even)
    even_layered = (norm_even.reshape(-1, 1, D), *even_layered)
    odd_w = _prep_odd(w_in_odd)
    odd_g = norm_odd.reshape(-1, 1, D)
    w_out = {"even": w_out_even.astype(BF16), "odd": w_out_odd.astype(BF16)}

    def in_args(i):
        if i % 2 == 0:
            return "even", (i // 2, even_layered, even_shared)
        return "odd", (i // 2, odd_g, odd_w, cos, sin)

    h = x
    kind, args = in_args(0)
    proj = _proj_call(h, None, None, kind, args, None, tm)
    for i in range(depth):
        j = i // 2
        z = proj[0]
        if kind == "even":
            o_a, o_b = _even_mixers(*proj, gla_norm_even[j])
            prev_args = (o_a, o_b, z, w_out[kind], j)
        else:
            o_c, outs, lses = _odd_mixers(*proj, ret_gn_w_odd[j], ret_gn_b_odd[j])
            prev_args = (o_c, outs, lses, z, w_out[kind], j)
        nxt, nxt_args = in_args(i + 1) if i + 1 < depth else (None, None)
        h, *proj = _proj_call(h, kind, prev_args, nxt, nxt_args, final_norm.reshape(1, D), tm)
        kind = nxt
    return h
```

```python
import functools
import math

import numpy as np
import jax
import jax.numpy as jnp
from jax import lax
from jax.experimental import pallas as pl
from jax.experimental.pallas import tpu as pltpu

F32 = jnp.float32
BF16 = jnp.bfloat16

EPS = 1e-6
ROPE_THETA = 10000.0
HEAD_DIM = 64
VAL_DIM = 128
ATT_HEADS = 8
REC_HEADS = 4
GLA_RANK = 16
GLA_TAU = 16.0
GLA_CHUNK = 64
RET_CHUNK = 256
DIL_BLOCK = 128
DIL_PATTERNS = ((128, 1), (512, 4), (2048, 16))

Z_WIDTH = 3584
DIL_TILE0 = 4
DIL_QBLOCKS = 8
DIL_VROWS = 144
ZE_WIDTH = 3072
FOX_PAD = 128
FOX_VROWS = 80
LOG2E = 1.4426950408889634
COL_TILE = 512
TOKEN_TILE = 512
SCAN_TILE = 1024
FINAL_TILE = 1024
FOX_BLOCK = 256
LANES = 128
V7X_VMEM_LIMIT = 56 * 1024 * 1024

NT_DIMS = (((1,), (1,)), ((), ()))
TN_DIMS = (((0,), (0,)), ((), ()))


def _params(*sem):
    return pltpu.CompilerParams(dimension_semantics=sem, vmem_limit_bytes=V7X_VMEM_LIMIT)


def _split2(x):
    hi = x.astype(BF16)
    lo = (x - hi.astype(F32)).astype(BF16)
    return hi, lo


def _split3(x):
    hi, lo = _split2(x)
    lo2 = (x - hi.astype(F32) - lo.astype(F32)).astype(BF16)
    return hi, lo, lo2


def _log_sigmoid(x):
    return jnp.minimum(x, 0.0) - jnp.log(1.0 + jnp.exp(-jnp.abs(x)))


def _rms_normed(x, g):
    ms = jnp.mean(x * x, axis=-1, keepdims=True)
    return x * lax.rsqrt(ms + EPS) * g


def _in_even_body(x, g_ref, w_ref, wq_ref, wv_ref, ws_ref, bf_ref, wlr_ref, blr_ref, place_ref,
                  qaug_ref, vaug_ref, z_ref, qt_ref, vt_ref, glog_ref, u_ref, carry_ref):
    tm = x.shape[0]

    @pl.when(pl.program_id(1) == 0)
    def _():
        carry_ref[...] = jnp.zeros_like(carry_ref)

    u_ref[...] = _rms_normed(x, g_ref[...]).astype(BF16)
    u = u_ref[...]

    zs = jnp.dot(u, ws_ref[...], preferred_element_type=F32)
    logf = _log_sigmoid(zs + bf_ref[...])
    row = lax.broadcasted_iota(jnp.int32, (LANES, LANES), 0)
    col = lax.broadcasted_iota(jnp.int32, (LANES, LANES), 1)
    tril = jnp.where(row >= col, 1.0, 0.0).astype(BF16)
    carry = carry_ref[...]
    blocks = []
    for r0 in range(0, tm, LANES):
        blk_sum = sum(jnp.dot(tril, t[r0:r0 + LANES, :], preferred_element_type=F32)
                      for t in _split3(logf)) + carry
        carry = blk_sum[LANES - 1:LANES, :]
        blocks.append(blk_sum)
    csum = jnp.concatenate(blocks, axis=0)
    carry_ref[...] = carry
    hi, lo, lo2 = (t.astype(F32) for t in _split3(csum * LOG2E))
    lane = lax.broadcasted_iota(jnp.int32, (tm, LANES), 1)
    packed = jnp.where(lane < ATT_HEADS, hi,
                       jnp.where(lane < 2 * ATT_HEADS, pltpu.roll(lo, ATT_HEADS, 1),
                                 jnp.where(lane < 3 * ATT_HEADS, pltpu.roll(lo2, 2 * ATT_HEADS, 1), 0.0)))
    c_cols = jnp.dot(packed.astype(BF16), place_ref[...], preferred_element_type=F32)

    glr = jnp.dot(zs.astype(BF16), wlr_ref[...], preferred_element_type=F32)
    glog_ref[0] = _log_sigmoid(glr + blr_ref[...]) * (1.0 / GLA_TAU)

    for j in range(ZE_WIDTH // COL_TILE):
        cols = slice(j * COL_TILE, (j + 1) * COL_TILE)
        r = jnp.dot(u, w_ref[:, cols], preferred_element_type=F32)
        if j in (2, 3):
            r = r + c_cols[:, (j - 2) * COL_TILE:(j - 1) * COL_TILE]
        z_ref[0, :, cols] = r.astype(BF16)

    def feature_major(w_t_ref, aug_ref, out_ref, tile):
        for j in range(w_t_ref.shape[0] // tile):
            rows = slice(j * tile, (j + 1) * tile)
            r = lax.dot_general(w_t_ref[rows, :], u, NT_DIMS, preferred_element_type=F32)
            aug = aug_ref[rows, :]
            for c in range(tm // LANES):
                lanes = slice(c * LANES, (c + 1) * LANES)
                out_ref[0, rows, lanes] = (r[:, lanes] + aug).astype(BF16)

    feature_major(wv_ref, vaug_ref, vt_ref, ATT_HEADS * FOX_VROWS // 2)
    qtok = jnp.dot(u, wq_ref[...], preferred_element_type=F32)
    aug = qaug_ref[...].astype(BF16)
    for g in range(ATT_HEADS // 2):
        q_t = qtok[:, g * LANES:(g + 1) * LANES].T
        for half in range(2):
            r0 = (2 * g + half) * FOX_PAD
            qt_ref[0, r0:r0 + HEAD_DIM, :] = q_t[half * HEAD_DIM:(half + 1) * HEAD_DIM, :].astype(BF16)
            qt_ref[0, r0 + HEAD_DIM:r0 + FOX_PAD, :] = aug


def _rope_tile(x, cos, sin):
    lane = lax.broadcasted_iota(jnp.int32, (x.shape[0], LANES), 1)
    first_half = (lane % HEAD_DIM) < (HEAD_DIM // 2)
    outs = []
    for c in range(x.shape[1] // LANES):
        xc = x[:, c * LANES:(c + 1) * LANES]
        partner = jnp.where(first_half,
                            pltpu.roll(xc, LANES - HEAD_DIM // 2, 1),
                            pltpu.roll(xc, HEAD_DIM // 2, 1))
        outs.append(xc * cos + partner * sin)
    return jnp.concatenate(outs, axis=-1)


def _in_odd_body(x, g_ref, w_ref, cos_ref, sin_ref, z_ref, zd4_ref, zd16_ref, u_ref, r_ref):
    tm = x.shape[0]
    u_ref[...] = _rms_normed(x, g_ref[...]).astype(BF16)
    u = u_ref[...]
    cos = cos_ref[...]
    sin = sin_ref[...]
    for j in range(Z_WIDTH // COL_TILE):
        cols = slice(j * COL_TILE, (j + 1) * COL_TILE)
        r = jnp.dot(u, w_ref[:, cols], preferred_element_type=F32)
        if j in (3, 4, 5):
            r = _rope_tile(r, cos, sin)
        z_ref[0, :, cols] = r.astype(BF16)
        if j >= DIL_TILE0:
            t = j - DIL_TILE0
            for c in range(COL_TILE // LANES):
                slot = t * (COL_TILE // LANES) + c
                r_ref[slot] = r[:, c * LANES:(c + 1) * LANES]
                dcols = slice(t * COL_TILE + c * LANES, t * COL_TILE + (c + 1) * LANES)
                for dil, ref in ((4, zd4_ref), (16, zd16_ref)):
                    for res in range(dil):
                        ref[0, res, :, dcols] = (
                            r_ref[slot, pl.ds(res, tm // dil, stride=dil), :].astype(BF16))


def _fox_kernel(qt_ref, kp_ref, vt_ref, o_ref, s_ref, acc_ref, mp_ref, mc_ref, ot_ref, mask_ref, *, blk):
    i = pl.program_id(1)
    heads = range(ATT_HEADS)
    rows = lambda h: slice(h * FOX_PAD, (h + 1) * FOX_PAD)
    vrows = lambda h: slice(h * FOX_VROWS, (h + 1) * FOX_VROWS)

    @pl.when(i == 0)
    def _():
        key = lax.broadcasted_iota(jnp.int32, (blk, blk), 0)
        qry = lax.broadcasted_iota(jnp.int32, (blk, blk), 1)
        mask_ref[0] = jnp.zeros((blk, blk), F32)
        mask_ref[1] = jnp.where(key <= qry, 0.0, -jnp.inf)

    def score_stage(h, j):
        ks = pl.multiple_of(j * blk, blk)
        s = jnp.dot(kp_ref[0, pl.ds(ks, blk), rows(h)], qt_ref[0, rows(h), :],
                    preferred_element_type=F32)
        s = s + mask_ref[jnp.where(j == i, 1, 0)]
        s_ref[h] = s
        m_old = mc_ref[h]
        mp_ref[h] = m_old
        mc_ref[h] = jnp.maximum(m_old, jnp.max(s, axis=0, keepdims=True))

    def value_stage(h, j):
        ks = pl.multiple_of(j * blk, blk)
        m_new = mc_ref[h]
        alpha = jnp.exp2(mp_ref[h] - m_new)
        p = jnp.exp2((s_ref[h] - m_new).astype(BF16))
        pv = jnp.dot(vt_ref[0, vrows(h), pl.ds(ks, blk)], p, preferred_element_type=F32)
        acc_ref[h] = alpha * acc_ref[h] + pv

    def trip(j):
        for h in heads:
            value_stage(h, j - 1)
            score_stage(h, j)

    for h in heads:
        acc_ref[h] = jnp.zeros(acc_ref.shape[1:], F32)
        mc_ref[h] = jnp.full((1, blk), -jnp.inf, F32)
        score_stage(h, 0)

    odd = i % 2

    @pl.when(odd == 1)
    def _():
        trip(1)

    def pair(t, _):
        j = 1 + odd + 2 * t
        trip(j)
        trip(j + 1)
        return 0

    lax.fori_loop(0, i // 2, pair, 0)

    for h in heads:
        value_stage(h, i)
        acc = acc_ref[h]
        ot_ref[h * HEAD_DIM:(h + 1) * HEAD_DIM, :] = (
            acc[:HEAD_DIM, :] / acc[HEAD_DIM:HEAD_DIM + 1, :])
    o_ref[0] = ot_ref[...].T.astype(BF16)


def _fox_attention(z, q_t, v_t, blk):
    B, S, _ = z.shape
    fox_rows = ATT_HEADS * FOX_PAD
    val_rows = ATT_HEADS * FOX_VROWS
    return pl.pallas_call(
        functools.partial(_fox_kernel, blk=blk),
        grid=(B, S // blk),
        in_specs=[
            pl.BlockSpec((1, fox_rows, blk), lambda b, i: (b, 0, i)),
            pl.BlockSpec((1, S, fox_rows), lambda b, i: (b, 0, 1)),
            pl.BlockSpec((1, val_rows, S), lambda b, i: (b, 0, 0)),
        ],
        out_specs=pl.BlockSpec((1, blk, COL_TILE), lambda b, i: (b, i, 0)),
        out_shape=jax.ShapeDtypeStruct((B, S, COL_TILE), BF16),
        scratch_shapes=[pltpu.VMEM((ATT_HEADS, blk, blk), F32),
                        pltpu.VMEM((ATT_HEADS, FOX_VROWS, blk), F32),
                        pltpu.VMEM((ATT_HEADS, 1, blk), F32),
                        pltpu.VMEM((ATT_HEADS, 1, blk), F32),
                        pltpu.VMEM((ATT_HEADS * HEAD_DIM, blk), F32),
                        pltpu.VMEM((2, blk, blk), F32)],
        compiler_params=_params("arbitrary", "arbitrary"),
        name="fox_attention",
    )(q_t, z, v_t)


def _gla_kernel(q_ref, k_ref, v_ref, g_ref, gn_ref, o_ref,
                qhat_ref, top_ref, bot_ref, kinc_ref, dec_ref, dect_ref, state_ref):
    tg = q_ref.shape[1]
    C = GLA_CHUNK
    blk = 2 * C
    n_chunks = tg // C
    pairs = REC_HEADS // 2

    @pl.when(pl.program_id(1) == 0)
    def _():
        state_ref[...] = jnp.zeros_like(state_ref)

    row = lax.broadcasted_iota(jnp.int32, (blk, blk), 0)
    col = lax.broadcasted_iota(jnp.int32, (blk, blk), 1)
    chunk_start = row & (-C)
    tril = jnp.where(col <= row, jnp.where(col >= chunk_start, 1.0, 0.0), 0.0).astype(BF16)
    ghi, glo = _split2(g_ref[0])
    b = jnp.concatenate(
        [jnp.dot(tril, ghi[r0:r0 + blk, :], preferred_element_type=F32)
         + jnp.dot(tril, glo[r0:r0 + blk, :], preferred_element_type=F32)
         for r0 in range(0, tg, blk)], axis=0)
    q = q_ref[0].astype(F32)
    k = k_ref[0].astype(F32)
    tok = lax.broadcasted_iota(jnp.int32, (tg, LANES), 0)
    lane = lax.broadcasted_iota(jnp.int32, (tg, LANES), 1)
    in_a = (tok & C) == 0
    first = lane < HEAD_DIM
    qt = q * jnp.exp(b)
    kt = k * jnp.exp(-b)
    top_ref[...] = kt
    dec_ref[...] = jnp.zeros_like(dec_ref)
    kd_rows, dec_rows = [], []
    for c in range(n_chunks):
        rows = slice(c * C, (c + 1) * C)
        b_last = b[(c + 1) * C - 1:(c + 1) * C, :]
        kd_rows.append(k[rows, :] * jnp.exp(b_last - b[rows, :]))
        dec_rows.append(jnp.exp(b_last))
        dec_ref[c:c + 1, :] = dec_rows[-1]
    for c in range(0, n_chunks, 2):
        bot_ref[c * C:(c + 1) * C, :] = kd_rows[c]
        bot_ref[(c + 1) * C:(c + 2) * C, :] = kt[(c + 1) * C:(c + 2) * C, :]
        kinc_ref[c * C:(c + 1) * C, :] = kd_rows[c] * dec_rows[c + 1]
        kinc_ref[(c + 1) * C:(c + 2) * C, :] = kd_rows[c + 1]
    dect_ref[...] = dec_ref[...].T
    for half in range(2):
        for g in range(pairs):
            gs = slice(g * LANES, (g + 1) * LANES)
            qh = jnp.where(first, qt[:, gs], 0.0) if half == 0 else jnp.where(first, 0.0, qt[:, gs])
            qhat_ref[half, :, 2 * g * LANES:(2 * g + 1) * LANES] = jnp.where(in_a, qh, 0.0).astype(BF16)
            qhat_ref[half, :, (2 * g + 1) * LANES:(2 * g + 2) * LANES] = jnp.where(in_a, 0.0, qh).astype(BF16)

    rb = lax.broadcasted_iota(jnp.int32, (blk, blk), 0)
    cb = lax.broadcasted_iota(jnp.int32, (blk, blk), 1)
    lower = cb <= rb
    low_rows = rb < HEAD_DIM

    for n in range(tg // blk):
        rows = slice(n * blk, (n + 1) * blk)
        for g in range(pairs):
            gs = slice(g * LANES, (g + 1) * LANES)
            top_t = top_ref[rows, gs].T.astype(BF16)
            bot_t = bot_ref[rows, gs].T.astype(BF16)
            kinc_t = kinc_ref[rows, gs].T.astype(BF16)
            st = state_ref[g]
            dec_a = dect_ref[gs, 2 * n:2 * n + 1]
            dec_b = dect_ref[gs, 2 * n + 1:2 * n + 2]
            rhs = jnp.concatenate([jnp.concatenate([top_t, st.astype(BF16)], axis=1),
                                   jnp.concatenate([bot_t, (st * dec_a).astype(BF16)], axis=1)], axis=0)
            inc = []
            for half in range(2):
                h = 2 * g + half
                vs = slice(h * VAL_DIM, (h + 1) * VAL_DIM)
                v = v_ref[0, rows, vs]
                sc = jnp.dot(qhat_ref[half, rows, 2 * g * LANES:(2 * g + 2) * LANES], rhs,
                             preferred_element_type=F32)
                att = jnp.where(lower, sc[:, :blk], 0.0).astype(BF16)
                both = jnp.dot(jnp.concatenate([att, kinc_t], axis=0), v, preferred_element_type=F32)
                o = both[:blk, :] + sc[:, blk:]
                inc.append(both[blk:, :])
                ms = jnp.mean(o * o, axis=-1, keepdims=True)
                o_ref[0, rows, vs] = (o * lax.rsqrt(ms + EPS) * gn_ref[:, vs]).astype(BF16)
            state_ref[g] = st * (dec_a * dec_b) + jnp.where(low_rows, inc[0], inc[1])


def _gla(z, glog, gla_g, tg):
    B, S, _ = z.shape
    kw = REC_HEADS * HEAD_DIM
    vw = REC_HEADS * VAL_DIM
    return pl.pallas_call(
        _gla_kernel,
        grid=(B, S // tg),
        in_specs=[
            pl.BlockSpec((1, tg, kw), lambda b, s: (b, s, 10)),
            pl.BlockSpec((1, tg, kw), lambda b, s: (b, s, 11)),
            pl.BlockSpec((1, tg, vw), lambda b, s: (b, s, 4)),
            pl.BlockSpec((1, tg, kw), lambda b, s: (b, s, 0)),
            pl.BlockSpec((1, vw), lambda b, s: (0, 0)),
        ],
        out_specs=pl.BlockSpec((1, tg, vw), lambda b, s: (b, s, 0)),
        out_shape=jax.ShapeDtypeStruct((B, S, vw), BF16),
        scratch_shapes=[
            pltpu.VMEM((2, tg, 2 * kw), BF16),
            pltpu.VMEM((tg, kw), F32), pltpu.VMEM((tg, kw), F32), pltpu.VMEM((tg, kw), F32),
            pltpu.VMEM((LANES, kw), F32), pltpu.VMEM((kw, LANES), F32),
            pltpu.VMEM((REC_HEADS // 2, 2 * HEAD_DIM, VAL_DIM), F32),
        ],
        compiler_params=_params("arbitrary", "arbitrary"),
        name="gla",
    )(z, z, z, glog, gla_g)


def _ret_kernel(q_ref, k_ref, v_ref, gw_ref, gb_ref, o_ref, state_ref, dmat_ref, zeta_ref):
    tg = q_ref.shape[1]
    C = RET_CHUNK
    n_chunks = tg // C
    pairs = REC_HEADS // 2
    log_gamma = [math.log(1.0 - 2.0 ** (-5.0 - h)) for h in range(REC_HEADS)]

    @pl.when(pl.program_id(1) == 0)
    def _():
        state_ref[...] = jnp.zeros_like(state_ref)

    ri = lax.broadcasted_iota(jnp.int32, (C, C), 0)
    ci = lax.broadcasted_iota(jnp.int32, (C, C), 1)
    diff = (ri - ci).astype(F32)
    idx = lax.broadcasted_iota(jnp.int32, (C, 1), 0).astype(F32)
    low_lanes = lax.broadcasted_iota(jnp.int32, (C, LANES), 1) < HEAD_DIM
    low_rows = lax.broadcasted_iota(jnp.int32, (LANES, VAL_DIM), 0) < HEAD_DIM
    for h in range(REC_HEADS):
        dmat_ref[h] = jnp.where(diff >= 0, jnp.exp(jnp.maximum(diff, 0.0) * log_gamma[h]), 0.0)
    for g in range(pairs):
        zeta_ref[g] = jnp.where(low_lanes, jnp.exp((C - 1.0 - idx) * log_gamma[2 * g]),
                                jnp.exp((C - 1.0 - idx) * log_gamma[2 * g + 1]))

    for c in range(n_chunks):
        rows = slice(c * C, (c + 1) * C)
        for g in range(pairs):
            gs = slice(g * LANES, (g + 1) * LANES)
            q_pair = q_ref[0, rows, gs]
            k_f32 = k_ref[0, rows, gs].astype(F32)
            k_t = k_f32.T.astype(BF16)
            kz_t = (k_f32 * zeta_ref[g]).T.astype(BF16)
            st = state_ref[g]
            rhs = jnp.concatenate([k_t, st.astype(BF16)], axis=1)
            zero = jnp.zeros_like(q_pair)
            inc = []
            for half in range(2):
                h = 2 * g + half
                vs = slice(h * VAL_DIM, (h + 1) * VAL_DIM)
                xi = jnp.exp((idx + 1.0) * log_gamma[h])
                q = jnp.where(low_lanes, q_pair, zero) if half == 0 else jnp.where(low_lanes, zero, q_pair)
                v = v_ref[0, rows, vs]
                sc = jnp.dot(q, rhs, preferred_element_type=F32)
                att = (sc[:, :C] * dmat_ref[h]).astype(BF16)
                both = jnp.dot(jnp.concatenate([att, kz_t], axis=0), v, preferred_element_type=F32)
                o = both[:C, :] + xi * sc[:, C:]
                inc.append(both[C:, :])
                o = o - jnp.mean(o, axis=-1, keepdims=True)
                ms = jnp.mean(o * o, axis=-1, keepdims=True)
                o = o * lax.rsqrt(ms + EPS) * gw_ref[:, vs] + gb_ref[:, vs]
                o_ref[0, rows, vs] = o.astype(BF16)
            decay = jnp.where(low_rows, math.exp(C * log_gamma[2 * g]), math.exp(C * log_gamma[2 * g + 1]))
            state_ref[g] = st * decay + jnp.where(low_rows, inc[0], inc[1])


def _retention(z, gn_w, gn_b, tg):
    B, S, _ = z.shape
    kw = REC_HEADS * HEAD_DIM
    vw = REC_HEADS * VAL_DIM
    return pl.pallas_call(
        _ret_kernel,
        grid=(B, S // tg),
        in_specs=[
            pl.BlockSpec((1, tg, kw), lambda b, s: (b, s, 6)),
            pl.BlockSpec((1, tg, kw), lambda b, s: (b, s, 7)),
            pl.BlockSpec((1, tg, vw), lambda b, s: (b, s, 2)),
            pl.BlockSpec((1, vw), lambda b, s: (0, 0)),
            pl.BlockSpec((1, vw), lambda b, s: (0, 0)),
        ],
        out_specs=pl.BlockSpec((1, tg, vw), lambda b, s: (b, s, 0)),
        out_shape=jax.ShapeDtypeStruct((B, S, vw), BF16),
        scratch_shapes=[pltpu.VMEM((REC_HEADS // 2, 2 * HEAD_DIM, VAL_DIM), F32),
                        pltpu.VMEM((REC_HEADS, RET_CHUNK, RET_CHUNK), F32),
                        pltpu.VMEM((REC_HEADS // 2, RET_CHUNK, LANES), F32)],
        compiler_params=_params("arbitrary", "arbitrary"),
        name="retention",
    )(z, z, z, gn_w, gn_b)


def _dil_kernel(q_ref, k_ref, v_ref, o_ref, lse_ref,
                vt_ref, qt_ref, s_ref, mx_ref, bias_ref, ot_ref, lt_ref, *, blk, nblk, ncls, span):
    n = pl.program_id(2)
    m = k_ref.shape[2]
    pairs = ATT_HEADS // 2
    chains = [(g, half) for g in range(pairs) for half in range(2)]
    units = [(rc, t) for rc in range(ncls) for t in range(nblk)]
    feat = lax.broadcasted_iota(jnp.int32, (LANES, blk), 0)

    @pl.when(n == 0)
    def _():
        for rc in range(ncls):
            for g in range(pairs):
                vt_ref[rc, g * DIL_VROWS + LANES:(g + 1) * DIL_VROWS, :] = jnp.ones(
                    (DIL_VROWS - LANES, m), BF16)

        def build(kb, _):
            r0 = pl.multiple_of(kb * blk, blk)
            for rc in range(ncls):
                for g in range(pairs):
                    v = v_ref[0, rc, pl.ds(r0, blk), g * LANES:(g + 1) * LANES].astype(F32)
                    vt_ref[rc, g * DIL_VROWS:g * DIL_VROWS + LANES, pl.ds(r0, blk)] = v.T.astype(BF16)
            return 0

        lax.fori_loop(0, m // blk, build, 0)

    single_step = m == nblk * blk
    starts, wins = [], []
    for t in range(nblk):
        qb = n * nblk + t
        first = single_step and t == 0
        win = blk if first else 2 * blk
        start = 0 if first else pl.multiple_of(jnp.maximum(qb - 1, 0) * blk, blk)
        starts.append(start)
        wins.append(win)
        ki_w = lax.broadcasted_iota(jnp.int32, (win, blk), 0)
        qi_w = lax.broadcasted_iota(jnp.int32, (win, blk), 1)
        rel = qb * blk + qi_w - start - ki_w
        bias_ref[t, :win] = jnp.where(rel >= 0, jnp.where(rel <= span, 0.0, -jnp.inf), -jnp.inf)
    for u, (rc, t) in enumerate(units):
        for g in range(pairs):
            q_t = q_ref[0, rc, t * blk:(t + 1) * blk, g * LANES:(g + 1) * LANES].astype(F32).T
            qt_ref[u, 2 * g] = jnp.where(feat < HEAD_DIM, q_t, 0.0).astype(BF16)
            qt_ref[u, 2 * g + 1] = jnp.where(feat < HEAD_DIM, 0.0, q_t).astype(BF16)

    for u, (rc, t) in enumerate(units):
        for c, (g, half) in enumerate(chains):
            k = k_ref[0, rc, pl.ds(starts[t], wins[t]), g * LANES:(g + 1) * LANES]
            s = jnp.dot(k, qt_ref[u, c], preferred_element_type=F32) + bias_ref[t, :wins[t]]
            s_ref[u, c, :wins[t]] = s
            mx_ref[u, c] = jnp.max(s, axis=0, keepdims=True)

    for u, (rc, t) in enumerate(units):
        lt_ref[u] = jnp.zeros(lt_ref.shape[1:], F32)
        for c, (g, half) in enumerate(chains):
            mx = mx_ref[u, c]
            p = jnp.exp2((s_ref[u, c, :wins[t]] - mx).astype(BF16))
            acc = jnp.dot(vt_ref[rc, g * DIL_VROWS:(g + 1) * DIL_VROWS, pl.ds(starts[t], wins[t])], p,
                          preferred_element_type=F32)
            den = acc[LANES:LANES + 1, :]
            ot_ref[u, c * HEAD_DIM:(c + 1) * HEAD_DIM, :] = (
                acc[half * HEAD_DIM:(half + 1) * HEAD_DIM, :] * (1.0 / den))
            lt_ref[u, c:c + 1, :] = (mx + jnp.log2(den)) * (1.0 / LOG2E)
        qrows = slice(t * blk, (t + 1) * blk)
        o_ref[0, rc, qrows, :] = ot_ref[u].T.astype(BF16)
        lse_ref[0, rc, qrows, :] = lt_ref[u].T


def _dilated_branch(src, tile0, window, dil, units):
    B, _, m, _ = src.shape
    blk = DIL_BLOCK
    nblk = min(units, m // blk)
    ncls = min(dil, units // nblk)
    tq = nblk * blk
    n_units = ncls * nblk
    return pl.pallas_call(
        functools.partial(_dil_kernel, blk=blk, nblk=nblk, ncls=ncls, span=window // dil),
        grid=(B, dil // ncls, m // tq),
        in_specs=[
            pl.BlockSpec((1, ncls, tq, COL_TILE), lambda b, r, n: (b, r, n, tile0)),
            pl.BlockSpec((1, ncls, m, COL_TILE), lambda b, r, n: (b, r, 0, tile0 + 1)),
            pl.BlockSpec((1, ncls, m, COL_TILE), lambda b, r, n: (b, r, 0, tile0 + 2)),
        ],
        out_specs=[
            pl.BlockSpec((1, ncls, tq, COL_TILE), lambda b, r, n: (b, r, n, 0)),
            pl.BlockSpec((1, ncls, tq, LANES), lambda b, r, n: (b, r, n, 0)),
        ],
        out_shape=[
            jax.ShapeDtypeStruct((B, dil, m, COL_TILE), BF16),
            jax.ShapeDtypeStruct((B, dil, m, LANES), F32),
        ],
        scratch_shapes=[pltpu.VMEM((ncls, ATT_HEADS // 2 * DIL_VROWS, m), BF16),
                        pltpu.VMEM((n_units, ATT_HEADS, LANES, blk), BF16),
                        pltpu.VMEM((n_units, ATT_HEADS, 2 * blk, blk), F32),
                        pltpu.VMEM((n_units, ATT_HEADS, 1, blk), F32),
                        pltpu.VMEM((nblk, 2 * blk, blk), F32),
                        pltpu.VMEM((n_units, ATT_HEADS * HEAD_DIM, blk), F32),
                        pltpu.VMEM((n_units, LANES, blk), F32)],
        compiler_params=_params("arbitrary", "arbitrary", "arbitrary"),
        name=f"dilated_{dil}",
    )(src, src, src)


def _silu(x):
    return x / (1.0 + jnp.exp(-x))


def _mix_out(mix_a, mix_b, gate_ref, h_ref, w_ref):
    half = mix_a.shape[1]
    gate = _silu(gate_ref[0].astype(F32))
    a = (mix_a * gate[:, :half]).astype(BF16)
    b = (mix_b * gate[:, half:]).astype(BF16)
    y = (jnp.dot(a, w_ref[:half, :], preferred_element_type=F32)
         + jnp.dot(b, w_ref[half:, :], preferred_element_type=F32))
    return h_ref[0] + y


def _dilated_mix(d1_ref, d4_ref, d16_ref, l1_ref, l4_ref, l16_ref, e_ref,
                 o4_ref, o16_ref, ls4_ref, ls16_ref):
    tm = d1_ref.shape[1]
    chunks = d1_ref.shape[2] // LANES
    for dil, src, lsrc, dst, ldst in ((4, d4_ref, l4_ref, o4_ref, ls4_ref),
                                      (16, d16_ref, l16_ref, o16_ref, ls16_ref)):
        for res in range(dil):
            token_rows = pl.ds(res, tm // dil, stride=dil)
            ldst[token_rows, :] = lsrc[0, res]
            for c in range(chunks):
                dst[c, token_rows, :] = src[0, res, :, c * LANES:(c + 1) * LANES].astype(F32)
    l1, l2, l3 = l1_ref[0], ls4_ref[...], ls16_ref[...]
    mx = jnp.maximum(jnp.maximum(l1, l2), l3)
    e = e_ref[...]
    expand = lambda w: jnp.dot(w.astype(BF16), e, preferred_element_type=F32)
    x1, x2, x3 = expand(jnp.exp(l1 - mx)), expand(jnp.exp(l2 - mx)), expand(jnp.exp(l3 - mx))
    o4 = jnp.concatenate([o4_ref[c] for c in range(chunks)], axis=-1)
    o16 = jnp.concatenate([o16_ref[c] for c in range(chunks)], axis=-1)
    return (x1 * d1_ref[0].astype(F32) + x2 * o4 + x3 * o16) / (x1 + x2 + x3)


N_IN = {"even": 11, "odd": 4}
N_OUT = {"even": 4, "odd": 3}
N_SCRATCH = {"even": 2, "odd": 2}


def _proj_kernel(*refs, prev, nxt, final):
    refs = list(refs)
    take = lambda n: [refs.pop(0) for _ in range(n)]
    if prev is None:
        (x_ref,) = take(1)
    elif prev == "even":
        oa_ref, ob_ref, gate_ref, h_ref, wout_ref = take(5)
    else:
        oc_ref, d1_ref, d4_ref, d16_ref, l1_ref, l4_ref, l16_ref, e_ref, gate_ref, h_ref, wout_ref = take(11)
    if final:
        (fg_ref,) = take(1)
    in_params = take(N_IN[nxt]) if nxt else []
    if prev is not None:
        (hout_ref,) = take(1)
    in_outs = take(N_OUT[nxt]) if nxt else []
    mix_scratch = take(4) if prev == "odd" else []
    in_scratch = take(N_SCRATCH[nxt]) if nxt else []

    if prev is None:
        x = x_ref[0]
    elif prev == "even":
        x = _mix_out(oa_ref[0].astype(F32), ob_ref[0].astype(F32), gate_ref, h_ref, wout_ref)
    else:
        o_d = _dilated_mix(d1_ref, d4_ref, d16_ref, l1_ref, l4_ref, l16_ref, e_ref, *mix_scratch)
        x = _mix_out(oc_ref[0].astype(F32), o_d, gate_ref, h_ref, wout_ref)
    if final:
        x = _rms_normed(x, fg_ref[...])
    if prev is not None:
        hout_ref[0] = x
    if nxt == "even":
        _in_even_body(x, *in_params, *in_outs, *in_scratch)
    elif nxt == "odd":
        _in_odd_body(x, *in_params, *in_outs, *in_scratch)


def _proj_call(h, prev, prev_args, nxt, nxt_args, final_g, tm):
    B, S, D = h.shape
    half = COL_TILE
    tok = lambda b, s: (b, s, 0)
    const2 = lambda b, s: (0, 0)
    once = dict(pipeline_mode=pl.Buffered(1))
    weight = lambda shape: pl.BlockSpec(shape, (lambda b, s: (0,) * len(shape)), **once)

    def layer_weight(arr, j):
        return pl.BlockSpec((None, *arr.shape[1:]), (lambda b, s: (j,) + (0,) * (arr.ndim - 1)), **once)
    res_spec = lambda dil, c: pl.BlockSpec((1, dil, tm // dil, c), lambda b, s: (b, 0, s, 0))
    part = pl.BlockSpec((1, tm, half), tok)
    h_spec = pl.BlockSpec((1, tm, D), tok)
    args, in_specs, out_specs, out_shape, scratch = [], [], [], [], []

    if prev is None:
        args += [h]
        in_specs += [h_spec]
    elif prev == "even":
        o_a, o_b, z, w_out, jp = prev_args
        args += [o_a, o_b, z, h, w_out]
        in_specs += [part, part, pl.BlockSpec((1, tm, 2 * half), tok), h_spec, layer_weight(w_out, jp)]
    else:
        o_c, outs, lses, z, w_out, jp = prev_args
        expand = jnp.pad(jnp.repeat(jnp.eye(ATT_HEADS, dtype=BF16), HEAD_DIM, axis=1),
                         ((0, LANES - ATT_HEADS), (0, 0)))
        args += [o_c, *outs, *lses, expand, z, h, w_out]
        in_specs += [part, part, res_spec(4, half), res_spec(16, half),
                     pl.BlockSpec((1, tm, LANES), tok), res_spec(4, LANES), res_spec(16, LANES),
                     weight((LANES, half)), pl.BlockSpec((1, tm, 2 * half), tok), h_spec,
                     layer_weight(w_out, jp)]
    final = nxt is None
    if final:
        args += [final_g]
        in_specs += [weight((1, D))]

    if nxt == "even":
        jn, layered, shared = nxt_args
        fox_rows, val_rows, kw = ATT_HEADS * FOX_PAD, ATT_HEADS * FOX_VROWS, REC_HEADS * HEAD_DIM
        args += [*layered, *shared]
        in_specs += [layer_weight(a, jn) for a in layered] + [weight(a.shape) for a in shared]
    elif nxt == "odd":
        jn, g, w_main, cos, sin = nxt_args
        args += [g, w_main, cos, sin]
        in_specs += [layer_weight(g, jn), layer_weight(w_main, jn),
                     pl.BlockSpec((tm, LANES), lambda b, s: (s, 0)),
                     pl.BlockSpec((tm, LANES), lambda b, s: (s, 0))]

    if prev is not None:
        out_specs += [h_spec]
        out_shape += [jax.ShapeDtypeStruct((B, S, D), F32)]
    if nxt == "even":
        out_specs += [pl.BlockSpec((1, tm, ZE_WIDTH), tok),
                      pl.BlockSpec((1, fox_rows, tm), lambda b, s: (b, 0, s)),
                      pl.BlockSpec((1, val_rows, tm), lambda b, s: (b, 0, s)),
                      pl.BlockSpec((1, tm, kw), tok)]
        out_shape += [jax.ShapeDtypeStruct((B, S, ZE_WIDTH), BF16),
                      jax.ShapeDtypeStruct((B, fox_rows, S), BF16),
                      jax.ShapeDtypeStruct((B, val_rows, S), BF16),
                      jax.ShapeDtypeStruct((B, S, kw), F32)]
    elif nxt == "odd":
        dw = 3 * COL_TILE
        out_specs += [pl.BlockSpec((1, tm, Z_WIDTH), tok), res_spec(4, dw), res_spec(16, dw)]
        out_shape += [jax.ShapeDtypeStruct((B, S, Z_WIDTH), BF16),
                      jax.ShapeDtypeStruct((B, 4, S // 4, dw), BF16),
                      jax.ShapeDtypeStruct((B, 16, S // 16, dw), BF16)]

    if prev == "odd":
        scratch += [pltpu.VMEM((half // LANES, tm, LANES), F32),
                    pltpu.VMEM((half // LANES, tm, LANES), F32),
                    pltpu.VMEM((tm, LANES), F32), pltpu.VMEM((tm, LANES), F32)]
    if nxt == "even":
        scratch += [pltpu.VMEM((tm, D), BF16), pltpu.VMEM((1, LANES), F32)]
    elif nxt == "odd":
        scratch += [pltpu.VMEM((tm, D), BF16), pltpu.VMEM((3 * COL_TILE // LANES, tm, LANES), F32)]

    outs = pl.pallas_call(
        functools.partial(_proj_kernel, prev=prev, nxt=nxt, final=final),
        grid=(B, S // tm),
        in_specs=in_specs,
        out_specs=out_specs,
        out_shape=out_shape,
        scratch_shapes=scratch,
        compiler_params=_params("arbitrary", "arbitrary"),
        name=f"proj_{prev}_{nxt}",
    )(*args)
    return outs


def _prep_even(w_in, b_f, w_lr, b_lr):
    fw = ATT_HEADS * HEAD_DIM
    kw = REC_HEADS * HEAD_DIM
    vw = REC_HEADS * VAL_DIM
    sizes = (fw, fw, fw, ATT_HEADS, kw, kw, vw, GLA_RANK, fw + vw)
    offs = np.cumsum((0,) + sizes)
    fq, fk, fv, ff, gq, gk, gv, glr, gate = (w_in[..., offs[i]:offs[i + 1]] for i in range(9))
    scale = HEAD_DIM ** -0.5
    L, D = w_in.shape[:2]

    def pad_heads(w, width):
        w = w.reshape(L, D, ATT_HEADS, HEAD_DIM)
        w = jnp.pad(w, ((0, 0), (0, 0), (0, 0), (0, width - HEAD_DIM)))
        return w.reshape(L, D, ATT_HEADS * width)

    w_main = jnp.concatenate([gate, pad_heads(fk, FOX_PAD), gv, gq * scale, gk], axis=-1).astype(BF16)
    wq = (fq * (scale * LOG2E)).astype(BF16)
    wv_t = jnp.swapaxes(pad_heads(fv, FOX_VROWS), 1, 2).astype(BF16)
    pad = LANES - ATT_HEADS - GLA_RANK
    w_small = jnp.pad(jnp.concatenate([ff, glr], axis=-1), ((0, 0), (0, 0), (0, pad))).astype(BF16)
    b_f_pad = jnp.pad(b_f, ((0, 0), (0, LANES - ATT_HEADS))).reshape(L, 1, LANES)
    w_lr_pad = jnp.pad(w_lr, ((0, 0), (ATT_HEADS, pad), (0, 0))).astype(BF16)
    q_aug = np.zeros((FOX_PAD - HEAD_DIM, TOKEN_TILE), np.float32)
    q_aug[:3, :] = -1.0
    v_aug = np.zeros((ATT_HEADS, FOX_VROWS, LANES), np.float32)
    v_aug[:, HEAD_DIM, :] = 1.0
    place = np.zeros((LANES, ATT_HEADS * FOX_PAD), np.float32)
    for i in range(3):
        for h in range(ATT_HEADS):
            place[ATT_HEADS * i + h, h * FOX_PAD + HEAD_DIM + i] = 1.0
    layered = (w_main, wq, wv_t, w_small, b_f_pad, w_lr_pad, b_lr.reshape(L, 1, kw))
    shared = (jnp.asarray(place, BF16), jnp.asarray(q_aug), jnp.asarray(v_aug.reshape(-1, LANES)))
    return layered, shared


def _prep_odd(w_in):
    kw = REC_HEADS * HEAD_DIM
    vw = REC_HEADS * VAL_DIM
    dw = ATT_HEADS * HEAD_DIM
    sizes = (kw, kw, vw, dw, dw, dw, vw + dw)
    offs = np.cumsum((0,) + sizes)
    rq, rk, rv, dq, dk, dv, gate = (w_in[..., offs[i]:offs[i + 1]] for i in range(7))
    scale = HEAD_DIM ** -0.5
    return jnp.concatenate([gate, rv, rq * scale, rk, dq * (scale * LOG2E), dk, dv], axis=-1).astype(BF16)


def _rope_tables(S):
    inv = jnp.power(ROPE_THETA, -jnp.arange(0, HEAD_DIM, 2, dtype=F32) / HEAD_DIM)
    ang = jnp.arange(S, dtype=F32)[:, None] * inv[None, :]
    cos, sin = jnp.cos(ang), jnp.sin(ang)
    reps = LANES // HEAD_DIM
    cos_t = jnp.tile(jnp.concatenate([cos, cos], axis=1), (1, reps))
    sin_t = jnp.tile(jnp.concatenate([-sin, sin], axis=1), (1, reps))
    return cos_t, sin_t


def _tile(S, pref):
    return min(S, pref)


def _even_mixers(z, q_t, v_t, glog, gla_g):
    S = z.shape[1]
    o_a = _fox_attention(z, q_t, v_t, _tile(S, FOX_BLOCK))
    o_b = _gla(z, glog, gla_g.reshape(1, -1), _tile(S, SCAN_TILE))
    return o_a, o_b


def _odd_mixers(z, zd4, zd16, gn_w, gn_b):
    B, S, _ = z.shape
    o_c = _retention(z, gn_w.reshape(1, -1), gn_b.reshape(1, -1), _tile(S, SCAN_TILE))
    sources = {1: (z.reshape(B, 1, S, Z_WIDTH), DIL_TILE0), 4: (zd4, 0), 16: (zd16, 0)}
    outs, lses = zip(*[_dilated_branch(*sources[dil], window, dil, DIL_QBLOCKS)
                       for window, dil in DIL_PATTERNS])
    outs = (outs[0].reshape(B, S, COL_TILE), *outs[1:])
    lses = (lses[0].reshape(B, S, LANES), *lses[1:])
    return o_c, outs, lses


def kernel(x, norm_even, w_in_even, b_f_even, w_lr_even, b_lr_even, gla_norm_even, w_out_even,
           norm_odd, w_in_odd, ret_gn_w_odd, ret_gn_b_odd, w_out_odd, final_norm):
    depth = norm_even.shape[0] + norm_odd.shape[0]
    B, S, D = x.shape
    tm = _tile(S, TOKEN_TILE)
    cos, sin = _rope_tables(S)
    even_layered, even_shared = _prep_even(w_in_even, b_f_even, w_lr_even, b_lr_even)
    even_layered = (norm_even.reshape(-1, 1, D), *even_layered)
    odd_w = _prep_odd(w_in_odd)
    odd_g = norm_odd.reshape(-1, 1, D)
    w_out = {"even": w_out_even.astype(BF16), "odd": w_out_odd.astype(BF16)}

    def in_args(i):
        if i % 2 == 0:
            return "even", (i // 2, even_layered, even_shared)
        return "odd", (i // 2, odd_g, odd_w, cos, sin)

    h = x
    kind, args = in_args(0)
    proj = _proj_call(h, None, None, kind, args, None, tm)
    for i in range(depth):
        j = i // 2
        z = proj[0]
        if kind == "even":
            o_a, o_b = _even_mixers(*proj, gla_norm_even[j])
            prev_args = (o_a, o_b, z, w_out[kind], j)
        else:
            o_c, outs, lses = _odd_mixers(*proj, ret_gn_w_odd[j], ret_gn_b_odd[j])
            prev_args = (o_c, outs, lses, z, w_out[kind], j)
        nxt, nxt_args = in_args(i + 1) if i + 1 < depth else (None, None)
        tm_call = tm if nxt else _tile(S, FINAL_TILE)
        h, *proj = _proj_call(h, kind, prev_args, nxt, nxt_args, final_norm.reshape(1, D), tm_call)
        kind = nxt
    return h
```

```python
import functools
import math

import numpy as np
import jax
import jax.numpy as jnp
from jax import lax
from jax.experimental import pallas as pl
from jax.experimental.pallas import tpu as pltpu

F32 = jnp.float32
BF16 = jnp.bfloat16

EPS = 1e-6
ROPE_THETA = 10000.0
HEAD_DIM = 64
VAL_DIM = 128
ATT_HEADS = 8
REC_HEADS = 4
GLA_RANK = 16
GLA_TAU = 16.0
GLA_CHUNK = 64
RET_CHUNK = 256
DIL_BLOCK = 128
DIL_PATTERNS = ((128, 1), (512, 4), (2048, 16))

Z_WIDTH = 3584
DIL_TILE0 = 4
DIL_QBLOCKS = 8
DIL_VROWS = 144
ZE_WIDTH = 3072
FOX_PAD = 128
FOX_VROWS = 80
LOG2E = 1.4426950408889634
COL_TILE = 512
TOKEN_TILE = 512
SCAN_TILE = 1024
FINAL_TILE = 1024
FOX_BLOCK = 256
LANES = 128
V7X_VMEM_LIMIT = 56 * 1024 * 1024

NT_DIMS = (((1,), (1,)), ((), ()))
TN_DIMS = (((0,), (0,)), ((), ()))


def _params(*sem):
    return pltpu.CompilerParams(dimension_semantics=sem, vmem_limit_bytes=V7X_VMEM_LIMIT)


def _split2(x):
    hi = x.astype(BF16)
    lo = (x - hi.astype(F32)).astype(BF16)
    return hi, lo


def _split3(x):
    hi, lo = _split2(x)
    lo2 = (x - hi.astype(F32) - lo.astype(F32)).astype(BF16)
    return hi, lo, lo2


def _log_sigmoid(x):
    return jnp.minimum(x, 0.0) - jnp.log(1.0 + jnp.exp(-jnp.abs(x)))


def _rms_normed(x, g):
    ms = jnp.mean(x * x, axis=-1, keepdims=True)
    return x * lax.rsqrt(ms + EPS) * g


def _in_even_body(x, g_ref, w_ref, wq_ref, wv_ref, ws_ref, bf_ref, wlr_ref, blr_ref, place_ref,
                  qaug_ref, vaug_ref, z_ref, qt_ref, vt_ref, glog_ref, u_ref, carry_ref):
    tm = x.shape[0]

    @pl.when(pl.program_id(1) == 0)
    def _():
        carry_ref[...] = jnp.zeros_like(carry_ref)

    u_ref[...] = _rms_normed(x, g_ref[...]).astype(BF16)
    u = u_ref[...]

    zs = jnp.dot(u, ws_ref[...], preferred_element_type=F32)
    logf = _log_sigmoid(zs + bf_ref[...])
    row = lax.broadcasted_iota(jnp.int32, (LANES, LANES), 0)
    col = lax.broadcasted_iota(jnp.int32, (LANES, LANES), 1)
    tril = jnp.where(row >= col, 1.0, 0.0).astype(BF16)
    carry = carry_ref[...]
    blocks = []
    for r0 in range(0, tm, LANES):
        blk_sum = sum(jnp.dot(tril, t[r0:r0 + LANES, :], preferred_element_type=F32)
                      for t in _split3(logf)) + carry
        carry = blk_sum[LANES - 1:LANES, :]
        blocks.append(blk_sum)
    csum = jnp.concatenate(blocks, axis=0)
    carry_ref[...] = carry
    hi, lo, lo2 = (t.astype(F32) for t in _split3(csum * LOG2E))
    lane = lax.broadcasted_iota(jnp.int32, (tm, LANES), 1)
    packed = jnp.where(lane < ATT_HEADS, hi,
                       jnp.where(lane < 2 * ATT_HEADS, pltpu.roll(lo, ATT_HEADS, 1),
                                 jnp.where(lane < 3 * ATT_HEADS, pltpu.roll(lo2, 2 * ATT_HEADS, 1), 0.0)))
    c_cols = jnp.dot(packed.astype(BF16), place_ref[...], preferred_element_type=F32)

    glr = jnp.dot(zs.astype(BF16), wlr_ref[...], preferred_element_type=F32)
    glog_ref[0] = _log_sigmoid(glr + blr_ref[...]) * (1.0 / GLA_TAU)

    for j in range(ZE_WIDTH // COL_TILE):
        cols = slice(j * COL_TILE, (j + 1) * COL_TILE)
        r = jnp.dot(u, w_ref[:, cols], preferred_element_type=F32)
        if j in (2, 3):
            r = r + c_cols[:, (j - 2) * COL_TILE:(j - 1) * COL_TILE]
        z_ref[0, :, cols] = r.astype(BF16)

    def feature_major(w_t_ref, aug_ref, out_ref, tile):
        for j in range(w_t_ref.shape[0] // tile):
            rows = slice(j * tile, (j + 1) * tile)
            r = lax.dot_general(w_t_ref[rows, :], u, NT_DIMS, preferred_element_type=F32)
            aug = aug_ref[rows, :]
            for c in range(tm // LANES):
                lanes = slice(c * LANES, (c + 1) * LANES)
                out_ref[0, rows, lanes] = (r[:, lanes] + aug).astype(BF16)

    feature_major(wv_ref, vaug_ref, vt_ref, ATT_HEADS * FOX_VROWS // 2)
    qtok = jnp.dot(u, wq_ref[...], preferred_element_type=F32)
    aug = qaug_ref[...].astype(BF16)
    for g in range(ATT_HEADS // 2):
        q_t = qtok[:, g * LANES:(g + 1) * LANES].T
        for half in range(2):
            r0 = (2 * g + half) * FOX_PAD
            qt_ref[0, r0:r0 + HEAD_DIM, :] = q_t[half * HEAD_DIM:(half + 1) * HEAD_DIM, :].astype(BF16)
            qt_ref[0, r0 + HEAD_DIM:r0 + FOX_PAD, :] = aug


def _rope_tile(x, cos, sin):
    lane = lax.broadcasted_iota(jnp.int32, (x.shape[0], LANES), 1)
    first_half = (lane % HEAD_DIM) < (HEAD_DIM // 2)
    outs = []
    for c in range(x.shape[1] // LANES):
        xc = x[:, c * LANES:(c + 1) * LANES]
        partner = jnp.where(first_half,
                            pltpu.roll(xc, LANES - HEAD_DIM // 2, 1),
                            pltpu.roll(xc, HEAD_DIM // 2, 1))
        outs.append(xc * cos + partner * sin)
    return jnp.concatenate(outs, axis=-1)


def _in_odd_body(x, g_ref, w_ref, cos_ref, sin_ref, z_ref, zd4_ref, zd16_ref, u_ref, r_ref):
    tm = x.shape[0]
    u_ref[...] = _rms_normed(x, g_ref[...]).astype(BF16)
    u = u_ref[...]
    cos = cos_ref[...]
    sin = sin_ref[...]
    for j in range(Z_WIDTH // COL_TILE):
        cols = slice(j * COL_TILE, (j + 1) * COL_TILE)
        r = jnp.dot(u, w_ref[:, cols], preferred_element_type=F32)
        if j in (3, 4, 5):
            r = _rope_tile(r, cos, sin)
        z_ref[0, :, cols] = r.astype(BF16)
        if j >= DIL_TILE0:
            t = j - DIL_TILE0
            for c in range(COL_TILE // LANES):
                slot = t * (COL_TILE // LANES) + c
                r_ref[slot] = r[:, c * LANES:(c + 1) * LANES]
                dcols = slice(t * COL_TILE + c * LANES, t * COL_TILE + (c + 1) * LANES)
                for dil, ref in ((4, zd4_ref), (16, zd16_ref)):
                    for res in range(dil):
                        ref[0, res, :, dcols] = (
                            r_ref[slot, pl.ds(res, tm // dil, stride=dil), :].astype(BF16))


def _fox_kernel(qt_ref, kp_ref, vt_ref, o_ref, s_ref, acc_ref, mp_ref, mc_ref, ot_ref, mask_ref, *, tq, tk):
    i = pl.program_id(1)
    heads = range(ATT_HEADS)
    rows = lambda h: slice(h * FOX_PAD, (h + 1) * FOX_PAD)
    vrows = lambda h: slice(h * FOX_VROWS, (h + 1) * FOX_VROWS)

    @pl.when(i == 0)
    def _():
        key = lax.broadcasted_iota(jnp.int32, (tk, tq), 0)
        qry = lax.broadcasted_iota(jnp.int32, (tk, tq), 1)
        mask_ref[0] = jnp.zeros((tk, tq), F32)
        mask_ref[1] = jnp.where(key <= qry, 0.0, -jnp.inf)
        mask_ref[2] = jnp.where(key + tk <= qry, 0.0, -jnp.inf)

    def score_stage(h, j):
        ks = pl.multiple_of(j * tk, tk)
        s = jnp.dot(kp_ref[0, pl.ds(ks, tk), rows(h)], qt_ref[0, rows(h), :],
                    preferred_element_type=F32)
        s = s + mask_ref[jnp.clip(j - 2 * i + 1, 0, 2)]
        s_ref[h] = s
        m_old = mc_ref[h]
        mp_ref[h] = m_old
        mc_ref[h] = jnp.maximum(m_old, jnp.max(s, axis=0, keepdims=True))

    def value_stage(h, j):
        ks = pl.multiple_of(j * tk, tk)
        m_new = mc_ref[h]
        alpha = jnp.exp2(mp_ref[h] - m_new)
        p = jnp.exp2((s_ref[h] - m_new).astype(BF16))
        pv = jnp.dot(vt_ref[0, vrows(h), pl.ds(ks, tk)], p, preferred_element_type=F32)
        acc_ref[h] = alpha * acc_ref[h] + pv

    def trip(j):
        for h in heads:
            value_stage(h, j - 1)
            score_stage(h, j)

    for h in heads:
        acc_ref[h] = jnp.zeros(acc_ref.shape[1:], F32)
        mc_ref[h] = jnp.full((1, tq), -jnp.inf, F32)
        score_stage(h, 0)

    trip(1)

    def pair(t, _):
        j = 2 + 2 * t
        trip(j)
        trip(j + 1)
        return 0

    lax.fori_loop(0, i, pair, 0)

    for h in heads:
        value_stage(h, 2 * i + 1)
        acc = acc_ref[h]
        ot_ref[h * HEAD_DIM:(h + 1) * HEAD_DIM, :] = (
            acc[:HEAD_DIM, :] / acc[HEAD_DIM:HEAD_DIM + 1, :])
    o_ref[0] = ot_ref[...].T.astype(BF16)


def _fox_attention(z, q_t, v_t, tk):
    B, S, _ = z.shape
    tq = 2 * tk
    fox_rows = ATT_HEADS * FOX_PAD
    val_rows = ATT_HEADS * FOX_VROWS
    return pl.pallas_call(
        functools.partial(_fox_kernel, tq=tq, tk=tk),
        grid=(B, S // tq),
        in_specs=[
            pl.BlockSpec((1, fox_rows, tq), lambda b, i: (b, 0, i)),
            pl.BlockSpec((1, S, fox_rows), lambda b, i: (b, 0, 1)),
            pl.BlockSpec((1, val_rows, S), lambda b, i: (b, 0, 0)),
        ],
        out_specs=pl.BlockSpec((1, tq, COL_TILE), lambda b, i: (b, i, 0)),
        out_shape=jax.ShapeDtypeStruct((B, S, COL_TILE), BF16),
        scratch_shapes=[pltpu.VMEM((ATT_HEADS, tk, tq), F32),
                        pltpu.VMEM((ATT_HEADS, FOX_VROWS, tq), F32),
                        pltpu.VMEM((ATT_HEADS, 1, tq), F32),
                        pltpu.VMEM((ATT_HEADS, 1, tq), F32),
                        pltpu.VMEM((ATT_HEADS * HEAD_DIM, tq), F32),
                        pltpu.VMEM((3, tk, tq), F32)],
        compiler_params=_params("arbitrary", "arbitrary"),
        name="fox_attention",
    )(q_t, z, v_t)


def _gla_kernel(q_ref, k_ref, v_ref, g_ref, gn_ref, o_ref,
                qhat_ref, top_ref, bot_ref, kinc_ref, dec_ref, dect_ref, state_ref):
    tg = q_ref.shape[1]
    C = GLA_CHUNK
    blk = 2 * C
    n_chunks = tg // C
    pairs = REC_HEADS // 2

    @pl.when(pl.program_id(1) == 0)
    def _():
        state_ref[...] = jnp.zeros_like(state_ref)

    row = lax.broadcasted_iota(jnp.int32, (blk, blk), 0)
    col = lax.broadcasted_iota(jnp.int32, (blk, blk), 1)
    chunk_start = row & (-C)
    tril = jnp.where(col <= row, jnp.where(col >= chunk_start, 1.0, 0.0), 0.0).astype(BF16)
    ghi, glo = _split2(g_ref[0])
    b = jnp.concatenate(
        [jnp.dot(tril, ghi[r0:r0 + blk, :], preferred_element_type=F32)
         + jnp.dot(tril, glo[r0:r0 + blk, :], preferred_element_type=F32)
         for r0 in range(0, tg, blk)], axis=0)
    q = q_ref[0].astype(F32)
    k = k_ref[0].astype(F32)
    tok = lax.broadcasted_iota(jnp.int32, (tg, LANES), 0)
    lane = lax.broadcasted_iota(jnp.int32, (tg, LANES), 1)
    in_a = (tok & C) == 0
    first = lane < HEAD_DIM
    qt = q * jnp.exp(b)
    kt = k * jnp.exp(-b)
    top_ref[...] = kt
    dec_ref[...] = jnp.zeros_like(dec_ref)
    kd_rows, dec_rows = [], []
    for c in range(n_chunks):
        rows = slice(c * C, (c + 1) * C)
        b_last = b[(c + 1) * C - 1:(c + 1) * C, :]
        kd_rows.append(k[rows, :] * jnp.exp(b_last - b[rows, :]))
        dec_rows.append(jnp.exp(b_last))
        dec_ref[c:c + 1, :] = dec_rows[-1]
    for c in range(0, n_chunks, 2):
        bot_ref[c * C:(c + 1) * C, :] = kd_rows[c]
        bot_ref[(c + 1) * C:(c + 2) * C, :] = kt[(c + 1) * C:(c + 2) * C, :]
        kinc_ref[c * C:(c + 1) * C, :] = kd_rows[c] * dec_rows[c + 1]
        kinc_ref[(c + 1) * C:(c + 2) * C, :] = kd_rows[c + 1]
    dect_ref[...] = dec_ref[...].T
    for half in range(2):
        for g in range(pairs):
            gs = slice(g * LANES, (g + 1) * LANES)
            qh = jnp.where(first, qt[:, gs], 0.0) if half == 0 else jnp.where(first, 0.0, qt[:, gs])
            qhat_ref[half, :, 2 * g * LANES:(2 * g + 1) * LANES] = jnp.where(in_a, qh, 0.0).astype(BF16)
            qhat_ref[half, :, (2 * g + 1) * LANES:(2 * g + 2) * LANES] = jnp.where(in_a, 0.0, qh).astype(BF16)

    rb = lax.broadcasted_iota(jnp.int32, (blk, blk), 0)
    cb = lax.broadcasted_iota(jnp.int32, (blk, blk), 1)
    lower = cb <= rb
    low_rows = rb < HEAD_DIM

    for n in range(tg // blk):
        rows = slice(n * blk, (n + 1) * blk)
        for g in range(pairs):
            gs = slice(g * LANES, (g + 1) * LANES)
            top_t = top_ref[rows, gs].T.astype(BF16)
            bot_t = bot_ref[rows, gs].T.astype(BF16)
            kinc_t = kinc_ref[rows, gs].T.astype(BF16)
            st = state_ref[g]
            dec_a = dect_ref[gs, 2 * n:2 * n + 1]
            dec_b = dect_ref[gs, 2 * n + 1:2 * n + 2]
            rhs = jnp.concatenate([jnp.concatenate([top_t, st.astype(BF16)], axis=1),
                                   jnp.concatenate([bot_t, (st * dec_a).astype(BF16)], axis=1)], axis=0)
            inc = []
            for half in range(2):
                h = 2 * g + half
                vs = slice(h * VAL_DIM, (h + 1) * VAL_DIM)
                v = v_ref[0, rows, vs]
                sc = jnp.dot(qhat_ref[half, rows, 2 * g * LANES:(2 * g + 2) * LANES], rhs,
                             preferred_element_type=F32)
                att = jnp.where(lower, sc[:, :blk], 0.0).astype(BF16)
                both = jnp.dot(jnp.concatenate([att, kinc_t], axis=0), v, preferred_element_type=F32)
                o = both[:blk, :] + sc[:, blk:]
                inc.append(both[blk:, :])
                ms = jnp.mean(o * o, axis=-1, keepdims=True)
                o_ref[0, rows, vs] = (o * lax.rsqrt(ms + EPS) * gn_ref[:, vs]).astype(BF16)
            state_ref[g] = st * (dec_a * dec_b) + jnp.where(low_rows, inc[0], inc[1])


def _gla(z, glog, gla_g, tg):
    B, S, _ = z.shape
    kw = REC_HEADS * HEAD_DIM
    vw = REC_HEADS * VAL_DIM
    return pl.pallas_call(
        _gla_kernel,
        grid=(B, S // tg),
        in_specs=[
            pl.BlockSpec((1, tg, kw), lambda b, s: (b, s, 10)),
            pl.BlockSpec((1, tg, kw), lambda b, s: (b, s, 11)),
            pl.BlockSpec((1, tg, vw), lambda b, s: (b, s, 4)),
            pl.BlockSpec((1, tg, kw), lambda b, s: (b, s, 0)),
            pl.BlockSpec((1, vw), lambda b, s: (0, 0)),
        ],
        out_specs=pl.BlockSpec((1, tg, vw), lambda b, s: (b, s, 0)),
        out_shape=jax.ShapeDtypeStruct((B, S, vw), BF16),
        scratch_shapes=[
            pltpu.VMEM((2, tg, 2 * kw), BF16),
            pltpu.VMEM((tg, kw), F32), pltpu.VMEM((tg, kw), F32), pltpu.VMEM((tg, kw), F32),
            pltpu.VMEM((LANES, kw), F32), pltpu.VMEM((kw, LANES), F32),
            pltpu.VMEM((REC_HEADS // 2, 2 * HEAD_DIM, VAL_DIM), F32),
        ],
        compiler_params=_params("arbitrary", "arbitrary"),
        name="gla",
    )(z, z, z, glog, gla_g)


def _ret_kernel(q_ref, k_ref, v_ref, gw_ref, gb_ref, o_ref, state_ref, dmat_ref, zeta_ref):
    tg = q_ref.shape[1]
    C = RET_CHUNK
    n_chunks = tg // C
    pairs = REC_HEADS // 2
    log_gamma = [math.log(1.0 - 2.0 ** (-5.0 - h)) for h in range(REC_HEADS)]

    @pl.when(pl.program_id(1) == 0)
    def _():
        state_ref[...] = jnp.zeros_like(state_ref)

    ri = lax.broadcasted_iota(jnp.int32, (C, C), 0)
    ci = lax.broadcasted_iota(jnp.int32, (C, C), 1)
    diff = (ri - ci).astype(F32)
    idx = lax.broadcasted_iota(jnp.int32, (C, 1), 0).astype(F32)
    low_lanes = lax.broadcasted_iota(jnp.int32, (C, LANES), 1) < HEAD_DIM
    low_rows = lax.broadcasted_iota(jnp.int32, (LANES, VAL_DIM), 0) < HEAD_DIM
    for h in range(REC_HEADS):
        dmat_ref[h] = jnp.where(diff >= 0, jnp.exp(jnp.maximum(diff, 0.0) * log_gamma[h]), 0.0)
    for g in range(pairs):
        zeta_ref[g] = jnp.where(low_lanes, jnp.exp((C - 1.0 - idx) * log_gamma[2 * g]),
                                jnp.exp((C - 1.0 - idx) * log_gamma[2 * g + 1]))

    for c in range(n_chunks):
        rows = slice(c * C, (c + 1) * C)
        for g in range(pairs):
            gs = slice(g * LANES, (g + 1) * LANES)
            q_pair = q_ref[0, rows, gs]
            k_f32 = k_ref[0, rows, gs].astype(F32)
            k_t = k_f32.T.astype(BF16)
            kz_t = (k_f32 * zeta_ref[g]).T.astype(BF16)
            st = state_ref[g]
            rhs = jnp.concatenate([k_t, st.astype(BF16)], axis=1)
            zero = jnp.zeros_like(q_pair)
            inc = []
            for half in range(2):
                h = 2 * g + half
                vs = slice(h * VAL_DIM, (h + 1) * VAL_DIM)
                xi = jnp.exp((idx + 1.0) * log_gamma[h])
                q = jnp.where(low_lanes, q_pair, zero) if half == 0 else jnp.where(low_lanes, zero, q_pair)
                v = v_ref[0, rows, vs]
                sc = jnp.dot(q, rhs, preferred_element_type=F32)
                att = (sc[:, :C] * dmat_ref[h]).astype(BF16)
                both = jnp.dot(jnp.concatenate([att, kz_t], axis=0), v, preferred_element_type=F32)
                o = both[:C, :] + xi * sc[:, C:]
                inc.append(both[C:, :])
                o = o - jnp.mean(o, axis=-1, keepdims=True)
                ms = jnp.mean(o * o, axis=-1, keepdims=True)
                o = o * lax.rsqrt(ms + EPS) * gw_ref[:, vs] + gb_ref[:, vs]
                o_ref[0, rows, vs] = o.astype(BF16)
            decay = jnp.where(low_rows, math.exp(C * log_gamma[2 * g]), math.exp(C * log_gamma[2 * g + 1]))
            state_ref[g] = st * decay + jnp.where(low_rows, inc[0], inc[1])


def _retention(z, gn_w, gn_b, tg):
    B, S, _ = z.shape
    kw = REC_HEADS * HEAD_DIM
    vw = REC_HEADS * VAL_DIM
    return pl.pallas_call(
        _ret_kernel,
        grid=(B, S // tg),
        in_specs=[
            pl.BlockSpec((1, tg, kw), lambda b, s: (b, s, 6)),
            pl.BlockSpec((1, tg, kw), lambda b, s: (b, s, 7)),
            pl.BlockSpec((1, tg, vw), lambda b, s: (b, s, 2)),
            pl.BlockSpec((1, vw), lambda b, s: (0, 0)),
            pl.BlockSpec((1, vw), lambda b, s: (0, 0)),
        ],
        out_specs=pl.BlockSpec((1, tg, vw), lambda b, s: (b, s, 0)),
        out_shape=jax.ShapeDtypeStruct((B, S, vw), BF16),
        scratch_shapes=[pltpu.VMEM((REC_HEADS // 2, 2 * HEAD_DIM, VAL_DIM), F32),
                        pltpu.VMEM((REC_HEADS, RET_CHUNK, RET_CHUNK), F32),
                        pltpu.VMEM((REC_HEADS // 2, RET_CHUNK, LANES), F32)],
        compiler_params=_params("arbitrary", "arbitrary"),
        name="retention",
    )(z, z, z, gn_w, gn_b)


def _dil_kernel(q_ref, k_ref, v_ref, o_ref, lse_ref,
                vt_ref, qt_ref, s_ref, mx_ref, bias_ref, ot_ref, lt_ref, *, blk, nblk, ncls, span):
    n = pl.program_id(2)
    m = k_ref.shape[2]
    pairs = ATT_HEADS // 2
    chains = [(g, half) for g in range(pairs) for half in range(2)]
    units = [(rc, t) for rc in range(ncls) for t in range(nblk)]
    feat = lax.broadcasted_iota(jnp.int32, (LANES, blk), 0)

    @pl.when(n == 0)
    def _():
        for rc in range(ncls):
            for g in range(pairs):
                vt_ref[rc, g * DIL_VROWS + LANES:(g + 1) * DIL_VROWS, :] = jnp.ones(
                    (DIL_VROWS - LANES, m), BF16)

        def build(kb, _):
            r0 = pl.multiple_of(kb * blk, blk)
            for rc in range(ncls):
                for g in range(pairs):
                    v = v_ref[0, rc, pl.ds(r0, blk), g * LANES:(g + 1) * LANES].astype(F32)
                    vt_ref[rc, g * DIL_VROWS:g * DIL_VROWS + LANES, pl.ds(r0, blk)] = v.T.astype(BF16)
            return 0

        lax.fori_loop(0, m // blk, build, 0)

    single_step = m == nblk * blk
    starts, wins = [], []
    for t in range(nblk):
        qb = n * nblk + t
        first = single_step and t == 0
        win = blk if first else 2 * blk
        start = 0 if first else pl.multiple_of(jnp.maximum(qb - 1, 0) * blk, blk)
        starts.append(start)
        wins.append(win)
        ki_w = lax.broadcasted_iota(jnp.int32, (win, blk), 0)
        qi_w = lax.broadcasted_iota(jnp.int32, (win, blk), 1)
        rel = qb * blk + qi_w - start - ki_w
        bias_ref[t, :win] = jnp.where(rel >= 0, jnp.where(rel <= span, 0.0, -jnp.inf), -jnp.inf)
    for u, (rc, t) in enumerate(units):
        for g in range(pairs):
            q_t = q_ref[0, rc, t * blk:(t + 1) * blk, g * LANES:(g + 1) * LANES].astype(F32).T
            qt_ref[u, 2 * g] = jnp.where(feat < HEAD_DIM, q_t, 0.0).astype(BF16)
            qt_ref[u, 2 * g + 1] = jnp.where(feat < HEAD_DIM, 0.0, q_t).astype(BF16)

    for u, (rc, t) in enumerate(units):
        for c, (g, half) in enumerate(chains):
            k = k_ref[0, rc, pl.ds(starts[t], wins[t]), g * LANES:(g + 1) * LANES]
            s = jnp.dot(k, qt_ref[u, c], preferred_element_type=F32) + bias_ref[t, :wins[t]]
            s_ref[u, c, :wins[t]] = s
            mx_ref[u, c] = jnp.max(s, axis=0, keepdims=True)

    for u, (rc, t) in enumerate(units):
        lt_ref[u] = jnp.zeros(lt_ref.shape[1:], F32)
        for c, (g, half) in enumerate(chains):
            mx = mx_ref[u, c]
            p = jnp.exp2((s_ref[u, c, :wins[t]] - mx).astype(BF16))
            acc = jnp.dot(vt_ref[rc, g * DIL_VROWS:(g + 1) * DIL_VROWS, pl.ds(starts[t], wins[t])], p,
                          preferred_element_type=F32)
            den = acc[LANES:LANES + 1, :]
            ot_ref[u, c * HEAD_DIM:(c + 1) * HEAD_DIM, :] = (
                acc[half * HEAD_DIM:(half + 1) * HEAD_DIM, :] * (1.0 / den))
            lt_ref[u, c:c + 1, :] = (mx + jnp.log2(den)) * (1.0 / LOG2E)
        qrows = slice(t * blk, (t + 1) * blk)
        o_ref[0, rc, qrows, :] = ot_ref[u].T.astype(BF16)
        lse_ref[0, rc, qrows, :] = lt_ref[u].T


def _dilated_branch(src, tile0, window, dil, units):
    B, _, m, _ = src.shape
    blk = DIL_BLOCK
    nblk = min(units, m // blk)
    ncls = min(dil, units // nblk)
    tq = nblk * blk
    n_units = ncls * nblk
    return pl.pallas_call(
        functools.partial(_dil_kernel, blk=blk, nblk=nblk, ncls=ncls, span=window // dil),
        grid=(B, dil // ncls, m // tq),
        in_specs=[
            pl.BlockSpec((1, ncls, tq, COL_TILE), lambda b, r, n: (b, r, n, tile0)),
            pl.BlockSpec((1, ncls, m, COL_TILE), lambda b, r, n: (b, r, 0, tile0 + 1)),
            pl.BlockSpec((1, ncls, m, COL_TILE), lambda b, r, n: (b, r, 0, tile0 + 2)),
        ],
        out_specs=[
            pl.BlockSpec((1, ncls, tq, COL_TILE), lambda b, r, n: (b, r, n, 0)),
            pl.BlockSpec((1, ncls, tq, LANES), lambda b, r, n: (b, r, n, 0)),
        ],
        out_shape=[
            jax.ShapeDtypeStruct((B, dil, m, COL_TILE), BF16),
            jax.ShapeDtypeStruct((B, dil, m, LANES), F32),
        ],
        scratch_shapes=[pltpu.VMEM((ncls, ATT_HEADS // 2 * DIL_VROWS, m), BF16),
                        pltpu.VMEM((n_units, ATT_HEADS, LANES, blk), BF16),
                        pltpu.VMEM((n_units, ATT_HEADS, 2 * blk, blk), F32),
                        pltpu.VMEM((n_units, ATT_HEADS, 1, blk), F32),
                        pltpu.VMEM((nblk, 2 * blk, blk), F32),
                        pltpu.VMEM((n_units, ATT_HEADS * HEAD_DIM, blk), F32),
                        pltpu.VMEM((n_units, LANES, blk), F32)],
        compiler_params=_params("arbitrary", "arbitrary", "arbitrary"),
        name=f"dilated_{dil}",
    )(src, src, src)


def _silu(x):
    return x / (1.0 + jnp.exp(-x))


def _mix_out(mix_a, mix_b, gate_ref, h_ref, w_ref):
    half = mix_a.shape[1]
    gate = _silu(gate_ref[0].astype(F32))
    a = (mix_a * gate[:, :half]).astype(BF16)
    b = (mix_b * gate[:, half:]).astype(BF16)
    y = (jnp.dot(a, w_ref[:half, :], preferred_element_type=F32)
         + jnp.dot(b, w_ref[half:, :], preferred_element_type=F32))
    return h_ref[0] + y


def _dilated_mix(d1_ref, d4_ref, d16_ref, l1_ref, l4_ref, l16_ref, e_ref,
                 o4_ref, o16_ref, ls4_ref, ls16_ref):
    tm = d1_ref.shape[1]
    chunks = d1_ref.shape[2] // LANES
    for dil, src, lsrc, dst, ldst in ((4, d4_ref, l4_ref, o4_ref, ls4_ref),
                                      (16, d16_ref, l16_ref, o16_ref, ls16_ref)):
        for res in range(dil):
            token_rows = pl.ds(res, tm // dil, stride=dil)
            ldst[token_rows, :] = lsrc[0, res]
            for c in range(chunks):
                dst[c, token_rows, :] = src[0, res, :, c * LANES:(c + 1) * LANES].astype(F32)
    l1, l2, l3 = l1_ref[0], ls4_ref[...], ls16_ref[...]
    mx = jnp.maximum(jnp.maximum(l1, l2), l3)
    e = e_ref[...]
    expand = lambda w: jnp.dot(w.astype(BF16), e, preferred_element_type=F32)
    x1, x2, x3 = expand(jnp.exp(l1 - mx)), expand(jnp.exp(l2 - mx)), expand(jnp.exp(l3 - mx))
    o4 = jnp.concatenate([o4_ref[c] for c in range(chunks)], axis=-1)
    o16 = jnp.concatenate([o16_ref[c] for c in range(chunks)], axis=-1)
    return (x1 * d1_ref[0].astype(F32) + x2 * o4 + x3 * o16) / (x1 + x2 + x3)


N_IN = {"even": 11, "odd": 4}
N_OUT = {"even": 4, "odd": 3}
N_SCRATCH = {"even": 2, "odd": 2}


def _proj_kernel(*refs, prev, nxt, final):
    refs = list(refs)
    take = lambda n: [refs.pop(0) for _ in range(n)]
    if prev is None:
        (x_ref,) = take(1)
    elif prev == "even":
        oa_ref, ob_ref, gate_ref, h_ref, wout_ref = take(5)
    else:
        oc_ref, d1_ref, d4_ref, d16_ref, l1_ref, l4_ref, l16_ref, e_ref, gate_ref, h_ref, wout_ref = take(11)
    if final:
        (fg_ref,) = take(1)
    in_params = take(N_IN[nxt]) if nxt else []
    if prev is not None:
        (hout_ref,) = take(1)
    in_outs = take(N_OUT[nxt]) if nxt else []
    mix_scratch = take(4) if prev == "odd" else []
    in_scratch = take(N_SCRATCH[nxt]) if nxt else []

    if prev is None:
        x = x_ref[0]
    elif prev == "even":
        x = _mix_out(oa_ref[0].astype(F32), ob_ref[0].astype(F32), gate_ref, h_ref, wout_ref)
    else:
        o_d = _dilated_mix(d1_ref, d4_ref, d16_ref, l1_ref, l4_ref, l16_ref, e_ref, *mix_scratch)
        x = _mix_out(oc_ref[0].astype(F32), o_d, gate_ref, h_ref, wout_ref)
    if final:
        x = _rms_normed(x, fg_ref[...])
    if prev is not None:
        hout_ref[0] = x
    if nxt == "even":
        _in_even_body(x, *in_params, *in_outs, *in_scratch)
    elif nxt == "odd":
        _in_odd_body(x, *in_params, *in_outs, *in_scratch)


def _proj_call(h, prev, prev_args, nxt, nxt_args, final_g, tm):
    B, S, D = h.shape
    half = COL_TILE
    tok = lambda b, s: (b, s, 0)
    const2 = lambda b, s: (0, 0)
    once = dict(pipeline_mode=pl.Buffered(1))
    weight = lambda shape: pl.BlockSpec(shape, (lambda b, s: (0,) * len(shape)), **once)

    def layer_weight(arr, j):
        return pl.BlockSpec((None, *arr.shape[1:]), (lambda b, s: (j,) + (0,) * (arr.ndim - 1)), **once)
    res_spec = lambda dil, c: pl.BlockSpec((1, dil, tm // dil, c), lambda b, s: (b, 0, s, 0))
    part = pl.BlockSpec((1, tm, half), tok)
    h_spec = pl.BlockSpec((1, tm, D), tok)
    args, in_specs, out_specs, out_shape, scratch = [], [], [], [], []

    if prev is None:
        args += [h]
        in_specs += [h_spec]
    elif prev == "even":
        o_a, o_b, z, w_out, jp = prev_args
        args += [o_a, o_b, z, h, w_out]
        in_specs += [part, part, pl.BlockSpec((1, tm, 2 * half), tok), h_spec, layer_weight(w_out, jp)]
    else:
        o_c, outs, lses, z, w_out, jp = prev_args
        expand = jnp.pad(jnp.repeat(jnp.eye(ATT_HEADS, dtype=BF16), HEAD_DIM, axis=1),
                         ((0, LANES - ATT_HEADS), (0, 0)))
        args += [o_c, *outs, *lses, expand, z, h, w_out]
        in_specs += [part, part, res_spec(4, half), res_spec(16, half),
                     pl.BlockSpec((1, tm, LANES), tok), res_spec(4, LANES), res_spec(16, LANES),
                     weight((LANES, half)), pl.BlockSpec((1, tm, 2 * half), tok), h_spec,
                     layer_weight(w_out, jp)]
    final = nxt is None
    if final:
        args += [final_g]
        in_specs += [weight((1, D))]

    if nxt == "even":
        jn, layered, shared = nxt_args
        fox_rows, val_rows, kw = ATT_HEADS * FOX_PAD, ATT_HEADS * FOX_VROWS, REC_HEADS * HEAD_DIM
        args += [*layered, *shared]
        in_specs += [layer_weight(a, jn) for a in layered] + [weight(a.shape) for a in shared]
    elif nxt == "odd":
        jn, g, w_main, cos, sin = nxt_args
        args += [g, w_main, cos, sin]
        in_specs += [layer_weight(g, jn), layer_weight(w_main, jn),
                     pl.BlockSpec((tm, LANES), lambda b, s: (s, 0)),
                     pl.BlockSpec((tm, LANES), lambda b, s: (s, 0))]

    if prev is not None:
        out_specs += [h_spec]
        out_shape += [jax.ShapeDtypeStruct((B, S, D), F32)]
    if nxt == "even":
        out_specs += [pl.BlockSpec((1, tm, ZE_WIDTH), tok),
                      pl.BlockSpec((1, fox_rows, tm), lambda b, s: (b, 0, s)),
                      pl.BlockSpec((1, val_rows, tm), lambda b, s: (b, 0, s)),
                      pl.BlockSpec((1, tm, kw), tok)]
        out_shape += [jax.ShapeDtypeStruct((B, S, ZE_WIDTH), BF16),
                      jax.ShapeDtypeStruct((B, fox_rows, S), BF16),
                      jax.ShapeDtypeStruct((B, val_rows, S), BF16),
                      jax.ShapeDtypeStruct((B, S, kw), F32)]
    elif nxt == "odd":
        dw = 3 * COL_TILE
        out_specs += [pl.BlockSpec((1, tm, Z_WIDTH), tok), res_spec(4, dw), res_spec(16, dw)]
        out_shape += [jax.ShapeDtypeStruct((B, S, Z_WIDTH), BF16),
                      jax.ShapeDtypeStruct((B, 4, S // 4, dw), BF16),
                      jax.ShapeDtypeStruct((B, 16, S // 16, dw), BF16)]

    if prev == "odd":
        scratch += [pltpu.VMEM((half // LANES, tm, LANES), F32),
                    pltpu.VMEM((half // LANES, tm, LANES), F32),
                    pltpu.VMEM((tm, LANES), F32), pltpu.VMEM((tm, LANES), F32)]
    if nxt == "even":
        scratch += [pltpu.VMEM((tm, D), BF16), pltpu.VMEM((1, LANES), F32)]
    elif nxt == "odd":
        scratch += [pltpu.VMEM((tm, D), BF16), pltpu.VMEM((3 * COL_TILE // LANES, tm, LANES), F32)]

    outs = pl.pallas_call(
        functools.partial(_proj_kernel, prev=prev, nxt=nxt, final=final),
        grid=(B, S // tm),
        in_specs=in_specs,
        out_specs=out_specs,
        out_shape=out_shape,
        scratch_shapes=scratch,
        compiler_params=_params("arbitrary", "arbitrary"),
        name=f"proj_{prev}_{nxt}",
    )(*args)
    return outs


def _prep_even(w_in, b_f, w_lr, b_lr):
    fw = ATT_HEADS * HEAD_DIM
    kw = REC_HEADS * HEAD_DIM
    vw = REC_HEADS * VAL_DIM
    sizes = (fw, fw, fw, ATT_HEADS, kw, kw, vw, GLA_RANK, fw + vw)
    offs = np.cumsum((0,) + sizes)
    fq, fk, fv, ff, gq, gk, gv, glr, gate = (w_in[..., offs[i]:offs[i + 1]] for i in range(9))
    scale = HEAD_DIM ** -0.5
    L, D = w_in.shape[:2]

    def pad_heads(w, width):
        w = w.reshape(L, D, ATT_HEADS, HEAD_DIM)
        w = jnp.pad(w, ((0, 0), (0, 0), (0, 0), (0, width - HEAD_DIM)))
        return w.reshape(L, D, ATT_HEADS * width)

    w_main = jnp.concatenate([gate, pad_heads(fk, FOX_PAD), gv, gq * scale, gk], axis=-1).astype(BF16)
    wq = (fq * (scale * LOG2E)).astype(BF16)
    wv_t = jnp.swapaxes(pad_heads(fv, FOX_VROWS), 1, 2).astype(BF16)
    pad = LANES - ATT_HEADS - GLA_RANK
    w_small = jnp.pad(jnp.concatenate([ff, glr], axis=-1), ((0, 0), (0, 0), (0, pad))).astype(BF16)
    b_f_pad = jnp.pad(b_f, ((0, 0), (0, LANES - ATT_HEADS))).reshape(L, 1, LANES)
    w_lr_pad = jnp.pad(w_lr, ((0, 0), (ATT_HEADS, pad), (0, 0))).astype(BF16)
    q_aug = np.zeros((FOX_PAD - HEAD_DIM, TOKEN_TILE), np.float32)
    q_aug[:3, :] = -1.0
    v_aug = np.zeros((ATT_HEADS, FOX_VROWS, LANES), np.float32)
    v_aug[:, HEAD_DIM, :] = 1.0
    place = np.zeros((LANES, ATT_HEADS * FOX_PAD), np.float32)
    for i in range(3):
        for h in range(ATT_HEADS):
            place[ATT_HEADS * i + h, h * FOX_PAD + HEAD_DIM + i] = 1.0
    layered = (w_main, wq, wv_t, w_small, b_f_pad, w_lr_pad, b_lr.reshape(L, 1, kw))
    shared = (jnp.asarray(place, BF16), jnp.asarray(q_aug), jnp.asarray(v_aug.reshape(-1, LANES)))
    return layered, shared


def _prep_odd(w_in):
    kw = REC_HEADS * HEAD_DIM
    vw = REC_HEADS * VAL_DIM
    dw = ATT_HEADS * HEAD_DIM
    sizes = (kw, kw, vw, dw, dw, dw, vw + dw)
    offs = np.cumsum((0,) + sizes)
    rq, rk, rv, dq, dk, dv, gate = (w_in[..., offs[i]:offs[i + 1]] for i in range(7))
    scale = HEAD_DIM ** -0.5
    return jnp.concatenate([gate, rv, rq * scale, rk, dq * (scale * LOG2E), dk, dv], axis=-1).astype(BF16)


def _rope_tables(S):
    inv = jnp.power(ROPE_THETA, -jnp.arange(0, HEAD_DIM, 2, dtype=F32) / HEAD_DIM)
    ang = jnp.arange(S, dtype=F32)[:, None] * inv[None, :]
    cos, sin = jnp.cos(ang), jnp.sin(ang)
    reps = LANES // HEAD_DIM
    cos_t = jnp.tile(jnp.concatenate([cos, cos], axis=1), (1, reps))
    sin_t = jnp.tile(jnp.concatenate([-sin, sin], axis=1), (1, reps))
    return cos_t, sin_t


def _tile(S, pref):
    return min(S, pref)


def _even_mixers(z, q_t, v_t, glog, gla_g):
    S = z.shape[1]
    o_a = _fox_attention(z, q_t, v_t, _tile(S, FOX_BLOCK))
    o_b = _gla(z, glog, gla_g.reshape(1, -1), _tile(S, SCAN_TILE))
    return o_a, o_b


def _odd_mixers(z, zd4, zd16, gn_w, gn_b):
    B, S, _ = z.shape
    o_c = _retention(z, gn_w.reshape(1, -1), gn_b.reshape(1, -1), _tile(S, SCAN_TILE))
    sources = {1: (z.reshape(B, 1, S, Z_WIDTH), DIL_TILE0), 4: (zd4, 0), 16: (zd16, 0)}
    outs, lses = zip(*[_dilated_branch(*sources[dil], window, dil, DIL_QBLOCKS)
                       for window, dil in DIL_PATTERNS])
    outs = (outs[0].reshape(B, S, COL_TILE), *outs[1:])
    lses = (lses[0].reshape(B, S, LANES), *lses[1:])
    return o_c, outs, lses


def kernel(x, norm_even, w_in_even, b_f_even, w_lr_even, b_lr_even, gla_norm_even, w_out_even,
           norm_odd, w_in_odd, ret_gn_w_odd, ret_gn_b_odd, w_out_odd, final_norm):
    depth = norm_even.shape[0] + norm_odd.shape[0]
    B, S, D = x.shape
    tm = _tile(S, TOKEN_TILE)
    cos, sin = _rope_tables(S)
    even_layered, even_shared = _prep_even(w_in_even, b_f_even, w_lr_even, b_lr_even)
    even_layered = (norm_even.reshape(-1, 1, D), *even_layered)
    odd_w = _prep_odd(w_in_odd)
    odd_g = norm_odd.reshape(-1, 1, D)
    w_out = {"even": w_out_even.astype(BF16), "odd": w_out_odd.astype(BF16)}

    def in_args(i):
        if i % 2 == 0:
            return "even", (i // 2, even_layered, even_shared)
        return "odd", (i // 2, odd_g, odd_w, cos, sin)

    h = x
    kind, args = in_args(0)
    proj = _proj_call(h, None, None, kind, args, None, tm)
    for i in range(depth):
        j = i // 2
        z = proj[0]
        if kind == "even":
            o_a, o_b = _even_mixers(*proj, gla_norm_even[j])
            prev_args = (o_a, o_b, z, w_out[kind], j)
        else:
            o_c, outs, lses = _odd_mixers(*proj, ret_gn_w_odd[j], ret_gn_b_odd[j])
            prev_args = (o_c, outs, lses, z, w_out[kind], j)
        nxt, nxt_args = in_args(i + 1) if i + 1 < depth else (None, None)
        tm_call = tm if nxt else _tile(S, FINAL_TILE)
        h, *proj = _proj_call(h, kind, prev_args, nxt, nxt_args, final_norm.reshape(1, D), tm_call)
        kind = nxt
    return h
```

```python
import functools
import math

import numpy as np
import jax
import jax.numpy as jnp
from jax import lax
from jax.experimental import pallas as pl
from jax.experimental.pallas import tpu as pltpu

F32 = jnp.float32
BF16 = jnp.bfloat16

EPS = 1e-6
ROPE_THETA = 10000.0
HEAD_DIM = 64
VAL_DIM = 128
ATT_HEADS = 8
REC_HEADS = 4
GLA_RANK = 16
GLA_TAU = 16.0
GLA_CHUNK = 64
RET_CHUNK = 256
DIL_BLOCK = 128
DIL_PATTERNS = ((128, 1), (512, 4), (2048, 16))

Z_WIDTH = 3584
DIL_TILE0 = 4
DIL_QBLOCKS = 8
DIL_VROWS = 144
ZE_WIDTH = 3072
FOX_PAD = 128
FOX_VROWS = 80
LOG2E = 1.4426950408889634
COL_TILE = 512
TOKEN_TILE = 512
SCAN_TILE = 1024
FINAL_TILE = 1024
FOX_BLOCK = 256
LANES = 128
V7X_VMEM_LIMIT = 56 * 1024 * 1024

NT_DIMS = (((1,), (1,)), ((), ()))
TN_DIMS = (((0,), (0,)), ((), ()))


def _params(*sem):
    return pltpu.CompilerParams(dimension_semantics=sem, vmem_limit_bytes=V7X_VMEM_LIMIT)


def _split2(x):
    hi = x.astype(BF16)
    lo = (x - hi.astype(F32)).astype(BF16)
    return hi, lo


def _split3(x):
    hi, lo = _split2(x)
    lo2 = (x - hi.astype(F32) - lo.astype(F32)).astype(BF16)
    return hi, lo, lo2


def _log_sigmoid(x):
    return jnp.minimum(x, 0.0) - jnp.log(1.0 + jnp.exp(-jnp.abs(x)))


def _rms_normed(x, g):
    ms = jnp.mean(x * x, axis=-1, keepdims=True)
    return x * lax.rsqrt(ms + EPS) * g


def _in_even_body(x, g_ref, w_ref, wq_ref, wv_ref, ws_ref, bf_ref, wlr_ref, blr_ref, place_ref,
                  qaug_ref, vaug_ref, z_ref, qt_ref, vt_ref, glog_ref, u_ref, carry_ref):
    tm = x.shape[0]

    @pl.when(pl.program_id(1) == 0)
    def _():
        carry_ref[...] = jnp.zeros_like(carry_ref)

    u_ref[...] = _rms_normed(x, g_ref[...]).astype(BF16)
    u = u_ref[...]

    zs = jnp.dot(u, ws_ref[...], preferred_element_type=F32)
    logf = _log_sigmoid(zs + bf_ref[...])
    row = lax.broadcasted_iota(jnp.int32, (LANES, LANES), 0)
    col = lax.broadcasted_iota(jnp.int32, (LANES, LANES), 1)
    tril = jnp.where(row >= col, 1.0, 0.0).astype(BF16)
    carry = carry_ref[...]
    blocks = []
    for r0 in range(0, tm, LANES):
        blk_sum = sum(jnp.dot(tril, t[r0:r0 + LANES, :], preferred_element_type=F32)
                      for t in _split3(logf)) + carry
        carry = blk_sum[LANES - 1:LANES, :]
        blocks.append(blk_sum)
    csum = jnp.concatenate(blocks, axis=0)
    carry_ref[...] = carry
    hi, lo, lo2 = (t.astype(F32) for t in _split3(csum * LOG2E))
    lane = lax.broadcasted_iota(jnp.int32, (tm, LANES), 1)
    packed = jnp.where(lane < ATT_HEADS, hi,
                       jnp.where(lane < 2 * ATT_HEADS, pltpu.roll(lo, ATT_HEADS, 1),
                                 jnp.where(lane < 3 * ATT_HEADS, pltpu.roll(lo2, 2 * ATT_HEADS, 1), 0.0)))
    c_cols = jnp.dot(packed.astype(BF16), place_ref[...], preferred_element_type=F32)

    glr = jnp.dot(zs.astype(BF16), wlr_ref[...], preferred_element_type=F32)
    glog_ref[0] = _log_sigmoid(glr + blr_ref[...]) * (1.0 / GLA_TAU)

    low = lax.broadcasted_iota(jnp.int32, (tm, LANES), 1) < HEAD_DIM
    for j in range(w_ref.shape[1] // COL_TILE):
        r = jnp.dot(u, w_ref[:, j * COL_TILE:(j + 1) * COL_TILE], preferred_element_type=F32)
        if j == 2:
            for g in range(ATT_HEADS // 2):
                pair = r[:, g * LANES:(g + 1) * LANES]
                for half, feats in ((0, pair), (1, pltpu.roll(pair, HEAD_DIM, 1))):
                    h = 2 * g + half
                    z0 = 2 * COL_TILE + h * FOX_PAD
                    z_ref[0, :, z0:z0 + FOX_PAD] = (
                        jnp.where(low, feats, 0.0) + c_cols[:, h * FOX_PAD:(h + 1) * FOX_PAD]).astype(BF16)
        else:
            t = j if j < 2 else j + 1
            z_ref[0, :, t * COL_TILE:(t + 1) * COL_TILE] = r.astype(BF16)

    def feature_major(w_t_ref, aug_ref, out_ref, tile):
        for j in range(w_t_ref.shape[0] // tile):
            rows = slice(j * tile, (j + 1) * tile)
            r = lax.dot_general(w_t_ref[rows, :], u, NT_DIMS, preferred_element_type=F32)
            aug = aug_ref[rows, :]
            for c in range(tm // LANES):
                lanes = slice(c * LANES, (c + 1) * LANES)
                out_ref[0, rows, lanes] = (r[:, lanes] + aug).astype(BF16)

    feature_major(wv_ref, vaug_ref, vt_ref, ATT_HEADS * FOX_VROWS // 2)
    qtok = jnp.dot(u, wq_ref[...], preferred_element_type=F32)
    aug = qaug_ref[...].astype(BF16)
    for g in range(ATT_HEADS // 2):
        q_t = qtok[:, g * LANES:(g + 1) * LANES].T
        for half in range(2):
            r0 = (2 * g + half) * FOX_PAD
            qt_ref[0, r0:r0 + HEAD_DIM, :] = q_t[half * HEAD_DIM:(half + 1) * HEAD_DIM, :].astype(BF16)
            qt_ref[0, r0 + HEAD_DIM:r0 + FOX_PAD, :] = aug


def _rope_tile(x, cos, sin):
    lane = lax.broadcasted_iota(jnp.int32, (x.shape[0], LANES), 1)
    first_half = (lane % HEAD_DIM) < (HEAD_DIM // 2)
    outs = []
    for c in range(x.shape[1] // LANES):
        xc = x[:, c * LANES:(c + 1) * LANES]
        partner = jnp.where(first_half,
                            pltpu.roll(xc, LANES - HEAD_DIM // 2, 1),
                            pltpu.roll(xc, HEAD_DIM // 2, 1))
        outs.append(xc * cos + partner * sin)
    return jnp.concatenate(outs, axis=-1)


def _in_odd_body(x, g_ref, w_ref, cos_ref, sin_ref, z_ref, zd4_ref, zd16_ref, u_ref, r_ref):
    tm = x.shape[0]
    u_ref[...] = _rms_normed(x, g_ref[...]).astype(BF16)
    u = u_ref[...]
    cos = cos_ref[...]
    sin = sin_ref[...]
    for j in range(Z_WIDTH // COL_TILE):
        cols = slice(j * COL_TILE, (j + 1) * COL_TILE)
        r = jnp.dot(u, w_ref[:, cols], preferred_element_type=F32)
        if j in (3, 4, 5):
            r = _rope_tile(r, cos, sin)
        z_ref[0, :, cols] = r.astype(BF16)
        if j >= DIL_TILE0:
            t = j - DIL_TILE0
            for c in range(COL_TILE // LANES):
                slot = t * (COL_TILE // LANES) + c
                r_ref[slot] = r[:, c * LANES:(c + 1) * LANES]
                dcols = slice(t * COL_TILE + c * LANES, t * COL_TILE + (c + 1) * LANES)
                for dil, ref in ((4, zd4_ref), (16, zd16_ref)):
                    for res in range(dil):
                        ref[0, res, :, dcols] = (
                            r_ref[slot, pl.ds(res, tm // dil, stride=dil), :].astype(BF16))


def _fox_kernel(qt_ref, kp_ref, vt_ref, o_ref, s_ref, acc_ref, mp_ref, mc_ref, ot_ref, mask_ref, *, tq, tk):
    i = pl.program_id(1)
    heads = range(ATT_HEADS)
    rows = lambda h: slice(h * FOX_PAD, (h + 1) * FOX_PAD)
    vrows = lambda h: slice(h * FOX_VROWS, (h + 1) * FOX_VROWS)

    @pl.when(i == 0)
    def _():
        key = lax.broadcasted_iota(jnp.int32, (tk, tq), 0)
        qry = lax.broadcasted_iota(jnp.int32, (tk, tq), 1)
        mask_ref[0] = jnp.zeros((tk, tq), F32)
        mask_ref[1] = jnp.where(key <= qry, 0.0, -jnp.inf)
        mask_ref[2] = jnp.where(key + tk <= qry, 0.0, -jnp.inf)

    def score_stage(h, j):
        ks = pl.multiple_of(j * tk, tk)
        s = jnp.dot(kp_ref[0, pl.ds(ks, tk), rows(h)], qt_ref[0, rows(h), :],
                    preferred_element_type=F32)
        s = s + mask_ref[jnp.clip(j - 2 * i + 1, 0, 2)]
        s_ref[h] = s
        m_old = mc_ref[h]
        mp_ref[h] = m_old
        mc_ref[h] = jnp.maximum(m_old, jnp.max(s, axis=0, keepdims=True))

    def value_stage(h, j):
        ks = pl.multiple_of(j * tk, tk)
        m_new = mc_ref[h]
        alpha = jnp.exp2(mp_ref[h] - m_new)
        p = jnp.exp2((s_ref[h] - m_new).astype(BF16))
        pv = jnp.dot(vt_ref[0, vrows(h), pl.ds(ks, tk)], p, preferred_element_type=F32)
        acc_ref[h] = alpha * acc_ref[h] + pv

    def trip(j):
        for h in heads:
            value_stage(h, j - 1)
            score_stage(h, j)

    for h in heads:
        acc_ref[h] = jnp.zeros(acc_ref.shape[1:], F32)
        mc_ref[h] = jnp.full((1, tq), -jnp.inf, F32)
        score_stage(h, 0)

    trip(1)

    def pair(t, _):
        j = 2 + 2 * t
        trip(j)
        trip(j + 1)
        return 0

    lax.fori_loop(0, i, pair, 0)

    for h in heads:
        value_stage(h, 2 * i + 1)
        acc = acc_ref[h]
        ot_ref[h * HEAD_DIM:(h + 1) * HEAD_DIM, :] = (
            acc[:HEAD_DIM, :] / acc[HEAD_DIM:HEAD_DIM + 1, :])
    o_ref[0] = ot_ref[...].T.astype(BF16)


def _fox_attention(z, q_t, v_t, tk):
    B, S, _ = z.shape
    tq = 2 * tk
    fox_rows = ATT_HEADS * FOX_PAD
    val_rows = ATT_HEADS * FOX_VROWS
    return pl.pallas_call(
        functools.partial(_fox_kernel, tq=tq, tk=tk),
        grid=(B, S // tq),
        in_specs=[
            pl.BlockSpec((1, fox_rows, tq), lambda b, i: (b, 0, i)),
            pl.BlockSpec((1, S, fox_rows), lambda b, i: (b, 0, 1)),
            pl.BlockSpec((1, val_rows, S), lambda b, i: (b, 0, 0)),
        ],
        out_specs=pl.BlockSpec((1, tq, COL_TILE), lambda b, i: (b, i, 0)),
        out_shape=jax.ShapeDtypeStruct((B, S, COL_TILE), BF16),
        scratch_shapes=[pltpu.VMEM((ATT_HEADS, tk, tq), F32),
                        pltpu.VMEM((ATT_HEADS, FOX_VROWS, tq), F32),
                        pltpu.VMEM((ATT_HEADS, 1, tq), F32),
                        pltpu.VMEM((ATT_HEADS, 1, tq), F32),
                        pltpu.VMEM((ATT_HEADS * HEAD_DIM, tq), F32),
                        pltpu.VMEM((3, tk, tq), F32)],
        compiler_params=_params("arbitrary", "arbitrary"),
        name="fox_attention",
    )(q_t, z, v_t)


def _gla_kernel(q_ref, k_ref, v_ref, g_ref, gn_ref, o_ref,
                qhat_ref, top_ref, bot_ref, kinc_ref, dec_ref, dect_ref, state_ref):
    tg = q_ref.shape[1]
    C = GLA_CHUNK
    blk = 2 * C
    n_chunks = tg // C
    pairs = REC_HEADS // 2

    @pl.when(pl.program_id(1) == 0)
    def _():
        state_ref[...] = jnp.zeros_like(state_ref)

    row = lax.broadcasted_iota(jnp.int32, (blk, blk), 0)
    col = lax.broadcasted_iota(jnp.int32, (blk, blk), 1)
    chunk_start = row & (-C)
    tril = jnp.where(col <= row, jnp.where(col >= chunk_start, 1.0, 0.0), 0.0).astype(BF16)
    ghi, glo = _split2(g_ref[0])
    b = jnp.concatenate(
        [jnp.dot(tril, ghi[r0:r0 + blk, :], preferred_element_type=F32)
         + jnp.dot(tril, glo[r0:r0 + blk, :], preferred_element_type=F32)
         for r0 in range(0, tg, blk)], axis=0)
    q = q_ref[0].astype(F32)
    k = k_ref[0].astype(F32)
    tok = lax.broadcasted_iota(jnp.int32, (tg, LANES), 0)
    lane = lax.broadcasted_iota(jnp.int32, (tg, LANES), 1)
    in_a = (tok & C) == 0
    first = lane < HEAD_DIM
    qt = q * jnp.exp(b)
    kt = k * jnp.exp(-b)
    top_ref[...] = kt
    dec_ref[...] = jnp.zeros_like(dec_ref)
    kd_rows, dec_rows = [], []
    for c in range(n_chunks):
        rows = slice(c * C, (c + 1) * C)
        b_last = b[(c + 1) * C - 1:(c + 1) * C, :]
        kd_rows.append(k[rows, :] * jnp.exp(b_last - b[rows, :]))
        dec_rows.append(jnp.exp(b_last))
        dec_ref[c:c + 1, :] = dec_rows[-1]
    for c in range(0, n_chunks, 2):
        bot_ref[c * C:(c + 1) * C, :] = kd_rows[c]
        bot_ref[(c + 1) * C:(c + 2) * C, :] = kt[(c + 1) * C:(c + 2) * C, :]
        kinc_ref[c * C:(c + 1) * C, :] = kd_rows[c] * dec_rows[c + 1]
        kinc_ref[(c + 1) * C:(c + 2) * C, :] = kd_rows[c + 1]
    dect_ref[...] = dec_ref[...].T
    for half in range(2):
        for g in range(pairs):
            gs = slice(g * LANES, (g + 1) * LANES)
            qh = jnp.where(first, qt[:, gs], 0.0) if half == 0 else jnp.where(first, 0.0, qt[:, gs])
            qhat_ref[half, :, 2 * g * LANES:(2 * g + 1) * LANES] = jnp.where(in_a, qh, 0.0).astype(BF16)
            qhat_ref[half, :, (2 * g + 1) * LANES:(2 * g + 2) * LANES] = jnp.where(in_a, 0.0, qh).astype(BF16)

    rb = lax.broadcasted_iota(jnp.int32, (blk, blk), 0)
    cb = lax.broadcasted_iota(jnp.int32, (blk, blk), 1)
    lower = cb <= rb
    low_rows = rb < HEAD_DIM

    for n in range(tg // blk):
        rows = slice(n * blk, (n + 1) * blk)
        for g in range(pairs):
            gs = slice(g * LANES, (g + 1) * LANES)
            top_t = top_ref[rows, gs].T.astype(BF16)
            bot_t = bot_ref[rows, gs].T.astype(BF16)
            kinc_t = kinc_ref[rows, gs].T.astype(BF16)
            st = state_ref[g]
            dec_a = dect_ref[gs, 2 * n:2 * n + 1]
            dec_b = dect_ref[gs, 2 * n + 1:2 * n + 2]
            rhs = jnp.concatenate([jnp.concatenate([top_t, st.astype(BF16)], axis=1),
                                   jnp.concatenate([bot_t, (st * dec_a).astype(BF16)], axis=1)], axis=0)
            inc = []
            for half in range(2):
                h = 2 * g + half
                vs = slice(h * VAL_DIM, (h + 1) * VAL_DIM)
                v = v_ref[0, rows, vs]
                sc = jnp.dot(qhat_ref[half, rows, 2 * g * LANES:(2 * g + 2) * LANES], rhs,
                             preferred_element_type=F32)
                att = jnp.where(lower, sc[:, :blk], 0.0).astype(BF16)
                both = jnp.dot(jnp.concatenate([att, kinc_t], axis=0), v, preferred_element_type=F32)
                o = both[:blk, :] + sc[:, blk:]
                inc.append(both[blk:, :])
                ms = jnp.mean(o * o, axis=-1, keepdims=True)
                o_ref[0, rows, vs] = (o * lax.rsqrt(ms + EPS) * gn_ref[:, vs]).astype(BF16)
            state_ref[g] = st * (dec_a * dec_b) + jnp.where(low_rows, inc[0], inc[1])


def _gla(z, glog, gla_g, tg):
    B, S, _ = z.shape
    kw = REC_HEADS * HEAD_DIM
    vw = REC_HEADS * VAL_DIM
    return pl.pallas_call(
        _gla_kernel,
        grid=(B, S // tg),
        in_specs=[
            pl.BlockSpec((1, tg, kw), lambda b, s: (b, s, 10)),
            pl.BlockSpec((1, tg, kw), lambda b, s: (b, s, 11)),
            pl.BlockSpec((1, tg, vw), lambda b, s: (b, s, 4)),
            pl.BlockSpec((1, tg, kw), lambda b, s: (b, s, 0)),
            pl.BlockSpec((1, vw), lambda b, s: (0, 0)),
        ],
        out_specs=pl.BlockSpec((1, tg, vw), lambda b, s: (b, s, 0)),
        out_shape=jax.ShapeDtypeStruct((B, S, vw), BF16),
        scratch_shapes=[
            pltpu.VMEM((2, tg, 2 * kw), BF16),
            pltpu.VMEM((tg, kw), F32), pltpu.VMEM((tg, kw), F32), pltpu.VMEM((tg, kw), F32),
            pltpu.VMEM((LANES, kw), F32), pltpu.VMEM((kw, LANES), F32),
            pltpu.VMEM((REC_HEADS // 2, 2 * HEAD_DIM, VAL_DIM), F32),
        ],
        compiler_params=_params("arbitrary", "arbitrary"),
        name="gla",
    )(z, z, z, glog, gla_g)


def _ret_kernel(q_ref, k_ref, v_ref, gw_ref, gb_ref, o_ref, state_ref, dmat_ref, zeta_ref):
    tg = q_ref.shape[1]
    C = RET_CHUNK
    n_chunks = tg // C
    pairs = REC_HEADS // 2
    log_gamma = [math.log(1.0 - 2.0 ** (-5.0 - h)) for h in range(REC_HEADS)]

    @pl.when(pl.program_id(1) == 0)
    def _():
        state_ref[...] = jnp.zeros_like(state_ref)

    ri = lax.broadcasted_iota(jnp.int32, (C, C), 0)
    ci = lax.broadcasted_iota(jnp.int32, (C, C), 1)
    diff = (ri - ci).astype(F32)
    idx = lax.broadcasted_iota(jnp.int32, (C, 1), 0).astype(F32)
    low_lanes = lax.broadcasted_iota(jnp.int32, (C, LANES), 1) < HEAD_DIM
    low_rows = lax.broadcasted_iota(jnp.int32, (LANES, VAL_DIM), 0) < HEAD_DIM
    for h in range(REC_HEADS):
        dmat_ref[h] = jnp.where(diff >= 0, jnp.exp(jnp.maximum(diff, 0.0) * log_gamma[h]), 0.0)
    for g in range(pairs):
        zeta_ref[g] = jnp.where(low_lanes, jnp.exp((C - 1.0 - idx) * log_gamma[2 * g]),
                                jnp.exp((C - 1.0 - idx) * log_gamma[2 * g + 1]))

    for c in range(n_chunks):
        rows = slice(c * C, (c + 1) * C)
        for g in range(pairs):
            gs = slice(g * LANES, (g + 1) * LANES)
            q_pair = q_ref[0, rows, gs]
            k_f32 = k_ref[0, rows, gs].astype(F32)
            k_t = k_f32.T.astype(BF16)
            kz_t = (k_f32 * zeta_ref[g]).T.astype(BF16)
            st = state_ref[g]
            rhs = jnp.concatenate([k_t, st.astype(BF16)], axis=1)
            zero = jnp.zeros_like(q_pair)
            inc = []
            for half in range(2):
                h = 2 * g + half
                vs = slice(h * VAL_DIM, (h + 1) * VAL_DIM)
                xi = jnp.exp((idx + 1.0) * log_gamma[h])
                q = jnp.where(low_lanes, q_pair, zero) if half == 0 else jnp.where(low_lanes, zero, q_pair)
                v = v_ref[0, rows, vs]
                sc = jnp.dot(q, rhs, preferred_element_type=F32)
                att = (sc[:, :C] * dmat_ref[h]).astype(BF16)
                both = jnp.dot(jnp.concatenate([att, kz_t], axis=0), v, preferred_element_type=F32)
                o = both[:C, :] + xi * sc[:, C:]
                inc.append(both[C:, :])
                o = o - jnp.mean(o, axis=-1, keepdims=True)
                ms = jnp.mean(o * o, axis=-1, keepdims=True)
                o = o * lax.rsqrt(ms + EPS) * gw_ref[:, vs] + gb_ref[:, vs]
                o_ref[0, rows, vs] = o.astype(BF16)
            decay = jnp.where(low_rows, math.exp(C * log_gamma[2 * g]), math.exp(C * log_gamma[2 * g + 1]))
            state_ref[g] = st * decay + jnp.where(low_rows, inc[0], inc[1])


def _retention(z, gn_w, gn_b, tg):
    B, S, _ = z.shape
    kw = REC_HEADS * HEAD_DIM
    vw = REC_HEADS * VAL_DIM
    return pl.pallas_call(
        _ret_kernel,
        grid=(B, S // tg),
        in_specs=[
            pl.BlockSpec((1, tg, kw), lambda b, s: (b, s, 6)),
            pl.BlockSpec((1, tg, kw), lambda b, s: (b, s, 7)),
            pl.BlockSpec((1, tg, vw), lambda b, s: (b, s, 2)),
            pl.BlockSpec((1, vw), lambda b, s: (0, 0)),
            pl.BlockSpec((1, vw), lambda b, s: (0, 0)),
        ],
        out_specs=pl.BlockSpec((1, tg, vw), lambda b, s: (b, s, 0)),
        out_shape=jax.ShapeDtypeStruct((B, S, vw), BF16),
        scratch_shapes=[pltpu.VMEM((REC_HEADS // 2, 2 * HEAD_DIM, VAL_DIM), F32),
                        pltpu.VMEM((REC_HEADS, RET_CHUNK, RET_CHUNK), F32),
                        pltpu.VMEM((REC_HEADS // 2, RET_CHUNK, LANES), F32)],
        compiler_params=_params("arbitrary", "arbitrary"),
        name="retention",
    )(z, z, z, gn_w, gn_b)


def _dil_kernel(q_ref, k_ref, v_ref, o_ref, lse_ref,
                vt_ref, qt_ref, s_ref, mx_ref, bias_ref, ot_ref, lt_ref, *, blk, nblk, ncls, span):
    n = pl.program_id(2)
    m = k_ref.shape[2]
    pairs = ATT_HEADS // 2
    chains = [(g, half) for g in range(pairs) for half in range(2)]
    units = [(rc, t) for rc in range(ncls) for t in range(nblk)]
    feat = lax.broadcasted_iota(jnp.int32, (LANES, blk), 0)

    @pl.when(n == 0)
    def _():
        for rc in range(ncls):
            for g in range(pairs):
                vt_ref[rc, g * DIL_VROWS + LANES:(g + 1) * DIL_VROWS, :] = jnp.ones(
                    (DIL_VROWS - LANES, m), BF16)

        def build(kb, _):
            r0 = pl.multiple_of(kb * blk, blk)
            for rc in range(ncls):
                for g in range(pairs):
                    v = v_ref[0, rc, pl.ds(r0, blk), g * LANES:(g + 1) * LANES].astype(F32)
                    vt_ref[rc, g * DIL_VROWS:g * DIL_VROWS + LANES, pl.ds(r0, blk)] = v.T.astype(BF16)
            return 0

        lax.fori_loop(0, m // blk, build, 0)

    single_step = m == nblk * blk
    starts, wins = [], []
    for t in range(nblk):
        qb = n * nblk + t
        first = single_step and t == 0
        win = blk if first else 2 * blk
        start = 0 if first else pl.multiple_of(jnp.maximum(qb - 1, 0) * blk, blk)
        starts.append(start)
        wins.append(win)
        ki_w = lax.broadcasted_iota(jnp.int32, (win, blk), 0)
        qi_w = lax.broadcasted_iota(jnp.int32, (win, blk), 1)
        rel = qb * blk + qi_w - start - ki_w
        bias_ref[t, :win] = jnp.where(rel >= 0, jnp.where(rel <= span, 0.0, -jnp.inf), -jnp.inf)
    for u, (rc, t) in enumerate(units):
        for g in range(pairs):
            q_t = q_ref[0, rc, t * blk:(t + 1) * blk, g * LANES:(g + 1) * LANES].astype(F32).T
            qt_ref[u, 2 * g] = jnp.where(feat < HEAD_DIM, q_t, 0.0).astype(BF16)
            qt_ref[u, 2 * g + 1] = jnp.where(feat < HEAD_DIM, 0.0, q_t).astype(BF16)

    for u, (rc, t) in enumerate(units):
        for c, (g, half) in enumerate(chains):
            k = k_ref[0, rc, pl.ds(starts[t], wins[t]), g * LANES:(g + 1) * LANES]
            s = jnp.dot(k, qt_ref[u, c], preferred_element_type=F32) + bias_ref[t, :wins[t]]
            s_ref[u, c, :wins[t]] = s
            mx_ref[u, c] = jnp.max(s, axis=0, keepdims=True)

    for u, (rc, t) in enumerate(units):
        lt_ref[u] = jnp.zeros(lt_ref.shape[1:], F32)
        for c, (g, half) in enumerate(chains):
            mx = mx_ref[u, c]
            p = jnp.exp2((s_ref[u, c, :wins[t]] - mx).astype(BF16))
            acc = jnp.dot(vt_ref[rc, g * DIL_VROWS:(g + 1) * DIL_VROWS, pl.ds(starts[t], wins[t])], p,
                          preferred_element_type=F32)
            den = acc[LANES:LANES + 1, :]
            ot_ref[u, c * HEAD_DIM:(c + 1) * HEAD_DIM, :] = (
                acc[half * HEAD_DIM:(half + 1) * HEAD_DIM, :] * (1.0 / den))
            lt_ref[u, c:c + 1, :] = (mx + jnp.log2(den)) * (1.0 / LOG2E)
        qrows = slice(t * blk, (t + 1) * blk)
        o_ref[0, rc, qrows, :] = ot_ref[u].T.astype(BF16)
        lse_ref[0, rc, qrows, :] = lt_ref[u].T


def _dilated_branch(src, tile0, window, dil, units):
    B, _, m, _ = src.shape
    blk = DIL_BLOCK
    nblk = min(units, m // blk)
    ncls = min(dil, units // nblk)
    tq = nblk * blk
    n_units = ncls * nblk
    return pl.pallas_call(
        functools.partial(_dil_kernel, blk=blk, nblk=nblk, ncls=ncls, span=window // dil),
        grid=(B, dil // ncls, m // tq),
        in_specs=[
            pl.BlockSpec((1, ncls, tq, COL_TILE), lambda b, r, n: (b, r, n, tile0)),
            pl.BlockSpec((1, ncls, m, COL_TILE), lambda b, r, n: (b, r, 0, tile0 + 1)),
            pl.BlockSpec((1, ncls, m, COL_TILE), lambda b, r, n: (b, r, 0, tile0 + 2)),
        ],
        out_specs=[
            pl.BlockSpec((1, ncls, tq, COL_TILE), lambda b, r, n: (b, r, n, 0)),
            pl.BlockSpec((1, ncls, tq, LANES), lambda b, r, n: (b, r, n, 0)),
        ],
        out_shape=[
            jax.ShapeDtypeStruct((B, dil, m, COL_TILE), BF16),
            jax.ShapeDtypeStruct((B, dil, m, LANES), F32),
        ],
        scratch_shapes=[pltpu.VMEM((ncls, ATT_HEADS // 2 * DIL_VROWS, m), BF16),
                        pltpu.VMEM((n_units, ATT_HEADS, LANES, blk), BF16),
                        pltpu.VMEM((n_units, ATT_HEADS, 2 * blk, blk), F32),
                        pltpu.VMEM((n_units, ATT_HEADS, 1, blk), F32),
                        pltpu.VMEM((nblk, 2 * blk, blk), F32),
                        pltpu.VMEM((n_units, ATT_HEADS * HEAD_DIM, blk), F32),
                        pltpu.VMEM((n_units, LANES, blk), F32)],
        compiler_params=_params("arbitrary", "arbitrary", "arbitrary"),
        name=f"dilated_{dil}",
    )(src, src, src)


def _silu(x):
    return x / (1.0 + jnp.exp(-x))


def _mix_out(mix_a, mix_b, gate_ref, h_ref, w_ref):
    half = mix_a.shape[1]
    gate = _silu(gate_ref[0].astype(F32))
    a = (mix_a * gate[:, :half]).astype(BF16)
    b = (mix_b * gate[:, half:]).astype(BF16)
    y = (jnp.dot(a, w_ref[:half, :], preferred_element_type=F32)
         + jnp.dot(b, w_ref[half:, :], preferred_element_type=F32))
    return h_ref[0] + y


def _dilated_mix(d1_ref, d4_ref, d16_ref, l1_ref, l4_ref, l16_ref, e_ref,
                 o4_ref, o16_ref, ls4_ref, ls16_ref):
    tm = d1_ref.shape[1]
    chunks = d1_ref.shape[2] // LANES
    for dil, src, lsrc, dst, ldst in ((4, d4_ref, l4_ref, o4_ref, ls4_ref),
                                      (16, d16_ref, l16_ref, o16_ref, ls16_ref)):
        for res in range(dil):
            token_rows = pl.ds(res, tm // dil, stride=dil)
            ldst[token_rows, :] = lsrc[0, res]
            for c in range(chunks):
                dst[c, token_rows, :] = src[0, res, :, c * LANES:(c + 1) * LANES].astype(F32)
    l1, l2, l3 = l1_ref[0], ls4_ref[...], ls16_ref[...]
    mx = jnp.maximum(jnp.maximum(l1, l2), l3)
    e = e_ref[...]
    expand = lambda w: jnp.dot(w.astype(BF16), e, preferred_element_type=F32)
    x1, x2, x3 = expand(jnp.exp(l1 - mx)), expand(jnp.exp(l2 - mx)), expand(jnp.exp(l3 - mx))
    o4 = jnp.concatenate([o4_ref[c] for c in range(chunks)], axis=-1)
    o16 = jnp.concatenate([o16_ref[c] for c in range(chunks)], axis=-1)
    return (x1 * d1_ref[0].astype(F32) + x2 * o4 + x3 * o16) / (x1 + x2 + x3)


N_IN = {"even": 11, "odd": 4}
N_OUT = {"even": 4, "odd": 3}
N_SCRATCH = {"even": 2, "odd": 2}


def _proj_kernel(*refs, prev, nxt, final):
    refs = list(refs)
    take = lambda n: [refs.pop(0) for _ in range(n)]
    if prev is None:
        (x_ref,) = take(1)
    elif prev == "even":
        oa_ref, ob_ref, gate_ref, h_ref, wout_ref = take(5)
    else:
        oc_ref, d1_ref, d4_ref, d16_ref, l1_ref, l4_ref, l16_ref, e_ref, gate_ref, h_ref, wout_ref = take(11)
    if final:
        (fg_ref,) = take(1)
    in_params = take(N_IN[nxt]) if nxt else []
    if prev is not None:
        (hout_ref,) = take(1)
    in_outs = take(N_OUT[nxt]) if nxt else []
    mix_scratch = take(4) if prev == "odd" else []
    in_scratch = take(N_SCRATCH[nxt]) if nxt else []

    if prev is None:
        x = x_ref[0]
    elif prev == "even":
        x = _mix_out(oa_ref[0].astype(F32), ob_ref[0].astype(F32), gate_ref, h_ref, wout_ref)
    else:
        o_d = _dilated_mix(d1_ref, d4_ref, d16_ref, l1_ref, l4_ref, l16_ref, e_ref, *mix_scratch)
        x = _mix_out(oc_ref[0].astype(F32), o_d, gate_ref, h_ref, wout_ref)
    if final:
        x = _rms_normed(x, fg_ref[...])
    if prev is not None:
        hout_ref[0] = x
    if nxt == "even":
        _in_even_body(x, *in_params, *in_outs, *in_scratch)
    elif nxt == "odd":
        _in_odd_body(x, *in_params, *in_outs, *in_scratch)


def _proj_call(h, prev, prev_args, nxt, nxt_args, final_g, tm):
    B, S, D = h.shape
    half = COL_TILE
    tok = lambda b, s: (b, s, 0)
    const2 = lambda b, s: (0, 0)
    once = dict(pipeline_mode=pl.Buffered(1))
    weight = lambda shape: pl.BlockSpec(shape, (lambda b, s: (0,) * len(shape)), **once)

    def layer_weight(arr, j):
        return pl.BlockSpec((None, *arr.shape[1:]), (lambda b, s: (j,) + (0,) * (arr.ndim - 1)), **once)
    res_spec = lambda dil, c: pl.BlockSpec((1, dil, tm // dil, c), lambda b, s: (b, 0, s, 0))
    part = pl.BlockSpec((1, tm, half), tok)
    h_spec = pl.BlockSpec((1, tm, D), tok)
    args, in_specs, out_specs, out_shape, scratch = [], [], [], [], []

    if prev is None:
        args += [h]
        in_specs += [h_spec]
    elif prev == "even":
        o_a, o_b, z, w_out, jp = prev_args
        args += [o_a, o_b, z, h, w_out]
        in_specs += [part, part, pl.BlockSpec((1, tm, 2 * half), tok), h_spec, layer_weight(w_out, jp)]
    else:
        o_c, outs, lses, z, w_out, jp = prev_args
        expand = jnp.pad(jnp.repeat(jnp.eye(ATT_HEADS, dtype=BF16), HEAD_DIM, axis=1),
                         ((0, LANES - ATT_HEADS), (0, 0)))
        args += [o_c, *outs, *lses, expand, z, h, w_out]
        in_specs += [part, part, res_spec(4, half), res_spec(16, half),
                     pl.BlockSpec((1, tm, LANES), tok), res_spec(4, LANES), res_spec(16, LANES),
                     weight((LANES, half)), pl.BlockSpec((1, tm, 2 * half), tok), h_spec,
                     layer_weight(w_out, jp)]
    final = nxt is None
    if final:
        args += [final_g]
        in_specs += [weight((1, D))]

    if nxt == "even":
        jn, layered, shared = nxt_args
        fox_rows, val_rows, kw = ATT_HEADS * FOX_PAD, ATT_HEADS * FOX_VROWS, REC_HEADS * HEAD_DIM
        args += [*layered, *shared]
        in_specs += [layer_weight(a, jn) for a in layered] + [weight(a.shape) for a in shared]
    elif nxt == "odd":
        jn, g, w_main, cos, sin = nxt_args
        args += [g, w_main, cos, sin]
        in_specs += [layer_weight(g, jn), layer_weight(w_main, jn),
                     pl.BlockSpec((tm, LANES), lambda b, s: (s, 0)),
                     pl.BlockSpec((tm, LANES), lambda b, s: (s, 0))]

    if prev is not None:
        out_specs += [h_spec]
        out_shape += [jax.ShapeDtypeStruct((B, S, D), F32)]
    if nxt == "even":
        out_specs += [pl.BlockSpec((1, tm, ZE_WIDTH), tok),
                      pl.BlockSpec((1, fox_rows, tm), lambda b, s: (b, 0, s)),
                      pl.BlockSpec((1, val_rows, tm), lambda b, s: (b, 0, s)),
                      pl.BlockSpec((1, tm, kw), tok)]
        out_shape += [jax.ShapeDtypeStruct((B, S, ZE_WIDTH), BF16),
                      jax.ShapeDtypeStruct((B, fox_rows, S), BF16),
                      jax.ShapeDtypeStruct((B, val_rows, S), BF16),
                      jax.ShapeDtypeStruct((B, S, kw), F32)]
    elif nxt == "odd":
        dw = 3 * COL_TILE
        out_specs += [pl.BlockSpec((1, tm, Z_WIDTH), tok), res_spec(4, dw), res_spec(16, dw)]
        out_shape += [jax.ShapeDtypeStruct((B, S, Z_WIDTH), BF16),
                      jax.ShapeDtypeStruct((B, 4, S // 4, dw), BF16),
                      jax.ShapeDtypeStruct((B, 16, S // 16, dw), BF16)]

    if prev == "odd":
        scratch += [pltpu.VMEM((half // LANES, tm, LANES), F32),
                    pltpu.VMEM((half // LANES, tm, LANES), F32),
                    pltpu.VMEM((tm, LANES), F32), pltpu.VMEM((tm, LANES), F32)]
    if nxt == "even":
        scratch += [pltpu.VMEM((tm, D), BF16), pltpu.VMEM((1, LANES), F32)]
    elif nxt == "odd":
        scratch += [pltpu.VMEM((tm, D), BF16), pltpu.VMEM((3 * COL_TILE // LANES, tm, LANES), F32)]

    outs = pl.pallas_call(
        functools.partial(_proj_kernel, prev=prev, nxt=nxt, final=final),
        grid=(B, S // tm),
        in_specs=in_specs,
        out_specs=out_specs,
        out_shape=out_shape,
        scratch_shapes=scratch,
        compiler_params=_params("arbitrary", "arbitrary"),
        name=f"proj_{prev}_{nxt}",
    )(*args)
    return outs


def _prep_even(w_in, b_f, w_lr, b_lr):
    fw = ATT_HEADS * HEAD_DIM
    kw = REC_HEADS * HEAD_DIM
    vw = REC_HEADS * VAL_DIM
    sizes = (fw, fw, fw, ATT_HEADS, kw, kw, vw, GLA_RANK, fw + vw)
    offs = np.cumsum((0,) + sizes)
    fq, fk, fv, ff, gq, gk, gv, glr, gate = (w_in[..., offs[i]:offs[i + 1]] for i in range(9))
    scale = HEAD_DIM ** -0.5
    L, D = w_in.shape[:2]

    def pad_heads(w, width):
        w = w.reshape(L, D, ATT_HEADS, HEAD_DIM)
        w = jnp.pad(w, ((0, 0), (0, 0), (0, 0), (0, width - HEAD_DIM)))
        return w.reshape(L, D, ATT_HEADS * width)

    w_main = jnp.concatenate([gate, fk, gv, gq * scale, gk], axis=-1).astype(BF16)
    wq = (fq * (scale * LOG2E)).astype(BF16)
    wv_t = jnp.swapaxes(pad_heads(fv, FOX_VROWS), 1, 2).astype(BF16)
    pad = LANES - ATT_HEADS - GLA_RANK
    w_small = jnp.pad(jnp.concatenate([ff, glr], axis=-1), ((0, 0), (0, 0), (0, pad))).astype(BF16)
    b_f_pad = jnp.pad(b_f, ((0, 0), (0, LANES - ATT_HEADS))).reshape(L, 1, LANES)
    w_lr_pad = jnp.pad(w_lr, ((0, 0), (ATT_HEADS, pad), (0, 0))).astype(BF16)
    q_aug = np.zeros((FOX_PAD - HEAD_DIM, TOKEN_TILE), np.float32)
    q_aug[:3, :] = -1.0
    v_aug = np.zeros((ATT_HEADS, FOX_VROWS, LANES), np.float32)
    v_aug[:, HEAD_DIM, :] = 1.0
    place = np.zeros((LANES, ATT_HEADS * FOX_PAD), np.float32)
    for i in range(3):
        for h in range(ATT_HEADS):
            place[ATT_HEADS * i + h, h * FOX_PAD + HEAD_DIM + i] = 1.0
    layered = (w_main, wq, wv_t, w_small, b_f_pad, w_lr_pad, b_lr.reshape(L, 1, kw))
    shared = (jnp.asarray(place, BF16), jnp.asarray(q_aug), jnp.asarray(v_aug.reshape(-1, LANES)))
    return layered, shared


def _prep_odd(w_in):
    kw = REC_HEADS * HEAD_DIM
    vw = REC_HEADS * VAL_DIM
    dw = ATT_HEADS * HEAD_DIM
    sizes = (kw, kw, vw, dw, dw, dw, vw + dw)
    offs = np.cumsum((0,) + sizes)
    rq, rk, rv, dq, dk, dv, gate = (w_in[..., offs[i]:offs[i + 1]] for i in range(7))
    scale = HEAD_DIM ** -0.5
    return jnp.concatenate([gate, rv, rq * scale, rk, dq * (scale * LOG2E), dk, dv], axis=-1).astype(BF16)


def _rope_tables(S):
    inv = jnp.power(ROPE_THETA, -jnp.arange(0, HEAD_DIM, 2, dtype=F32) / HEAD_DIM)
    ang = jnp.arange(S, dtype=F32)[:, None] * inv[None, :]
    cos, sin = jnp.cos(ang), jnp.sin(ang)
    reps = LANES // HEAD_DIM
    cos_t = jnp.tile(jnp.concatenate([cos, cos], axis=1), (1, reps))
    sin_t = jnp.tile(jnp.concatenate([-sin, sin], axis=1), (1, reps))
    return cos_t, sin_t


def _tile(S, pref):
    return min(S, pref)


def _even_mixers(z, q_t, v_t, glog, gla_g):
    S = z.shape[1]
    o_a = _fox_attention(z, q_t, v_t, _tile(S, FOX_BLOCK))
    o_b = _gla(z, glog, gla_g.reshape(1, -1), _tile(S, SCAN_TILE))
    return o_a, o_b


def _odd_mixers(z, zd4, zd16, gn_w, gn_b):
    B, S, _ = z.shape
    o_c = _retention(z, gn_w.reshape(1, -1), gn_b.reshape(1, -1), _tile(S, SCAN_TILE))
    sources = {1: (z.reshape(B, 1, S, Z_WIDTH), DIL_TILE0), 4: (zd4, 0), 16: (zd16, 0)}
    outs, lses = zip(*[_dilated_branch(*sources[dil], window, dil, DIL_QBLOCKS)
                       for window, dil in DIL_PATTERNS])
    outs = (outs[0].reshape(B, S, COL_TILE), *outs[1:])
    lses = (lses[0].reshape(B, S, LANES), *lses[1:])
    return o_c, outs, lses


def kernel(x, norm_even, w_in_even, b_f_even, w_lr_even, b_lr_even, gla_norm_even, w_out_even,
           norm_odd, w_in_odd, ret_gn_w_odd, ret_gn_b_odd, w_out_odd, final_norm):
    depth = norm_even.shape[0] + norm_odd.shape[0]
    B, S, D = x.shape
    tm = _tile(S, TOKEN_TILE)
    cos, sin = _rope_tables(S)
    even_layered, even_shared = _prep_even(w_in_even, b_f_even, w_lr_even, b_lr_even)
    even_layered = (norm_even.reshape(-1, 1, D), *even_layered)
    odd_w = _prep_odd(w_in_odd)
    odd_g = norm_odd.reshape(-1, 1, D)
    w_out = {"even": w_out_even.astype(BF16), "odd": w_out_odd.astype(BF16)}

    def in_args(i):
        if i % 2 == 0:
            return "even", (i // 2, even_layered, even_shared)
        return "odd", (i // 2, odd_g, odd_w, cos, sin)

    h = x
    kind, args = in_args(0)
    proj = _proj_call(h, None, None, kind, args, None, tm)
    for i in range(depth):
        j = i // 2
        z = proj[0]
        if kind == "even":
            o_a, o_b = _even_mixers(*proj, gla_norm_even[j])
            prev_args = (o_a, o_b, z, w_out[kind], j)
        else:
            o_c, outs, lses = _odd_mixers(*proj, ret_gn_w_odd[j], ret_gn_b_odd[j])
            prev_args = (o_c, outs, lses, z, w_out[kind], j)
        nxt, nxt_args = in_args(i + 1) if i + 1 < depth else (None, None)
        tm_call = tm if nxt else _tile(S, FINAL_TILE)
        h, *proj = _proj_call(h, kind, prev_args, nxt, nxt_args, final_norm.reshape(1, D), tm_call)
        kind = nxt
    return h
```

```python
import functools
import math

import numpy as np
import jax
import jax.numpy as jnp
from jax import lax
from jax.experimental import pallas as pl
from jax.experimental.pallas import tpu as pltpu

F32 = jnp.float32
BF16 = jnp.bfloat16

EPS = 1e-6
ROPE_THETA = 10000.0
HEAD_DIM = 64
VAL_DIM = 128
ATT_HEADS = 8
REC_HEADS = 4
GLA_RANK = 16
GLA_TAU = 16.0
GLA_CHUNK = 64
RET_CHUNK = 256
DIL_BLOCK = 128
DIL_PATTERNS = ((128, 1), (512, 4), (2048, 16))

Z_WIDTH = 3584
DIL_TILE0 = 4
DIL_QBLOCKS = 8
DIL_VROWS = 144
ZE_WIDTH = 3072
FOX_PAD = 128
FOX_VROWS = 80
LOG2E = 1.4426950408889634
COL_TILE = 512
TOKEN_TILE = 512
SCAN_TILE = 1024
FINAL_TILE = 1024
FOX_BLOCK = 256
LANES = 128
V7X_VMEM_LIMIT = 56 * 1024 * 1024

NT_DIMS = (((1,), (1,)), ((), ()))
TN_DIMS = (((0,), (0,)), ((), ()))


def _params(*sem):
    return pltpu.CompilerParams(dimension_semantics=sem, vmem_limit_bytes=V7X_VMEM_LIMIT)


def _split2(x):
    hi = x.astype(BF16)
    lo = (x - hi.astype(F32)).astype(BF16)
    return hi, lo


def _split3(x):
    hi, lo = _split2(x)
    lo2 = (x - hi.astype(F32) - lo.astype(F32)).astype(BF16)
    return hi, lo, lo2


def _log_sigmoid(x):
    return jnp.minimum(x, 0.0) - jnp.log(1.0 + jnp.exp(-jnp.abs(x)))


def _rms_normed(x, g):
    ms = jnp.mean(x * x, axis=-1, keepdims=True)
    return x * lax.rsqrt(ms + EPS) * g


def _in_even_body(x, g_ref, w_ref, wq_ref, wv_ref, ws_ref, bf_ref, wlr_ref, blr_ref, place_ref,
                  qaug_ref, vaug_ref, z_ref, qt_ref, vt_ref, glog_ref, u_ref, carry_ref):
    tm = x.shape[0]

    @pl.when(pl.program_id(1) == 0)
    def _():
        carry_ref[...] = jnp.zeros_like(carry_ref)

    u_ref[...] = _rms_normed(x, g_ref[...]).astype(BF16)
    u = u_ref[...]

    zs = jnp.dot(u, ws_ref[...], preferred_element_type=F32)
    logf = _log_sigmoid(zs + bf_ref[...])
    row = lax.broadcasted_iota(jnp.int32, (LANES, LANES), 0)
    col = lax.broadcasted_iota(jnp.int32, (LANES, LANES), 1)
    tril = jnp.where(row >= col, 1.0, 0.0).astype(BF16)
    carry = carry_ref[...]
    blocks = []
    for r0 in range(0, tm, LANES):
        blk_sum = sum(jnp.dot(tril, t[r0:r0 + LANES, :], preferred_element_type=F32)
                      for t in _split3(logf)) + carry
        carry = blk_sum[LANES - 1:LANES, :]
        blocks.append(blk_sum)
    csum = jnp.concatenate(blocks, axis=0)
    carry_ref[...] = carry
    hi, lo, lo2 = (t.astype(F32) for t in _split3(csum * LOG2E))
    lane = lax.broadcasted_iota(jnp.int32, (tm, LANES), 1)
    packed = jnp.where(lane < ATT_HEADS, hi,
                       jnp.where(lane < 2 * ATT_HEADS, pltpu.roll(lo, ATT_HEADS, 1),
                                 jnp.where(lane < 3 * ATT_HEADS, pltpu.roll(lo2, 2 * ATT_HEADS, 1), 0.0)))
    c_cols = jnp.dot(packed.astype(BF16), place_ref[...], preferred_element_type=F32)

    glr = jnp.dot(zs.astype(BF16), wlr_ref[...], preferred_element_type=F32)
    glog_ref[0] = _log_sigmoid(glr + blr_ref[...]) * (1.0 / GLA_TAU)

    low = lax.broadcasted_iota(jnp.int32, (tm, LANES), 1) < HEAD_DIM
    for j in range(w_ref.shape[1] // COL_TILE):
        r = jnp.dot(u, w_ref[:, j * COL_TILE:(j + 1) * COL_TILE], preferred_element_type=F32)
        if j == 2:
            for g in range(ATT_HEADS // 2):
                pair = r[:, g * LANES:(g + 1) * LANES]
                for half, feats in ((0, pair), (1, pltpu.roll(pair, HEAD_DIM, 1))):
                    h = 2 * g + half
                    z0 = 2 * COL_TILE + h * FOX_PAD
                    z_ref[0, :, z0:z0 + FOX_PAD] = (
                        jnp.where(low, feats, 0.0) + c_cols[:, h * FOX_PAD:(h + 1) * FOX_PAD]).astype(BF16)
        else:
            t = j if j < 2 else j + 1
            z_ref[0, :, t * COL_TILE:(t + 1) * COL_TILE] = r.astype(BF16)

    def feature_major(w_tok_ref, aug_ref, out_ref, rows_per_head):
        tok = jnp.dot(u, w_tok_ref[...], preferred_element_type=F32)
        aug = aug_ref[...].astype(BF16)
        for g in range(ATT_HEADS // 2):
            t = tok[:, g * LANES:(g + 1) * LANES].T
            for half in range(2):
                r0 = (2 * g + half) * rows_per_head
                out_ref[0, r0:r0 + HEAD_DIM, :] = t[half * HEAD_DIM:(half + 1) * HEAD_DIM, :].astype(BF16)
                out_ref[0, r0 + HEAD_DIM:r0 + rows_per_head, :] = aug

    feature_major(wq_ref, qaug_ref, qt_ref, FOX_PAD)
    feature_major(wv_ref, vaug_ref, vt_ref, FOX_VROWS)


def _rope_tile(x, cos, sin):
    lane = lax.broadcasted_iota(jnp.int32, (x.shape[0], LANES), 1)
    first_half = (lane % HEAD_DIM) < (HEAD_DIM // 2)
    outs = []
    for c in range(x.shape[1] // LANES):
        xc = x[:, c * LANES:(c + 1) * LANES]
        partner = jnp.where(first_half,
                            pltpu.roll(xc, LANES - HEAD_DIM // 2, 1),
                            pltpu.roll(xc, HEAD_DIM // 2, 1))
        outs.append(xc * cos + partner * sin)
    return jnp.concatenate(outs, axis=-1)


def _in_odd_body(x, g_ref, w_ref, cos_ref, sin_ref, z_ref, zd4_ref, zd16_ref, u_ref, r_ref):
    tm = x.shape[0]
    u_ref[...] = _rms_normed(x, g_ref[...]).astype(BF16)
    u = u_ref[...]
    cos = cos_ref[...]
    sin = sin_ref[...]
    for j in reversed(range(Z_WIDTH // COL_TILE)):
        cols = slice(j * COL_TILE, (j + 1) * COL_TILE)
        r = jnp.dot(u, w_ref[:, cols], preferred_element_type=F32)
        if j in (3, 4, 5):
            r = _rope_tile(r, cos, sin)
        z_ref[0, :, cols] = r.astype(BF16)
        if j >= DIL_TILE0:
            t = j - DIL_TILE0
            for c in range(COL_TILE // LANES):
                slot = t * (COL_TILE // LANES) + c
                r_ref[slot] = r[:, c * LANES:(c + 1) * LANES]
                dcols = slice(t * COL_TILE + c * LANES, t * COL_TILE + (c + 1) * LANES)
                for dil, ref in ((4, zd4_ref), (16, zd16_ref)):
                    for res in range(dil):
                        ref[0, res, :, dcols] = (
                            r_ref[slot, pl.ds(res, tm // dil, stride=dil), :].astype(BF16))


def _fox_kernel(qt_ref, kp_ref, vt_ref, o_ref, s_ref, acc_ref, mp_ref, mc_ref, ot_ref, mask_ref, *, tq, tk):
    i = pl.program_id(1)
    heads = range(ATT_HEADS)
    rows = lambda h: slice(h * FOX_PAD, (h + 1) * FOX_PAD)
    vrows = lambda h: slice(h * FOX_VROWS, (h + 1) * FOX_VROWS)

    @pl.when(i == 0)
    def _():
        key = lax.broadcasted_iota(jnp.int32, (tk, tq), 0)
        qry = lax.broadcasted_iota(jnp.int32, (tk, tq), 1)
        mask_ref[0] = jnp.zeros((tk, tq), F32)
        mask_ref[1] = jnp.where(key <= qry, 0.0, -jnp.inf)
        mask_ref[2] = jnp.where(key + tk <= qry, 0.0, -jnp.inf)

    def score_stage(h, j):
        ks = pl.multiple_of(j * tk, tk)
        s = jnp.dot(kp_ref[0, pl.ds(ks, tk), rows(h)], qt_ref[0, rows(h), :],
                    preferred_element_type=F32)
        s = s + mask_ref[jnp.clip(j - 2 * i + 1, 0, 2)]
        s_ref[h] = s
        m_old = mc_ref[h]
        mp_ref[h] = m_old
        mc_ref[h] = jnp.maximum(m_old, jnp.max(s, axis=0, keepdims=True))

    def value_stage(h, j):
        ks = pl.multiple_of(j * tk, tk)
        m_new = mc_ref[h]
        alpha = jnp.exp2(mp_ref[h] - m_new)
        p = jnp.exp2((s_ref[h] - m_new).astype(BF16))
        pv = jnp.dot(vt_ref[0, vrows(h), pl.ds(ks, tk)], p, preferred_element_type=F32)
        acc_ref[h] = alpha * acc_ref[h] + pv

    def trip(j):
        for h in heads:
            value_stage(h, j - 1)
            score_stage(h, j)

    for h in heads:
        acc_ref[h] = jnp.zeros(acc_ref.shape[1:], F32)
        mc_ref[h] = jnp.full((1, tq), -jnp.inf, F32)
        score_stage(h, 0)

    trip(1)

    def pair(t, _):
        j = 2 + 2 * t
        trip(j)
        trip(j + 1)
        return 0

    lax.fori_loop(0, i, pair, 0)

    for h in heads:
        value_stage(h, 2 * i + 1)
        acc = acc_ref[h]
        ot_ref[h * HEAD_DIM:(h + 1) * HEAD_DIM, :] = (
            acc[:HEAD_DIM, :] / acc[HEAD_DIM:HEAD_DIM + 1, :])
    o_ref[0] = ot_ref[...].T.astype(BF16)


def _fox_attention(z, q_t, v_t, tk):
    B, S, _ = z.shape
    tq = 2 * tk
    fox_rows = ATT_HEADS * FOX_PAD
    val_rows = ATT_HEADS * FOX_VROWS
    return pl.pallas_call(
        functools.partial(_fox_kernel, tq=tq, tk=tk),
        grid=(B, S // tq),
        in_specs=[
            pl.BlockSpec((1, fox_rows, tq), lambda b, i: (b, 0, i)),
            pl.BlockSpec((1, S, fox_rows), lambda b, i: (b, 0, 1)),
            pl.BlockSpec((1, val_rows, S), lambda b, i: (b, 0, 0)),
        ],
        out_specs=pl.BlockSpec((1, tq, COL_TILE), lambda b, i: (b, i, 0)),
        out_shape=jax.ShapeDtypeStruct((B, S, COL_TILE), BF16),
        scratch_shapes=[pltpu.VMEM((ATT_HEADS, tk, tq), F32),
                        pltpu.VMEM((ATT_HEADS, FOX_VROWS, tq), F32),
                        pltpu.VMEM((ATT_HEADS, 1, tq), F32),
                        pltpu.VMEM((ATT_HEADS, 1, tq), F32),
                        pltpu.VMEM((ATT_HEADS * HEAD_DIM, tq), F32),
                        pltpu.VMEM((3, tk, tq), F32)],
        compiler_params=_params("arbitrary", "arbitrary"),
        name="fox_attention",
    )(q_t, z, v_t)


def _gla_kernel(q_ref, k_ref, v_ref, g_ref, gn_ref, o_ref,
                qhat_ref, top_ref, bot_ref, kinc_ref, dec_ref, dect_ref, state_ref):
    tg = q_ref.shape[1]
    C = GLA_CHUNK
    blk = 2 * C
    n_chunks = tg // C
    pairs = REC_HEADS // 2

    @pl.when(pl.program_id(1) == 0)
    def _():
        state_ref[...] = jnp.zeros_like(state_ref)

    row = lax.broadcasted_iota(jnp.int32, (blk, blk), 0)
    col = lax.broadcasted_iota(jnp.int32, (blk, blk), 1)
    chunk_start = row & (-C)
    tril = jnp.where(col <= row, jnp.where(col >= chunk_start, 1.0, 0.0), 0.0).astype(BF16)
    ghi, glo = _split2(g_ref[0])
    b = jnp.concatenate(
        [jnp.dot(tril, ghi[r0:r0 + blk, :], preferred_element_type=F32)
         + jnp.dot(tril, glo[r0:r0 + blk, :], preferred_element_type=F32)
         for r0 in range(0, tg, blk)], axis=0)
    q = q_ref[0].astype(F32)
    k = k_ref[0].astype(F32)
    tok = lax.broadcasted_iota(jnp.int32, (tg, LANES), 0)
    lane = lax.broadcasted_iota(jnp.int32, (tg, LANES), 1)
    in_a = (tok & C) == 0
    first = lane < HEAD_DIM
    qt = q * jnp.exp(b)
    kt = k * jnp.exp(-b)
    top_ref[...] = kt
    dec_ref[...] = jnp.zeros_like(dec_ref)
    kd_rows, dec_rows = [], []
    for c in range(n_chunks):
        rows = slice(c * C, (c + 1) * C)
        b_last = b[(c + 1) * C - 1:(c + 1) * C, :]
        kd_rows.append(k[rows, :] * jnp.exp(b_last - b[rows, :]))
        dec_rows.append(jnp.exp(b_last))
        dec_ref[c:c + 1, :] = dec_rows[-1]
    for c in range(0, n_chunks, 2):
        bot_ref[c * C:(c + 1) * C, :] = kd_rows[c]
        bot_ref[(c + 1) * C:(c + 2) * C, :] = kt[(c + 1) * C:(c + 2) * C, :]
        kinc_ref[c * C:(c + 1) * C, :] = kd_rows[c] * dec_rows[c + 1]
        kinc_ref[(c + 1) * C:(c + 2) * C, :] = kd_rows[c + 1]
    dect_ref[...] = dec_ref[...].T
    for half in range(2):
        for g in range(pairs):
            gs = slice(g * LANES, (g + 1) * LANES)
            qh = jnp.where(first, qt[:, gs], 0.0) if half == 0 else jnp.where(first, 0.0, qt[:, gs])
            qhat_ref[half, :, 2 * g * LANES:(2 * g + 1) * LANES] = jnp.where(in_a, qh, 0.0).astype(BF16)
            qhat_ref[half, :, (2 * g + 1) * LANES:(2 * g + 2) * LANES] = jnp.where(in_a, 0.0, qh).astype(BF16)

    rb = lax.broadcasted_iota(jnp.int32, (blk, blk), 0)
    cb = lax.broadcasted_iota(jnp.int32, (blk, blk), 1)
    lower = cb <= rb
    low_rows = rb < HEAD_DIM

    for n in range(tg // blk):
        rows = slice(n * blk, (n + 1) * blk)
        for g in range(pairs):
            gs = slice(g * LANES, (g + 1) * LANES)
            top_t = top_ref[rows, gs].T.astype(BF16)
            bot_t = bot_ref[rows, gs].T.astype(BF16)
            kinc_t = kinc_ref[rows, gs].T.astype(BF16)
            st = state_ref[g]
            dec_a = dect_ref[gs, 2 * n:2 * n + 1]
            dec_b = dect_ref[gs, 2 * n + 1:2 * n + 2]
            rhs = jnp.concatenate([jnp.concatenate([top_t, st.astype(BF16)], axis=1),
                                   jnp.concatenate([bot_t, (st * dec_a).astype(BF16)], axis=1)], axis=0)
            inc = []
            for half in range(2):
                h = 2 * g + half
                vs = slice(h * VAL_DIM, (h + 1) * VAL_DIM)
                v = v_ref[0, rows, vs]
                sc = jnp.dot(qhat_ref[half, rows, 2 * g * LANES:(2 * g + 2) * LANES], rhs,
                             preferred_element_type=F32)
                att = jnp.where(lower, sc[:, :blk], 0.0).astype(BF16)
                both = jnp.dot(jnp.concatenate([att, kinc_t], axis=0), v, preferred_element_type=F32)
                o = both[:blk, :] + sc[:, blk:]
                inc.append(both[blk:, :])
                ms = jnp.mean(o * o, axis=-1, keepdims=True)
                o_ref[0, rows, vs] = (o * lax.rsqrt(ms + EPS) * gn_ref[:, vs]).astype(BF16)
            state_ref[g] = st * (dec_a * dec_b) + jnp.where(low_rows, inc[0], inc[1])


def _gla(z, glog, gla_g, tg):
    B, S, _ = z.shape
    kw = REC_HEADS * HEAD_DIM
    vw = REC_HEADS * VAL_DIM
    return pl.pallas_call(
        _gla_kernel,
        grid=(B, S // tg),
        in_specs=[
            pl.BlockSpec((1, tg, kw), lambda b, s: (b, s, 10)),
            pl.BlockSpec((1, tg, kw), lambda b, s: (b, s, 11)),
            pl.BlockSpec((1, tg, vw), lambda b, s: (b, s, 4)),
            pl.BlockSpec((1, tg, kw), lambda b, s: (b, s, 0)),
            pl.BlockSpec((1, vw), lambda b, s: (0, 0)),
        ],
        out_specs=pl.BlockSpec((1, tg, vw), lambda b, s: (b, s, 0)),
        out_shape=jax.ShapeDtypeStruct((B, S, vw), BF16),
        scratch_shapes=[
            pltpu.VMEM((2, tg, 2 * kw), BF16),
            pltpu.VMEM((tg, kw), F32), pltpu.VMEM((tg, kw), F32), pltpu.VMEM((tg, kw), F32),
            pltpu.VMEM((LANES, kw), F32), pltpu.VMEM((kw, LANES), F32),
            pltpu.VMEM((REC_HEADS // 2, 2 * HEAD_DIM, VAL_DIM), F32),
        ],
        compiler_params=_params("arbitrary", "arbitrary"),
        name="gla",
    )(z, z, z, glog, gla_g)


def _ret_kernel(q_ref, k_ref, v_ref, gw_ref, gb_ref, o_ref, state_ref, dmat_ref, zeta_ref):
    tg = q_ref.shape[1]
    C = RET_CHUNK
    n_chunks = tg // C
    pairs = REC_HEADS // 2
    log_gamma = [math.log(1.0 - 2.0 ** (-5.0 - h)) for h in range(REC_HEADS)]

    @pl.when(pl.program_id(1) == 0)
    def _():
        state_ref[...] = jnp.zeros_like(state_ref)

    ri = lax.broadcasted_iota(jnp.int32, (C, C), 0)
    ci = lax.broadcasted_iota(jnp.int32, (C, C), 1)
    diff = (ri - ci).astype(F32)
    idx = lax.broadcasted_iota(jnp.int32, (C, 1), 0).astype(F32)
    low_lanes = lax.broadcasted_iota(jnp.int32, (C, LANES), 1) < HEAD_DIM
    low_rows = lax.broadcasted_iota(jnp.int32, (LANES, VAL_DIM), 0) < HEAD_DIM
    for h in range(REC_HEADS):
        dmat_ref[h] = jnp.where(diff >= 0, jnp.exp(jnp.maximum(diff, 0.0) * log_gamma[h]), 0.0)
    for g in range(pairs):
        zeta_ref[g] = jnp.where(low_lanes, jnp.exp((C - 1.0 - idx) * log_gamma[2 * g]),
                                jnp.exp((C - 1.0 - idx) * log_gamma[2 * g + 1]))

    for c in range(n_chunks):
        rows = slice(c * C, (c + 1) * C)
        for g in range(pairs):
            gs = slice(g * LANES, (g + 1) * LANES)
            q_pair = q_ref[0, rows, gs]
            k_f32 = k_ref[0, rows, gs].astype(F32)
            k_t = k_f32.T.astype(BF16)
            kz_t = (k_f32 * zeta_ref[g]).T.astype(BF16)
            st = state_ref[g]
            rhs = jnp.concatenate([k_t, st.astype(BF16)], axis=1)
            zero = jnp.zeros_like(q_pair)
            inc = []
            for half in range(2):
                h = 2 * g + half
                vs = slice(h * VAL_DIM, (h + 1) * VAL_DIM)
                xi = jnp.exp((idx + 1.0) * log_gamma[h])
                q = jnp.where(low_lanes, q_pair, zero) if half == 0 else jnp.where(low_lanes, zero, q_pair)
                v = v_ref[0, rows, vs]
                sc = jnp.dot(q, rhs, preferred_element_type=F32)
                att = (sc[:, :C] * dmat_ref[h]).astype(BF16)
                both = jnp.dot(jnp.concatenate([att, kz_t], axis=0), v, preferred_element_type=F32)
                o = both[:C, :] + xi * sc[:, C:]
                inc.append(both[C:, :])
                o = o - jnp.mean(o, axis=-1, keepdims=True)
                ms = jnp.mean(o * o, axis=-1, keepdims=True)
                o = o * lax.rsqrt(ms + EPS) * gw_ref[:, vs] + gb_ref[:, vs]
                o_ref[0, rows, vs] = o.astype(BF16)
            decay = jnp.where(low_rows, math.exp(C * log_gamma[2 * g]), math.exp(C * log_gamma[2 * g + 1]))
            state_ref[g] = st * decay + jnp.where(low_rows, inc[0], inc[1])


def _retention(z, gn_w, gn_b, tg):
    B, S, _ = z.shape
    kw = REC_HEADS * HEAD_DIM
    vw = REC_HEADS * VAL_DIM
    return pl.pallas_call(
        _ret_kernel,
        grid=(B, S // tg),
        in_specs=[
            pl.BlockSpec((1, tg, kw), lambda b, s: (b, s, 6)),
            pl.BlockSpec((1, tg, kw), lambda b, s: (b, s, 7)),
            pl.BlockSpec((1, tg, vw), lambda b, s: (b, s, 2)),
            pl.BlockSpec((1, vw), lambda b, s: (0, 0)),
            pl.BlockSpec((1, vw), lambda b, s: (0, 0)),
        ],
        out_specs=pl.BlockSpec((1, tg, vw), lambda b, s: (b, s, 0)),
        out_shape=jax.ShapeDtypeStruct((B, S, vw), BF16),
        scratch_shapes=[pltpu.VMEM((REC_HEADS // 2, 2 * HEAD_DIM, VAL_DIM), F32),
                        pltpu.VMEM((REC_HEADS, RET_CHUNK, RET_CHUNK), F32),
                        pltpu.VMEM((REC_HEADS // 2, RET_CHUNK, LANES), F32)],
        compiler_params=_params("arbitrary", "arbitrary"),
        name="retention",
    )(z, z, z, gn_w, gn_b)


def _dil_kernel(q_ref, k_ref, v_ref, o_ref, lse_ref,
                vt_ref, qt_ref, s_ref, mx_ref, bias_ref, ot_ref, lt_ref, *, blk, nblk, ncls, span):
    n = pl.program_id(2)
    m = k_ref.shape[2]
    pairs = ATT_HEADS // 2
    chains = [(g, half) for g in range(pairs) for half in range(2)]
    units = [(rc, t) for rc in range(ncls) for t in range(nblk)]
    feat = lax.broadcasted_iota(jnp.int32, (LANES, blk), 0)

    @pl.when(n == 0)
    def _():
        for rc in range(ncls):
            for g in range(pairs):
                vt_ref[rc, g * DIL_VROWS + LANES:(g + 1) * DIL_VROWS, :] = jnp.ones(
                    (DIL_VROWS - LANES, m), BF16)

        def build(kb, _):
            r0 = pl.multiple_of(kb * blk, blk)
            for rc in range(ncls):
                for g in range(pairs):
                    v = v_ref[0, rc, pl.ds(r0, blk), g * LANES:(g + 1) * LANES].astype(F32)
                    vt_ref[rc, g * DIL_VROWS:g * DIL_VROWS + LANES, pl.ds(r0, blk)] = v.T.astype(BF16)
            return 0

        lax.fori_loop(0, m // blk, build, 0)

    single_step = m == nblk * blk
    starts, wins = [], []
    for t in range(nblk):
        qb = n * nblk + t
        first = single_step and t == 0
        win = blk if first else 2 * blk
        start = 0 if first else pl.multiple_of(jnp.maximum(qb - 1, 0) * blk, blk)
        starts.append(start)
        wins.append(win)
        ki_w = lax.broadcasted_iota(jnp.int32, (win, blk), 0)
        qi_w = lax.broadcasted_iota(jnp.int32, (win, blk), 1)
        rel = qb * blk + qi_w - start - ki_w
        bias_ref[t, :win] = jnp.where(rel >= 0, jnp.where(rel <= span, 0.0, -jnp.inf), -jnp.inf)
    for u, (rc, t) in enumerate(units):
        for g in range(pairs):
            q_t = q_ref[0, rc, t * blk:(t + 1) * blk, g * LANES:(g + 1) * LANES].astype(F32).T
            qt_ref[u, 2 * g] = jnp.where(feat < HEAD_DIM, q_t, 0.0).astype(BF16)
            qt_ref[u, 2 * g + 1] = jnp.where(feat < HEAD_DIM, 0.0, q_t).astype(BF16)

    for u, (rc, t) in enumerate(units):
        for c, (g, half) in enumerate(chains):
            k = k_ref[0, rc, pl.ds(starts[t], wins[t]), g * LANES:(g + 1) * LANES]
            s = jnp.dot(k, qt_ref[u, c], preferred_element_type=F32) + bias_ref[t, :wins[t]]
            s_ref[u, c, :wins[t]] = s
            mx_ref[u, c] = jnp.max(s, axis=0, keepdims=True)

    for u, (rc, t) in enumerate(units):
        lt_ref[u] = jnp.zeros(lt_ref.shape[1:], F32)
        for c, (g, half) in enumerate(chains):
            mx = mx_ref[u, c]
            p = jnp.exp2((s_ref[u, c, :wins[t]] - mx).astype(BF16))
            acc = jnp.dot(vt_ref[rc, g * DIL_VROWS:(g + 1) * DIL_VROWS, pl.ds(starts[t], wins[t])], p,
                          preferred_element_type=F32)
            den = acc[LANES:LANES + 1, :]
            ot_ref[u, c * HEAD_DIM:(c + 1) * HEAD_DIM, :] = (
                acc[half * HEAD_DIM:(half + 1) * HEAD_DIM, :] * (1.0 / den))
            lt_ref[u, c:c + 1, :] = (mx + jnp.log2(den)) * (1.0 / LOG2E)
        qrows = slice(t * blk, (t + 1) * blk)
        o_ref[0, rc, qrows, :] = ot_ref[u].T.astype(BF16)
        lse_ref[0, rc, qrows, :] = lt_ref[u].T


def _dilated_branch(src, tile0, window, dil, units):
    B, _, m, _ = src.shape
    blk = DIL_BLOCK
    nblk = min(units, m // blk)
    ncls = min(dil, units // nblk)
    tq = nblk * blk
    n_units = ncls * nblk
    return pl.pallas_call(
        functools.partial(_dil_kernel, blk=blk, nblk=nblk, ncls=ncls, span=window // dil),
        grid=(B, dil // ncls, m // tq),
        in_specs=[
            pl.BlockSpec((1, ncls, tq, COL_TILE), lambda b, r, n: (b, r, n, tile0)),
            pl.BlockSpec((1, ncls, m, COL_TILE), lambda b, r, n: (b, r, 0, tile0 + 1)),
            pl.BlockSpec((1, ncls, m, COL_TILE), lambda b, r, n: (b, r, 0, tile0 + 2)),
        ],
        out_specs=[
            pl.BlockSpec((1, ncls, tq, COL_TILE), lambda b, r, n: (b, r, n, 0)),
            pl.BlockSpec((1, ncls, tq, LANES), lambda b, r, n: (b, r, n, 0)),
        ],
        out_shape=[
            jax.ShapeDtypeStruct((B, dil, m, COL_TILE), BF16),
            jax.ShapeDtypeStruct((B, dil, m, LANES), F32),
        ],
        scratch_shapes=[pltpu.VMEM((ncls, ATT_HEADS // 2 * DIL_VROWS, m), BF16),
                        pltpu.VMEM((n_units, ATT_HEADS, LANES, blk), BF16),
                        pltpu.VMEM((n_units, ATT_HEADS, 2 * blk, blk), F32),
                        pltpu.VMEM((n_units, ATT_HEADS, 1, blk), F32),
                        pltpu.VMEM((nblk, 2 * blk, blk), F32),
                        pltpu.VMEM((n_units, ATT_HEADS * HEAD_DIM, blk), F32),
                        pltpu.VMEM((n_units, LANES, blk), F32)],
        compiler_params=_params("arbitrary", "arbitrary", "arbitrary"),
        name=f"dilated_{dil}",
    )(src, src, src)


def _silu(x):
    return x / (1.0 + jnp.exp(-x))


def _mix_out(mix_a, mix_b, gate_ref, h_ref, w_ref):
    half = mix_a.shape[1]
    gate = _silu(gate_ref[0].astype(F32))
    a = (mix_a * gate[:, :half]).astype(BF16)
    b = (mix_b * gate[:, half:]).astype(BF16)
    y = (jnp.dot(a, w_ref[:half, :], preferred_element_type=F32)
         + jnp.dot(b, w_ref[half:, :], preferred_element_type=F32))
    return h_ref[0] + y


def _dilated_mix(d1_ref, d4_ref, d16_ref, l1_ref, l4_ref, l16_ref, e_ref,
                 o4_ref, o16_ref, ls4_ref, ls16_ref):
    tm = d1_ref.shape[1]
    chunks = d1_ref.shape[2] // LANES
    for dil, src, lsrc, dst, ldst in ((4, d4_ref, l4_ref, o4_ref, ls4_ref),
                                      (16, d16_ref, l16_ref, o16_ref, ls16_ref)):
        for res in range(dil):
            token_rows = pl.ds(res, tm // dil, stride=dil)
            ldst[token_rows, :] = lsrc[0, res]
            for c in range(chunks):
                dst[c, token_rows, :] = src[0, res, :, c * LANES:(c + 1) * LANES].astype(F32)
    l1, l2, l3 = l1_ref[0], ls4_ref[...], ls16_ref[...]
    mx = jnp.maximum(jnp.maximum(l1, l2), l3)
    e = e_ref[...]
    expand = lambda w: jnp.dot(w.astype(BF16), e, preferred_element_type=F32)
    x1, x2, x3 = expand(jnp.exp(l1 - mx)), expand(jnp.exp(l2 - mx)), expand(jnp.exp(l3 - mx))
    o4 = jnp.concatenate([o4_ref[c] for c in range(chunks)], axis=-1)
    o16 = jnp.concatenate([o16_ref[c] for c in range(chunks)], axis=-1)
    return (x1 * d1_ref[0].astype(F32) + x2 * o4 + x3 * o16) / (x1 + x2 + x3)


N_IN = {"even": 11, "odd": 4}
N_OUT = {"even": 4, "odd": 3}
N_SCRATCH = {"even": 2, "odd": 2}


def _proj_kernel(*refs, prev, nxt, final):
    refs = list(refs)
    take = lambda n: [refs.pop(0) for _ in range(n)]
    if prev is None:
        (x_ref,) = take(1)
    elif prev == "even":
        oa_ref, ob_ref, gate_ref, h_ref, wout_ref = take(5)
    else:
        oc_ref, d1_ref, d4_ref, d16_ref, l1_ref, l4_ref, l16_ref, e_ref, gate_ref, h_ref, wout_ref = take(11)
    if final:
        (fg_ref,) = take(1)
    in_params = take(N_IN[nxt]) if nxt else []
    if prev is not None:
        (hout_ref,) = take(1)
    in_outs = take(N_OUT[nxt]) if nxt else []
    mix_scratch = take(4) if prev == "odd" else []
    in_scratch = take(N_SCRATCH[nxt]) if nxt else []

    if prev is None:
        x = x_ref[0]
    elif prev == "even":
        x = _mix_out(oa_ref[0].astype(F32), ob_ref[0].astype(F32), gate_ref, h_ref, wout_ref)
    else:
        o_d = _dilated_mix(d1_ref, d4_ref, d16_ref, l1_ref, l4_ref, l16_ref, e_ref, *mix_scratch)
        x = _mix_out(oc_ref[0].astype(F32), o_d, gate_ref, h_ref, wout_ref)
    if final:
        x = _rms_normed(x, fg_ref[...])
    if prev is not None:
        hout_ref[0] = x
    if nxt == "even":
        _in_even_body(x, *in_params, *in_outs, *in_scratch)
    elif nxt == "odd":
        _in_odd_body(x, *in_params, *in_outs, *in_scratch)


def _proj_call(h, prev, prev_args, nxt, nxt_args, final_g, tm):
    B, S, D = h.shape
    half = COL_TILE
    tok = lambda b, s: (b, s, 0)
    const2 = lambda b, s: (0, 0)
    once = dict(pipeline_mode=pl.Buffered(1))
    weight = lambda shape: pl.BlockSpec(shape, (lambda b, s: (0,) * len(shape)), **once)

    def layer_weight(arr, j):
        return pl.BlockSpec((None, *arr.shape[1:]), (lambda b, s: (j,) + (0,) * (arr.ndim - 1)), **once)
    res_spec = lambda dil, c: pl.BlockSpec((1, dil, tm // dil, c), lambda b, s: (b, 0, s, 0))
    part = pl.BlockSpec((1, tm, half), tok)
    h_spec = pl.BlockSpec((1, tm, D), tok)
    args, in_specs, out_specs, out_shape, scratch = [], [], [], [], []

    if prev is None:
        args += [h]
        in_specs += [h_spec]
    elif prev == "even":
        o_a, o_b, z, w_out, jp = prev_args
        args += [o_a, o_b, z, h, w_out]
        in_specs += [part, part, pl.BlockSpec((1, tm, 2 * half), tok), h_spec, layer_weight(w_out, jp)]
    else:
        o_c, outs, lses, z, w_out, jp = prev_args
        expand = jnp.pad(jnp.repeat(jnp.eye(ATT_HEADS, dtype=BF16), HEAD_DIM, axis=1),
                         ((0, LANES - ATT_HEADS), (0, 0)))
        args += [o_c, *outs, *lses, expand, z, h, w_out]
        in_specs += [part, part, res_spec(4, half), res_spec(16, half),
                     pl.BlockSpec((1, tm, LANES), tok), res_spec(4, LANES), res_spec(16, LANES),
                     weight((LANES, half)), pl.BlockSpec((1, tm, 2 * half), tok), h_spec,
                     layer_weight(w_out, jp)]
    final = nxt is None
    if final:
        args += [final_g]
        in_specs += [weight((1, D))]

    if nxt == "even":
        jn, layered, shared = nxt_args
        fox_rows, val_rows, kw = ATT_HEADS * FOX_PAD, ATT_HEADS * FOX_VROWS, REC_HEADS * HEAD_DIM
        args += [*layered, *shared]
        in_specs += [layer_weight(a, jn) for a in layered] + [weight(a.shape) for a in shared]
    elif nxt == "odd":
        jn, g, w_main, cos, sin = nxt_args
        args += [g, w_main, cos, sin]
        in_specs += [layer_weight(g, jn), layer_weight(w_main, jn),
                     pl.BlockSpec((tm, LANES), lambda b, s: (s, 0)),
                     pl.BlockSpec((tm, LANES), lambda b, s: (s, 0))]

    if prev is not None:
        out_specs += [h_spec]
        out_shape += [jax.ShapeDtypeStruct((B, S, D), F32)]
    if nxt == "even":
        out_specs += [pl.BlockSpec((1, tm, ZE_WIDTH), tok),
                      pl.BlockSpec((1, fox_rows, tm), lambda b, s: (b, 0, s)),
                      pl.BlockSpec((1, val_rows, tm), lambda b, s: (b, 0, s)),
                      pl.BlockSpec((1, tm, kw), tok)]
        out_shape += [jax.ShapeDtypeStruct((B, S, ZE_WIDTH), BF16),
                      jax.ShapeDtypeStruct((B, fox_rows, S), BF16),
                      jax.ShapeDtypeStruct((B, val_rows, S), BF16),
                      jax.ShapeDtypeStruct((B, S, kw), F32)]
    elif nxt == "odd":
        dw = 3 * COL_TILE
        out_specs += [pl.BlockSpec((1, tm, Z_WIDTH), tok), res_spec(4, dw), res_spec(16, dw)]
        out_shape += [jax.ShapeDtypeStruct((B, S, Z_WIDTH), BF16),
                      jax.ShapeDtypeStruct((B, 4, S // 4, dw), BF16),
                      jax.ShapeDtypeStruct((B, 16, S // 16, dw), BF16)]

    if prev == "odd":
        scratch += [pltpu.VMEM((half // LANES, tm, LANES), F32),
                    pltpu.VMEM((half // LANES, tm, LANES), F32),
                    pltpu.VMEM((tm, LANES), F32), pltpu.VMEM((tm, LANES), F32)]
    if nxt == "even":
        scratch += [pltpu.VMEM((tm, D), BF16), pltpu.VMEM((1, LANES), F32)]
    elif nxt == "odd":
        scratch += [pltpu.VMEM((tm, D), BF16), pltpu.VMEM((3 * COL_TILE // LANES, tm, LANES), F32)]

    outs = pl.pallas_call(
        functools.partial(_proj_kernel, prev=prev, nxt=nxt, final=final),
        grid=(B, S // tm),
        in_specs=in_specs,
        out_specs=out_specs,
        out_shape=out_shape,
        scratch_shapes=scratch,
        compiler_params=_params("arbitrary", "arbitrary"),
        name=f"proj_{prev}_{nxt}",
    )(*args)
    return outs


def _prep_even(w_in, b_f, w_lr, b_lr):
    fw = ATT_HEADS * HEAD_DIM
    kw = REC_HEADS * HEAD_DIM
    vw = REC_HEADS * VAL_DIM
    sizes = (fw, fw, fw, ATT_HEADS, kw, kw, vw, GLA_RANK, fw + vw)
    offs = np.cumsum((0,) + sizes)
    fq, fk, fv, ff, gq, gk, gv, glr, gate = (w_in[..., offs[i]:offs[i + 1]] for i in range(9))
    scale = HEAD_DIM ** -0.5
    L = w_in.shape[0]

    w_main = jnp.concatenate([gate, fk, gv, gq * scale, gk], axis=-1).astype(BF16)
    wq = (fq * (scale * LOG2E)).astype(BF16)
    wv = fv.astype(BF16)
    pad = LANES - ATT_HEADS - GLA_RANK
    w_small = jnp.pad(jnp.concatenate([ff, glr], axis=-1), ((0, 0), (0, 0), (0, pad))).astype(BF16)
    b_f_pad = jnp.pad(b_f, ((0, 0), (0, LANES - ATT_HEADS))).reshape(L, 1, LANES)
    w_lr_pad = jnp.pad(w_lr, ((0, 0), (ATT_HEADS, pad), (0, 0))).astype(BF16)
    q_aug = np.zeros((FOX_PAD - HEAD_DIM, TOKEN_TILE), np.float32)
    q_aug[:3, :] = -1.0
    v_aug = np.zeros((FOX_VROWS - HEAD_DIM, TOKEN_TILE), np.float32)
    v_aug[0, :] = 1.0
    place = np.zeros((LANES, ATT_HEADS * FOX_PAD), np.float32)
    for i in range(3):
        for h in range(ATT_HEADS):
            place[ATT_HEADS * i + h, h * FOX_PAD + HEAD_DIM + i] = 1.0
    layered = (w_main, wq, wv, w_small, b_f_pad, w_lr_pad, b_lr.reshape(L, 1, kw))
    shared = (jnp.asarray(place, BF16), jnp.asarray(q_aug), jnp.asarray(v_aug))
    return layered, shared


def _prep_odd(w_in):
    kw = REC_HEADS * HEAD_DIM
    vw = REC_HEADS * VAL_DIM
    dw = ATT_HEADS * HEAD_DIM
    sizes = (kw, kw, vw, dw, dw, dw, vw + dw)
    offs = np.cumsum((0,) + sizes)
    rq, rk, rv, dq, dk, dv, gate = (w_in[..., offs[i]:offs[i + 1]] for i in range(7))
    scale = HEAD_DIM ** -0.5
    return jnp.concatenate([gate, rv, rq * scale, rk, dq * (scale * LOG2E), dk, dv], axis=-1).astype(BF16)


def _rope_tables(S):
    inv = jnp.power(ROPE_THETA, -jnp.arange(0, HEAD_DIM, 2, dtype=F32) / HEAD_DIM)
    ang = jnp.arange(S, dtype=F32)[:, None] * inv[None, :]
    cos, sin = jnp.cos(ang), jnp.sin(ang)
    reps = LANES // HEAD_DIM
    cos_t = jnp.tile(jnp.concatenate([cos, cos], axis=1), (1, reps))
    sin_t = jnp.tile(jnp.concatenate([-sin, sin], axis=1), (1, reps))
    return cos_t, sin_t


def _tile(S, pref):
    return min(S, pref)


def _even_mixers(z, q_t, v_t, glog, gla_g):
    S = z.shape[1]
    o_a = _fox_attention(z, q_t, v_t, _tile(S, FOX_BLOCK))
    o_b = _gla(z, glog, gla_g.reshape(1, -1), _tile(S, SCAN_TILE))
    return o_a, o_b


def _odd_mixers(z, zd4, zd16, gn_w, gn_b):
    B, S, _ = z.shape
    o_c = _retention(z, gn_w.reshape(1, -1), gn_b.reshape(1, -1), _tile(S, SCAN_TILE))
    sources = {1: (z.reshape(B, 1, S, Z_WIDTH), DIL_TILE0), 4: (zd4, 0), 16: (zd16, 0)}
    outs, lses = zip(*[_dilated_branch(*sources[dil], window, dil, DIL_QBLOCKS)
                       for window, dil in DIL_PATTERNS])
    outs = (outs[0].reshape(B, S, COL_TILE), *outs[1:])
    lses = (lses[0].reshape(B, S, LANES), *lses[1:])
    return o_c, outs, lses


def kernel(x, norm_even, w_in_even, b_f_even, w_lr_even, b_lr_even, gla_norm_even, w_out_even,
           norm_odd, w_in_odd, ret_gn_w_odd, ret_gn_b_odd, w_out_odd, final_norm):
    depth = norm_even.shape[0] + norm_odd.shape[0]
    B, S, D = x.shape
    tm = _tile(S, TOKEN_TILE)
    cos, sin = _rope_tables(S)
    even_layered, even_shared = _prep_even(w_in_even, b_f_even, w_lr_even, b_lr_even)
    even_layered = (norm_even.reshape(-1, 1, D), *even_layered)
    odd_w = _prep_odd(w_in_odd)
    odd_g = norm_odd.reshape(-1, 1, D)
    w_out = {"even": w_out_even.astype(BF16), "odd": w_out_odd.astype(BF16)}

    def in_args(i):
        if i % 2 == 0:
            return "even", (i // 2, even_layered, even_shared)
        return "odd", (i // 2, odd_g, odd_w, cos, sin)

    h = x
    kind, args = in_args(0)
    proj = _proj_call(h, None, None, kind, args, None, tm)
    for i in range(depth):
        j = i // 2
        z = proj[0]
        if kind == "even":
            o_a, o_b = _even_mixers(*proj, gla_norm_even[j])
            prev_args = (o_a, o_b, z, w_out[kind], j)
        else:
            o_c, outs, lses = _odd_mixers(*proj, ret_gn_w_odd[j], ret_gn_b_odd[j])
            prev_args = (o_c, outs, lses, z, w_out[kind], j)
        nxt, nxt_args = in_args(i + 1) if i + 1 < depth else (None, None)
        tm_call = tm if nxt else _tile(S, FINAL_TILE)
        h, *proj = _proj_call(h, kind, prev_args, nxt, nxt_args, final_norm.reshape(1, D), tm_call)
        kind = nxt
    return h
```

```python
import functools
import math

import numpy as np
import jax
import jax.numpy as jnp
from jax import lax
from jax.experimental import pallas as pl
from jax.experimental.pallas import tpu as pltpu

F32 = jnp.float32
BF16 = jnp.bfloat16

EPS = 1e-6
ROPE_THETA = 10000.0
HEAD_DIM = 64
VAL_DIM = 128
ATT_HEADS = 8
REC_HEADS = 4
GLA_RANK = 16
GLA_TAU = 16.0
GLA_CHUNK = 64
RET_CHUNK = 256
DIL_BLOCK = 128
DIL_PATTERNS = ((128, 1), (512, 4), (2048, 16))

Z_WIDTH = 3584
DIL_TILE0 = 4
DIL_QBLOCKS = 8
DIL_VROWS = 144
ZE_WIDTH = 3072
FOX_PAD = 128
FOX_VROWS = 80
LOG2E = 1.4426950408889634
COL_TILE = 512
TOKEN_TILE = 512
SCAN_TILE = 1024
FINAL_TILE = 1024
FOX_BLOCK = 256
LANES = 128
V7X_VMEM_LIMIT = 56 * 1024 * 1024

NT_DIMS = (((1,), (1,)), ((), ()))
TN_DIMS = (((0,), (0,)), ((), ()))


def _params(*sem):
    return pltpu.CompilerParams(dimension_semantics=sem, vmem_limit_bytes=V7X_VMEM_LIMIT)


def _split2(x):
    hi = x.astype(BF16)
    lo = (x - hi.astype(F32)).astype(BF16)
    return hi, lo


def _split3(x):
    hi, lo = _split2(x)
    lo2 = (x - hi.astype(F32) - lo.astype(F32)).astype(BF16)
    return hi, lo, lo2


def _log_sigmoid(x):
    return jnp.minimum(x, 0.0) - jnp.log(1.0 + jnp.exp(-jnp.abs(x)))


def _rms_normed(x, g):
    ms = jnp.mean(x * x, axis=-1, keepdims=True)
    return x * lax.rsqrt(ms + EPS) * g


def _in_even_body(x, g_ref, w_ref, wq_ref, wv_ref, ws_ref, bf_ref, wlr_ref, blr_ref, place_ref,
                  qaug_ref, vaug_ref, z_ref, qt_ref, vt_ref, glog_ref, u_ref, carry_ref):
    tm = x.shape[0]

    @pl.when(pl.program_id(1) == 0)
    def _():
        carry_ref[...] = jnp.zeros_like(carry_ref)

    u_ref[...] = _rms_normed(x, g_ref[...]).astype(BF16)
    u = u_ref[...]

    zs = jnp.dot(u, ws_ref[...], preferred_element_type=F32)
    logf = _log_sigmoid(zs + bf_ref[...])
    row = lax.broadcasted_iota(jnp.int32, (LANES, LANES), 0)
    col = lax.broadcasted_iota(jnp.int32, (LANES, LANES), 1)
    tril = jnp.where(row >= col, 1.0, 0.0).astype(BF16)
    carry = carry_ref[...]
    blocks = []
    for r0 in range(0, tm, LANES):
        blk_sum = sum(jnp.dot(tril, t[r0:r0 + LANES, :], preferred_element_type=F32)
                      for t in _split3(logf)) + carry
        carry = blk_sum[LANES - 1:LANES, :]
        blocks.append(blk_sum)
    csum = jnp.concatenate(blocks, axis=0)
    carry_ref[...] = carry
    hi, lo, lo2 = (t.astype(F32) for t in _split3(csum * LOG2E))
    lane = lax.broadcasted_iota(jnp.int32, (tm, LANES), 1)
    packed = jnp.where(lane < ATT_HEADS, hi,
                       jnp.where(lane < 2 * ATT_HEADS, pltpu.roll(lo, ATT_HEADS, 1),
                                 jnp.where(lane < 3 * ATT_HEADS, pltpu.roll(lo2, 2 * ATT_HEADS, 1), 0.0)))
    c_cols = jnp.dot(packed.astype(BF16), place_ref[...], preferred_element_type=F32)

    glr = jnp.dot(zs.astype(BF16), wlr_ref[...], preferred_element_type=F32)
    glog_ref[0] = _log_sigmoid(glr + blr_ref[...]) * (1.0 / GLA_TAU)

    low = lax.broadcasted_iota(jnp.int32, (tm, LANES), 1) < HEAD_DIM
    for j in range(w_ref.shape[1] // COL_TILE):
        r = jnp.dot(u, w_ref[:, j * COL_TILE:(j + 1) * COL_TILE], preferred_element_type=F32)
        if j == 2:
            for g in range(ATT_HEADS // 2):
                pair = r[:, g * LANES:(g + 1) * LANES]
                for half, feats in ((0, pair), (1, pltpu.roll(pair, HEAD_DIM, 1))):
                    h = 2 * g + half
                    z0 = 2 * COL_TILE + h * FOX_PAD
                    z_ref[0, :, z0:z0 + FOX_PAD] = (
                        jnp.where(low, feats, 0.0) + c_cols[:, h * FOX_PAD:(h + 1) * FOX_PAD]).astype(BF16)
        else:
            t = j if j < 2 else j + 1
            z_ref[0, :, t * COL_TILE:(t + 1) * COL_TILE] = r.astype(BF16)

    def feature_major(w_tok_ref, aug_ref, out_ref, rows_per_head):
        tok = jnp.dot(u, w_tok_ref[...], preferred_element_type=F32)
        aug = aug_ref[...].astype(BF16)
        for g in range(ATT_HEADS // 2):
            t = tok[:, g * LANES:(g + 1) * LANES].T
            for half in range(2):
                r0 = (2 * g + half) * rows_per_head
                out_ref[0, r0:r0 + HEAD_DIM, :] = t[half * HEAD_DIM:(half + 1) * HEAD_DIM, :].astype(BF16)
                out_ref[0, r0 + HEAD_DIM:r0 + rows_per_head, :] = aug

    feature_major(wq_ref, qaug_ref, qt_ref, FOX_PAD)
    feature_major(wv_ref, vaug_ref, vt_ref, FOX_VROWS)


def _rope_tile(x, cos, sin):
    lane = lax.broadcasted_iota(jnp.int32, (x.shape[0], LANES), 1)
    first_half = (lane % HEAD_DIM) < (HEAD_DIM // 2)
    outs = []
    for c in range(x.shape[1] // LANES):
        xc = x[:, c * LANES:(c + 1) * LANES]
        partner = jnp.where(first_half,
                            pltpu.roll(xc, LANES - HEAD_DIM // 2, 1),
                            pltpu.roll(xc, HEAD_DIM // 2, 1))
        outs.append(xc * cos + partner * sin)
    return jnp.concatenate(outs, axis=-1)


def _in_odd_body(x, g_ref, w_ref, cos_ref, sin_ref, z_ref, zd4_ref, zd16_ref, u_ref, r_ref):
    tm = x.shape[0]
    u_ref[...] = _rms_normed(x, g_ref[...]).astype(BF16)
    u = u_ref[...]
    cos = cos_ref[...]
    sin = sin_ref[...]
    for j in reversed(range(Z_WIDTH // COL_TILE)):
        cols = slice(j * COL_TILE, (j + 1) * COL_TILE)
        r = jnp.dot(u, w_ref[:, cols], preferred_element_type=F32)
        if j in (3, 4, 5):
            r = _rope_tile(r, cos, sin)
        z_ref[0, :, cols] = r.astype(BF16)
        if j >= DIL_TILE0:
            t = j - DIL_TILE0
            for c in range(COL_TILE // LANES):
                slot = t * (COL_TILE // LANES) + c
                r_ref[slot] = r[:, c * LANES:(c + 1) * LANES]
                dcols = slice(t * COL_TILE + c * LANES, t * COL_TILE + (c + 1) * LANES)
                for dil, ref in ((4, zd4_ref), (16, zd16_ref)):
                    for res in range(dil):
                        ref[0, res, :, dcols] = (
                            r_ref[slot, pl.ds(res, tm // dil, stride=dil), :].astype(BF16))


def _fox_kernel(qt_ref, kp_ref, vt_ref, o_ref, s_ref, acc_ref, mp_ref, mc_ref, ot_ref, mask_ref, *, tq, tk):
    i = pl.program_id(1)
    heads = range(ATT_HEADS)
    rows = lambda h: slice(h * FOX_PAD, (h + 1) * FOX_PAD)
    vrows = lambda h: slice(h * FOX_VROWS, (h + 1) * FOX_VROWS)

    @pl.when(i == 0)
    def _():
        key = lax.broadcasted_iota(jnp.int32, (tk, tq), 0)
        qry = lax.broadcasted_iota(jnp.int32, (tk, tq), 1)
        mask_ref[0] = jnp.zeros((tk, tq), F32)
        mask_ref[1] = jnp.where(key <= qry, 0.0, -jnp.inf)
        mask_ref[2] = jnp.where(key + tk <= qry, 0.0, -jnp.inf)

    def score_stage(h, j):
        ks = pl.multiple_of(j * tk, tk)
        s = jnp.dot(kp_ref[0, pl.ds(ks, tk), rows(h)], qt_ref[0, rows(h), :],
                    preferred_element_type=F32)
        s = s + mask_ref[jnp.clip(j - 2 * i + 1, 0, 2)]
        s_ref[h] = s
        m_old = mc_ref[h]
        mp_ref[h] = m_old
        mc_ref[h] = jnp.maximum(m_old, jnp.max(s, axis=0, keepdims=True))

    def value_stage(h, j):
        ks = pl.multiple_of(j * tk, tk)
        m_new = mc_ref[h]
        alpha = jnp.exp2(mp_ref[h] - m_new)
        p = jnp.exp2((s_ref[h] - m_new).astype(BF16))
        pv = jnp.dot(vt_ref[0, vrows(h), pl.ds(ks, tk)], p, preferred_element_type=F32)
        acc_ref[h] = alpha * acc_ref[h] + pv

    def trip(j):
        for h in heads:
            value_stage(h, j - 1)
            score_stage(h, j)

    for h in heads:
        acc_ref[h] = jnp.zeros(acc_ref.shape[1:], F32)
        mc_ref[h] = jnp.full((1, tq), -jnp.inf, F32)
        score_stage(h, 0)

    trip(1)

    def pair(t, _):
        j = 2 + 2 * t
        trip(j)
        trip(j + 1)
        return 0

    lax.fori_loop(0, i, pair, 0)

    for h in heads:
        value_stage(h, 2 * i + 1)
        acc = acc_ref[h]
        ot_ref[h * HEAD_DIM:(h + 1) * HEAD_DIM, :] = (
            acc[:HEAD_DIM, :] / acc[HEAD_DIM:HEAD_DIM + 1, :])
    o_ref[0] = ot_ref[...].T.astype(BF16)


def _fox_attention(z, q_t, v_t, tk):
    B, S, _ = z.shape
    tq = 2 * tk
    fox_rows = ATT_HEADS * FOX_PAD
    val_rows = ATT_HEADS * FOX_VROWS
    return pl.pallas_call(
        functools.partial(_fox_kernel, tq=tq, tk=tk),
        grid=(B, S // tq),
        in_specs=[
            pl.BlockSpec((1, fox_rows, tq), lambda b, i: (b, 0, i)),
            pl.BlockSpec((1, S, fox_rows), lambda b, i: (b, 0, 1)),
            pl.BlockSpec((1, val_rows, S), lambda b, i: (b, 0, 0)),
        ],
        out_specs=pl.BlockSpec((1, tq, COL_TILE), lambda b, i: (b, i, 0)),
        out_shape=jax.ShapeDtypeStruct((B, S, COL_TILE), BF16),
        scratch_shapes=[pltpu.VMEM((ATT_HEADS, tk, tq), F32),
                        pltpu.VMEM((ATT_HEADS, FOX_VROWS, tq), F32),
                        pltpu.VMEM((ATT_HEADS, 1, tq), F32),
                        pltpu.VMEM((ATT_HEADS, 1, tq), F32),
                        pltpu.VMEM((ATT_HEADS * HEAD_DIM, tq), F32),
                        pltpu.VMEM((3, tk, tq), F32)],
        compiler_params=_params("arbitrary", "arbitrary"),
        name="fox_attention",
    )(q_t, z, v_t)


def _gla_kernel(q_ref, k_ref, v_ref, g_ref, gn_ref, o_ref,
                qhat_ref, top_ref, bot_ref, kinc_ref, dec_ref, dect_ref, state_ref):
    tg = q_ref.shape[1]
    C = GLA_CHUNK
    blk = 2 * C
    n_chunks = tg // C
    pairs = REC_HEADS // 2

    @pl.when(pl.program_id(1) == 0)
    def _():
        state_ref[...] = jnp.zeros_like(state_ref)

    row = lax.broadcasted_iota(jnp.int32, (blk, blk), 0)
    col = lax.broadcasted_iota(jnp.int32, (blk, blk), 1)
    chunk_start = row & (-C)
    tril = jnp.where(col <= row, jnp.where(col >= chunk_start, 1.0, 0.0), 0.0).astype(BF16)
    ghi, glo = _split2(g_ref[0])
    b = jnp.concatenate(
        [jnp.dot(tril, ghi[r0:r0 + blk, :], preferred_element_type=F32)
         + jnp.dot(tril, glo[r0:r0 + blk, :], preferred_element_type=F32)
         for r0 in range(0, tg, blk)], axis=0)
    q = q_ref[0].astype(F32)
    k = k_ref[0].astype(F32)
    tok = lax.broadcasted_iota(jnp.int32, (tg, LANES), 0)
    lane = lax.broadcasted_iota(jnp.int32, (tg, LANES), 1)
    in_a = (tok & C) == 0
    first = lane < HEAD_DIM
    qt = q * jnp.exp(b)
    kt = k * jnp.exp(-b)
    top_ref[...] = kt
    dec_ref[...] = jnp.zeros_like(dec_ref)
    kd_rows, dec_rows = [], []
    for c in range(n_chunks):
        rows = slice(c * C, (c + 1) * C)
        b_last = b[(c + 1) * C - 1:(c + 1) * C, :]
        kd_rows.append(k[rows, :] * jnp.exp(b_last - b[rows, :]))
        dec_rows.append(jnp.exp(b_last))
        dec_ref[c:c + 1, :] = dec_rows[-1]
    for c in range(0, n_chunks, 2):
        bot_ref[c * C:(c + 1) * C, :] = kd_rows[c]
        bot_ref[(c + 1) * C:(c + 2) * C, :] = kt[(c + 1) * C:(c + 2) * C, :]
        kinc_ref[c * C:(c + 1) * C, :] = kd_rows[c] * dec_rows[c + 1]
        kinc_ref[(c + 1) * C:(c + 2) * C, :] = kd_rows[c + 1]
    dect_ref[...] = dec_ref[...].T
    for half in range(2):
        for g in range(pairs):
            gs = slice(g * LANES, (g + 1) * LANES)
            qh = jnp.where(first, qt[:, gs], 0.0) if half == 0 else jnp.where(first, 0.0, qt[:, gs])
            qhat_ref[half, :, 2 * g * LANES:(2 * g + 1) * LANES] = jnp.where(in_a, qh, 0.0).astype(BF16)
            qhat_ref[half, :, (2 * g + 1) * LANES:(2 * g + 2) * LANES] = jnp.where(in_a, 0.0, qh).astype(BF16)

    rb = lax.broadcasted_iota(jnp.int32, (blk, blk), 0)
    cb = lax.broadcasted_iota(jnp.int32, (blk, blk), 1)
    lower = cb <= rb
    low_rows = rb < HEAD_DIM

    for n in range(tg // blk):
        rows = slice(n * blk, (n + 1) * blk)
        for g in range(pairs):
            gs = slice(g * LANES, (g + 1) * LANES)
            top_t = top_ref[rows, gs].T.astype(BF16)
            bot_t = bot_ref[rows, gs].T.astype(BF16)
            kinc_t = kinc_ref[rows, gs].T.astype(BF16)
            st = state_ref[g]
            dec_a = dect_ref[gs, 2 * n:2 * n + 1]
            dec_b = dect_ref[gs, 2 * n + 1:2 * n + 2]
            rhs = jnp.concatenate([jnp.concatenate([top_t, st.astype(BF16)], axis=1),
                                   jnp.concatenate([bot_t, (st * dec_a).astype(BF16)], axis=1)], axis=0)
            inc = []
            for half in range(2):
                h = 2 * g + half
                vs = slice(h * VAL_DIM, (h + 1) * VAL_DIM)
                v = v_ref[0, rows, vs]
                sc = jnp.dot(qhat_ref[half, rows, 2 * g * LANES:(2 * g + 2) * LANES], rhs,
                             preferred_element_type=F32)
                att = jnp.where(lower, sc[:, :blk], 0.0).astype(BF16)
                both = jnp.dot(jnp.concatenate([att, kinc_t], axis=0), v, preferred_element_type=F32)
                o = both[:blk, :] + sc[:, blk:]
                inc.append(both[blk:, :])
                ms = jnp.mean(o * o, axis=-1, keepdims=True)
                o_ref[0, rows, vs] = (o * lax.rsqrt(ms + EPS) * gn_ref[:, vs]).astype(BF16)
            state_ref[g] = st * (dec_a * dec_b) + jnp.where(low_rows, inc[0], inc[1])


def _gla(z, glog, gla_g, tg):
    B, S, _ = z.shape
    kw = REC_HEADS * HEAD_DIM
    vw = REC_HEADS * VAL_DIM
    return pl.pallas_call(
        _gla_kernel,
        grid=(B, S // tg),
        in_specs=[
            pl.BlockSpec((1, tg, kw), lambda b, s: (b, s, 10)),
            pl.BlockSpec((1, tg, kw), lambda b, s: (b, s, 11)),
            pl.BlockSpec((1, tg, vw), lambda b, s: (b, s, 4)),
            pl.BlockSpec((1, tg, kw), lambda b, s: (b, s, 0)),
            pl.BlockSpec((1, vw), lambda b, s: (0, 0)),
        ],
        out_specs=pl.BlockSpec((1, tg, vw), lambda b, s: (b, s, 0)),
        out_shape=jax.ShapeDtypeStruct((B, S, vw), BF16),
        scratch_shapes=[
            pltpu.VMEM((2, tg, 2 * kw), BF16),
            pltpu.VMEM((tg, kw), F32), pltpu.VMEM((tg, kw), F32), pltpu.VMEM((tg, kw), F32),
            pltpu.VMEM((LANES, kw), F32), pltpu.VMEM((kw, LANES), F32),
            pltpu.VMEM((REC_HEADS // 2, 2 * HEAD_DIM, VAL_DIM), F32),
        ],
        compiler_params=_params("arbitrary", "arbitrary"),
        name="gla",
    )(z, z, z, glog, gla_g)


def _ret_kernel(q_ref, k_ref, v_ref, gw_ref, gb_ref, o_ref, state_ref, dmat_ref, zeta_ref):
    tg = q_ref.shape[1]
    C = RET_CHUNK
    n_chunks = tg // C
    pairs = REC_HEADS // 2
    log_gamma = [math.log(1.0 - 2.0 ** (-5.0 - h)) for h in range(REC_HEADS)]

    @pl.when(pl.program_id(1) == 0)
    def _():
        state_ref[...] = jnp.zeros_like(state_ref)

    ri = lax.broadcasted_iota(jnp.int32, (C, C), 0)
    ci = lax.broadcasted_iota(jnp.int32, (C, C), 1)
    diff = (ri - ci).astype(F32)
    idx = lax.broadcasted_iota(jnp.int32, (C, 1), 0).astype(F32)
    low_lanes = lax.broadcasted_iota(jnp.int32, (C, LANES), 1) < HEAD_DIM
    low_rows = lax.broadcasted_iota(jnp.int32, (LANES, VAL_DIM), 0) < HEAD_DIM
    for h in range(REC_HEADS):
        dmat_ref[h] = jnp.where(diff >= 0, jnp.exp(jnp.maximum(diff, 0.0) * log_gamma[h]), 0.0)
    for g in range(pairs):
        zeta_ref[g] = jnp.where(low_lanes, jnp.exp((C - 1.0 - idx) * log_gamma[2 * g]),
                                jnp.exp((C - 1.0 - idx) * log_gamma[2 * g + 1]))

    for c in range(n_chunks):
        rows = slice(c * C, (c + 1) * C)
        for g in range(pairs):
            gs = slice(g * LANES, (g + 1) * LANES)
            q_pair = q_ref[0, rows, gs]
            k_f32 = k_ref[0, rows, gs].astype(F32)
            k_t = k_f32.T.astype(BF16)
            kz_t = (k_f32 * zeta_ref[g]).T.astype(BF16)
            st = state_ref[g]
            rhs = jnp.concatenate([k_t, st.astype(BF16)], axis=1)
            zero = jnp.zeros_like(q_pair)
            inc = []
            for half in range(2):
                h = 2 * g + half
                vs = slice(h * VAL_DIM, (h + 1) * VAL_DIM)
                xi = jnp.exp((idx + 1.0) * log_gamma[h])
                q = jnp.where(low_lanes, q_pair, zero) if half == 0 else jnp.where(low_lanes, zero, q_pair)
                v = v_ref[0, rows, vs]
                sc = jnp.dot(q, rhs, preferred_element_type=F32)
                att = (sc[:, :C] * dmat_ref[h]).astype(BF16)
                both = jnp.dot(jnp.concatenate([att, kz_t], axis=0), v, preferred_element_type=F32)
                o = both[:C, :] + xi * sc[:, C:]
                inc.append(both[C:, :])
                o = o - jnp.mean(o, axis=-1, keepdims=True)
                ms = jnp.mean(o * o, axis=-1, keepdims=True)
                o = o * lax.rsqrt(ms + EPS) * gw_ref[:, vs] + gb_ref[:, vs]
                o_ref[0, rows, vs] = o.astype(BF16)
            decay = jnp.where(low_rows, math.exp(C * log_gamma[2 * g]), math.exp(C * log_gamma[2 * g + 1]))
            state_ref[g] = st * decay + jnp.where(low_rows, inc[0], inc[1])


def _retention(z, gn_w, gn_b, tg):
    B, S, _ = z.shape
    kw = REC_HEADS * HEAD_DIM
    vw = REC_HEADS * VAL_DIM
    return pl.pallas_call(
        _ret_kernel,
        grid=(B, S // tg),
        in_specs=[
            pl.BlockSpec((1, tg, kw), lambda b, s: (b, s, 6)),
            pl.BlockSpec((1, tg, kw), lambda b, s: (b, s, 7)),
            pl.BlockSpec((1, tg, vw), lambda b, s: (b, s, 2)),
            pl.BlockSpec((1, vw), lambda b, s: (0, 0)),
            pl.BlockSpec((1, vw), lambda b, s: (0, 0)),
        ],
        out_specs=pl.BlockSpec((1, tg, vw), lambda b, s: (b, s, 0)),
        out_shape=jax.ShapeDtypeStruct((B, S, vw), BF16),
        scratch_shapes=[pltpu.VMEM((REC_HEADS // 2, 2 * HEAD_DIM, VAL_DIM), F32),
                        pltpu.VMEM((REC_HEADS, RET_CHUNK, RET_CHUNK), F32),
                        pltpu.VMEM((REC_HEADS // 2, RET_CHUNK, LANES), F32)],
        compiler_params=_params("arbitrary", "arbitrary"),
        name="retention",
    )(z, z, z, gn_w, gn_b)


def _dil_kernel(q_ref, k_ref, v_ref, o_ref, lse_ref,
                vt_ref, qt_ref, s_ref, mx_ref, bias_ref, ot_ref, lt_ref, *, blk, nblk, ncls, span):
    n = pl.program_id(2)
    m = k_ref.shape[2]
    pairs = ATT_HEADS // 2
    chains = [(g, half) for g in range(pairs) for half in range(2)]
    units = [(rc, t) for rc in range(ncls) for t in range(nblk)]
    feat = lax.broadcasted_iota(jnp.int32, (LANES, blk), 0)

    @pl.when(n == 0)
    def _():
        for rc in range(ncls):
            for g in range(pairs):
                vt_ref[rc, g * DIL_VROWS + LANES:(g + 1) * DIL_VROWS, :] = jnp.ones(
                    (DIL_VROWS - LANES, m), BF16)

        def build(kb, _):
            r0 = pl.multiple_of(kb * blk, blk)
            for rc in range(ncls):
                for g in range(pairs):
                    v = v_ref[0, rc, pl.ds(r0, blk), g * LANES:(g + 1) * LANES].astype(F32)
                    vt_ref[rc, g * DIL_VROWS:g * DIL_VROWS + LANES, pl.ds(r0, blk)] = v.T.astype(BF16)
            return 0

        lax.fori_loop(0, m // blk, build, 0)

    single_step = m == nblk * blk
    starts, wins = [], []
    for t in range(nblk):
        qb = n * nblk + t
        first = single_step and t == 0
        win = blk if first else 2 * blk
        start = 0 if first else pl.multiple_of(jnp.maximum(qb - 1, 0) * blk, blk)
        starts.append(start)
        wins.append(win)
        ki_w = lax.broadcasted_iota(jnp.int32, (win, blk), 0)
        qi_w = lax.broadcasted_iota(jnp.int32, (win, blk), 1)
        rel = qb * blk + qi_w - start - ki_w
        bias_ref[t, :win] = jnp.where(rel >= 0, jnp.where(rel <= span, 0.0, -jnp.inf), -jnp.inf)
    for u, (rc, t) in enumerate(units):
        for g in range(pairs):
            q_t = q_ref[0, rc, t * blk:(t + 1) * blk, g * LANES:(g + 1) * LANES].astype(F32).T
            qt_ref[u, 2 * g] = jnp.where(feat < HEAD_DIM, q_t, 0.0).astype(BF16)
            qt_ref[u, 2 * g + 1] = jnp.where(feat < HEAD_DIM, 0.0, q_t).astype(BF16)

    for u, (rc, t) in enumerate(units):
        for c, (g, half) in enumerate(chains):
            k = k_ref[0, rc, pl.ds(starts[t], wins[t]), g * LANES:(g + 1) * LANES]
            s = jnp.dot(k, qt_ref[u, c], preferred_element_type=F32) + bias_ref[t, :wins[t]]
            s_ref[u, c, :wins[t]] = s
            mx_ref[u, c] = jnp.max(s, axis=0, keepdims=True)

    for u, (rc, t) in enumerate(units):
        lt_ref[u] = jnp.zeros(lt_ref.shape[1:], F32)
        for c, (g, half) in enumerate(chains):
            mx = mx_ref[u, c]
            p = jnp.exp2((s_ref[u, c, :wins[t]] - mx).astype(BF16))
            acc = jnp.dot(vt_ref[rc, g * DIL_VROWS:(g + 1) * DIL_VROWS, pl.ds(starts[t], wins[t])], p,
                          preferred_element_type=F32)
            den = acc[LANES:LANES + 1, :]
            ot_ref[u, c * HEAD_DIM:(c + 1) * HEAD_DIM, :] = (
                acc[half * HEAD_DIM:(half + 1) * HEAD_DIM, :] * (1.0 / den))
            lt_ref[u, c:c + 1, :] = (mx + jnp.log2(den)) * (1.0 / LOG2E)
        qrows = slice(t * blk, (t + 1) * blk)
        o_ref[0, rc, qrows, :] = ot_ref[u].T.astype(BF16)
        lse_ref[0, rc, qrows, :] = lt_ref[u].T


def _dilated_branch(src, tile0, window, dil, units):
    B, _, m, _ = src.shape
    blk = DIL_BLOCK
    nblk = min(units, m // blk)
    ncls = min(dil, units // nblk)
    tq = nblk * blk
    n_units = ncls * nblk
    return pl.pallas_call(
        functools.partial(_dil_kernel, blk=blk, nblk=nblk, ncls=ncls, span=window // dil),
        grid=(B, dil // ncls, m // tq),
        in_specs=[
            pl.BlockSpec((1, ncls, tq, COL_TILE), lambda b, r, n: (b, r, n, tile0)),
            pl.BlockSpec((1, ncls, m, COL_TILE), lambda b, r, n: (b, r, 0, tile0 + 1)),
            pl.BlockSpec((1, ncls, m, COL_TILE), lambda b, r, n: (b, r, 0, tile0 + 2)),
        ],
        out_specs=[
            pl.BlockSpec((1, ncls, tq, COL_TILE), lambda b, r, n: (b, r, n, 0)),
            pl.BlockSpec((1, ncls, tq, LANES), lambda b, r, n: (b, r, n, 0)),
        ],
        out_shape=[
            jax.ShapeDtypeStruct((B, dil, m, COL_TILE), BF16),
            jax.ShapeDtypeStruct((B, dil, m, LANES), F32),
        ],
        scratch_shapes=[pltpu.VMEM((ncls, ATT_HEADS // 2 * DIL_VROWS, m), BF16),
                        pltpu.VMEM((n_units, ATT_HEADS, LANES, blk), BF16),
                        pltpu.VMEM((n_units, ATT_HEADS, 2 * blk, blk), F32),
                        pltpu.VMEM((n_units, ATT_HEADS, 1, blk), F32),
                        pltpu.VMEM((nblk, 2 * blk, blk), F32),
                        pltpu.VMEM((n_units, ATT_HEADS * HEAD_DIM, blk), F32),
                        pltpu.VMEM((n_units, LANES, blk), F32)],
        compiler_params=_params("arbitrary", "arbitrary", "arbitrary"),
        name=f"dilated_{dil}",
    )(src, src, src)


def _silu(x):
    return x / (1.0 + jnp.exp(-x))


def _mix_out(mix_a, mix_b, gate_ref, h_ref, w_ref):
    half = mix_a.shape[1]
    gate = _silu(gate_ref[0].astype(F32))
    a = (mix_a * gate[:, :half]).astype(BF16)
    b = (mix_b * gate[:, half:]).astype(BF16)
    y = (jnp.dot(a, w_ref[:half, :], preferred_element_type=F32)
         + jnp.dot(b, w_ref[half:, :], preferred_element_type=F32))
    return h_ref[0] + y


def _dilated_mix(d1_ref, d4_ref, d16_ref, l1_ref, l4_ref, l16_ref, e_ref,
                 o4_ref, o16_ref, ls4_ref, ls16_ref):
    tm = d1_ref.shape[1]
    chunks = d1_ref.shape[2] // LANES
    for dil, src, lsrc, dst, ldst in ((4, d4_ref, l4_ref, o4_ref, ls4_ref),
                                      (16, d16_ref, l16_ref, o16_ref, ls16_ref)):
        for res in range(dil):
            token_rows = pl.ds(res, tm // dil, stride=dil)
            ldst[token_rows, :] = lsrc[0, res]
            for c in range(chunks):
                dst[c, token_rows, :] = src[0, res, :, c * LANES:(c + 1) * LANES].astype(F32)
    l1, l2, l3 = l1_ref[0], ls4_ref[...], ls16_ref[...]
    mx = jnp.maximum(jnp.maximum(l1, l2), l3)
    e = e_ref[...]
    expand = lambda w: jnp.dot(w.astype(BF16), e, preferred_element_type=F32)
    x1, x2, x3 = expand(jnp.exp(l1 - mx)), expand(jnp.exp(l2 - mx)), expand(jnp.exp(l3 - mx))
    o4 = jnp.concatenate([o4_ref[c] for c in range(chunks)], axis=-1)
    o16 = jnp.concatenate([o16_ref[c] for c in range(chunks)], axis=-1)
    return (x1 * d1_ref[0].astype(F32) + x2 * o4 + x3 * o16) / (x1 + x2 + x3)


N_IN = {"even": 11, "odd": 4}
N_OUT = {"even": 4, "odd": 3}
N_SCRATCH = {"even": 2, "odd": 2}


def _proj_kernel(*refs, prev, nxt, final):
    refs = list(refs)
    take = lambda n: [refs.pop(0) for _ in range(n)]
    if prev is None:
        (x_ref,) = take(1)
    elif prev == "even":
        oa_ref, ob_ref, gate_ref, h_ref, wout_ref = take(5)
    else:
        oc_ref, d1_ref, d4_ref, d16_ref, l1_ref, l4_ref, l16_ref, e_ref, gate_ref, h_ref, wout_ref = take(11)
    if final:
        (fg_ref,) = take(1)
    in_params = take(N_IN[nxt]) if nxt else []
    if prev is not None:
        (hout_ref,) = take(1)
    in_outs = take(N_OUT[nxt]) if nxt else []
    mix_scratch = take(4) if prev == "odd" else []
    in_scratch = take(N_SCRATCH[nxt]) if nxt else []

    if prev is None:
        x = x_ref[0]
    elif prev == "even":
        x = _mix_out(oa_ref[0].astype(F32), ob_ref[0].astype(F32), gate_ref, h_ref, wout_ref)
    else:
        o_d = _dilated_mix(d1_ref, d4_ref, d16_ref, l1_ref, l4_ref, l16_ref, e_ref, *mix_scratch)
        x = _mix_out(oc_ref[0].astype(F32), o_d, gate_ref, h_ref, wout_ref)
    if final:
        x = _rms_normed(x, fg_ref[...])
    if prev is not None:
        hout_ref[0] = x
    if nxt == "even":
        _in_even_body(x, *in_params, *in_outs, *in_scratch)
    elif nxt == "odd":
        _in_odd_body(x, *in_params, *in_outs, *in_scratch)


def _proj_call(h, prev, prev_args, nxt, nxt_args, final_g, tm):
    B, S, D = h.shape
    half = COL_TILE
    tok = lambda b, s: (b, s, 0)
    const2 = lambda b, s: (0, 0)
    once = dict(pipeline_mode=pl.Buffered(1))
    weight = lambda shape: pl.BlockSpec(shape, (lambda b, s: (0,) * len(shape)), **once)

    def layer_weight(arr, j):
        return pl.BlockSpec((None, *arr.shape[1:]), (lambda b, s: (j,) + (0,) * (arr.ndim - 1)), **once)
    res_spec = lambda dil, c: pl.BlockSpec((1, dil, tm // dil, c), lambda b, s: (b, 0, s, 0))
    part = pl.BlockSpec((1, tm, half), tok)
    h_spec = pl.BlockSpec((1, tm, D), tok)
    args, in_specs, out_specs, out_shape, scratch = [], [], [], [], []

    if prev is None:
        args += [h]
        in_specs += [h_spec]
    elif prev == "even":
        o_a, o_b, z, w_out, jp = prev_args
        args += [o_a, o_b, z, h, w_out]
        in_specs += [part, part, pl.BlockSpec((1, tm, 2 * half), tok), h_spec, layer_weight(w_out, jp)]
    else:
        o_c, outs, lses, z, w_out, jp = prev_args
        expand = jnp.pad(jnp.repeat(jnp.eye(ATT_HEADS, dtype=BF16), HEAD_DIM, axis=1),
                         ((0, LANES - ATT_HEADS), (0, 0)))
        args += [o_c, *outs, *lses, expand, z, h, w_out]
        in_specs += [part, part, res_spec(4, half), res_spec(16, half),
                     pl.BlockSpec((1, tm, LANES), tok), res_spec(4, LANES), res_spec(16, LANES),
                     weight((LANES, half)), pl.BlockSpec((1, tm, 2 * half), tok), h_spec,
                     layer_weight(w_out, jp)]
    final = nxt is None
    if final:
        args += [final_g]
        in_specs += [weight((1, D))]

    if nxt == "even":
        jn, layered, shared = nxt_args
        fox_rows, val_rows, kw = ATT_HEADS * FOX_PAD, ATT_HEADS * FOX_VROWS, REC_HEADS * HEAD_DIM
        args += [*layered, *shared]
        in_specs += [layer_weight(a, jn) for a in layered] + [weight(a.shape) for a in shared]
    elif nxt == "odd":
        jn, g, w_main, cos, sin = nxt_args
        args += [g, w_main, cos, sin]
        in_specs += [layer_weight(g, jn), layer_weight(w_main, jn),
                     pl.BlockSpec((tm, LANES), lambda b, s: (s, 0)),
                     pl.BlockSpec((tm, LANES), lambda b, s: (s, 0))]

    if prev is not None:
        out_specs += [h_spec]
        out_shape += [jax.ShapeDtypeStruct((B, S, D), F32)]
    if nxt == "even":
        out_specs += [pl.BlockSpec((1, tm, ZE_WIDTH), tok),
                      pl.BlockSpec((1, fox_rows, tm), lambda b, s: (b, 0, s)),
                      pl.BlockSpec((1, val_rows, tm), lambda b, s: (b, 0, s)),
                      pl.BlockSpec((1, tm, kw), tok)]
        out_shape += [jax.ShapeDtypeStruct((B, S, ZE_WIDTH), BF16),
                      jax.ShapeDtypeStruct((B, fox_rows, S), BF16),
                      jax.ShapeDtypeStruct((B, val_rows, S), BF16),
                      jax.ShapeDtypeStruct((B, S, kw), F32)]
    elif nxt == "odd":
        dw = 3 * COL_TILE
        out_specs += [pl.BlockSpec((1, tm, Z_WIDTH), tok), res_spec(4, dw), res_spec(16, dw)]
        out_shape += [jax.ShapeDtypeStruct((B, S, Z_WIDTH), BF16),
                      jax.ShapeDtypeStruct((B, 4, S // 4, dw), BF16),
                      jax.ShapeDtypeStruct((B, 16, S // 16, dw), BF16)]

    if prev == "odd":
        scratch += [pltpu.VMEM((half // LANES, tm, LANES), F32),
                    pltpu.VMEM((half // LANES, tm, LANES), F32),
                    pltpu.VMEM((tm, LANES), F32), pltpu.VMEM((tm, LANES), F32)]
    if nxt == "even":
        scratch += [pltpu.VMEM((tm, D), BF16), pltpu.VMEM((1, LANES), F32)]
    elif nxt == "odd":
        scratch += [pltpu.VMEM((tm, D), BF16), pltpu.VMEM((3 * COL_TILE // LANES, tm, LANES), F32)]

    outs = pl.pallas_call(
        functools.partial(_proj_kernel, prev=prev, nxt=nxt, final=final),
        grid=(B, S // tm),
        in_specs=in_specs,
        out_specs=out_specs,
        out_shape=out_shape,
        scratch_shapes=scratch,
        compiler_params=_params("arbitrary", "arbitrary"),
        name=f"proj_{prev}_{nxt}",
    )(*args)
    return outs


def _prep_even(w_in, b_f, w_lr, b_lr):
    fw = ATT_HEADS * HEAD_DIM
    kw = REC_HEADS * HEAD_DIM
    vw = REC_HEADS * VAL_DIM
    sizes = (fw, fw, fw, ATT_HEADS, kw, kw, vw, GLA_RANK, fw + vw)
    offs = np.cumsum((0,) + sizes)
    fq, fk, fv, ff, gq, gk, gv, glr, gate = (w_in[..., offs[i]:offs[i + 1]] for i in range(9))
    scale = HEAD_DIM ** -0.5
    L = w_in.shape[0]

    w_main = jnp.concatenate([gate, fk, gv, gq * scale, gk], axis=-1).astype(BF16)
    wq = (fq * (scale * LOG2E)).astype(BF16)
    wv = fv.astype(BF16)
    pad = LANES - ATT_HEADS - GLA_RANK
    w_small = jnp.pad(jnp.concatenate([ff, glr], axis=-1), ((0, 0), (0, 0), (0, pad))).astype(BF16)
    b_f_pad = jnp.pad(b_f, ((0, 0), (0, LANES - ATT_HEADS))).reshape(L, 1, LANES)
    w_lr_pad = jnp.pad(w_lr, ((0, 0), (ATT_HEADS, pad), (0, 0))).astype(BF16)
    q_aug = np.zeros((FOX_PAD - HEAD_DIM, TOKEN_TILE), np.float32)
    q_aug[:3, :] = -1.0
    v_aug = np.zeros((FOX_VROWS - HEAD_DIM, TOKEN_TILE), np.float32)
    v_aug[0, :] = 1.0
    place = np.zeros((LANES, ATT_HEADS * FOX_PAD), np.float32)
    for i in range(3):
        for h in range(ATT_HEADS):
            place[ATT_HEADS * i + h, h * FOX_PAD + HEAD_DIM + i] = 1.0
    layered = (w_main, wq, wv, w_small, b_f_pad, w_lr_pad, b_lr.reshape(L, 1, kw))
    shared = (jnp.asarray(place, BF16), jnp.asarray(q_aug), jnp.asarray(v_aug))
    return layered, shared


def _prep_odd(w_in):
    kw = REC_HEADS * HEAD_DIM
    vw = REC_HEADS * VAL_DIM
    dw = ATT_HEADS * HEAD_DIM
    sizes = (kw, kw, vw, dw, dw, dw, vw + dw)
    offs = np.cumsum((0,) + sizes)
    rq, rk, rv, dq, dk, dv, gate = (w_in[..., offs[i]:offs[i + 1]] for i in range(7))
    scale = HEAD_DIM ** -0.5
    return jnp.concatenate([gate, rv, rq * scale, rk, dq * (scale * LOG2E), dk, dv], axis=-1).astype(BF16)


def _rope_tables(S):
    inv = jnp.power(ROPE_THETA, -jnp.arange(0, HEAD_DIM, 2, dtype=F32) / HEAD_DIM)
    ang = jnp.arange(S, dtype=F32)[:, None] * inv[None, :]
    cos, sin = jnp.cos(ang), jnp.sin(ang)
    reps = LANES // HEAD_DIM
    cos_t = jnp.tile(jnp.concatenate([cos, cos], axis=1), (1, reps))
    sin_t = jnp.tile(jnp.concatenate([-sin, sin], axis=1), (1, reps))
    return cos_t, sin_t


def _tile(S, pref):
    return min(S, pref)


def _even_mixers(z, q_t, v_t, glog, gla_g):
    S = z.shape[1]
    o_a = _fox_attention(z, q_t, v_t, _tile(S, FOX_BLOCK))
    o_b = _gla(z, glog, gla_g.reshape(1, -1), _tile(S, SCAN_TILE))
    return o_a, o_b


def _odd_mixers(z, zd4, zd16, gn_w, gn_b):
    B, S, _ = z.shape
    o_c = _retention(z, gn_w.reshape(1, -1), gn_b.reshape(1, -1), _tile(S, SCAN_TILE))
    sources = {1: (z.reshape(B, 1, S, Z_WIDTH), DIL_TILE0), 4: (zd4, 0), 16: (zd16, 0)}
    outs, lses = zip(*[_dilated_branch(*sources[dil], window, dil, DIL_QBLOCKS * (1 if dil == 1 else 2))
                       for window, dil in DIL_PATTERNS])
    outs = (outs[0].reshape(B, S, COL_TILE), *outs[1:])
    lses = (lses[0].reshape(B, S, LANES), *lses[1:])
    return o_c, outs, lses


def kernel(x, norm_even, w_in_even, b_f_even, w_lr_even, b_lr_even, gla_norm_even, w_out_even,
           norm_odd, w_in_odd, ret_gn_w_odd, ret_gn_b_odd, w_out_odd, final_norm):
    depth = norm_even.shape[0] + norm_odd.shape[0]
    B, S, D = x.shape
    tm = _tile(S, TOKEN_TILE)
    cos, sin = _rope_tables(S)
    even_layered, even_shared = _prep_even(w_in_even, b_f_even, w_lr_even, b_lr_even)
    even_layered = (norm_even.reshape(-1, 1, D), *even_layered)
    odd_w = _prep_odd(w_in_odd)
    odd_g = norm_odd.reshape(-1, 1, D)
    w_out = {"even": w_out_even.astype(BF16), "odd": w_out_odd.astype(BF16)}

    def in_args(i):
        if i % 2 == 0:
            return "even", (i // 2, even_layered, even_shared)
        return "odd", (i // 2, odd_g, odd_w, cos, sin)

    h = x
    kind, args = in_args(0)
    proj = _proj_call(h, None, None, kind, args, None, tm)
    for i in range(depth):
        j = i // 2
        z = proj[0]
        if kind == "even":
            o_a, o_b = _even_mixers(*proj, gla_norm_even[j])
            prev_args = (o_a, o_b, z, w_out[kind], j)
        else:
            o_c, outs, lses = _odd_mixers(*proj, ret_gn_w_odd[j], ret_gn_b_odd[j])
            prev_args = (o_c, outs, lses, z, w_out[kind], j)
        nxt, nxt_args = in_args(i + 1) if i + 1 < depth else (None, None)
        tm_call = tm if nxt else _tile(S, FINAL_TILE)
        h, *proj = _proj_call(h, kind, prev_args, nxt, nxt_args, final_norm.reshape(1, D), tm_call)
        kind = nxt
    return h
```
